```python
import math
import jax, jax.numpy as jnp
from jax import lax
import numpy as np

D_MODEL = 1024
BATCH = 8
SEQ = 4096
DEPTH = 1
DEC_BATCH = 8
DEC_SEQ = 16
PAST_LEN = 1024

CHUNK = 64
D_MIX = D_MODEL
D_POOL = D_MIX // 2
D_SSM = D_MIX - D_POOL
POOL_WINDOWS = (2, 4, 8, 16)
N_POOL_GROUPS = len(POOL_WINDOWS)
POOL_GROUP = D_POOL // N_POOL_GROUPS
POOL_BUF = max(POOL_WINDOWS) - 1
SSM_GROUP = 16
N_SSM_GROUPS = D_SSM // SSM_GROUP
SSM_STATE = 64
DT_MIN = 1e-3
DT_MAX = 1e-1
N_MEM = 256
N_XHEADS = 4
XHEAD_DIM = D_MODEL // N_XHEADS
N_EXPERTS = 32
TOP_K = 4
D_EXPERT = D_MODEL
SWIGLU_LIMIT = 7.0
SWIGLU_ALPHA = 1.702
MOE_BLOCK = 256
EPS = 1e-6

kernel_name = 'hybrid_pool_s5_xattn_moe_stream_step'


def rmsnorm(x, g):
    xf = x.astype(jnp.float32)
    y = xf * lax.rsqrt(jnp.mean(xf * xf, axis=-1, keepdims=True) + EPS)
    return (y * g.astype(jnp.float32)).astype(x.dtype)


def pool_mixer(u, hist, w_pool, pool_scale, pos0):
    b, t, _ = u.shape
    up = jnp.concatenate([hist.astype(u.dtype), u], axis=1)
    upf = up.astype(jnp.float32)
    cs = jnp.concatenate([jnp.zeros_like(upf[:, :1]), jnp.cumsum(upf, axis=1)], axis=1)
    end = cs[:, POOL_BUF + 1:]
    pos = pos0 + jnp.arange(t)
    means = []
    for gi, w in enumerate(POOL_WINDOWS):
        sl = slice(gi * POOL_GROUP, (gi + 1) * POOL_GROUP)
        start = cs[:, POOL_BUF + 1 - w:POOL_BUF + 1 - w + t, sl]
        cnt = jnp.minimum(pos + 1, w).astype(jnp.float32)[None, :, None]
        means.append((end[:, :, sl] - start) / cnt)
    d = (jnp.concatenate(means, axis=-1) - u.astype(jnp.float32)).astype(u.dtype)
    d = d.reshape(b, t, N_POOL_GROUPS, POOL_GROUP)
    y = jnp.einsum('btgc,gcd->btgd', d, w_pool).reshape(b, t, D_POOL)
    return y * pool_scale, up[:, -POOL_BUF:]


def ssm_mixer(u, h0, a_re, a_im, log_dt, b_re, b_im, c_re, c_im, d_skip, w_glu, b_glu):
    b, t, _ = u.shape
    f32 = jnp.float32
    lam = lax.complex(a_re.astype(f32), a_im.astype(f32))
    dt = jnp.exp(log_dt.astype(f32))[:, None]
    lam_bar = jnp.exp(lam * dt)
    b_bar = ((lam_bar - 1.0) / lam)[:, :, None] * lax.complex(b_re.astype(f32), b_im.astype(f32))
    c_mat = lax.complex(c_re.astype(f32), c_im.astype(f32))
    ug = u.astype(f32).reshape(b, t, N_SSM_GROUPS, SSM_GROUP)
    blk = min(CHUNK, t)
    nb = t // blk
    u_blocks = ug.reshape(b, nb, blk, N_SSM_GROUPS, SSM_GROUP).transpose(1, 0, 2, 3, 4)

    def combine(left, right):
        a_l, h_l = left
        a_r, h_r = right
        return a_r * a_l, a_r * h_l + h_r

    def block_step(h, u_blk):
        bu = jnp.einsum('gpc,blgc->blgp', b_bar, u_blk.astype(jnp.complex64))
        bu = bu.at[:, 0].add(lam_bar * h)
        a = jnp.broadcast_to(lam_bar, bu.shape)
        _, hs = lax.associative_scan(combine, (a, bu), axis=1)
        y_blk = jnp.einsum('gcp,blgp->blgc', c_mat, hs).real
        return hs[:, -1], y_blk

    h_last, ys = lax.scan(block_step, h0, u_blocks)
    ys = ys.transpose(1, 0, 2, 3, 4).reshape(b, t, N_SSM_GROUPS, SSM_GROUP)
    y = ys + d_skip.astype(f32).reshape(N_SSM_GROUPS, SSM_GROUP) * ug
    y = y.reshape(b, t, D_SSM).astype(u.dtype)
    g = jax.nn.gelu(y)
    return g * jax.nn.sigmoid(g @ w_glu + b_glu), h_last


def memory_kv(mem, g, w_k, w_v):
    b = mem.shape[0]
    m = rmsnorm(mem, g)
    k = (m @ w_k).reshape(b, N_MEM, N_XHEADS, XHEAD_DIM)
    v = (m @ w_v).reshape(b, N_MEM, N_XHEADS, XHEAD_DIM)
    return k, v


def cross_attention(h, k, v, w_q, w_o):
    b, t, _ = h.shape
    q = (h @ w_q).reshape(b, t, N_XHEADS, XHEAD_DIM)
    s = jnp.einsum('bthd,bmhd->bhtm', q, k.astype(q.dtype)).astype(jnp.float32) * (XHEAD_DIM ** -0.5)
    p = jax.nn.softmax(s, axis=-1).astype(h.dtype)
    o = jnp.einsum('bhtm,bmhd->bthd', p, v.astype(h.dtype)).reshape(b, t, D_MODEL)
    return o @ w_o


def moe(x, w_router, b_router, w_gate, b_gate, w_up, b_up, w_down, b_down):
    b, t, d = x.shape
    n = b * t
    xt = x.reshape(n, d)
    logits = (xt @ w_router).astype(jnp.float32) + b_router.astype(jnp.float32)
    top_l, top_e = lax.top_k(logits, TOP_K)
    gates = jax.nn.softmax(top_l, axis=-1)
    flat_e = top_e.reshape(-1)
    order = jnp.argsort(flat_e)
    sorted_e = flat_e[order]
    sorted_tok = order // TOP_K
    counts = jnp.bincount(flat_e, length=N_EXPERTS)
    padded = (counts + MOE_BLOCK - 1) // MOE_BLOCK * MOE_BLOCK
    start = jnp.cumsum(counts) - counts
    pend = jnp.cumsum(padded)
    pstart = pend - padded
    dest = pstart[sorted_e] + (jnp.arange(n * TOP_K) - start[sorted_e])
    n_blocks = -(-(n * TOP_K + N_EXPERTS * (MOE_BLOCK - 1)) // MOE_BLOCK)
    rows = jnp.full((n_blocks * MOE_BLOCK,), n, jnp.int32).at[dest].set(sorted_tok.astype(jnp.int32))
    xpad = jnp.concatenate([xt, jnp.zeros((1, d), xt.dtype)], axis=0)[rows].reshape(n_blocks, MOE_BLOCK, d)
    blk_e = jnp.minimum(jnp.searchsorted(pend, jnp.arange(n_blocks) * MOE_BLOCK, side='right'), N_EXPERTS - 1)

    def expert_block(args):
        xb, e = args
        g = xb @ w_gate[e] + b_gate[e]
        u = xb @ w_up[e] + b_up[e]
        g = jnp.minimum(g, SWIGLU_LIMIT)
        u = jnp.clip(u, -SWIGLU_LIMIT, SWIGLU_LIMIT)
        hdn = g * jax.nn.sigmoid(SWIGLU_ALPHA * g) * (u + 1.0)
        return hdn @ w_down[e] + b_down[e]

    ypad = lax.map(expert_block, (xpad, blk_e)).reshape(-1, d)
    y_assign = ypad[dest]
    w_assign = gates.reshape(-1)[order]
    y = jax.ops.segment_sum(y_assign * w_assign[:, None].astype(y_assign.dtype), sorted_tok, num_segments=n)
    return y.reshape(b, t, d)


def layer(x, mem_k, mem_v, pool_hist, ssm_h0, pos0, p):
    z = rmsnorm(x, p['norm_mix']) @ p['w_in']
    y_pool, pool_new = pool_mixer(z[..., :D_POOL], pool_hist, p['w_pool'], p['pool_scale'], pos0)
    y_ssm, ssm_new = ssm_mixer(z[..., D_POOL:], ssm_h0, p['ssm_a_re'], p['ssm_a_im'], p['ssm_log_dt'],
                               p['ssm_b_re'], p['ssm_b_im'], p['ssm_c_re'], p['ssm_c_im'], p['ssm_d'],
                               p['w_glu'], p['b_glu'])
    x = x + jnp.concatenate([y_pool, y_ssm], axis=-1) @ p['w_out']
    x = x + cross_attention(rmsnorm(x, p['norm_xattn']), mem_k, mem_v, p['w_q'], p['w_o'])
    x = x + moe(rmsnorm(x, p['norm_ffn']), p['w_router'], p['b_router'], p['w_gate'], p['b_gate'],
                p['w_up'], p['b_up'], p['w_down'], p['b_down'])
    return x, pool_new, ssm_new


def setup_inputs(seed: int = 0) -> dict:
    key = jax.random.key(seed)
    keys = iter(jax.random.split(key, 64))
    L = DEPTH

    def nrm(shape, scale):
        return scale * jax.random.normal(next(keys), shape, jnp.float32)

    inp = {}
    inp['x_prompt'] = nrm((BATCH, SEQ, D_MODEL), 1.0)
    inp['x_sample'] = nrm((DEC_BATCH, DEC_SEQ, D_MODEL), 1.0)
    inp['mem_prompt'] = nrm((BATCH, N_MEM, D_MODEL), 1.0)
    inp['cache_mem_k'] = nrm((L, DEC_BATCH, N_MEM, N_XHEADS, XHEAD_DIM), 1.0)
    inp['cache_mem_v'] = nrm((L, DEC_BATCH, N_MEM, N_XHEADS, XHEAD_DIM), 1.0)
    inp['cache_pool'] = nrm((L, DEC_BATCH, POOL_BUF, D_POOL), 1.0)
    inp['state_ssm_re'] = nrm((L, DEC_BATCH, N_SSM_GROUPS, SSM_STATE), 0.3)
    inp['state_ssm_im'] = nrm((L, DEC_BATCH, N_SSM_GROUPS, SSM_STATE), 0.3)
    inp['norm_mix'] = 1.0 + nrm((L, D_MODEL), 0.05)
    inp['w_in'] = nrm((L, D_MODEL, D_MIX), D_MODEL ** -0.5)
    inp['w_pool'] = nrm((L, N_POOL_GROUPS, POOL_GROUP, POOL_GROUP), POOL_GROUP ** -0.5)
    inp['pool_scale'] = 1.0 + nrm((L, D_POOL), 0.1)
    inp['ssm_a_re'] = -0.5 + nrm((L, N_SSM_GROUPS, SSM_STATE), 0.01)
    inp['ssm_a_im'] = math.pi * jnp.arange(SSM_STATE, dtype=jnp.float32) + nrm((L, N_SSM_GROUPS, SSM_STATE), 0.01)
    inp['ssm_log_dt'] = jax.random.uniform(next(keys), (L, N_SSM_GROUPS), jnp.float32,
                                           minval=math.log(DT_MIN), maxval=math.log(DT_MAX))
    inp['ssm_b_re'] = nrm((L, N_SSM_GROUPS, SSM_STATE, SSM_GROUP), (2 * SSM_GROUP) ** -0.5)
    inp['ssm_b_im'] = nrm((L, N_SSM_GROUPS, SSM_STATE, SSM_GROUP), (2 * SSM_GROUP) ** -0.5)
    inp['ssm_c_re'] = nrm((L, N_SSM_GROUPS, SSM_GROUP, SSM_STATE), (2 * SSM_STATE) ** -0.5)
    inp['ssm_c_im'] = nrm((L, N_SSM_GROUPS, SSM_GROUP, SSM_STATE), (2 * SSM_STATE) ** -0.5)
    inp['ssm_d'] = nrm((L, D_SSM), 1.0)
    inp['w_glu'] = nrm((L, D_SSM, D_SSM), D_SSM ** -0.5)
    inp['b_glu'] = nrm((L, D_SSM), 0.01)
    inp['w_out'] = nrm((L, D_MIX, D_MODEL), D_MIX ** -0.5)
    inp['norm_xattn'] = 1.0 + nrm((L, D_MODEL), 0.05)
    inp['norm_mem'] = 1.0 + nrm((L, D_MODEL), 0.05)
    inp['w_q'] = nrm((L, D_MODEL, D_MODEL), D_MODEL ** -0.5)
    inp['w_k'] = nrm((L, D_MODEL, D_MODEL), D_MODEL ** -0.5)
    inp['w_v'] = nrm((L, D_MODEL, D_MODEL), D_MODEL ** -0.5)
    inp['w_o'] = nrm((L, D_MODEL, D_MODEL), D_MODEL ** -0.5)
    inp['norm_ffn'] = 1.0 + nrm((L, D_MODEL), 0.05)
    inp['w_router'] = nrm((L, D_MODEL, N_EXPERTS), D_MODEL ** -0.5)
    inp['b_router'] = nrm((L, N_EXPERTS), 0.01)
    inp['w_gate'] = nrm((L, N_EXPERTS, D_MODEL, D_EXPERT), D_MODEL ** -0.5)
    inp['b_gate'] = nrm((L, N_EXPERTS, D_EXPERT), 0.01)
    inp['w_up'] = nrm((L, N_EXPERTS, D_MODEL, D_EXPERT), D_MODEL ** -0.5)
    inp['b_up'] = nrm((L, N_EXPERTS, D_EXPERT), 0.01)
    inp['w_down'] = nrm((L, N_EXPERTS, D_EXPERT, D_MODEL), D_EXPERT ** -0.5)
    inp['b_down'] = nrm((L, N_EXPERTS, D_MODEL), 0.01)
    inp['norm_final'] = 1.0 + nrm((D_MODEL,), 0.05)
    return inp


def reference(x_prompt, x_sample, mem_prompt, cache_mem_k, cache_mem_v, cache_pool, state_ssm_re, state_ssm_im,
              norm_mix, w_in, w_pool, pool_scale, ssm_a_re, ssm_a_im, ssm_log_dt, ssm_b_re, ssm_b_im,
              ssm_c_re, ssm_c_im, ssm_d, w_glu, b_glu, w_out,
              norm_xattn, norm_mem, w_q, w_k, w_v, w_o,
              norm_ffn, w_router, b_router, w_gate, b_gate, w_up, b_up, w_down, b_down, norm_final):
    hp = x_prompt
    hs = x_sample
    b_p = x_prompt.shape[0]
    mem_k_l, mem_v_l, pool_p_l, pool_s_l = [], [], [], []
    ssm_re_p_l, ssm_im_p_l, ssm_re_s_l, ssm_im_s_l = [], [], [], []
    for l in range(DEPTH):
        p = dict(norm_mix=norm_mix[l], w_in=w_in[l], w_pool=w_pool[l], pool_scale=pool_scale[l],
                 ssm_a_re=ssm_a_re[l], ssm_a_im=ssm_a_im[l], ssm_log_dt=ssm_log_dt[l],
                 ssm_b_re=ssm_b_re[l], ssm_b_im=ssm_b_im[l], ssm_c_re=ssm_c_re[l], ssm_c_im=ssm_c_im[l],
                 ssm_d=ssm_d[l], w_glu=w_glu[l], b_glu=b_glu[l], w_out=w_out[l],
                 norm_xattn=norm_xattn[l], w_q=w_q[l], w_o=w_o[l],
                 norm_ffn=norm_ffn[l], w_router=w_router[l], b_router=b_router[l],
                 w_gate=w_gate[l], b_gate=b_gate[l], w_up=w_up[l], b_up=b_up[l],
                 w_down=w_down[l], b_down=b_down[l])
        mk, mv = memory_kv(mem_prompt, norm_mem[l], w_k[l], w_v[l])
        pool0 = jnp.zeros((b_p, POOL_BUF, D_POOL), hp.dtype)
        ssm0 = jnp.zeros((b_p, N_SSM_GROUPS, SSM_STATE), jnp.complex64)
        hp, pool_p, ssm_p = layer(hp, mk, mv, pool0, ssm0, 0, p)
        h0 = lax.complex(state_ssm_re[l].astype(jnp.float32), state_ssm_im[l].astype(jnp.float32))
        hs, pool_s, ssm_s = layer(hs, cache_mem_k[l], cache_mem_v[l], cache_pool[l], h0, PAST_LEN, p)
        mem_k_l.append(mk)
        mem_v_l.append(mv)
        pool_p_l.append(pool_p)
        pool_s_l.append(pool_s)
        ssm_re_p_l.append(ssm_p.real)
        ssm_im_p_l.append(ssm_p.imag)
        ssm_re_s_l.append(ssm_s.real)
        ssm_im_s_l.append(ssm_s.imag)
    y_prompt = rmsnorm(hp, norm_final)
    y_sample = rmsnorm(hs, norm_final)
    new_mem_k = jnp.stack(mem_k_l)
    new_mem_v = jnp.stack(mem_v_l)
    new_pool_prompt = jnp.stack(pool_p_l)
    new_pool_sample = jnp.stack(pool_s_l)
    new_ssm_re_prompt = jnp.stack(ssm_re_p_l)
    new_ssm_im_prompt = jnp.stack(ssm_im_p_l)
    new_ssm_re_sample = jnp.stack(ssm_re_s_l)
    new_ssm_im_sample = jnp.stack(ssm_im_s_l)
    return (y_prompt, y_sample, new_mem_k, new_mem_v, new_pool_prompt, new_pool_sample,
            new_ssm_re_prompt, new_ssm_im_prompt, new_ssm_re_sample, new_ssm_im_sample)
```

```python
import functools
import math

import jax
import jax.numpy as jnp
from jax import lax
from jax.experimental import pallas as pl
from jax.experimental.pallas import tpu as pltpu

D_MODEL = 1024
D_POOL = 512
D_SSM = 512
POOL_WINDOWS = (2, 4, 8, 16)
POOL_GROUP = 128
HIST_ROWS = 16
POOL_BUF = 15
SSM_GROUP = 16
N_SSM_GROUPS = 32
SSM_STATE = 64
N_STATE = N_SSM_GROUPS * SSM_STATE
CHUNK = 64
N_MEM = 256
N_XHEADS = 4
XHEAD_DIM = 256
N_EXPERTS = 32
TOP_K = 4
SWIGLU_LIMIT = 7.0
SWIGLU_ALPHA = 1.702
MOE_BLOCK = 256
EPS = 1e-6
PAST_LEN = 1024

LANES = 128
SUBLANES = 8
SLAB = 256
N_SLABS = N_STATE // SLAB
ROW_TILE = D_MODEL // LANES
VMEM_LIMIT = 56 * 1024 * 1024

BF16 = jnp.bfloat16
F32 = jnp.float32


def _rms(x, g):
    return x * lax.rsqrt(jnp.mean(x * x, axis=-1, keepdims=True) + EPS) * g


def _dot(a, b):
    return jnp.dot(a, b, preferred_element_type=F32)


def _scan_chunk(re, im, pw_re, pw_im, sub):
    row = lax.broadcasted_iota(jnp.int32, re.shape, 0)
    shift, k = 1, 0
    while shift < sub:
        if shift < SUBLANES:
            keep = row >= shift
            sre = jnp.where(keep, pltpu.roll(re, shift, 0), 0.0)
            sim = jnp.where(keep, pltpu.roll(im, shift, 0), 0.0)
        else:
            zeros = jnp.zeros((shift, re.shape[1]), F32)
            sre = jnp.concatenate([zeros, re[:sub - shift]], axis=0)
            sim = jnp.concatenate([zeros, im[:sub - shift]], axis=0)
        pr = pw_re[k:k + 1, :]
        pi = pw_im[k:k + 1, :]
        re, im = re + (pr * sre - pi * sim), im + (pr * sim + pi * sre)
        shift *= 2
        k += 1
    return re, im


def _mixer_kernel(x_ref, hist_ref, h0re_ref, h0im_ref, gmix_ref, win_ref, wpool_ref, pscale_ref,
                  bmat_ref, cre_ref, cim_ref, pwre_ref, pwim_ref, dskip_ref, wglu_ref, bglu_ref, wout_ref,
                  h_ref, poolnew_ref, ssmre_ref, ssmim_ref,
                  hist_scr, hre_scr, him_scr, bure_scr, buim_scr, *, tt, sub, pos0):
    j = pl.program_id(1)

    @pl.when(j == 0)
    def _():
        hist_scr[...] = hist_ref[0]
        hre_scr[...] = h0re_ref[0]
        him_scr[...] = h0im_ref[0]

    x = x_ref[0]
    xn = _rms(x, gmix_ref[...])
    z = _dot(xn.astype(BF16), win_ref[...])
    zp = z[:, :D_POOL]
    u = z[:, D_POOL:]

    ext = jnp.concatenate([hist_scr[...], zp], axis=0)
    pos = pos0 + j * tt + lax.broadcasted_iota(jnp.int32, (tt, 1), 0)
    acc = ext
    outs = []
    for gi, w in enumerate(POOL_WINDOWS):
        lo = gi * POOL_GROUP
        acc = acc[:, POOL_GROUP * (1 if gi else 0):]
        acc = acc + pltpu.roll(acc, w // 2, 0)
        wsum = acc[HIST_ROWS:, :POOL_GROUP]
        cnt = jnp.minimum(pos + 1, w).astype(F32)
        d = wsum / cnt - zp[:, lo:lo + POOL_GROUP]
        outs.append(_dot(d.astype(BF16), wpool_ref[gi]))
    y_pool = jnp.concatenate(outs, axis=1) * pscale_ref[...]
    hist_scr[...] = ext[tt:tt + HIST_ROWS]
    poolnew_ref[0] = ext[tt:tt + HIST_ROWS]

    bu = _dot(u.astype(BF16), bmat_ref[...])
    for s in range(N_SLABS):
        bure_scr[s] = bu[:, s * SLAB:(s + 1) * SLAB]
        buim_scr[s] = bu[:, N_STATE + s * SLAB:N_STATE + (s + 1) * SLAB]

    def slab_body(s, carry):
        pw_re = pwre_ref[s]
        pw_im = pwim_ref[s]
        lam_re = pw_re[0:1, :]
        lam_im = pw_im[0:1, :]
        h_re = hre_scr[s]
        h_im = him_scr[s]
        first = lax.broadcasted_iota(jnp.int32, (sub, SLAB), 0) == 0
        for c in range(tt // sub):
            rows = pl.ds(c * sub, sub)
            re = bure_scr[s, rows, :]
            im = buim_scr[s, rows, :]
            re = re + jnp.where(first, lam_re * h_re - lam_im * h_im, 0.0)
            im = im + jnp.where(first, lam_re * h_im + lam_im * h_re, 0.0)
            re, im = _scan_chunk(re, im, pw_re, pw_im, sub)
            bure_scr[s, rows, :] = re
            buim_scr[s, rows, :] = im
            h_re = re[sub - 1:sub, :]
            h_im = im[sub - 1:sub, :]
        hre_scr[s] = h_re
        him_scr[s] = h_im
        return carry

    lax.fori_loop(0, N_SLABS, slab_body, 0)
    ssmre_ref[0] = hre_scr[...]
    ssmim_ref[0] = him_scr[...]

    hs_re = jnp.concatenate([bure_scr[s] for s in range(N_SLABS)], axis=1)
    hs_im = jnp.concatenate([buim_scr[s] for s in range(N_SLABS)], axis=1)
    y = _dot(hs_re.astype(BF16), cre_ref[...]) + _dot(hs_im.astype(BF16), cim_ref[...])
    y = y + dskip_ref[...] * u
    g = 0.5 * y * (1.0 + jnp.tanh(math.sqrt(2.0 / math.pi) * (y + 0.044715 * (y * y * y))))
    y_ssm = g * jax.nn.sigmoid(_dot(g.astype(BF16), wglu_ref[...]) + bglu_ref[...])

    mix = jnp.concatenate([y_pool, y_ssm], axis=1)
    h_ref[0] = x + _dot(mix.astype(BF16), wout_ref[...])


def _const_spec(shape):
    return pl.BlockSpec(shape, lambda *_: (0,) * len(shape))


def _mixer(x, hist, h0re, h0im, prm, *, tt, sub, pos0):
    b, t, d = x.shape
    nlev = prm['pw_re'].shape[1]
    kern = functools.partial(_mixer_kernel, tt=tt, sub=sub, pos0=pos0)
    per_b3 = lambda shp: pl.BlockSpec((1,) + shp, lambda i, j: (i, 0, 0))
    per_b4 = lambda shp: pl.BlockSpec((1,) + shp, lambda i, j: (i, 0, 0, 0))
    return pl.pallas_call(
        kern,
        grid=(b, t // tt),
        in_specs=[
            pl.BlockSpec((1, tt, d), lambda i, j: (i, j, 0)),
            per_b3((HIST_ROWS, D_POOL)),
            per_b4((N_SLABS, 1, SLAB)),
            per_b4((N_SLABS, 1, SLAB)),
            _const_spec((1, d)),
            _const_spec((d, d)),
            _const_spec((len(POOL_WINDOWS), POOL_GROUP, POOL_GROUP)),
            _const_spec((1, D_POOL)),
            _const_spec((D_SSM, 2 * N_STATE)),
            _const_spec((N_STATE, D_SSM)),
            _const_spec((N_STATE, D_SSM)),
            _const_spec((N_SLABS, nlev, SLAB)),
            _const_spec((N_SLABS, nlev, SLAB)),
            _const_spec((1, D_SSM)),
            _const_spec((D_SSM, D_SSM)),
            _const_spec((1, D_SSM)),
            _const_spec((d, d)),
        ],
        out_specs=[
            pl.BlockSpec((1, tt, d), lambda i, j: (i, j, 0)),
            per_b3((HIST_ROWS, D_POOL)),
            per_b4((N_SLABS, 1, SLAB)),
            per_b4((N_SLABS, 1, SLAB)),
        ],
        out_shape=[
            jax.ShapeDtypeStruct((b, t, d), F32),
            jax.ShapeDtypeStruct((b, HIST_ROWS, D_POOL), F32),
            jax.ShapeDtypeStruct((b, N_SLABS, 1, SLAB), F32),
            jax.ShapeDtypeStruct((b, N_SLABS, 1, SLAB), F32),
        ],
        scratch_shapes=[
            pltpu.VMEM((HIST_ROWS, D_POOL), F32),
            pltpu.VMEM((N_SLABS, 1, SLAB), F32),
            pltpu.VMEM((N_SLABS, 1, SLAB), F32),
            pltpu.VMEM((N_SLABS, tt, SLAB), F32),
            pltpu.VMEM((N_SLABS, tt, SLAB), F32),
        ],
        compiler_params=pltpu.CompilerParams(
            dimension_semantics=("arbitrary", "arbitrary"), vmem_limit_bytes=VMEM_LIMIT),
    )(x, hist, h0re, h0im, prm['g_mix'], prm['w_in'], prm['w_pool'], prm['pool_scale'],
      prm['bmat'], prm['c_re'], prm['c_im'], prm['pw_re'], prm['pw_im'], prm['d_skip'],
      prm['w_glu'], prm['b_glu'], prm['w_out'])


def _memkv_kernel(m_ref, g_ref, wk_ref, wv_ref, k_ref, v_ref):
    m = _rms(m_ref[0], g_ref[...]).astype(BF16)
    k_ref[0] = _dot(m, wk_ref[...])
    v_ref[0] = _dot(m, wv_ref[...])


def _memkv(mem, g, wk, wv):
    b, n, d = mem.shape
    blk = pl.BlockSpec((1, n, d), lambda i: (i, 0, 0))
    return pl.pallas_call(
        _memkv_kernel,
        grid=(b,),
        in_specs=[blk, _const_spec((1, d)), _const_spec((d, d)), _const_spec((d, d))],
        out_specs=[blk, blk],
        out_shape=[jax.ShapeDtypeStruct((b, n, d), F32)] * 2,
        compiler_params=pltpu.CompilerParams(vmem_limit_bytes=VMEM_LIMIT),
    )(mem, g, wk, wv)


def _attn_kernel(h_ref, k_ref, v_ref, gx_ref, wq_ref, wo_ref, gf_ref, wr_ref, br_ref,
                 h2_ref, xn_ref, e_ref, gate_ref, *, tt):
    h = h_ref[0]
    hn = _rms(h, gx_ref[...])
    q = _dot(hn.astype(BF16), wq_ref[...])
    outs = []
    for hd in range(N_XHEADS):
        cols = slice(hd * XHEAD_DIM, (hd + 1) * XHEAD_DIM)
        s = lax.dot_general(q[:, cols].astype(BF16), k_ref[0, :, cols], (((1,), (1,)), ((), ())),
                            preferred_element_type=F32) * (XHEAD_DIM ** -0.5)
        p = jnp.exp(s - jnp.max(s, axis=-1, keepdims=True))
        p = p / jnp.sum(p, axis=-1, keepdims=True)
        outs.append(_dot(p.astype(BF16), v_ref[0, :, cols]))
    o = jnp.concatenate(outs, axis=1)
    h2 = h + _dot(o.astype(BF16), wo_ref[...])
    h2_ref[0] = h2

    xn = _rms(h2, gf_ref[...])
    for s in range(ROW_TILE):
        xn_ref[pl.ds(s, tt, stride=ROW_TILE), :] = xn[:, s * LANES:(s + 1) * LANES]
    logits = _dot(xn.astype(BF16), wr_ref[...]) + br_ref[...]
    lane = lax.broadcasted_iota(jnp.int32, logits.shape, 1)
    e_out = jnp.zeros(logits.shape, jnp.int32)
    top = []
    for k in range(TOP_K):
        m = jnp.max(logits, axis=-1, keepdims=True)
        idx = jnp.min(jnp.where(logits == m, lane, LANES), axis=-1, keepdims=True)
        e_out = jnp.where(lane == k, idx, e_out)
        top.append(m)
        logits = jnp.where(lane == idx, -jnp.inf, logits)
    ex = [jnp.exp(m - top[0]) for m in top]
    tot = ex[0] + ex[1] + ex[2] + ex[3]
    g_out = jnp.zeros(logits.shape, F32)
    for k in range(TOP_K):
        g_out = jnp.where(lane == k, ex[k] / tot, g_out)
    e_ref[...] = e_out
    gate_ref[...] = g_out


def _attn(h, k, v, prm, *, tt):
    b, t, d = h.shape
    n = b * t
    kern = functools.partial(_attn_kernel, tt=tt)
    nt = t // tt
    kv = pl.BlockSpec((1, N_MEM, d), lambda i, j: (i, 0, 0))
    tok = lambda width: pl.BlockSpec((tt, width), lambda i, j: (i * nt + j, 0))
    return pl.pallas_call(
        kern,
        grid=(b, nt),
        in_specs=[
            pl.BlockSpec((1, tt, d), lambda i, j: (i, j, 0)), kv, kv,
            _const_spec((1, d)), _const_spec((d, d)), _const_spec((d, d)),
            _const_spec((1, d)), _const_spec((d, LANES)), _const_spec((1, LANES)),
        ],
        out_specs=[
            pl.BlockSpec((1, tt, d), lambda i, j: (i, j, 0)),
            pl.BlockSpec((tt * ROW_TILE, LANES), lambda i, j: (i * nt + j, 0)),
            tok(LANES), tok(LANES),
        ],
        out_shape=[
            jax.ShapeDtypeStruct((b, t, d), F32),
            jax.ShapeDtypeStruct((n * ROW_TILE, LANES), F32),
            jax.ShapeDtypeStruct((n, LANES), jnp.int32),
            jax.ShapeDtypeStruct((n, LANES), F32),
        ],
        compiler_params=pltpu.CompilerParams(
            dimension_semantics=("arbitrary", "arbitrary"), vmem_limit_bytes=VMEM_LIMIT),
    )(h, k, v, prm['g_xattn'], prm['w_q'], prm['w_o'], prm['g_ffn'], prm['w_router'], prm['b_router'])


RANK_TILE = 512


def _rank_kernel(e_ref, rank_ref, cnt_ref, carry_scr):
    i = pl.program_id(0)

    @pl.when(i == 0)
    def _():
        carry_scr[...] = jnp.zeros_like(carry_scr)

    e = e_ref[...]
    lane = lax.broadcasted_iota(jnp.int32, e.shape, 1)
    onehot = [lane == e[:, k:k + 1] for k in range(TOP_K)]
    hits = jnp.zeros(e.shape, F32)
    for oh in onehot:
        hits = hits + oh.astype(F32)
    r = lax.broadcasted_iota(jnp.int32, (RANK_TILE, RANK_TILE), 0)
    c = lax.broadcasted_iota(jnp.int32, (RANK_TILE, RANK_TILE), 1)
    before = (c < r).astype(BF16)
    base = _dot(before, hits.astype(BF16)) + carry_scr[...]
    rank = jnp.zeros(e.shape, jnp.int32)
    for k, oh in enumerate(onehot):
        rk = jnp.sum(jnp.where(oh, base, 0.0), axis=-1, keepdims=True).astype(jnp.int32)
        rank = jnp.where(lane == k, rk, rank)
    rank_ref[...] = rank
    carry_scr[...] = carry_scr[...] + jnp.sum(hits, axis=0, keepdims=True)
    cnt_ref[...] = carry_scr[...]


def _rank(e_all):
    n = e_all.shape[0]
    blk = pl.BlockSpec((RANK_TILE, LANES), lambda i: (i, 0))
    return pl.pallas_call(
        _rank_kernel,
        grid=(n // RANK_TILE,),
        in_specs=[blk],
        out_specs=[blk, _const_spec((1, LANES))],
        out_shape=[jax.ShapeDtypeStruct((n, LANES), jnp.int32), jax.ShapeDtypeStruct((1, LANES), F32)],
        scratch_shapes=[pltpu.VMEM((1, LANES), F32)],
        compiler_params=pltpu.CompilerParams(dimension_semantics=("arbitrary",)),
    )(e_all)


BLOCK_ROWS = MOE_BLOCK * ROW_TILE


def _row_copy_wait(src_ref, dst_ref, sem, rows):
    pltpu.make_async_copy(src_ref.at[pl.ds(0, rows * ROW_TILE)], dst_ref.at[pl.ds(0, rows * ROW_TILE)], sem).wait()


def _dispatch_kernel(last_blk_ref, nvalid_ref, dest_p_ref, dest_s_ref, xp_ref, xs_ref, xpad_ref,
                     zero_buf, sem, zero_sem, *, tile_p, tile_s, n_blocks):
    i = pl.program_id(0)
    last = pl.num_programs(0) - 1

    def zero_copy(blk):
        dst = xpad_ref.at[pl.ds(pl.multiple_of(blk * BLOCK_ROWS, BLOCK_ROWS), BLOCK_ROWS)]
        return pltpu.make_async_copy(zero_buf, dst, zero_sem)

    @pl.when(i == 0)
    def _():
        zero_buf[...] = jnp.zeros_like(zero_buf)

        def on_pad_blocks(fn):
            for e in range(N_EXPERTS):
                if e == 0:
                    fn(last_blk_ref[0])
                else:
                    pl.when(last_blk_ref[e] != last_blk_ref[e - 1])(functools.partial(fn, last_blk_ref[e]))
            lax.fori_loop(nvalid_ref[0], n_blocks, lambda blk, c: (fn(blk), c)[1], 0)

        on_pad_blocks(lambda blk: zero_copy(blk).start())
        on_pad_blocks(lambda blk: zero_copy(blk).wait())

    def push(dest_ref, x_ref, tile):
        def body(t, carry):
            src = x_ref.at[pl.ds(pl.multiple_of(t * ROW_TILE, ROW_TILE), ROW_TILE)]
            for k in range(TOP_K):
                row = dest_ref[0, 0, t * TOP_K + k]
                dst = xpad_ref.at[pl.ds(pl.multiple_of(row * ROW_TILE, ROW_TILE), ROW_TILE)]
                pltpu.make_async_copy(src, dst, sem).start()
            return carry

        lax.fori_loop(0, tile, body, 0)
        for _ in range(TOP_K):
            _row_copy_wait(x_ref, xpad_ref, sem, tile)

    pl.when(i < last)(lambda: push(dest_p_ref, xp_ref, tile_p))
    pl.when(i == last)(lambda: push(dest_s_ref, xs_ref, tile_s))


def _dispatch(last_blk, nvalid, dest_p, dest_s, xn_p, xn_s, n_blocks, *, tile_p):
    n_p = xn_p.shape[0] // ROW_TILE
    tile_s = xn_s.shape[0] // ROW_TILE
    steps_p = n_p // tile_p
    kern = functools.partial(_dispatch_kernel, tile_p=tile_p, tile_s=tile_s, n_blocks=n_blocks)
    return pl.pallas_call(
        kern,
        grid_spec=pltpu.PrefetchScalarGridSpec(
            num_scalar_prefetch=2,
            grid=(steps_p + 1,),
            in_specs=[
                pl.BlockSpec((1, 1, tile_p * TOP_K), lambda i, lb, nv: (jnp.minimum(i, steps_p - 1), 0, 0),
                             memory_space=pltpu.SMEM),
                pl.BlockSpec((1, 1, tile_s * TOP_K), lambda i, lb, nv: (0, 0, 0), memory_space=pltpu.SMEM),
                pl.BlockSpec((tile_p * ROW_TILE, LANES), lambda i, lb, nv: (jnp.minimum(i, steps_p - 1), 0)),
                pl.BlockSpec((tile_s * ROW_TILE, LANES), lambda i, lb, nv: (0, 0)),
            ],
            out_specs=pl.BlockSpec(memory_space=pl.ANY),
            scratch_shapes=[pltpu.VMEM((BLOCK_ROWS, LANES), F32), pltpu.SemaphoreType.DMA, pltpu.SemaphoreType.DMA],
        ),
        out_shape=jax.ShapeDtypeStruct((n_blocks * BLOCK_ROWS, LANES), F32),
        compiler_params=pltpu.CompilerParams(dimension_semantics=("arbitrary",)),
    )(last_blk, nvalid, dest_p.reshape(steps_p, 1, tile_p * TOP_K), dest_s.reshape(1, 1, tile_s * TOP_K), xn_p, xn_s)


def _expert_kernel(blk_e_ref, nvalid_ref, x_ref, wg_ref, bg_ref, wu_ref, bu_ref, wd_ref, bd_ref, y_ref):
    b = pl.program_id(0)

    @pl.when(b < nvalid_ref[0])
    def _():
        x = jnp.concatenate([x_ref[pl.ds(s, MOE_BLOCK, stride=ROW_TILE), :] for s in range(ROW_TILE)], axis=1)
        x = x.astype(BF16)
        g = _dot(x, wg_ref[0].astype(BF16)) + bg_ref[0]
        u = _dot(x, wu_ref[0].astype(BF16)) + bu_ref[0]
        g = jnp.minimum(g, SWIGLU_LIMIT)
        u = jnp.clip(u, -SWIGLU_LIMIT, SWIGLU_LIMIT)
        hdn = g * jax.nn.sigmoid(SWIGLU_ALPHA * g) * (u + 1.0)
        y = _dot(hdn.astype(BF16), wd_ref[0].astype(BF16)) + bd_ref[0]
        for s in range(ROW_TILE):
            y_ref[pl.ds(s, MOE_BLOCK, stride=ROW_TILE), :] = y[:, s * LANES:(s + 1) * LANES]

    @pl.when(b >= nvalid_ref[0])
    def _():
        y_ref[...] = jnp.zeros_like(y_ref)


def _experts(blk_e, nvalid, xpad, w_gate, b_gate, w_up, b_up, w_down, b_down, n_blocks):
    d = D_MODEL
    wspec = pl.BlockSpec((1, d, d), lambda b, be, nv: (be[b], 0, 0))
    bspec = pl.BlockSpec((1, 1, d), lambda b, be, nv: (be[b], 0, 0))
    xspec = pl.BlockSpec((BLOCK_ROWS, LANES), lambda b, be, nv: (jnp.minimum(b, nv[0] - 1), 0))
    return pl.pallas_call(
        _expert_kernel,
        grid_spec=pltpu.PrefetchScalarGridSpec(
            num_scalar_prefetch=2,
            grid=(n_blocks,),
            in_specs=[xspec, wspec, bspec, wspec, bspec, wspec, bspec],
            out_specs=pl.BlockSpec((BLOCK_ROWS, LANES), lambda b, be, nv: (b, 0)),
        ),
        out_shape=jax.ShapeDtypeStruct((n_blocks * BLOCK_ROWS, LANES), F32),
        compiler_params=pltpu.CompilerParams(dimension_semantics=("arbitrary",), vmem_limit_bytes=VMEM_LIMIT),
    )(blk_e, nvalid, xpad, w_gate, b_gate.reshape(N_EXPERTS, 1, d), w_up, b_up.reshape(N_EXPERTS, 1, d),
      w_down, b_down.reshape(N_EXPERTS, 1, d))


def _combine_kernel(dest_ref, h_ref, gate_ref, gfin_ref, ypad_ref, y_ref, buf, sem, *, tile):
    def body(t, carry):
        for k in range(TOP_K):
            row = dest_ref[0, 0, t * TOP_K + k]
            src = ypad_ref.at[pl.ds(pl.multiple_of(row * ROW_TILE, ROW_TILE), ROW_TILE)]
            dst = buf.at[pl.ds(pl.multiple_of((k * tile + t) * ROW_TILE, ROW_TILE), ROW_TILE)]
            pltpu.make_async_copy(src, dst, sem).start()
        return carry

    lax.fori_loop(0, tile, body, 0)
    for _ in range(TOP_K):
        _row_copy_wait(ypad_ref, buf, sem, tile)

    gates = gate_ref[...]
    h = h_ref[...]
    cols = []
    for s in range(ROW_TILE):
        acc = h[:, s * LANES:(s + 1) * LANES]
        for k in range(TOP_K):
            acc = acc + gates[:, k:k + 1] * buf[pl.ds(k * tile * ROW_TILE + s, tile, stride=ROW_TILE), :]
        cols.append(acc)
    y_ref[...] = _rms(jnp.concatenate(cols, axis=1), gfin_ref[...])


def _combine(dest, h2, gates, g_final, ypad, *, tile):
    n, d = h2.shape
    kern = functools.partial(_combine_kernel, tile=tile)
    return pl.pallas_call(
        kern,
        grid=(n // tile,),
        in_specs=[
            pl.BlockSpec((1, 1, tile * TOP_K), lambda i: (i, 0, 0), memory_space=pltpu.SMEM),
            pl.BlockSpec((tile, d), lambda i: (i, 0)),
            pl.BlockSpec((tile, LANES), lambda i: (i, 0)),
            _const_spec((1, d)),
            pl.BlockSpec(memory_space=pl.ANY),
        ],
        out_specs=pl.BlockSpec((tile, d), lambda i: (i, 0)),
        out_shape=jax.ShapeDtypeStruct((n, d), F32),
        scratch_shapes=[pltpu.VMEM((TOP_K * tile * ROW_TILE, LANES), F32), pltpu.SemaphoreType.DMA],
        compiler_params=pltpu.CompilerParams(dimension_semantics=("arbitrary",)),
    )(dest.reshape(n // tile, 1, tile * TOP_K), h2, gates, g_final, ypad)


def _ssm_params(a_re, a_im, log_dt, b_re, b_im, c_re, c_im, nlev):
    dt = jnp.exp(log_dt)[:, None]
    mag = jnp.exp(a_re * dt)
    lb_re = mag * jnp.cos(a_im * dt)
    lb_im = mag * jnp.sin(a_im * dt)
    den = a_re * a_re + a_im * a_im
    q_re = ((lb_re - 1.0) * a_re + lb_im * a_im) / den
    q_im = (lb_im * a_re - (lb_re - 1.0) * a_im) / den
    bb_re = q_re[:, :, None] * b_re - q_im[:, :, None] * b_im
    bb_im = q_re[:, :, None] * b_im + q_im[:, :, None] * b_re
    eye = jnp.eye(N_SSM_GROUPS, dtype=F32)
    blockdiag = lambda m: jnp.einsum('gab,gh->gahb', m, eye).reshape(
        N_SSM_GROUPS * m.shape[1], N_SSM_GROUPS * m.shape[2])
    bmat = jnp.concatenate([blockdiag(bb_re.transpose(0, 2, 1)), blockdiag(bb_im.transpose(0, 2, 1))], axis=1)
    cmat_re = blockdiag(c_re.transpose(0, 2, 1))
    cmat_im = blockdiag(-c_im.transpose(0, 2, 1))
    pw_re, pw_im = [lb_re.reshape(-1)], [lb_im.reshape(-1)]
    for _ in range(nlev - 1):
        r, i = pw_re[-1], pw_im[-1]
        pw_re.append(r * r - i * i)
        pw_im.append(2.0 * r * i)
    slabbed = lambda rows: jnp.stack(rows).reshape(nlev, N_SLABS, SLAB).transpose(1, 0, 2)
    return bmat.astype(BF16), cmat_re.astype(BF16), cmat_im.astype(BF16), slabbed(pw_re), slabbed(pw_im)


def kernel(x_prompt, x_sample, mem_prompt, cache_mem_k, cache_mem_v, cache_pool, state_ssm_re, state_ssm_im, norm_mix, w_in, w_pool, pool_scale, ssm_a_re, ssm_a_im, ssm_log_dt, ssm_b_re, ssm_b_im, ssm_c_re, ssm_c_im, ssm_d, w_glu, b_glu, w_out, norm_xattn, norm_mem, w_q, w_k, w_v, w_o, norm_ffn, w_router, b_router, w_gate, b_gate, w_up, b_up, w_down, b_down, norm_final):
    assert x_prompt.shape[2] == D_MODEL and norm_mix.shape[0] == 1
    bp, tp, d = x_prompt.shape
    bs, ts, _ = x_sample.shape
    n_p, n_s = bp * tp, bs * ts
    row = lambda v: v.reshape(1, -1)

    def mixer_params(sub):
        nlev = max(1, sub.bit_length() - 1)
        bmat, c_re, c_im, pw_re, pw_im = _ssm_params(ssm_a_re[0], ssm_a_im[0], ssm_log_dt[0], ssm_b_re[0],
                                                     ssm_b_im[0], ssm_c_re[0], ssm_c_im[0], nlev)
        return dict(g_mix=row(norm_mix[0]), w_in=w_in[0].astype(BF16), w_pool=w_pool[0].astype(BF16),
                    pool_scale=row(pool_scale[0]), bmat=bmat, c_re=c_re, c_im=c_im, pw_re=pw_re, pw_im=pw_im,
                    d_skip=row(ssm_d[0]), w_glu=w_glu[0].astype(BF16), b_glu=row(b_glu[0]),
                    w_out=w_out[0].astype(BF16))

    attn_prm = dict(g_xattn=row(norm_xattn[0]), w_q=w_q[0].astype(BF16), w_o=w_o[0].astype(BF16),
                    g_ffn=row(norm_ffn[0]),
                    w_router=jnp.pad(w_router[0], ((0, 0), (0, LANES - N_EXPERTS))).astype(BF16),
                    b_router=jnp.pad(row(b_router[0]), ((0, 0), (0, LANES - N_EXPERTS)), constant_values=-jnp.inf))
    slab_state = lambda s: s.reshape(s.shape[0], N_SLABS, 1, SLAB)

    zeros_state = jnp.zeros((bp, N_SLABS, 1, SLAB), F32)
    h1_p, pool_p, sre_p, sim_p = _mixer(x_prompt, jnp.zeros((bp, HIST_ROWS, D_POOL), F32), zeros_state, zeros_state,
                                        mixer_params(CHUNK), tt=256, sub=CHUNK, pos0=0)
    mk, mv = _memkv(mem_prompt, row(norm_mem[0]), w_k[0].astype(BF16), w_v[0].astype(BF16))
    h2_p, xn_p, e_p, g_p = _attn(h1_p, mk.astype(BF16), mv.astype(BF16), attn_prm, tt=256)

    hist_s = jnp.pad(cache_pool[0], ((0, 0), (HIST_ROWS - POOL_BUF, 0), (0, 0)))
    h1_s, pool_s, sre_s, sim_s = _mixer(x_sample, hist_s, slab_state(state_ssm_re[0]), slab_state(state_ssm_im[0]),
                                        mixer_params(min(CHUNK, ts)), tt=ts, sub=min(CHUNK, ts), pos0=PAST_LEN)
    ck = cache_mem_k[0].reshape(bs, N_MEM, d).astype(BF16)
    cv = cache_mem_v[0].reshape(bs, N_MEM, d).astype(BF16)
    h2_s, xn_s, e_s, g_s = _attn(h1_s, ck, cv, attn_prm, tt=ts)

    n_all = n_p + n_s
    n_rank = -(-n_all // RANK_TILE) * RANK_TILE
    e_all = jnp.concatenate([e_p, e_s, jnp.full((n_rank - n_all, LANES), -1, jnp.int32)], axis=0)
    rank, counts = _rank(e_all)
    counts = counts[0, :N_EXPERTS].astype(jnp.int32)
    padded = (counts + MOE_BLOCK - 1) // MOE_BLOCK * MOE_BLOCK
    pend = jnp.cumsum(padded)
    pstart = pend - padded
    n_blocks = -(-(n_all * TOP_K + N_EXPERTS * (MOE_BLOCK - 1)) // MOE_BLOCK)
    blk_e = jnp.minimum(jnp.searchsorted(pend, jnp.arange(n_blocks) * MOE_BLOCK, side='right'),
                        N_EXPERTS - 1).astype(jnp.int32)
    nvalid = (pend[-1:] // MOE_BLOCK).astype(jnp.int32)
    last_blk = jnp.maximum(pend // MOE_BLOCK - 1, 0).astype(jnp.int32)
    top_e = e_all[:n_all, :TOP_K]
    start_of = jnp.sum(jnp.where(top_e[:, :, None] == jnp.arange(N_EXPERTS)[None, None, :], pstart[None, None, :], 0),
                       axis=-1)
    dest = (start_of + rank[:n_all, :TOP_K]).astype(jnp.int32).reshape(-1)
    dest_p, dest_s = dest[:n_p * TOP_K], dest[n_p * TOP_K:]

    xpad = _dispatch(last_blk, nvalid, dest_p, dest_s, xn_p, xn_s, n_blocks, tile_p=256)
    ypad = _experts(blk_e, nvalid, xpad, w_gate[0], b_gate[0], w_up[0], b_up[0], w_down[0], b_down[0], n_blocks)
    g_fin = row(norm_final)
    y_p = _combine(dest_p, h2_p.reshape(n_p, d), g_p, g_fin, ypad, tile=256)
    y_s = _combine(dest_s, h2_s.reshape(n_s, d), g_s, g_fin, ypad, tile=n_s)

    unslab = lambda s: s.reshape(1, s.shape[0], N_SSM_GROUPS, SSM_STATE)
    kv5 = lambda a: a.reshape(1, bp, N_MEM, N_XHEADS, XHEAD_DIM)
    return (y_p.reshape(bp, tp, d), y_s.reshape(bs, ts, d), kv5(mk), kv5(mv),
            pool_p[None, :, HIST_ROWS - POOL_BUF:], pool_s[None, :, HIST_ROWS - POOL_BUF:],
            unslab(sre_p), unslab(sim_p), unslab(sre_s), unslab(sim_s))
```

```python
import functools
import math

import jax
import jax.numpy as jnp
from jax import lax
from jax.experimental import pallas as pl
from jax.experimental.pallas import tpu as pltpu

D_MODEL = 1024
D_POOL = 512
D_SSM = 512
POOL_WINDOWS = (2, 4, 8, 16)
POOL_GROUP = 128
HIST_ROWS = 16
POOL_BUF = 15
SSM_GROUP = 16
N_SSM_GROUPS = 32
SSM_STATE = 64
N_STATE = N_SSM_GROUPS * SSM_STATE
CHUNK = 64
N_MEM = 256
N_XHEADS = 4
XHEAD_DIM = 256
N_EXPERTS = 32
TOP_K = 4
SWIGLU_LIMIT = 7.0
SWIGLU_ALPHA = 1.702
MOE_BLOCK = 256
EPS = 1e-6
PAST_LEN = 1024

LANES = 128
SUBLANES = 8
SLAB = 256
N_SLABS = N_STATE // SLAB
ROW_TILE = D_MODEL // LANES
N_DMA_PRIORITIES = 2
VMEM_LIMIT = 56 * 1024 * 1024

BF16 = jnp.bfloat16
F32 = jnp.float32


def _rms(x, g):
    return x * lax.rsqrt(jnp.mean(x * x, axis=-1, keepdims=True) + EPS) * g


def _dot(a, b):
    return jnp.dot(a, b, preferred_element_type=F32)


def _scan_chunk(re, im, pw_re, pw_im, sub):
    row = lax.broadcasted_iota(jnp.int32, re.shape, 0)
    shift, k = 1, 0
    while shift < sub:
        if shift < SUBLANES:
            keep = row >= shift
            sre = jnp.where(keep, pltpu.roll(re, shift, 0), 0.0)
            sim = jnp.where(keep, pltpu.roll(im, shift, 0), 0.0)
        else:
            zeros = jnp.zeros((shift, re.shape[1]), F32)
            sre = jnp.concatenate([zeros, re[:sub - shift]], axis=0)
            sim = jnp.concatenate([zeros, im[:sub - shift]], axis=0)
        pr = pw_re[k:k + 1, :]
        pi = pw_im[k:k + 1, :]
        re, im = re + (pr * sre - pi * sim), im + (pr * sim + pi * sre)
        shift *= 2
        k += 1
    return re, im


def _mixer_kernel(x_ref, hist_ref, h0re_ref, h0im_ref, gmix_ref, win_ref, wpool_ref, pscale_ref,
                  bmat_ref, cre_ref, cim_ref, pwre_ref, pwim_ref, dskip_ref, wglu_ref, bglu_ref, wout_ref,
                  h_ref, poolnew_ref, ssmre_ref, ssmim_ref,
                  hist_scr, hre_scr, him_scr, bure_scr, buim_scr, *, tt, sub, pos0):
    j = pl.program_id(1)

    @pl.when(j == 0)
    def _():
        hist_scr[...] = hist_ref[0]
        hre_scr[...] = h0re_ref[0]
        him_scr[...] = h0im_ref[0]

    x = x_ref[0]
    xn = _rms(x, gmix_ref[...])
    z = _dot(xn.astype(BF16), win_ref[...])
    zp = z[:, :D_POOL]
    u = z[:, D_POOL:]

    ext = jnp.concatenate([hist_scr[...], zp], axis=0)
    pos = pos0 + j * tt + lax.broadcasted_iota(jnp.int32, (tt, 1), 0)
    acc = ext
    outs = []
    for gi, w in enumerate(POOL_WINDOWS):
        lo = gi * POOL_GROUP
        acc = acc[:, POOL_GROUP * (1 if gi else 0):]
        acc = acc + pltpu.roll(acc, w // 2, 0)
        wsum = acc[HIST_ROWS:, :POOL_GROUP]
        cnt = jnp.minimum(pos + 1, w).astype(F32)
        d = wsum / cnt - zp[:, lo:lo + POOL_GROUP]
        outs.append(_dot(d.astype(BF16), wpool_ref[gi]))
    y_pool = jnp.concatenate(outs, axis=1) * pscale_ref[...]
    hist_scr[...] = ext[tt:tt + HIST_ROWS]
    poolnew_ref[0] = ext[tt:tt + HIST_ROWS]

    bu = _dot(u.astype(BF16), bmat_ref[...])
    for s in range(N_SLABS):
        bure_scr[s] = bu[:, s * SLAB:(s + 1) * SLAB]
        buim_scr[s] = bu[:, N_STATE + s * SLAB:N_STATE + (s + 1) * SLAB]

    def slab_body(s, carry):
        pw_re = pwre_ref[s]
        pw_im = pwim_ref[s]
        lam_re = pw_re[0:1, :]
        lam_im = pw_im[0:1, :]
        h_re = hre_scr[s]
        h_im = him_scr[s]
        first = lax.broadcasted_iota(jnp.int32, (sub, SLAB), 0) == 0
        for c in range(tt // sub):
            rows = pl.ds(c * sub, sub)
            re = bure_scr[s, rows, :]
            im = buim_scr[s, rows, :]
            re = re + jnp.where(first, lam_re * h_re - lam_im * h_im, 0.0)
            im = im + jnp.where(first, lam_re * h_im + lam_im * h_re, 0.0)
            re, im = _scan_chunk(re, im, pw_re, pw_im, sub)
            bure_scr[s, rows, :] = re
            buim_scr[s, rows, :] = im
            h_re = re[sub - 1:sub, :]
            h_im = im[sub - 1:sub, :]
        hre_scr[s] = h_re
        him_scr[s] = h_im
        return carry

    lax.fori_loop(0, N_SLABS, slab_body, 0)
    ssmre_ref[0] = hre_scr[...]
    ssmim_ref[0] = him_scr[...]

    hs_re = jnp.concatenate([bure_scr[s] for s in range(N_SLABS)], axis=1)
    hs_im = jnp.concatenate([buim_scr[s] for s in range(N_SLABS)], axis=1)
    y = _dot(hs_re.astype(BF16), cre_ref[...]) + _dot(hs_im.astype(BF16), cim_ref[...])
    y = y + dskip_ref[...] * u
    g = 0.5 * y * (1.0 + jnp.tanh(math.sqrt(2.0 / math.pi) * (y + 0.044715 * (y * y * y))))
    y_ssm = g * jax.nn.sigmoid(_dot(g.astype(BF16), wglu_ref[...]) + bglu_ref[...])

    mix = jnp.concatenate([y_pool, y_ssm], axis=1)
    h_ref[0] = x + _dot(mix.astype(BF16), wout_ref[...])


def _const_spec(shape):
    return pl.BlockSpec(shape, lambda *_: (0,) * len(shape))


def _mixer(x, hist, h0re, h0im, prm, *, tt, sub, pos0):
    b, t, d = x.shape
    nlev = prm['pw_re'].shape[1]
    kern = functools.partial(_mixer_kernel, tt=tt, sub=sub, pos0=pos0)
    per_b3 = lambda shp: pl.BlockSpec((1,) + shp, lambda i, j: (i, 0, 0))
    per_b4 = lambda shp: pl.BlockSpec((1,) + shp, lambda i, j: (i, 0, 0, 0))
    return pl.pallas_call(
        kern,
        grid=(b, t // tt),
        in_specs=[
            pl.BlockSpec((1, tt, d), lambda i, j: (i, j, 0)),
            per_b3((HIST_ROWS, D_POOL)),
            per_b4((N_SLABS, 1, SLAB)),
            per_b4((N_SLABS, 1, SLAB)),
            _const_spec((1, d)),
            _const_spec((d, d)),
            _const_spec((len(POOL_WINDOWS), POOL_GROUP, POOL_GROUP)),
            _const_spec((1, D_POOL)),
            _const_spec((D_SSM, 2 * N_STATE)),
            _const_spec((N_STATE, D_SSM)),
            _const_spec((N_STATE, D_SSM)),
            _const_spec((N_SLABS, nlev, SLAB)),
            _const_spec((N_SLABS, nlev, SLAB)),
            _const_spec((1, D_SSM)),
            _const_spec((D_SSM, D_SSM)),
            _const_spec((1, D_SSM)),
            _const_spec((d, d)),
        ],
        out_specs=[
            pl.BlockSpec((1, tt, d), lambda i, j: (i, j, 0)),
            per_b3((HIST_ROWS, D_POOL)),
            per_b4((N_SLABS, 1, SLAB)),
            per_b4((N_SLABS, 1, SLAB)),
        ],
        out_shape=[
            jax.ShapeDtypeStruct((b, t, d), F32),
            jax.ShapeDtypeStruct((b, HIST_ROWS, D_POOL), F32),
            jax.ShapeDtypeStruct((b, N_SLABS, 1, SLAB), F32),
            jax.ShapeDtypeStruct((b, N_SLABS, 1, SLAB), F32),
        ],
        scratch_shapes=[
            pltpu.VMEM((HIST_ROWS, D_POOL), F32),
            pltpu.VMEM((N_SLABS, 1, SLAB), F32),
            pltpu.VMEM((N_SLABS, 1, SLAB), F32),
            pltpu.VMEM((N_SLABS, tt, SLAB), F32),
            pltpu.VMEM((N_SLABS, tt, SLAB), F32),
        ],
        compiler_params=pltpu.CompilerParams(
            dimension_semantics=("arbitrary", "arbitrary"), vmem_limit_bytes=VMEM_LIMIT),
    )(x, hist, h0re, h0im, prm['g_mix'], prm['w_in'], prm['w_pool'], prm['pool_scale'],
      prm['bmat'], prm['c_re'], prm['c_im'], prm['pw_re'], prm['pw_im'], prm['d_skip'],
      prm['w_glu'], prm['b_glu'], prm['w_out'])


def _memkv_kernel(m_ref, g_ref, wk_ref, wv_ref, k_ref, v_ref):
    m = _rms(m_ref[0], g_ref[...]).astype(BF16)
    k_ref[0] = _dot(m, wk_ref[...])
    v_ref[0] = _dot(m, wv_ref[...])


def _memkv(mem, g, wk, wv):
    b, n, d = mem.shape
    blk = pl.BlockSpec((1, n, d), lambda i: (i, 0, 0))
    return pl.pallas_call(
        _memkv_kernel,
        grid=(b,),
        in_specs=[blk, _const_spec((1, d)), _const_spec((d, d)), _const_spec((d, d))],
        out_specs=[blk, blk],
        out_shape=[jax.ShapeDtypeStruct((b, n, d), F32)] * 2,
        compiler_params=pltpu.CompilerParams(vmem_limit_bytes=VMEM_LIMIT),
    )(mem, g, wk, wv)


def _attn_kernel(h_ref, k_ref, v_ref, gx_ref, wq_ref, wo_ref, gf_ref, wr_ref, br_ref,
                 h2_ref, xn_ref, e_ref, gate_ref, *, tt):
    h = h_ref[0]
    hn = _rms(h, gx_ref[...])
    q = _dot(hn.astype(BF16), wq_ref[...])
    outs = []
    for hd in range(N_XHEADS):
        cols = slice(hd * XHEAD_DIM, (hd + 1) * XHEAD_DIM)
        s = lax.dot_general(q[:, cols].astype(BF16), k_ref[0, :, cols], (((1,), (1,)), ((), ())),
                            preferred_element_type=F32) * (XHEAD_DIM ** -0.5)
        p = jnp.exp(s - jnp.max(s, axis=-1, keepdims=True))
        p = p / jnp.sum(p, axis=-1, keepdims=True)
        outs.append(_dot(p.astype(BF16), v_ref[0, :, cols]))
    o = jnp.concatenate(outs, axis=1)
    h2 = h + _dot(o.astype(BF16), wo_ref[...])
    h2_ref[0] = h2

    xn = _rms(h2, gf_ref[...])
    for s in range(ROW_TILE):
        xn_ref[pl.ds(s, tt, stride=ROW_TILE), :] = xn[:, s * LANES:(s + 1) * LANES]
    logits = _dot(xn.astype(BF16), wr_ref[...]) + br_ref[...]
    lane = lax.broadcasted_iota(jnp.int32, logits.shape, 1)
    e_out = jnp.zeros(logits.shape, jnp.int32)
    top = []
    for k in range(TOP_K):
        m = jnp.max(logits, axis=-1, keepdims=True)
        idx = jnp.min(jnp.where(logits == m, lane, LANES), axis=-1, keepdims=True)
        e_out = jnp.where(lane == k, idx, e_out)
        top.append(m)
        logits = jnp.where(lane == idx, -jnp.inf, logits)
    ex = [jnp.exp(m - top[0]) for m in top]
    tot = ex[0] + ex[1] + ex[2] + ex[3]
    g_out = jnp.zeros(logits.shape, F32)
    for k in range(TOP_K):
        g_out = jnp.where(lane == k, ex[k] / tot, g_out)
    e_ref[...] = e_out
    gate_ref[...] = g_out


def _attn(h, k, v, prm, *, tt):
    b, t, d = h.shape
    n = b * t
    kern = functools.partial(_attn_kernel, tt=tt)
    nt = t // tt
    kv = pl.BlockSpec((1, N_MEM, d), lambda i, j: (i, 0, 0))
    tok = lambda width: pl.BlockSpec((tt, width), lambda i, j: (i * nt + j, 0))
    return pl.pallas_call(
        kern,
        grid=(b, nt),
        in_specs=[
            pl.BlockSpec((1, tt, d), lambda i, j: (i, j, 0)), kv, kv,
            _const_spec((1, d)), _const_spec((d, d)), _const_spec((d, d)),
            _const_spec((1, d)), _const_spec((d, LANES)), _const_spec((1, LANES)),
        ],
        out_specs=[
            pl.BlockSpec((1, tt, d), lambda i, j: (i, j, 0)),
            pl.BlockSpec((tt * ROW_TILE, LANES), lambda i, j: (i * nt + j, 0)),
            tok(LANES), tok(LANES),
        ],
        out_shape=[
            jax.ShapeDtypeStruct((b, t, d), F32),
            jax.ShapeDtypeStruct((n * ROW_TILE, LANES), F32),
            jax.ShapeDtypeStruct((n, LANES), jnp.int32),
            jax.ShapeDtypeStruct((n, LANES), F32),
        ],
        compiler_params=pltpu.CompilerParams(
            dimension_semantics=("arbitrary", "arbitrary"), vmem_limit_bytes=VMEM_LIMIT),
    )(h, k, v, prm['g_xattn'], prm['w_q'], prm['w_o'], prm['g_ffn'], prm['w_router'], prm['b_router'])


RANK_TILE = 512


def _rank_kernel(e_ref, rank_ref, cnt_ref, carry_scr):
    i = pl.program_id(0)

    @pl.when(i == 0)
    def _():
        carry_scr[...] = jnp.zeros_like(carry_scr)

    e = e_ref[...]
    lane = lax.broadcasted_iota(jnp.int32, e.shape, 1)
    onehot = [lane == e[:, k:k + 1] for k in range(TOP_K)]
    hits = jnp.zeros(e.shape, F32)
    for oh in onehot:
        hits = hits + oh.astype(F32)
    r = lax.broadcasted_iota(jnp.int32, (RANK_TILE, RANK_TILE), 0)
    c = lax.broadcasted_iota(jnp.int32, (RANK_TILE, RANK_TILE), 1)
    before = (c < r).astype(BF16)
    base = _dot(before, hits.astype(BF16)) + carry_scr[...]
    rank = jnp.zeros(e.shape, jnp.int32)
    for k, oh in enumerate(onehot):
        rk = jnp.sum(jnp.where(oh, base, 0.0), axis=-1, keepdims=True).astype(jnp.int32)
        rank = jnp.where(lane == k, rk, rank)
    rank_ref[...] = rank
    carry_scr[...] = carry_scr[...] + jnp.sum(hits, axis=0, keepdims=True)
    cnt_ref[...] = carry_scr[...]


def _rank(e_all):
    n = e_all.shape[0]
    blk = pl.BlockSpec((RANK_TILE, LANES), lambda i: (i, 0))
    return pl.pallas_call(
        _rank_kernel,
        grid=(n // RANK_TILE,),
        in_specs=[blk],
        out_specs=[blk, _const_spec((1, LANES))],
        out_shape=[jax.ShapeDtypeStruct((n, LANES), jnp.int32), jax.ShapeDtypeStruct((1, LANES), F32)],
        scratch_shapes=[pltpu.VMEM((1, LANES), F32)],
        compiler_params=pltpu.CompilerParams(dimension_semantics=("arbitrary",)),
    )(e_all)


BLOCK_ROWS = MOE_BLOCK * ROW_TILE


def _row_copy_wait(src_ref, dst_ref, sem, rows):
    pltpu.make_async_copy(src_ref.at[pl.ds(0, rows * ROW_TILE)], dst_ref.at[pl.ds(0, rows * ROW_TILE)], sem).wait()


def _dispatch_kernel(last_blk_ref, nvalid_ref, dest_p_ref, dest_s_ref, xp_ref, xs_ref, xpad_ref,
                     zero_buf, sem, zero_sem, *, tile_p, tile_s, n_blocks):
    i = pl.program_id(0)
    last = pl.num_programs(0) - 1

    def zero_copy(blk):
        dst = xpad_ref.at[pl.ds(pl.multiple_of(blk * BLOCK_ROWS, BLOCK_ROWS), BLOCK_ROWS)]
        return pltpu.make_async_copy(zero_buf, dst, zero_sem)

    @pl.when(i == 0)
    def _():
        zero_buf[...] = jnp.zeros_like(zero_buf)

        def on_pad_blocks(fn):
            for e in range(N_EXPERTS):
                if e == 0:
                    fn(last_blk_ref[0])
                else:
                    pl.when(last_blk_ref[e] != last_blk_ref[e - 1])(functools.partial(fn, last_blk_ref[e]))
            lax.fori_loop(nvalid_ref[0], n_blocks, lambda blk, c: (fn(blk), c)[1], 0)

        on_pad_blocks(lambda blk: zero_copy(blk).start())
        on_pad_blocks(lambda blk: zero_copy(blk).wait())

    def push(dest_ref, x_ref, tile):
        def body(t, carry):
            src = x_ref.at[pl.ds(pl.multiple_of(t * ROW_TILE, ROW_TILE), ROW_TILE)]
            for k in range(TOP_K):
                row = dest_ref[0, 0, t * TOP_K + k]
                dst = xpad_ref.at[pl.ds(pl.multiple_of(row * ROW_TILE, ROW_TILE), ROW_TILE)]
                pltpu.make_async_copy(src, dst, sem).start(priority=k % N_DMA_PRIORITIES)
            return carry

        lax.fori_loop(0, tile, body, 0)
        for _ in range(TOP_K):
            _row_copy_wait(x_ref, xpad_ref, sem, tile)

    pl.when(i < last)(lambda: push(dest_p_ref, xp_ref, tile_p))
    pl.when(i == last)(lambda: push(dest_s_ref, xs_ref, tile_s))


def _dispatch(last_blk, nvalid, dest_p, dest_s, xn_p, xn_s, n_blocks, *, tile_p):
    n_p = xn_p.shape[0] // ROW_TILE
    tile_s = xn_s.shape[0] // ROW_TILE
    steps_p = n_p // tile_p
    kern = functools.partial(_dispatch_kernel, tile_p=tile_p, tile_s=tile_s, n_blocks=n_blocks)
    return pl.pallas_call(
        kern,
        grid_spec=pltpu.PrefetchScalarGridSpec(
            num_scalar_prefetch=2,
            grid=(steps_p + 1,),
            in_specs=[
                pl.BlockSpec((1, 1, tile_p * TOP_K), lambda i, lb, nv: (jnp.minimum(i, steps_p - 1), 0, 0),
                             memory_space=pltpu.SMEM),
                pl.BlockSpec((1, 1, tile_s * TOP_K), lambda i, lb, nv: (0, 0, 0), memory_space=pltpu.SMEM),
                pl.BlockSpec((tile_p * ROW_TILE, LANES), lambda i, lb, nv: (jnp.minimum(i, steps_p - 1), 0)),
                pl.BlockSpec((tile_s * ROW_TILE, LANES), lambda i, lb, nv: (0, 0)),
            ],
            out_specs=pl.BlockSpec(memory_space=pl.ANY),
            scratch_shapes=[pltpu.VMEM((BLOCK_ROWS, LANES), F32), pltpu.SemaphoreType.DMA, pltpu.SemaphoreType.DMA],
        ),
        out_shape=jax.ShapeDtypeStruct((n_blocks * BLOCK_ROWS, LANES), F32),
        compiler_params=pltpu.CompilerParams(dimension_semantics=("arbitrary",)),
    )(last_blk, nvalid, dest_p.reshape(steps_p, 1, tile_p * TOP_K), dest_s.reshape(1, 1, tile_s * TOP_K), xn_p, xn_s)


def _expert_kernel(blk_e_ref, nvalid_ref, x_ref, wg_ref, bg_ref, wu_ref, bu_ref, wd_ref, bd_ref, y_ref,
                   wg_bf, wu_bf, wd_bf):
    b = pl.program_id(0)

    @pl.when(jnp.logical_or(b == 0, blk_e_ref[b] != blk_e_ref[jnp.maximum(b - 1, 0)]))
    def _():
        wg_bf[...] = wg_ref[0].astype(BF16)
        wu_bf[...] = wu_ref[0].astype(BF16)
        wd_bf[...] = wd_ref[0].astype(BF16)

    @pl.when(b < nvalid_ref[0])
    def _():
        x = jnp.concatenate([x_ref[pl.ds(s, MOE_BLOCK, stride=ROW_TILE), :] for s in range(ROW_TILE)], axis=1)
        x = x.astype(BF16)
        g = _dot(x, wg_bf[...]) + bg_ref[0]
        u = _dot(x, wu_bf[...]) + bu_ref[0]
        g = jnp.minimum(g, SWIGLU_LIMIT)
        u = jnp.clip(u, -SWIGLU_LIMIT, SWIGLU_LIMIT)
        hdn = g * jax.nn.sigmoid(SWIGLU_ALPHA * g) * (u + 1.0)
        y = _dot(hdn.astype(BF16), wd_bf[...]) + bd_ref[0]
        for s in range(ROW_TILE):
            y_ref[pl.ds(s, MOE_BLOCK, stride=ROW_TILE), :] = y[:, s * LANES:(s + 1) * LANES]

    @pl.when(b >= nvalid_ref[0])
    def _():
        y_ref[...] = jnp.zeros_like(y_ref)


def _experts(blk_e, nvalid, xpad, w_gate, b_gate, w_up, b_up, w_down, b_down, n_blocks):
    d = D_MODEL
    wspec = pl.BlockSpec((1, d, d), lambda b, be, nv: (be[b], 0, 0))
    bspec = pl.BlockSpec((1, 1, d), lambda b, be, nv: (be[b], 0, 0))
    xspec = pl.BlockSpec((BLOCK_ROWS, LANES), lambda b, be, nv: (jnp.minimum(b, nv[0] - 1), 0))
    return pl.pallas_call(
        _expert_kernel,
        grid_spec=pltpu.PrefetchScalarGridSpec(
            num_scalar_prefetch=2,
            grid=(n_blocks,),
            in_specs=[xspec, wspec, bspec, wspec, bspec, wspec, bspec],
            out_specs=pl.BlockSpec((BLOCK_ROWS, LANES), lambda b, be, nv: (b, 0)),
            scratch_shapes=[pltpu.VMEM((d, d), BF16)] * 3,
        ),
        out_shape=jax.ShapeDtypeStruct((n_blocks * BLOCK_ROWS, LANES), F32),
        compiler_params=pltpu.CompilerParams(dimension_semantics=("arbitrary",), vmem_limit_bytes=VMEM_LIMIT),
    )(blk_e, nvalid, xpad, w_gate, b_gate.reshape(N_EXPERTS, 1, d), w_up, b_up.reshape(N_EXPERTS, 1, d),
      w_down, b_down.reshape(N_EXPERTS, 1, d))


def _combine_kernel(dest_ref, h_ref, gate_ref, gfin_ref, ypad_ref, y_ref, buf, sem, *, tile):
    def body(t, carry):
        for k in range(TOP_K):
            row = dest_ref[0, 0, t * TOP_K + k]
            src = ypad_ref.at[pl.ds(pl.multiple_of(row * ROW_TILE, ROW_TILE), ROW_TILE)]
            dst = buf.at[pl.ds(pl.multiple_of((k * tile + t) * ROW_TILE, ROW_TILE), ROW_TILE)]
            pltpu.make_async_copy(src, dst, sem).start(priority=k % N_DMA_PRIORITIES)
        return carry

    lax.fori_loop(0, tile, body, 0)
    for _ in range(TOP_K):
        _row_copy_wait(ypad_ref, buf, sem, tile)

    gates = gate_ref[...]
    h = h_ref[...]
    cols = []
    for s in range(ROW_TILE):
        acc = h[:, s * LANES:(s + 1) * LANES]
        for k in range(TOP_K):
            acc = acc + gates[:, k:k + 1] * buf[pl.ds(k * tile * ROW_TILE + s, tile, stride=ROW_TILE), :]
        cols.append(acc)
    y_ref[...] = _rms(jnp.concatenate(cols, axis=1), gfin_ref[...])


def _combine(dest, h2, gates, g_final, ypad, *, tile):
    n, d = h2.shape
    kern = functools.partial(_combine_kernel, tile=tile)
    return pl.pallas_call(
        kern,
        grid=(n // tile,),
        in_specs=[
            pl.BlockSpec((1, 1, tile * TOP_K), lambda i: (i, 0, 0), memory_space=pltpu.SMEM),
            pl.BlockSpec((tile, d), lambda i: (i, 0)),
            pl.BlockSpec((tile, LANES), lambda i: (i, 0)),
            _const_spec((1, d)),
            pl.BlockSpec(memory_space=pl.ANY),
        ],
        out_specs=pl.BlockSpec((tile, d), lambda i: (i, 0)),
        out_shape=jax.ShapeDtypeStruct((n, d), F32),
        scratch_shapes=[pltpu.VMEM((TOP_K * tile * ROW_TILE, LANES), F32), pltpu.SemaphoreType.DMA],
        compiler_params=pltpu.CompilerParams(dimension_semantics=("arbitrary",)),
    )(dest.reshape(n // tile, 1, tile * TOP_K), h2, gates, g_final, ypad)


def _ssm_params(a_re, a_im, log_dt, b_re, b_im, c_re, c_im, nlev):
    dt = jnp.exp(log_dt)[:, None]
    mag = jnp.exp(a_re * dt)
    lb_re = mag * jnp.cos(a_im * dt)
    lb_im = mag * jnp.sin(a_im * dt)
    den = a_re * a_re + a_im * a_im
    q_re = ((lb_re - 1.0) * a_re + lb_im * a_im) / den
    q_im = (lb_im * a_re - (lb_re - 1.0) * a_im) / den
    bb_re = q_re[:, :, None] * b_re - q_im[:, :, None] * b_im
    bb_im = q_re[:, :, None] * b_im + q_im[:, :, None] * b_re
    eye = jnp.eye(N_SSM_GROUPS, dtype=F32)
    blockdiag = lambda m: jnp.einsum('gab,gh->gahb', m, eye).reshape(
        N_SSM_GROUPS * m.shape[1], N_SSM_GROUPS * m.shape[2])
    bmat = jnp.concatenate([blockdiag(bb_re.transpose(0, 2, 1)), blockdiag(bb_im.transpose(0, 2, 1))], axis=1)
    cmat_re = blockdiag(c_re.transpose(0, 2, 1))
    cmat_im = blockdiag(-c_im.transpose(0, 2, 1))
    pw_re, pw_im = [lb_re.reshape(-1)], [lb_im.reshape(-1)]
    for _ in range(nlev - 1):
        r, i = pw_re[-1], pw_im[-1]
        pw_re.append(r * r - i * i)
        pw_im.append(2.0 * r * i)
    slabbed = lambda rows: jnp.stack(rows).reshape(nlev, N_SLABS, SLAB).transpose(1, 0, 2)
    return bmat.astype(BF16), cmat_re.astype(BF16), cmat_im.astype(BF16), slabbed(pw_re), slabbed(pw_im)


def kernel(x_prompt, x_sample, mem_prompt, cache_mem_k, cache_mem_v, cache_pool, state_ssm_re, state_ssm_im, norm_mix, w_in, w_pool, pool_scale, ssm_a_re, ssm_a_im, ssm_log_dt, ssm_b_re, ssm_b_im, ssm_c_re, ssm_c_im, ssm_d, w_glu, b_glu, w_out, norm_xattn, norm_mem, w_q, w_k, w_v, w_o, norm_ffn, w_router, b_router, w_gate, b_gate, w_up, b_up, w_down, b_down, norm_final):
    assert x_prompt.shape[2] == D_MODEL and norm_mix.shape[0] == 1
    bp, tp, d = x_prompt.shape
    bs, ts, _ = x_sample.shape
    n_p, n_s = bp * tp, bs * ts
    row = lambda v: v.reshape(1, -1)

    sub_s = min(CHUNK, ts)
    bmat, c_re, c_im, pw_re, pw_im = _ssm_params(ssm_a_re[0], ssm_a_im[0], ssm_log_dt[0], ssm_b_re[0], ssm_b_im[0],
                                                 ssm_c_re[0], ssm_c_im[0], CHUNK.bit_length() - 1)
    mix_prm = dict(g_mix=row(norm_mix[0]), w_in=w_in[0].astype(BF16), w_pool=w_pool[0].astype(BF16),
                   pool_scale=row(pool_scale[0]), bmat=bmat, c_re=c_re, c_im=c_im, pw_re=pw_re, pw_im=pw_im,
                   d_skip=row(ssm_d[0]), w_glu=w_glu[0].astype(BF16), b_glu=row(b_glu[0]),
                   w_out=w_out[0].astype(BF16))
    nlev_s = max(1, sub_s.bit_length() - 1)
    mix_prm_s = dict(mix_prm, pw_re=pw_re[:, :nlev_s], pw_im=pw_im[:, :nlev_s])

    attn_prm = dict(g_xattn=row(norm_xattn[0]), w_q=w_q[0].astype(BF16), w_o=w_o[0].astype(BF16),
                    g_ffn=row(norm_ffn[0]),
                    w_router=jnp.pad(w_router[0], ((0, 0), (0, LANES - N_EXPERTS))).astype(BF16),
                    b_router=jnp.pad(row(b_router[0]), ((0, 0), (0, LANES - N_EXPERTS)), constant_values=-jnp.inf))
    slab_state = lambda s: s.reshape(s.shape[0], N_SLABS, 1, SLAB)

    zeros_state = jnp.zeros((bp, N_SLABS, 1, SLAB), F32)
    h1_p, pool_p, sre_p, sim_p = _mixer(x_prompt, jnp.zeros((bp, HIST_ROWS, D_POOL), F32), zeros_state, zeros_state,
                                        mix_prm, tt=256, sub=CHUNK, pos0=0)
    mk, mv = _memkv(mem_prompt, row(norm_mem[0]), w_k[0].astype(BF16), w_v[0].astype(BF16))
    h2_p, xn_p, e_p, g_p = _attn(h1_p, mk.astype(BF16), mv.astype(BF16), attn_prm, tt=256)

    hist_s = jnp.pad(cache_pool[0], ((0, 0), (HIST_ROWS - POOL_BUF, 0), (0, 0)))
    h1_s, pool_s, sre_s, sim_s = _mixer(x_sample, hist_s, slab_state(state_ssm_re[0]), slab_state(state_ssm_im[0]),
                                        mix_prm_s, tt=ts, sub=sub_s, pos0=PAST_LEN)
    ck = cache_mem_k[0].reshape(bs, N_MEM, d).astype(BF16)
    cv = cache_mem_v[0].reshape(bs, N_MEM, d).astype(BF16)
    h2_s, xn_s, e_s, g_s = _attn(h1_s, ck, cv, attn_prm, tt=ts)

    n_all = n_p + n_s
    n_rank = -(-n_all // RANK_TILE) * RANK_TILE
    e_all = jnp.concatenate([e_p, e_s, jnp.full((n_rank - n_all, LANES), -1, jnp.int32)], axis=0)
    rank, counts = _rank(e_all)
    counts = counts[0, :N_EXPERTS].astype(jnp.int32)
    padded = (counts + MOE_BLOCK - 1) // MOE_BLOCK * MOE_BLOCK
    pend = jnp.cumsum(padded)
    pstart = pend - padded
    n_blocks = -(-(n_all * TOP_K + N_EXPERTS * (MOE_BLOCK - 1)) // MOE_BLOCK)
    blk_e = jnp.minimum(jnp.sum(pend[None, :] <= (jnp.arange(n_blocks) * MOE_BLOCK)[:, None], axis=1),
                        N_EXPERTS - 1).astype(jnp.int32)
    nvalid = (pend[-1:] // MOE_BLOCK).astype(jnp.int32)
    last_blk = jnp.maximum(pend // MOE_BLOCK - 1, 0).astype(jnp.int32)
    top_e = e_all[:n_all, :TOP_K]
    start_of = jnp.sum(jnp.where(top_e[:, :, None] == jnp.arange(N_EXPERTS)[None, None, :], pstart[None, None, :], 0),
                       axis=-1)
    dest = (start_of + rank[:n_all, :TOP_K]).astype(jnp.int32).reshape(-1)
    dest_p, dest_s = dest[:n_p * TOP_K], dest[n_p * TOP_K:]

    xpad = _dispatch(last_blk, nvalid, dest_p, dest_s, xn_p, xn_s, n_blocks, tile_p=256)
    ypad = _experts(blk_e, nvalid, xpad, w_gate[0], b_gate[0], w_up[0], b_up[0], w_down[0], b_down[0], n_blocks)
    g_fin = row(norm_final)
    y_p = _combine(dest_p, h2_p.reshape(n_p, d), g_p, g_fin, ypad, tile=256)
    y_s = _combine(dest_s, h2_s.reshape(n_s, d), g_s, g_fin, ypad, tile=n_s)

    unslab = lambda s: s.reshape(1, s.shape[0], N_SSM_GROUPS, SSM_STATE)
    kv5 = lambda a: a.reshape(1, bp, N_MEM, N_XHEADS, XHEAD_DIM)
    return (y_p.reshape(bp, tp, d), y_s.reshape(bs, ts, d), kv5(mk), kv5(mv),
            pool_p[None, :, HIST_ROWS - POOL_BUF:], pool_s[None, :, HIST_ROWS - POOL_BUF:],
            unslab(sre_p), unslab(sim_p), unslab(sre_s), unslab(sim_s))
```

```python
import functools
import math

import jax
import jax.numpy as jnp
from jax import lax
from jax.experimental import pallas as pl
from jax.experimental.pallas import tpu as pltpu

D_MODEL = 1024
D_POOL = 512
D_SSM = 512
POOL_WINDOWS = (2, 4, 8, 16)
POOL_GROUP = 128
HIST_ROWS = 16
POOL_BUF = 15
SSM_GROUP = 16
N_SSM_GROUPS = 32
SSM_STATE = 64
N_STATE = N_SSM_GROUPS * SSM_STATE
CHUNK = 64
N_MEM = 256
N_XHEADS = 4
XHEAD_DIM = 256
N_EXPERTS = 32
TOP_K = 4
SWIGLU_LIMIT = 7.0
SWIGLU_ALPHA = 1.702
MOE_BLOCK = 256
EPS = 1e-6
PAST_LEN = 1024

LANES = 128
SUBLANES = 8
SLAB = 256
N_SLABS = N_STATE // SLAB
SSM_PACK = LANES // SSM_GROUP
N_PACKS = N_SSM_GROUPS // SSM_PACK
PACK_STATES = SSM_PACK * SSM_STATE
SLABS_PER_PACK = PACK_STATES // SLAB
ROW_TILE = D_MODEL // LANES
N_DMA_PRIORITIES = 2
VMEM_LIMIT = 56 * 1024 * 1024

BF16 = jnp.bfloat16
F32 = jnp.float32


def _rms(x, g):
    return x * lax.rsqrt(jnp.mean(x * x, axis=-1, keepdims=True) + EPS) * g


def _dot(a, b):
    return jnp.dot(a, b, preferred_element_type=F32)


def _scan_slab(s, tre_ref, tim_ref, hre_scr, him_scr, bure_scr, buim_scr, tt):
    t_re = tre_ref[s]
    t_im = tim_ref[s]
    row = lax.broadcasted_iota(jnp.int32, (SUBLANES, SLAB), 0)
    levels = []
    for shift in (1, 2, 4):
        bcast = lambda t: jnp.broadcast_to(t[shift - 1:shift, :], (SUBLANES, SLAB))
        levels.append((shift, row >= shift, bcast(t_re), bcast(t_im)))
    c_re = hre_scr[s]
    c_im = him_scr[s]
    for v in range(tt // SUBLANES):
        rows = pl.ds(v * SUBLANES, SUBLANES)
        re = bure_scr[s, rows, :]
        im = buim_scr[s, rows, :]
        for shift, keep, pr, pi in levels:
            sre = jnp.where(keep, pltpu.roll(re, shift, 0), 0.0)
            sim = jnp.where(keep, pltpu.roll(im, shift, 0), 0.0)
            re, im = re + (pr * sre - pi * sim), im + (pr * sim + pi * sre)
        re, im = re + (t_re * c_re - t_im * c_im), im + (t_re * c_im + t_im * c_re)
        bure_scr[s, rows, :] = re
        buim_scr[s, rows, :] = im
        c_re = re[SUBLANES - 1:SUBLANES, :]
        c_im = im[SUBLANES - 1:SUBLANES, :]
    hre_scr[s] = c_re
    him_scr[s] = c_im


def _mixer_kernel(x_ref, hist_ref, h0re_ref, h0im_ref, gmix_ref, win_ref, wpool_ref, pscale_ref,
                  bmat_ref, cre_ref, cim_ref, tre_ref, tim_ref, dskip_ref, wglu_ref, bglu_ref, wout_ref,
                  h_ref, poolnew_ref, ssmre_ref, ssmim_ref,
                  hist_scr, hre_scr, him_scr, bure_scr, buim_scr, *, tt, pos0):
    j = pl.program_id(1)

    @pl.when(j == 0)
    def _():
        hist_scr[...] = hist_ref[0]
        hre_scr[...] = h0re_ref[0]
        him_scr[...] = h0im_ref[0]

    x = x_ref[0]
    xn = _rms(x, gmix_ref[...])
    z = _dot(xn.astype(BF16), win_ref[...])
    zp = z[:, :D_POOL]
    u = z[:, D_POOL:]

    ext = jnp.concatenate([hist_scr[...], zp], axis=0)
    pos = pos0 + j * tt + lax.broadcasted_iota(jnp.int32, (tt, 1), 0)
    acc = ext
    outs = []
    for gi, w in enumerate(POOL_WINDOWS):
        lo = gi * POOL_GROUP
        acc = acc[:, POOL_GROUP * (1 if gi else 0):]
        acc = acc + pltpu.roll(acc, w // 2, 0)
        wsum = acc[HIST_ROWS:, :POOL_GROUP]
        cnt = jnp.minimum(pos + 1, w).astype(F32)
        d = wsum / cnt - zp[:, lo:lo + POOL_GROUP]
        outs.append(_dot(d.astype(BF16), wpool_ref[gi]))
    y_pool = jnp.concatenate(outs, axis=1) * pscale_ref[...]
    hist_scr[...] = ext[tt:tt + HIST_ROWS]
    poolnew_ref[0] = ext[tt:tt + HIST_ROWS]

    ub = u.astype(BF16)
    for c in range(N_PACKS):
        bu = _dot(ub[:, c * LANES:(c + 1) * LANES], bmat_ref[c])
        for i in range(SLABS_PER_PACK):
            bure_scr[c * SLABS_PER_PACK + i] = bu[:, i * SLAB:(i + 1) * SLAB]
            buim_scr[c * SLABS_PER_PACK + i] = bu[:, PACK_STATES + i * SLAB:PACK_STATES + (i + 1) * SLAB]

    def slab_body(s, carry):
        _scan_slab(s, tre_ref, tim_ref, hre_scr, him_scr, bure_scr, buim_scr, tt)
        return carry

    lax.fori_loop(0, N_SLABS, slab_body, 0)
    ssmre_ref[0] = hre_scr[...]
    ssmim_ref[0] = him_scr[...]

    ys = []
    for c in range(N_PACKS):
        slabs = range(c * SLABS_PER_PACK, (c + 1) * SLABS_PER_PACK)
        hs_re = jnp.concatenate([bure_scr[s] for s in slabs], axis=1)
        hs_im = jnp.concatenate([buim_scr[s] for s in slabs], axis=1)
        ys.append(_dot(hs_re.astype(BF16), cre_ref[c]) + _dot(hs_im.astype(BF16), cim_ref[c]))
    y = jnp.concatenate(ys, axis=1) + dskip_ref[...] * u
    g = 0.5 * y * (1.0 + jnp.tanh(math.sqrt(2.0 / math.pi) * (y + 0.044715 * (y * y * y))))
    y_ssm = g * jax.nn.sigmoid(_dot(g.astype(BF16), wglu_ref[...]) + bglu_ref[...])

    mix = jnp.concatenate([y_pool, y_ssm], axis=1)
    h_ref[0] = x + _dot(mix.astype(BF16), wout_ref[...])


def _const_spec(shape):
    return pl.BlockSpec(shape, lambda *_: (0,) * len(shape))


def _mixer(x, hist, h0re, h0im, prm, *, tt, pos0):
    b, t, d = x.shape
    kern = functools.partial(_mixer_kernel, tt=tt, pos0=pos0)
    per_b3 = lambda shp: pl.BlockSpec((1,) + shp, lambda i, j: (i, 0, 0))
    per_b4 = lambda shp: pl.BlockSpec((1,) + shp, lambda i, j: (i, 0, 0, 0))
    return pl.pallas_call(
        kern,
        grid=(b, t // tt),
        in_specs=[
            pl.BlockSpec((1, tt, d), lambda i, j: (i, j, 0)),
            per_b3((HIST_ROWS, D_POOL)),
            per_b4((N_SLABS, 1, SLAB)),
            per_b4((N_SLABS, 1, SLAB)),
            _const_spec((1, d)),
            _const_spec((d, d)),
            _const_spec((len(POOL_WINDOWS), POOL_GROUP, POOL_GROUP)),
            _const_spec((1, D_POOL)),
            _const_spec((N_PACKS, LANES, 2 * PACK_STATES)),
            _const_spec((N_PACKS, PACK_STATES, LANES)),
            _const_spec((N_PACKS, PACK_STATES, LANES)),
            _const_spec((N_SLABS, SUBLANES, SLAB)),
            _const_spec((N_SLABS, SUBLANES, SLAB)),
            _const_spec((1, D_SSM)),
            _const_spec((D_SSM, D_SSM)),
            _const_spec((1, D_SSM)),
            _const_spec((d, d)),
        ],
        out_specs=[
            pl.BlockSpec((1, tt, d), lambda i, j: (i, j, 0)),
            per_b3((HIST_ROWS, D_POOL)),
            per_b4((N_SLABS, 1, SLAB)),
            per_b4((N_SLABS, 1, SLAB)),
        ],
        out_shape=[
            jax.ShapeDtypeStruct((b, t, d), F32),
            jax.ShapeDtypeStruct((b, HIST_ROWS, D_POOL), F32),
            jax.ShapeDtypeStruct((b, N_SLABS, 1, SLAB), F32),
            jax.ShapeDtypeStruct((b, N_SLABS, 1, SLAB), F32),
        ],
        scratch_shapes=[
            pltpu.VMEM((HIST_ROWS, D_POOL), F32),
            pltpu.VMEM((N_SLABS, 1, SLAB), F32),
            pltpu.VMEM((N_SLABS, 1, SLAB), F32),
            pltpu.VMEM((N_SLABS, tt, SLAB), F32),
            pltpu.VMEM((N_SLABS, tt, SLAB), F32),
        ],
        compiler_params=pltpu.CompilerParams(
            dimension_semantics=("arbitrary", "arbitrary"), vmem_limit_bytes=VMEM_LIMIT),
    )(x, hist, h0re, h0im, prm['g_mix'], prm['w_in'], prm['w_pool'], prm['pool_scale'],
      prm['bmat'], prm['c_re'], prm['c_im'], prm['t_re'], prm['t_im'], prm['d_skip'],
      prm['w_glu'], prm['b_glu'], prm['w_out'])


def _memkv_kernel(m_ref, g_ref, wk_ref, wv_ref, k_ref, v_ref):
    m = _rms(m_ref[0], g_ref[...]).astype(BF16)
    k_ref[0] = _dot(m, wk_ref[...])
    v_ref[0] = _dot(m, wv_ref[...])


def _memkv(mem, g, wk, wv):
    b, n, d = mem.shape
    blk = pl.BlockSpec((1, n, d), lambda i: (i, 0, 0))
    return pl.pallas_call(
        _memkv_kernel,
        grid=(b,),
        in_specs=[blk, _const_spec((1, d)), _const_spec((d, d)), _const_spec((d, d))],
        out_specs=[blk, blk],
        out_shape=[jax.ShapeDtypeStruct((b, n, d), F32)] * 2,
        compiler_params=pltpu.CompilerParams(vmem_limit_bytes=VMEM_LIMIT),
    )(mem, g, wk, wv)


def _attn_kernel(h_ref, k_ref, v_ref, gx_ref, wq_ref, wo_ref, gf_ref, wr_ref, br_ref,
                 h2_ref, xn_ref, e_ref, gate_ref, *, tt):
    h = h_ref[0]
    hn = _rms(h, gx_ref[...])
    q = _dot(hn.astype(BF16), wq_ref[...])
    outs = []
    for hd in range(N_XHEADS):
        cols = slice(hd * XHEAD_DIM, (hd + 1) * XHEAD_DIM)
        s = lax.dot_general(q[:, cols].astype(BF16), k_ref[0, :, cols], (((1,), (1,)), ((), ())),
                            preferred_element_type=F32) * (XHEAD_DIM ** -0.5)
        p = jnp.exp(s - jnp.max(s, axis=-1, keepdims=True))
        p = p / jnp.sum(p, axis=-1, keepdims=True)
        outs.append(_dot(p.astype(BF16), v_ref[0, :, cols]))
    o = jnp.concatenate(outs, axis=1)
    h2 = h + _dot(o.astype(BF16), wo_ref[...])
    h2_ref[0] = h2

    xn = _rms(h2, gf_ref[...])
    for s in range(ROW_TILE):
        xn_ref[pl.ds(s, tt, stride=ROW_TILE), :] = xn[:, s * LANES:(s + 1) * LANES]
    logits = _dot(xn.astype(BF16), wr_ref[...]) + br_ref[...]
    lane = lax.broadcasted_iota(jnp.int32, logits.shape, 1)
    e_out = jnp.zeros(logits.shape, jnp.int32)
    top = []
    for k in range(TOP_K):
        m = jnp.max(logits, axis=-1, keepdims=True)
        idx = jnp.min(jnp.where(logits == m, lane, LANES), axis=-1, keepdims=True)
        e_out = jnp.where(lane == k, idx, e_out)
        top.append(m)
        logits = jnp.where(lane == idx, -jnp.inf, logits)
    ex = [jnp.exp(m - top[0]) for m in top]
    tot = ex[0] + ex[1] + ex[2] + ex[3]
    g_out = jnp.zeros(logits.shape, F32)
    for k in range(TOP_K):
        g_out = jnp.where(lane == k, ex[k] / tot, g_out)
    e_ref[...] = e_out
    gate_ref[...] = g_out


def _attn(h, k, v, prm, *, tt):
    b, t, d = h.shape
    n = b * t
    kern = functools.partial(_attn_kernel, tt=tt)
    nt = t // tt
    kv = pl.BlockSpec((1, N_MEM, d), lambda i, j: (i, 0, 0))
    tok = lambda width: pl.BlockSpec((tt, width), lambda i, j: (i * nt + j, 0))
    return pl.pallas_call(
        kern,
        grid=(b, nt),
        in_specs=[
            pl.BlockSpec((1, tt, d), lambda i, j: (i, j, 0)), kv, kv,
            _const_spec((1, d)), _const_spec((d, d)), _const_spec((d, d)),
            _const_spec((1, d)), _const_spec((d, LANES)), _const_spec((1, LANES)),
        ],
        out_specs=[
            pl.BlockSpec((1, tt, d), lambda i, j: (i, j, 0)),
            pl.BlockSpec((tt * ROW_TILE, LANES), lambda i, j: (i * nt + j, 0)),
            tok(LANES), tok(LANES),
        ],
        out_shape=[
            jax.ShapeDtypeStruct((b, t, d), F32),
            jax.ShapeDtypeStruct((n * ROW_TILE, LANES), F32),
            jax.ShapeDtypeStruct((n, LANES), jnp.int32),
            jax.ShapeDtypeStruct((n, LANES), F32),
        ],
        compiler_params=pltpu.CompilerParams(
            dimension_semantics=("arbitrary", "arbitrary"), vmem_limit_bytes=VMEM_LIMIT),
    )(h, k, v, prm['g_xattn'], prm['w_q'], prm['w_o'], prm['g_ffn'], prm['w_router'], prm['b_router'])


RANK_TILE = 512


def _rank_kernel(e_ref, rank_ref, cnt_ref, carry_scr):
    i = pl.program_id(0)

    @pl.when(i == 0)
    def _():
        carry_scr[...] = jnp.zeros_like(carry_scr)

    e = e_ref[...]
    lane = lax.broadcasted_iota(jnp.int32, e.shape, 1)
    onehot = [lane == e[:, k:k + 1] for k in range(TOP_K)]
    hits = jnp.zeros(e.shape, F32)
    for oh in onehot:
        hits = hits + oh.astype(F32)
    r = lax.broadcasted_iota(jnp.int32, (RANK_TILE, RANK_TILE), 0)
    c = lax.broadcasted_iota(jnp.int32, (RANK_TILE, RANK_TILE), 1)
    before = (c < r).astype(BF16)
    base = _dot(before, hits.astype(BF16)) + carry_scr[...]
    rank = jnp.zeros(e.shape, jnp.int32)
    for k, oh in enumerate(onehot):
        rk = jnp.sum(jnp.where(oh, base, 0.0), axis=-1, keepdims=True).astype(jnp.int32)
        rank = jnp.where(lane == k, rk, rank)
    rank_ref[...] = rank
    carry_scr[...] = carry_scr[...] + jnp.sum(hits, axis=0, keepdims=True)
    cnt_ref[...] = carry_scr[...]


def _rank(e_all):
    n = e_all.shape[0]
    blk = pl.BlockSpec((RANK_TILE, LANES), lambda i: (i, 0))
    return pl.pallas_call(
        _rank_kernel,
        grid=(n // RANK_TILE,),
        in_specs=[blk],
        out_specs=[blk, _const_spec((1, LANES))],
        out_shape=[jax.ShapeDtypeStruct((n, LANES), jnp.int32), jax.ShapeDtypeStruct((1, LANES), F32)],
        scratch_shapes=[pltpu.VMEM((1, LANES), F32)],
        compiler_params=pltpu.CompilerParams(dimension_semantics=("arbitrary",)),
    )(e_all)


BLOCK_ROWS = MOE_BLOCK * ROW_TILE


def _row_copy_wait(src_ref, dst_ref, sem, rows):
    pltpu.make_async_copy(src_ref.at[pl.ds(0, rows * ROW_TILE)], dst_ref.at[pl.ds(0, rows * ROW_TILE)], sem).wait()


def _dispatch_kernel(last_blk_ref, nvalid_ref, dest_p_ref, dest_s_ref, xp_ref, xs_ref, xpad_ref,
                     zero_buf, sem, zero_sem, *, tile_p, tile_s, n_blocks):
    i = pl.program_id(0)
    last = pl.num_programs(0) - 1

    def zero_copy(blk):
        dst = xpad_ref.at[pl.ds(pl.multiple_of(blk * BLOCK_ROWS, BLOCK_ROWS), BLOCK_ROWS)]
        return pltpu.make_async_copy(zero_buf, dst, zero_sem)

    @pl.when(i == 0)
    def _():
        zero_buf[...] = jnp.zeros_like(zero_buf)

        def on_pad_blocks(fn):
            for e in range(N_EXPERTS):
                if e == 0:
                    fn(last_blk_ref[0])
                else:
                    pl.when(last_blk_ref[e] != last_blk_ref[e - 1])(functools.partial(fn, last_blk_ref[e]))
            lax.fori_loop(nvalid_ref[0], n_blocks, lambda blk, c: (fn(blk), c)[1], 0)

        on_pad_blocks(lambda blk: zero_copy(blk).start())
        on_pad_blocks(lambda blk: zero_copy(blk).wait())

    def push(dest_ref, x_ref, tile):
        def body(t, carry):
            src = x_ref.at[pl.ds(pl.multiple_of(t * ROW_TILE, ROW_TILE), ROW_TILE)]
            for k in range(TOP_K):
                row = dest_ref[0, 0, t * TOP_K + k]
                dst = xpad_ref.at[pl.ds(pl.multiple_of(row * ROW_TILE, ROW_TILE), ROW_TILE)]
                pltpu.make_async_copy(src, dst, sem).start(priority=k % N_DMA_PRIORITIES)
            return carry

        lax.fori_loop(0, tile, body, 0)
        for _ in range(TOP_K):
            _row_copy_wait(x_ref, xpad_ref, sem, tile)

    pl.when(i < last)(lambda: push(dest_p_ref, xp_ref, tile_p))
    pl.when(i == last)(lambda: push(dest_s_ref, xs_ref, tile_s))


def _dispatch(last_blk, nvalid, dest_p, dest_s, xn_p, xn_s, n_blocks, *, tile_p):
    n_p = xn_p.shape[0] // ROW_TILE
    tile_s = xn_s.shape[0] // ROW_TILE
    steps_p = n_p // tile_p
    kern = functools.partial(_dispatch_kernel, tile_p=tile_p, tile_s=tile_s, n_blocks=n_blocks)
    return pl.pallas_call(
        kern,
        grid_spec=pltpu.PrefetchScalarGridSpec(
            num_scalar_prefetch=2,
            grid=(steps_p + 1,),
            in_specs=[
                pl.BlockSpec((1, 1, tile_p * TOP_K), lambda i, lb, nv: (jnp.minimum(i, steps_p - 1), 0, 0),
                             memory_space=pltpu.SMEM),
                pl.BlockSpec((1, 1, tile_s * TOP_K), lambda i, lb, nv: (0, 0, 0), memory_space=pltpu.SMEM),
                pl.BlockSpec((tile_p * ROW_TILE, LANES), lambda i, lb, nv: (jnp.minimum(i, steps_p - 1), 0)),
                pl.BlockSpec((tile_s * ROW_TILE, LANES), lambda i, lb, nv: (0, 0)),
            ],
            out_specs=pl.BlockSpec(memory_space=pl.ANY),
            scratch_shapes=[pltpu.VMEM((BLOCK_ROWS, LANES), F32), pltpu.SemaphoreType.DMA, pltpu.SemaphoreType.DMA],
        ),
        out_shape=jax.ShapeDtypeStruct((n_blocks * BLOCK_ROWS, LANES), F32),
        compiler_params=pltpu.CompilerParams(dimension_semantics=("arbitrary",)),
    )(last_blk, nvalid, dest_p.reshape(steps_p, 1, tile_p * TOP_K), dest_s.reshape(1, 1, tile_s * TOP_K), xn_p, xn_s)


def _expert_kernel(blk_e_ref, nvalid_ref, next_e_ref, x_ref, bg_ref, bu_ref, bd_ref, wg_hbm, wu_hbm, wd_hbm, y_ref,
                   stage, w_bf, sems):
    b = pl.program_id(0)
    valid = b < nvalid_ref[0]
    e = blk_e_ref[b]
    weights = (wg_hbm, wu_hbm, wd_hbm)

    def fetch(expert):
        return [pltpu.make_async_copy(w.at[expert], stage.at[i], sems.at[i]) for i, w in enumerate(weights)]

    @pl.when(b == 0)
    def _():
        for cp in fetch(e):
            cp.start()

    @pl.when(jnp.logical_and(valid, jnp.logical_or(b == 0, blk_e_ref[jnp.maximum(b - 1, 0)] != e)))
    def _():
        for i, cp in enumerate(fetch(e)):
            cp.wait()
            w_bf[i] = stage[i].astype(BF16)

        @pl.when(next_e_ref[b] >= 0)
        def _():
            for cp in fetch(next_e_ref[b]):
                cp.start()

    @pl.when(valid)
    def _():
        x = jnp.concatenate([x_ref[pl.ds(s, MOE_BLOCK, stride=ROW_TILE), :] for s in range(ROW_TILE)], axis=1)
        x = x.astype(BF16)
        g = _dot(x, w_bf[0]) + bg_ref[0]
        u = _dot(x, w_bf[1]) + bu_ref[0]
        g = jnp.minimum(g, SWIGLU_LIMIT)
        u = jnp.clip(u, -SWIGLU_LIMIT, SWIGLU_LIMIT)
        hdn = g * jax.nn.sigmoid(SWIGLU_ALPHA * g) * (u + 1.0)
        y = _dot(hdn.astype(BF16), w_bf[2]) + bd_ref[0]
        for s in range(ROW_TILE):
            y_ref[pl.ds(s, MOE_BLOCK, stride=ROW_TILE), :] = y[:, s * LANES:(s + 1) * LANES]

    @pl.when(b >= nvalid_ref[0])
    def _():
        y_ref[...] = jnp.zeros_like(y_ref)


def _experts(blk_e, nvalid, next_e, xpad, w_gate, b_gate, w_up, b_up, w_down, b_down, n_blocks):
    d = D_MODEL
    n_w = 3
    bspec = pl.BlockSpec((1, 1, d), lambda b, be, nv, ne: (be[b], 0, 0))
    xspec = pl.BlockSpec((BLOCK_ROWS, LANES), lambda b, be, nv, ne: (jnp.minimum(b, nv[0] - 1), 0))
    hbm = pl.BlockSpec(memory_space=pl.ANY)
    return pl.pallas_call(
        _expert_kernel,
        grid_spec=pltpu.PrefetchScalarGridSpec(
            num_scalar_prefetch=3,
            grid=(n_blocks,),
            in_specs=[xspec, bspec, bspec, bspec, hbm, hbm, hbm],
            out_specs=pl.BlockSpec((BLOCK_ROWS, LANES), lambda b, be, nv, ne: (b, 0)),
            scratch_shapes=[pltpu.VMEM((n_w, d, d), F32), pltpu.VMEM((n_w, d, d), BF16),
                            pltpu.SemaphoreType.DMA((n_w,))],
        ),
        out_shape=jax.ShapeDtypeStruct((n_blocks * BLOCK_ROWS, LANES), F32),
        compiler_params=pltpu.CompilerParams(dimension_semantics=("arbitrary",), vmem_limit_bytes=VMEM_LIMIT),
    )(blk_e, nvalid, next_e, xpad, b_gate.reshape(N_EXPERTS, 1, d), b_up.reshape(N_EXPERTS, 1, d),
      b_down.reshape(N_EXPERTS, 1, d), w_gate, w_up, w_down)


def _combine_kernel(dest_ref, h_ref, gate_ref, gfin_ref, ypad_ref, y_ref, buf, sem, *, tile):
    def body(t, carry):
        for k in range(TOP_K):
            row = dest_ref[0, 0, t * TOP_K + k]
            src = ypad_ref.at[pl.ds(pl.multiple_of(row * ROW_TILE, ROW_TILE), ROW_TILE)]
            dst = buf.at[pl.ds(pl.multiple_of((k * tile + t) * ROW_TILE, ROW_TILE), ROW_TILE)]
            pltpu.make_async_copy(src, dst, sem).start(priority=k % N_DMA_PRIORITIES)
        return carry

    lax.fori_loop(0, tile, body, 0)
    for _ in range(TOP_K):
        _row_copy_wait(ypad_ref, buf, sem, tile)

    gates = gate_ref[...]
    h = h_ref[...]
    cols = []
    for s in range(ROW_TILE):
        acc = h[:, s * LANES:(s + 1) * LANES]
        for k in range(TOP_K):
            acc = acc + gates[:, k:k + 1] * buf[pl.ds(k * tile * ROW_TILE + s, tile, stride=ROW_TILE), :]
        cols.append(acc)
    y_ref[...] = _rms(jnp.concatenate(cols, axis=1), gfin_ref[...])


def _combine(dest, h2, gates, g_final, ypad, *, tile):
    n, d = h2.shape
    kern = functools.partial(_combine_kernel, tile=tile)
    return pl.pallas_call(
        kern,
        grid=(n // tile,),
        in_specs=[
            pl.BlockSpec((1, 1, tile * TOP_K), lambda i: (i, 0, 0), memory_space=pltpu.SMEM),
            pl.BlockSpec((tile, d), lambda i: (i, 0)),
            pl.BlockSpec((tile, LANES), lambda i: (i, 0)),
            _const_spec((1, d)),
            pl.BlockSpec(memory_space=pl.ANY),
        ],
        out_specs=pl.BlockSpec((tile, d), lambda i: (i, 0)),
        out_shape=jax.ShapeDtypeStruct((n, d), F32),
        scratch_shapes=[pltpu.VMEM((TOP_K * tile * ROW_TILE, LANES), F32), pltpu.SemaphoreType.DMA],
        compiler_params=pltpu.CompilerParams(dimension_semantics=("arbitrary",)),
    )(dest.reshape(n // tile, 1, tile * TOP_K), h2, gates, g_final, ypad)


def _ssm_params(a_re, a_im, log_dt, b_re, b_im, c_re, c_im):
    dt = jnp.exp(log_dt)[:, None]
    mag = jnp.exp(a_re * dt)
    lb_re = mag * jnp.cos(a_im * dt)
    lb_im = mag * jnp.sin(a_im * dt)
    den = a_re * a_re + a_im * a_im
    q_re = ((lb_re - 1.0) * a_re + lb_im * a_im) / den
    q_im = (lb_im * a_re - (lb_re - 1.0) * a_im) / den
    bb_re = q_re[:, :, None] * b_re - q_im[:, :, None] * b_im
    bb_im = q_re[:, :, None] * b_im + q_im[:, :, None] * b_re
    eye = jnp.eye(SSM_PACK, dtype=F32)

    def pack_blockdiag(m):
        m = m.reshape(N_PACKS, SSM_PACK, m.shape[1], m.shape[2])
        return jnp.einsum('ngab,gh->ngahb', m, eye).reshape(N_PACKS, SSM_PACK * m.shape[2], SSM_PACK * m.shape[3])

    bmat = jnp.concatenate([pack_blockdiag(bb_re.transpose(0, 2, 1)), pack_blockdiag(bb_im.transpose(0, 2, 1))],
                           axis=2)
    cmat_re = pack_blockdiag(c_re.transpose(0, 2, 1))
    cmat_im = pack_blockdiag(-c_im.transpose(0, 2, 1))
    t_re, t_im = [lb_re.reshape(-1)], [lb_im.reshape(-1)]
    for _ in range(SUBLANES - 1):
        r, i = t_re[-1], t_im[-1]
        t_re.append(r * t_re[0] - i * t_im[0])
        t_im.append(r * t_im[0] + i * t_re[0])
    slabbed = lambda rows: jnp.stack(rows).reshape(SUBLANES, N_SLABS, SLAB).transpose(1, 0, 2)
    return bmat.astype(BF16), cmat_re.astype(BF16), cmat_im.astype(BF16), slabbed(t_re), slabbed(t_im)


def kernel(x_prompt, x_sample, mem_prompt, cache_mem_k, cache_mem_v, cache_pool, state_ssm_re, state_ssm_im, norm_mix, w_in, w_pool, pool_scale, ssm_a_re, ssm_a_im, ssm_log_dt, ssm_b_re, ssm_b_im, ssm_c_re, ssm_c_im, ssm_d, w_glu, b_glu, w_out, norm_xattn, norm_mem, w_q, w_k, w_v, w_o, norm_ffn, w_router, b_router, w_gate, b_gate, w_up, b_up, w_down, b_down, norm_final):
    assert x_prompt.shape[2] == D_MODEL and norm_mix.shape[0] == 1
    bp, tp, d = x_prompt.shape
    bs, ts, _ = x_sample.shape
    n_p, n_s = bp * tp, bs * ts
    row = lambda v: v.reshape(1, -1)

    bmat, c_re, c_im, t_re, t_im = _ssm_params(ssm_a_re[0], ssm_a_im[0], ssm_log_dt[0], ssm_b_re[0], ssm_b_im[0],
                                               ssm_c_re[0], ssm_c_im[0])
    mix_prm = dict(g_mix=row(norm_mix[0]), w_in=w_in[0].astype(BF16), w_pool=w_pool[0].astype(BF16),
                   pool_scale=row(pool_scale[0]), bmat=bmat, c_re=c_re, c_im=c_im, t_re=t_re, t_im=t_im,
                   d_skip=row(ssm_d[0]), w_glu=w_glu[0].astype(BF16), b_glu=row(b_glu[0]),
                   w_out=w_out[0].astype(BF16))

    attn_prm = dict(g_xattn=row(norm_xattn[0]), w_q=w_q[0].astype(BF16), w_o=w_o[0].astype(BF16),
                    g_ffn=row(norm_ffn[0]),
                    w_router=jnp.pad(w_router[0], ((0, 0), (0, LANES - N_EXPERTS))).astype(BF16),
                    b_router=jnp.pad(row(b_router[0]), ((0, 0), (0, LANES - N_EXPERTS)), constant_values=-jnp.inf))
    slab_state = lambda s: s.reshape(s.shape[0], N_SLABS, 1, SLAB)

    zeros_state = jnp.zeros((bp, N_SLABS, 1, SLAB), F32)
    h1_p, pool_p, sre_p, sim_p = _mixer(x_prompt, jnp.zeros((bp, HIST_ROWS, D_POOL), F32), zeros_state, zeros_state,
                                        mix_prm, tt=256, pos0=0)
    mk, mv = _memkv(mem_prompt, row(norm_mem[0]), w_k[0].astype(BF16), w_v[0].astype(BF16))
    h2_p, xn_p, e_p, g_p = _attn(h1_p, mk.astype(BF16), mv.astype(BF16), attn_prm, tt=256)

    hist_s = jnp.pad(cache_pool[0], ((0, 0), (HIST_ROWS - POOL_BUF, 0), (0, 0)))
    h1_s, pool_s, sre_s, sim_s = _mixer(x_sample, hist_s, slab_state(state_ssm_re[0]), slab_state(state_ssm_im[0]),
                                        mix_prm, tt=ts, pos0=PAST_LEN)
    ck = cache_mem_k[0].reshape(bs, N_MEM, d).astype(BF16)
    cv = cache_mem_v[0].reshape(bs, N_MEM, d).astype(BF16)
    h2_s, xn_s, e_s, g_s = _attn(h1_s, ck, cv, attn_prm, tt=ts)

    n_all = n_p + n_s
    n_rank = -(-n_all // RANK_TILE) * RANK_TILE
    e_all = jnp.concatenate([e_p, e_s, jnp.full((n_rank - n_all, LANES), -1, jnp.int32)], axis=0)
    rank, counts = _rank(e_all)
    counts = counts[0, :N_EXPERTS].astype(jnp.int32)
    padded = (counts + MOE_BLOCK - 1) // MOE_BLOCK * MOE_BLOCK
    pend = jnp.cumsum(padded)
    pstart = pend - padded
    n_blocks = -(-(n_all * TOP_K + N_EXPERTS * (MOE_BLOCK - 1)) // MOE_BLOCK)
    blk_e = jnp.minimum(jnp.sum(pend[None, :] <= (jnp.arange(n_blocks) * MOE_BLOCK)[:, None], axis=1),
                        N_EXPERTS - 1).astype(jnp.int32)
    nvalid = (pend[-1:] // MOE_BLOCK).astype(jnp.int32)
    last_blk = jnp.maximum(pend // MOE_BLOCK - 1, 0).astype(jnp.int32)
    after = (pend // MOE_BLOCK)[blk_e]
    next_e = jnp.where(after < nvalid[0], blk_e[jnp.minimum(after, n_blocks - 1)], -1).astype(jnp.int32)
    top_e = e_all[:n_all, :TOP_K]
    start_of = jnp.sum(jnp.where(top_e[:, :, None] == jnp.arange(N_EXPERTS)[None, None, :], pstart[None, None, :], 0),
                       axis=-1)
    dest = (start_of + rank[:n_all, :TOP_K]).astype(jnp.int32).reshape(-1)
    dest_p, dest_s = dest[:n_p * TOP_K], dest[n_p * TOP_K:]

    xpad = _dispatch(last_blk, nvalid, dest_p, dest_s, xn_p, xn_s, n_blocks, tile_p=256)
    ypad = _experts(blk_e, nvalid, next_e, xpad, w_gate[0], b_gate[0], w_up[0], b_up[0], w_down[0], b_down[0], n_blocks)
    g_fin = row(norm_final)
    y_p = _combine(dest_p, h2_p.reshape(n_p, d), g_p, g_fin, ypad, tile=256)
    y_s = _combine(dest_s, h2_s.reshape(n_s, d), g_s, g_fin, ypad, tile=n_s)

    unslab = lambda s: s.reshape(1, s.shape[0], N_SSM_GROUPS, SSM_STATE)
    kv5 = lambda a: a.reshape(1, bp, N_MEM, N_XHEADS, XHEAD_DIM)
    return (y_p.reshape(bp, tp, d), y_s.reshape(bs, ts, d), kv5(mk), kv5(mv),
            pool_p[None, :, HIST_ROWS - POOL_BUF:], pool_s[None, :, HIST_ROWS - POOL_BUF:],
            unslab(sre_p), unslab(sim_p), unslab(sre_s), unslab(sim_s))
```

```python
import functools
import math

import jax
import jax.numpy as jnp
from jax import lax
from jax.experimental import pallas as pl
from jax.experimental.pallas import tpu as pltpu

D_MODEL = 1024
D_POOL = 512
D_SSM = 512
POOL_WINDOWS = (2, 4, 8, 16)
POOL_GROUP = 128
HIST_ROWS = 16
POOL_BUF = 15
SSM_GROUP = 16
N_SSM_GROUPS = 32
SSM_STATE = 64
N_STATE = N_SSM_GROUPS * SSM_STATE
CHUNK = 64
N_MEM = 256
N_XHEADS = 4
XHEAD_DIM = 256
N_EXPERTS = 32
TOP_K = 4
SWIGLU_LIMIT = 7.0
SWIGLU_ALPHA = 1.702
MOE_BLOCK = 256
EPS = 1e-6
PAST_LEN = 1024

LANES = 128
SUBLANES = 8
SLAB = 256
N_SLABS = N_STATE // SLAB
SSM_PACK = LANES // SSM_GROUP
N_PACKS = N_SSM_GROUPS // SSM_PACK
PACK_STATES = SSM_PACK * SSM_STATE
SLABS_PER_PACK = PACK_STATES // SLAB
ROW_TILE = D_MODEL // LANES
N_DMA_PRIORITIES = 2
PUSH_UNROLL = 4
VMEM_LIMIT = 56 * 1024 * 1024

BF16 = jnp.bfloat16
F32 = jnp.float32


def _rms(x, g):
    return x * lax.rsqrt(jnp.mean(x * x, axis=-1, keepdims=True) + EPS) * g


def _dot(a, b):
    return jnp.dot(a, b, preferred_element_type=F32)


def _scan_slab(s, tre_ref, tim_ref, hre_scr, him_scr, bure_scr, buim_scr, tt):
    t_re = tre_ref[s]
    t_im = tim_ref[s]
    row = lax.broadcasted_iota(jnp.int32, (SUBLANES, SLAB), 0)
    levels = []
    for shift in (1, 2, 4):
        bcast = lambda t: jnp.broadcast_to(t[shift - 1:shift, :], (SUBLANES, SLAB))
        levels.append((shift, row >= shift, bcast(t_re), bcast(t_im)))
    c_re = hre_scr[s]
    c_im = him_scr[s]
    for v in range(tt // SUBLANES):
        rows = pl.ds(v * SUBLANES, SUBLANES)
        re = bure_scr[s, rows, :]
        im = buim_scr[s, rows, :]
        for shift, keep, pr, pi in levels:
            sre = jnp.where(keep, pltpu.roll(re, shift, 0), 0.0)
            sim = jnp.where(keep, pltpu.roll(im, shift, 0), 0.0)
            re, im = re + (pr * sre - pi * sim), im + (pr * sim + pi * sre)
        re, im = re + (t_re * c_re - t_im * c_im), im + (t_re * c_im + t_im * c_re)
        bure_scr[s, rows, :] = re
        buim_scr[s, rows, :] = im
        c_re = re[SUBLANES - 1:SUBLANES, :]
        c_im = im[SUBLANES - 1:SUBLANES, :]
    hre_scr[s] = c_re
    him_scr[s] = c_im


def _mixer_kernel(x_ref, hist_ref, h0re_ref, h0im_ref, gmix_ref, win_ref, wpool_ref, pscale_ref,
                  bmat_ref, cre_ref, cim_ref, tre_ref, tim_ref, dskip_ref, wglu_ref, bglu_ref, wout_ref,
                  h_ref, poolnew_ref, ssmre_ref, ssmim_ref,
                  hist_scr, hre_scr, him_scr, bure_scr, buim_scr, *, tt, pos0):
    j = pl.program_id(1)

    @pl.when(j == 0)
    def _():
        hist_scr[...] = hist_ref[0]
        hre_scr[...] = h0re_ref[0]
        him_scr[...] = h0im_ref[0]

    x = x_ref[0]
    xn = _rms(x, gmix_ref[...])
    z = _dot(xn.astype(BF16), win_ref[...])
    zp = z[:, :D_POOL]
    u = z[:, D_POOL:]

    ext = jnp.concatenate([hist_scr[...], zp], axis=0)
    pos = pos0 + j * tt + lax.broadcasted_iota(jnp.int32, (tt, 1), 0)
    acc = ext
    outs = []
    for gi, w in enumerate(POOL_WINDOWS):
        lo = gi * POOL_GROUP
        acc = acc[:, POOL_GROUP * (1 if gi else 0):]
        acc = acc + pltpu.roll(acc, w // 2, 0)
        wsum = acc[HIST_ROWS:, :POOL_GROUP]
        cnt = jnp.minimum(pos + 1, w).astype(F32)
        d = wsum / cnt - zp[:, lo:lo + POOL_GROUP]
        outs.append(_dot(d.astype(BF16), wpool_ref[gi]))
    y_pool = jnp.concatenate(outs, axis=1) * pscale_ref[...]
    hist_scr[...] = ext[tt:tt + HIST_ROWS]
    poolnew_ref[0] = ext[tt:tt + HIST_ROWS]

    ub = u.astype(BF16)
    for c in range(N_PACKS):
        bu = _dot(ub[:, c * LANES:(c + 1) * LANES], bmat_ref[c])
        for i in range(SLABS_PER_PACK):
            bure_scr[c * SLABS_PER_PACK + i] = bu[:, i * SLAB:(i + 1) * SLAB]
            buim_scr[c * SLABS_PER_PACK + i] = bu[:, PACK_STATES + i * SLAB:PACK_STATES + (i + 1) * SLAB]

    def slab_body(s, carry):
        _scan_slab(s, tre_ref, tim_ref, hre_scr, him_scr, bure_scr, buim_scr, tt)
        return carry

    lax.fori_loop(0, N_SLABS, slab_body, 0)
    ssmre_ref[0] = hre_scr[...]
    ssmim_ref[0] = him_scr[...]

    ys = []
    for c in range(N_PACKS):
        slabs = range(c * SLABS_PER_PACK, (c + 1) * SLABS_PER_PACK)
        hs_re = jnp.concatenate([bure_scr[s] for s in slabs], axis=1)
        hs_im = jnp.concatenate([buim_scr[s] for s in slabs], axis=1)
        ys.append(_dot(hs_re.astype(BF16), cre_ref[c]) + _dot(hs_im.astype(BF16), cim_ref[c]))
    y = jnp.concatenate(ys, axis=1) + dskip_ref[...] * u
    g = 0.5 * y * (1.0 + jnp.tanh(math.sqrt(2.0 / math.pi) * (y + 0.044715 * (y * y * y))))
    y_ssm = g * jax.nn.sigmoid(_dot(g.astype(BF16), wglu_ref[...]) + bglu_ref[...])

    mix = jnp.concatenate([y_pool, y_ssm], axis=1)
    h_ref[0] = x + _dot(mix.astype(BF16), wout_ref[...])


def _const_spec(shape):
    return pl.BlockSpec(shape, lambda *_: (0,) * len(shape))


def _mixer(x, hist, h0re, h0im, prm, *, tt, pos0):
    b, t, d = x.shape
    kern = functools.partial(_mixer_kernel, tt=tt, pos0=pos0)
    per_b3 = lambda shp: pl.BlockSpec((1,) + shp, lambda i, j: (i, 0, 0))
    per_b4 = lambda shp: pl.BlockSpec((1,) + shp, lambda i, j: (i, 0, 0, 0))
    return pl.pallas_call(
        kern,
        grid=(b, t // tt),
        in_specs=[
            pl.BlockSpec((1, tt, d), lambda i, j: (i, j, 0)),
            per_b3((HIST_ROWS, D_POOL)),
            per_b4((N_SLABS, 1, SLAB)),
            per_b4((N_SLABS, 1, SLAB)),
            _const_spec((1, d)),
            _const_spec((d, d)),
            _const_spec((len(POOL_WINDOWS), POOL_GROUP, POOL_GROUP)),
            _const_spec((1, D_POOL)),
            _const_spec((N_PACKS, LANES, 2 * PACK_STATES)),
            _const_spec((N_PACKS, PACK_STATES, LANES)),
            _const_spec((N_PACKS, PACK_STATES, LANES)),
            _const_spec((N_SLABS, SUBLANES, SLAB)),
            _const_spec((N_SLABS, SUBLANES, SLAB)),
            _const_spec((1, D_SSM)),
            _const_spec((D_SSM, D_SSM)),
            _const_spec((1, D_SSM)),
            _const_spec((d, d)),
        ],
        out_specs=[
            pl.BlockSpec((1, tt, d), lambda i, j: (i, j, 0)),
            per_b3((HIST_ROWS, D_POOL)),
            per_b4((N_SLABS, 1, SLAB)),
            per_b4((N_SLABS, 1, SLAB)),
        ],
        out_shape=[
            jax.ShapeDtypeStruct((b, t, d), F32),
            jax.ShapeDtypeStruct((b, HIST_ROWS, D_POOL), F32),
            jax.ShapeDtypeStruct((b, N_SLABS, 1, SLAB), F32),
            jax.ShapeDtypeStruct((b, N_SLABS, 1, SLAB), F32),
        ],
        scratch_shapes=[
            pltpu.VMEM((HIST_ROWS, D_POOL), F32),
            pltpu.VMEM((N_SLABS, 1, SLAB), F32),
            pltpu.VMEM((N_SLABS, 1, SLAB), F32),
            pltpu.VMEM((N_SLABS, tt, SLAB), F32),
            pltpu.VMEM((N_SLABS, tt, SLAB), F32),
        ],
        compiler_params=pltpu.CompilerParams(
            dimension_semantics=("arbitrary", "arbitrary"), vmem_limit_bytes=VMEM_LIMIT),
    )(x, hist, h0re, h0im, prm['g_mix'], prm['w_in'], prm['w_pool'], prm['pool_scale'],
      prm['bmat'], prm['c_re'], prm['c_im'], prm['t_re'], prm['t_im'], prm['d_skip'],
      prm['w_glu'], prm['b_glu'], prm['w_out'])


def _memkv_kernel(m_ref, g_ref, wk_ref, wv_ref, k_ref, v_ref):
    m = _rms(m_ref[0], g_ref[...]).astype(BF16)
    k_ref[0] = _dot(m, wk_ref[...])
    v_ref[0] = _dot(m, wv_ref[...])


def _memkv(mem, g, wk, wv):
    b, n, d = mem.shape
    blk = pl.BlockSpec((1, n, d), lambda i: (i, 0, 0))
    return pl.pallas_call(
        _memkv_kernel,
        grid=(b,),
        in_specs=[blk, _const_spec((1, d)), _const_spec((d, d)), _const_spec((d, d))],
        out_specs=[blk, blk],
        out_shape=[jax.ShapeDtypeStruct((b, n, d), F32)] * 2,
        compiler_params=pltpu.CompilerParams(vmem_limit_bytes=VMEM_LIMIT),
    )(mem, g, wk, wv)


def _attn_kernel(h_ref, kt_ref, v_ref, gx_ref, wq_ref, wo_ref, gf_ref, wr_ref, br_ref,
                 h2_ref, xn_ref, e_ref, gate_ref, *, tt):
    h = h_ref[0]
    hn = _rms(h, gx_ref[...])
    q = _dot(hn.astype(BF16), wq_ref[...])
    outs = []
    for hd in range(N_XHEADS):
        cols = slice(hd * XHEAD_DIM, (hd + 1) * XHEAD_DIM)
        s = _dot(q[:, cols].astype(BF16), kt_ref[0, cols, :]) * (XHEAD_DIM ** -0.5)
        p = jnp.exp(s - jnp.max(s, axis=-1, keepdims=True))
        p = p / jnp.sum(p, axis=-1, keepdims=True)
        outs.append(_dot(p.astype(BF16), v_ref[0, :, cols]))
    o = jnp.concatenate(outs, axis=1)
    h2 = h + _dot(o.astype(BF16), wo_ref[...])
    h2_ref[0] = h2

    xn = _rms(h2, gf_ref[...])
    for s in range(ROW_TILE):
        xn_ref[pl.ds(s, tt, stride=ROW_TILE), :] = xn[:, s * LANES:(s + 1) * LANES]
    logits = _dot(xn.astype(BF16), wr_ref[...]) + br_ref[...]
    lane = lax.broadcasted_iota(jnp.int32, logits.shape, 1)
    lane_f = lane.astype(F32)
    e_out = jnp.zeros(logits.shape, jnp.int32)
    top = []
    for k in range(TOP_K):
        m = jnp.max(logits, axis=-1, keepdims=True)
        idx = jnp.min(jnp.where(logits == m, lane_f, float(LANES)), axis=-1, keepdims=True)
        e_out = jnp.where(lane == k, idx.astype(jnp.int32), e_out)
        top.append(m)
        logits = jnp.where(lane_f == idx, -jnp.inf, logits)
    ex = [jnp.exp(m - top[0]) for m in top]
    tot = ex[0] + ex[1] + ex[2] + ex[3]
    g_out = jnp.zeros(logits.shape, F32)
    for k in range(TOP_K):
        g_out = jnp.where(lane == k, ex[k] / tot, g_out)
    e_ref[...] = e_out
    gate_ref[...] = g_out


def _attn(h, k, v, prm, *, tt):
    b, t, d = h.shape
    n = b * t
    kern = functools.partial(_attn_kernel, tt=tt)
    nt = t // tt
    kv = pl.BlockSpec((1, N_MEM, d), lambda i, j: (i, 0, 0))
    kt = pl.BlockSpec((1, d, N_MEM), lambda i, j: (i, 0, 0))
    tok = lambda width: pl.BlockSpec((tt, width), lambda i, j: (i * nt + j, 0))
    return pl.pallas_call(
        kern,
        grid=(b, nt),
        in_specs=[
            pl.BlockSpec((1, tt, d), lambda i, j: (i, j, 0)), kt, kv,
            _const_spec((1, d)), _const_spec((d, d)), _const_spec((d, d)),
            _const_spec((1, d)), _const_spec((d, LANES)), _const_spec((1, LANES)),
        ],
        out_specs=[
            pl.BlockSpec((1, tt, d), lambda i, j: (i, j, 0)),
            pl.BlockSpec((tt * ROW_TILE, LANES), lambda i, j: (i * nt + j, 0)),
            tok(LANES), tok(LANES),
        ],
        out_shape=[
            jax.ShapeDtypeStruct((b, t, d), F32),
            jax.ShapeDtypeStruct((n * ROW_TILE, LANES), F32),
            jax.ShapeDtypeStruct((n, LANES), jnp.int32),
            jax.ShapeDtypeStruct((n, LANES), F32),
        ],
        compiler_params=pltpu.CompilerParams(
            dimension_semantics=("arbitrary", "arbitrary"), vmem_limit_bytes=VMEM_LIMIT),
    )(h, k.transpose(0, 2, 1), v, prm['g_xattn'], prm['w_q'], prm['w_o'], prm['g_ffn'], prm['w_router'], prm['b_router'])


RANK_TILE = 512


def _rank_kernel(e_ref, rank_ref, cnt_ref, carry_scr):
    i = pl.program_id(0)

    @pl.when(i == 0)
    def _():
        carry_scr[...] = jnp.zeros_like(carry_scr)

    e = e_ref[...]
    lane = lax.broadcasted_iota(jnp.int32, e.shape, 1)
    onehot = [lane == e[:, k:k + 1] for k in range(TOP_K)]
    hits = jnp.zeros(e.shape, F32)
    for oh in onehot:
        hits = hits + oh.astype(F32)
    r = lax.broadcasted_iota(jnp.int32, (RANK_TILE, RANK_TILE), 0)
    c = lax.broadcasted_iota(jnp.int32, (RANK_TILE, RANK_TILE), 1)
    before = (c < r).astype(BF16)
    base = _dot(before, hits.astype(BF16)) + carry_scr[...]
    rank = jnp.zeros(e.shape, jnp.int32)
    for k, oh in enumerate(onehot):
        rk = jnp.sum(jnp.where(oh, base, 0.0), axis=-1, keepdims=True).astype(jnp.int32)
        rank = jnp.where(lane == k, rk, rank)
    rank_ref[...] = rank
    carry_scr[...] = carry_scr[...] + jnp.sum(hits, axis=0, keepdims=True)
    cnt_ref[...] = carry_scr[...]


def _rank(e_all):
    n = e_all.shape[0]
    blk = pl.BlockSpec((RANK_TILE, LANES), lambda i: (i, 0))
    return pl.pallas_call(
        _rank_kernel,
        grid=(n // RANK_TILE,),
        in_specs=[blk],
        out_specs=[blk, _const_spec((1, LANES))],
        out_shape=[jax.ShapeDtypeStruct((n, LANES), jnp.int32), jax.ShapeDtypeStruct((1, LANES), F32)],
        scratch_shapes=[pltpu.VMEM((1, LANES), F32)],
        compiler_params=pltpu.CompilerParams(dimension_semantics=("arbitrary",)),
    )(e_all)


BLOCK_ROWS = MOE_BLOCK * ROW_TILE


def _row_copy_wait(src_ref, dst_ref, sem, rows):
    pltpu.make_async_copy(src_ref.at[pl.ds(0, rows * ROW_TILE)], dst_ref.at[pl.ds(0, rows * ROW_TILE)], sem).wait()


def _dispatch_kernel(last_blk_ref, nvalid_ref, dest_p_ref, dest_s_ref, xp_ref, xs_ref, xpad_ref,
                     zero_buf, sem, zero_sem, *, tile_p, tile_s, n_blocks):
    i = pl.program_id(0)
    last = pl.num_programs(0) - 1

    def zero_copy(blk):
        dst = xpad_ref.at[pl.ds(pl.multiple_of(blk * BLOCK_ROWS, BLOCK_ROWS), BLOCK_ROWS)]
        return pltpu.make_async_copy(zero_buf, dst, zero_sem)

    @pl.when(i == 0)
    def _():
        zero_buf[...] = jnp.zeros_like(zero_buf)

        def on_pad_blocks(fn):
            for e in range(N_EXPERTS):
                if e == 0:
                    fn(last_blk_ref[0])
                else:
                    pl.when(last_blk_ref[e] != last_blk_ref[e - 1])(functools.partial(fn, last_blk_ref[e]))
            lax.fori_loop(nvalid_ref[0], n_blocks, lambda blk, c: (fn(blk), c)[1], 0)

        on_pad_blocks(lambda blk: zero_copy(blk).start())
        on_pad_blocks(lambda blk: zero_copy(blk).wait())

    def push(dest_ref, x_ref, tile):
        def body(it, carry):
            for j in range(PUSH_UNROLL):
                t = it * PUSH_UNROLL + j
                src = x_ref.at[pl.ds(pl.multiple_of(t * ROW_TILE, ROW_TILE), ROW_TILE)]
                for k in range(TOP_K):
                    row = dest_ref[0, 0, t * TOP_K + k]
                    dst = xpad_ref.at[pl.ds(pl.multiple_of(row * ROW_TILE, ROW_TILE), ROW_TILE)]
                    pltpu.make_async_copy(src, dst, sem).start(priority=k % N_DMA_PRIORITIES)
            return carry

        lax.fori_loop(0, tile // PUSH_UNROLL, body, 0)
        for _ in range(TOP_K):
            _row_copy_wait(x_ref, xpad_ref, sem, tile)

    pl.when(i < last)(lambda: push(dest_p_ref, xp_ref, tile_p))
    pl.when(i == last)(lambda: push(dest_s_ref, xs_ref, tile_s))


def _dispatch(last_blk, nvalid, dest_p, dest_s, xn_p, xn_s, n_blocks, *, tile_p):
    n_p = xn_p.shape[0] // ROW_TILE
    tile_s = xn_s.shape[0] // ROW_TILE
    steps_p = n_p // tile_p
    kern = functools.partial(_dispatch_kernel, tile_p=tile_p, tile_s=tile_s, n_blocks=n_blocks)
    return pl.pallas_call(
        kern,
        grid_spec=pltpu.PrefetchScalarGridSpec(
            num_scalar_prefetch=2,
            grid=(steps_p + 1,),
            in_specs=[
                pl.BlockSpec((1, 1, tile_p * TOP_K), lambda i, lb, nv: (jnp.minimum(i, steps_p - 1), 0, 0),
                             memory_space=pltpu.SMEM),
                pl.BlockSpec((1, 1, tile_s * TOP_K), lambda i, lb, nv: (0, 0, 0), memory_space=pltpu.SMEM),
                pl.BlockSpec((tile_p * ROW_TILE, LANES), lambda i, lb, nv: (jnp.minimum(i, steps_p - 1), 0)),
                pl.BlockSpec((tile_s * ROW_TILE, LANES), lambda i, lb, nv: (0, 0)),
            ],
            out_specs=pl.BlockSpec(memory_space=pl.ANY),
            scratch_shapes=[pltpu.VMEM((BLOCK_ROWS, LANES), F32), pltpu.SemaphoreType.DMA, pltpu.SemaphoreType.DMA],
        ),
        out_shape=jax.ShapeDtypeStruct((n_blocks * BLOCK_ROWS, LANES), F32),
        compiler_params=pltpu.CompilerParams(dimension_semantics=("arbitrary",)),
    )(last_blk, nvalid, dest_p.reshape(steps_p, 1, tile_p * TOP_K), dest_s.reshape(1, 1, tile_s * TOP_K), xn_p, xn_s)


def _expert_kernel(blk_e_ref, nvalid_ref, next_e_ref, x_ref, bg_ref, bu_ref, bd_ref, wg_hbm, wu_hbm, wd_hbm, y_ref,
                   stage, w_bf, sems):
    b = pl.program_id(0)
    valid = b < nvalid_ref[0]
    e = blk_e_ref[b]
    weights = (wg_hbm, wu_hbm, wd_hbm)

    def fetch(expert):
        return [pltpu.make_async_copy(w.at[expert], stage.at[i], sems.at[i]) for i, w in enumerate(weights)]

    @pl.when(b == 0)
    def _():
        for cp in fetch(e):
            cp.start()

    @pl.when(jnp.logical_and(valid, jnp.logical_or(b == 0, blk_e_ref[jnp.maximum(b - 1, 0)] != e)))
    def _():
        for i, cp in enumerate(fetch(e)):
            cp.wait()
            w_bf[i] = stage[i].astype(BF16)

        @pl.when(next_e_ref[b] >= 0)
        def _():
            for cp in fetch(next_e_ref[b]):
                cp.start()

    @pl.when(valid)
    def _():
        x = jnp.concatenate([x_ref[pl.ds(s, MOE_BLOCK, stride=ROW_TILE), :] for s in range(ROW_TILE)], axis=1)
        x = x.astype(BF16)
        g = _dot(x, w_bf[0]) + bg_ref[0]
        u = _dot(x, w_bf[1]) + bu_ref[0]
        g = jnp.minimum(g, SWIGLU_LIMIT)
        u = jnp.clip(u, -SWIGLU_LIMIT, SWIGLU_LIMIT)
        hdn = g * jax.nn.sigmoid(SWIGLU_ALPHA * g) * (u + 1.0)
        y = _dot(hdn.astype(BF16), w_bf[2]) + bd_ref[0]
        for s in range(ROW_TILE):
            y_ref[pl.ds(s, MOE_BLOCK, stride=ROW_TILE), :] = y[:, s * LANES:(s + 1) * LANES]

    @pl.when(b >= nvalid_ref[0])
    def _():
        y_ref[...] = jnp.zeros_like(y_ref)


def _experts(blk_e, nvalid, next_e, xpad, w_gate, b_gate, w_up, b_up, w_down, b_down, n_blocks):
    d = D_MODEL
    n_w = 3
    bspec = pl.BlockSpec((1, 1, d), lambda b, be, nv, ne: (be[b], 0, 0))
    xspec = pl.BlockSpec((BLOCK_ROWS, LANES), lambda b, be, nv, ne: (jnp.minimum(b, nv[0] - 1), 0))
    hbm = pl.BlockSpec(memory_space=pl.ANY)
    return pl.pallas_call(
        _expert_kernel,
        grid_spec=pltpu.PrefetchScalarGridSpec(
            num_scalar_prefetch=3,
            grid=(n_blocks,),
            in_specs=[xspec, bspec, bspec, bspec, hbm, hbm, hbm],
            out_specs=pl.BlockSpec((BLOCK_ROWS, LANES), lambda b, be, nv, ne: (b, 0)),
            scratch_shapes=[pltpu.VMEM((n_w, d, d), F32), pltpu.VMEM((n_w, d, d), BF16),
                            pltpu.SemaphoreType.DMA((n_w,))],
        ),
        out_shape=jax.ShapeDtypeStruct((n_blocks * BLOCK_ROWS, LANES), F32),
        compiler_params=pltpu.CompilerParams(dimension_semantics=("arbitrary",), vmem_limit_bytes=VMEM_LIMIT),
    )(blk_e, nvalid, next_e, xpad, b_gate.reshape(N_EXPERTS, 1, d), b_up.reshape(N_EXPERTS, 1, d),
      b_down.reshape(N_EXPERTS, 1, d), w_gate, w_up, w_down)


def _combine_kernel(dest_ref, dest_next_ref, h_ref, gate_ref, gfin_ref, ypad_ref, y_ref, buf_a, buf_b, sems, *, tile):
    i = pl.program_id(0)
    n_steps = pl.num_programs(0)

    def gather(dref, buf, sem, t, k):
        row = dref[0, 0, t * TOP_K + k]
        slot = (k * tile + t) * ROW_TILE
        if not isinstance(slot, int):
            slot = pl.multiple_of(slot, ROW_TILE)
        src = ypad_ref.at[pl.ds(pl.multiple_of(row * ROW_TILE, ROW_TILE), ROW_TILE)]
        pltpu.make_async_copy(src, buf.at[pl.ds(slot, ROW_TILE)], sem).start(priority=k % N_DMA_PRIORITIES)

    def drain(buf, sem):
        for _ in range(TOP_K):
            _row_copy_wait(ypad_ref, buf, sem, tile)

    @pl.when(i == 0)
    def _():
        def body(t, carry):
            for k in range(TOP_K):
                gather(dest_ref, buf_a, sems.at[0], t, k)
            return carry
        lax.fori_loop(0, tile, body, 0)

    def step(cur, cur_sem, nxt, nxt_sem):
        drain(cur, cur_sem)
        gates = gate_ref[...]
        h = h_ref[...]
        per_slab = tile // ROW_TILE
        cols = []
        for s in range(ROW_TILE):
            for t in range(s * per_slab, (s + 1) * per_slab):
                for k in range(TOP_K):
                    gather(dest_next_ref, nxt, nxt_sem, t, k)
            acc = h[:, s * LANES:(s + 1) * LANES]
            for k in range(TOP_K):
                acc = acc + gates[:, k:k + 1] * cur[pl.ds(k * tile * ROW_TILE + s, tile, stride=ROW_TILE), :]
            cols.append(acc)
        y_ref[...] = _rms(jnp.concatenate(cols, axis=1), gfin_ref[...])

    pl.when(i % 2 == 0)(lambda: step(buf_a, sems.at[0], buf_b, sems.at[1]))
    pl.when(i % 2 == 1)(lambda: step(buf_b, sems.at[1], buf_a, sems.at[0]))

    @pl.when(i == n_steps - 1)
    def _():
        pl.when(i % 2 == 0)(lambda: drain(buf_b, sems.at[1]))
        pl.when(i % 2 == 1)(lambda: drain(buf_a, sems.at[0]))


def _combine(dest, h2, gates, g_final, ypad, *, tile):
    n, d = h2.shape
    steps = n // tile
    kern = functools.partial(_combine_kernel, tile=tile)
    dest3 = dest.reshape(steps, 1, tile * TOP_K)
    buf = pltpu.VMEM((TOP_K * tile * ROW_TILE, LANES), F32)
    return pl.pallas_call(
        kern,
        grid=(steps,),
        in_specs=[
            pl.BlockSpec((1, 1, tile * TOP_K), lambda i: (i, 0, 0), memory_space=pltpu.SMEM),
            pl.BlockSpec((1, 1, tile * TOP_K), lambda i: (jnp.minimum(i + 1, steps - 1), 0, 0),
                         memory_space=pltpu.SMEM),
            pl.BlockSpec((tile, d), lambda i: (i, 0)),
            pl.BlockSpec((tile, LANES), lambda i: (i, 0)),
            _const_spec((1, d)),
            pl.BlockSpec(memory_space=pl.ANY),
        ],
        out_specs=pl.BlockSpec((tile, d), lambda i: (i, 0)),
        out_shape=jax.ShapeDtypeStruct((n, d), F32),
        scratch_shapes=[buf, buf, pltpu.SemaphoreType.DMA((2,))],
        compiler_params=pltpu.CompilerParams(dimension_semantics=("arbitrary",), vmem_limit_bytes=VMEM_LIMIT),
    )(dest3, dest3, h2, gates, g_final, ypad)


def _ssm_params(a_re, a_im, log_dt, b_re, b_im, c_re, c_im):
    dt = jnp.exp(log_dt)[:, None]
    mag = jnp.exp(a_re * dt)
    lb_re = mag * jnp.cos(a_im * dt)
    lb_im = mag * jnp.sin(a_im * dt)
    den = a_re * a_re + a_im * a_im
    q_re = ((lb_re - 1.0) * a_re + lb_im * a_im) / den
    q_im = (lb_im * a_re - (lb_re - 1.0) * a_im) / den
    bb_re = q_re[:, :, None] * b_re - q_im[:, :, None] * b_im
    bb_im = q_re[:, :, None] * b_im + q_im[:, :, None] * b_re
    eye = jnp.eye(SSM_PACK, dtype=F32)

    def pack_blockdiag(m):
        m = m.reshape(N_PACKS, SSM_PACK, m.shape[1], m.shape[2])
        return jnp.einsum('ngab,gh->ngahb', m, eye).reshape(N_PACKS, SSM_PACK * m.shape[2], SSM_PACK * m.shape[3])

    bmat = jnp.concatenate([pack_blockdiag(bb_re.transpose(0, 2, 1)), pack_blockdiag(bb_im.transpose(0, 2, 1))],
                           axis=2)
    cmat_re = pack_blockdiag(c_re.transpose(0, 2, 1))
    cmat_im = pack_blockdiag(-c_im.transpose(0, 2, 1))
    t_re, t_im = [lb_re.reshape(-1)], [lb_im.reshape(-1)]
    for _ in range(SUBLANES - 1):
        r, i = t_re[-1], t_im[-1]
        t_re.append(r * t_re[0] - i * t_im[0])
        t_im.append(r * t_im[0] + i * t_re[0])
    slabbed = lambda rows: jnp.stack(rows).reshape(SUBLANES, N_SLABS, SLAB).transpose(1, 0, 2)
    return bmat.astype(BF16), cmat_re.astype(BF16), cmat_im.astype(BF16), slabbed(t_re), slabbed(t_im)


def kernel(x_prompt, x_sample, mem_prompt, cache_mem_k, cache_mem_v, cache_pool, state_ssm_re, state_ssm_im, norm_mix, w_in, w_pool, pool_scale, ssm_a_re, ssm_a_im, ssm_log_dt, ssm_b_re, ssm_b_im, ssm_c_re, ssm_c_im, ssm_d, w_glu, b_glu, w_out, norm_xattn, norm_mem, w_q, w_k, w_v, w_o, norm_ffn, w_router, b_router, w_gate, b_gate, w_up, b_up, w_down, b_down, norm_final):
    assert x_prompt.shape[2] == D_MODEL and norm_mix.shape[0] == 1
    bp, tp, d = x_prompt.shape
    bs, ts, _ = x_sample.shape
    n_p, n_s = bp * tp, bs * ts
    row = lambda v: v.reshape(1, -1)

    bmat, c_re, c_im, t_re, t_im = _ssm_params(ssm_a_re[0], ssm_a_im[0], ssm_log_dt[0], ssm_b_re[0], ssm_b_im[0],
                                               ssm_c_re[0], ssm_c_im[0])
    mix_prm = dict(g_mix=row(norm_mix[0]), w_in=w_in[0].astype(BF16), w_pool=w_pool[0].astype(BF16),
                   pool_scale=row(pool_scale[0]), bmat=bmat, c_re=c_re, c_im=c_im, t_re=t_re, t_im=t_im,
                   d_skip=row(ssm_d[0]), w_glu=w_glu[0].astype(BF16), b_glu=row(b_glu[0]),
                   w_out=w_out[0].astype(BF16))

    attn_prm = dict(g_xattn=row(norm_xattn[0]), w_q=w_q[0].astype(BF16), w_o=w_o[0].astype(BF16),
                    g_ffn=row(norm_ffn[0]),
                    w_router=jnp.pad(w_router[0], ((0, 0), (0, LANES - N_EXPERTS))).astype(BF16),
                    b_router=jnp.pad(row(b_router[0]), ((0, 0), (0, LANES - N_EXPERTS)), constant_values=-jnp.inf))
    slab_state = lambda s: s.reshape(s.shape[0], N_SLABS, 1, SLAB)

    zeros_state = jnp.zeros((bp, N_SLABS, 1, SLAB), F32)
    h1_p, pool_p, sre_p, sim_p = _mixer(x_prompt, jnp.zeros((bp, HIST_ROWS, D_POOL), F32), zeros_state, zeros_state,
                                        mix_prm, tt=256, pos0=0)
    mk, mv = _memkv(mem_prompt, row(norm_mem[0]), w_k[0].astype(BF16), w_v[0].astype(BF16))
    h2_p, xn_p, e_p, g_p = _attn(h1_p, mk.astype(BF16), mv.astype(BF16), attn_prm, tt=256)

    hist_s = jnp.pad(cache_pool[0], ((0, 0), (HIST_ROWS - POOL_BUF, 0), (0, 0)))
    h1_s, pool_s, sre_s, sim_s = _mixer(x_sample, hist_s, slab_state(state_ssm_re[0]), slab_state(state_ssm_im[0]),
                                        mix_prm, tt=ts, pos0=PAST_LEN)
    ck = cache_mem_k[0].reshape(bs, N_MEM, d).astype(BF16)
    cv = cache_mem_v[0].reshape(bs, N_MEM, d).astype(BF16)
    h2_s, xn_s, e_s, g_s = _attn(h1_s, ck, cv, attn_prm, tt=ts)

    n_all = n_p + n_s
    n_rank = -(-n_all // RANK_TILE) * RANK_TILE
    e_all = jnp.concatenate([e_p, e_s, jnp.full((n_rank - n_all, LANES), -1, jnp.int32)], axis=0)
    rank, counts = _rank(e_all)
    counts = counts[0, :N_EXPERTS].astype(jnp.int32)
    padded = (counts + MOE_BLOCK - 1) // MOE_BLOCK * MOE_BLOCK
    pend = jnp.cumsum(padded)
    pstart = pend - padded
    n_blocks = -(-(n_all * TOP_K + N_EXPERTS * (MOE_BLOCK - 1)) // MOE_BLOCK)
    blk_e = jnp.minimum(jnp.sum(pend[None, :] <= (jnp.arange(n_blocks) * MOE_BLOCK)[:, None], axis=1),
                        N_EXPERTS - 1).astype(jnp.int32)
    nvalid = (pend[-1:] // MOE_BLOCK).astype(jnp.int32)
    last_blk = jnp.maximum(pend // MOE_BLOCK - 1, 0).astype(jnp.int32)
    after = (pend // MOE_BLOCK)[blk_e]
    next_e = jnp.where(after < nvalid[0], blk_e[jnp.minimum(after, n_blocks - 1)], -1).astype(jnp.int32)
    top_e = e_all[:n_all, :TOP_K]
    start_of = jnp.sum(jnp.where(top_e[:, :, None] == jnp.arange(N_EXPERTS)[None, None, :], pstart[None, None, :], 0),
                       axis=-1)
    dest = (start_of + rank[:n_all, :TOP_K]).astype(jnp.int32).reshape(-1)
    dest_p, dest_s = dest[:n_p * TOP_K], dest[n_p * TOP_K:]

    xpad = _dispatch(last_blk, nvalid, dest_p, dest_s, xn_p, xn_s, n_blocks, tile_p=256)
    ypad = _experts(blk_e, nvalid, next_e, xpad, w_gate[0], b_gate[0], w_up[0], b_up[0], w_down[0], b_down[0], n_blocks)
    g_fin = row(norm_final)
    y_p = _combine(dest_p, h2_p.reshape(n_p, d), g_p, g_fin, ypad, tile=256)
    y_s = _combine(dest_s, h2_s.reshape(n_s, d), g_s, g_fin, ypad, tile=n_s)

    unslab = lambda s: s.reshape(1, s.shape[0], N_SSM_GROUPS, SSM_STATE)
    kv5 = lambda a: a.reshape(1, bp, N_MEM, N_XHEADS, XHEAD_DIM)
    return (y_p.reshape(bp, tp, d), y_s.reshape(bs, ts, d), kv5(mk), kv5(mv),
            pool_p[None, :, HIST_ROWS - POOL_BUF:], pool_s[None, :, HIST_ROWS - POOL_BUF:],
            unslab(sre_p), unslab(sim_p), unslab(sre_s), unslab(sim_s))
```

```python
import functools
import math

import jax
import jax.numpy as jnp
from jax import lax
from jax.experimental import pallas as pl
from jax.experimental.pallas import tpu as pltpu

D_MODEL = 1024
D_POOL = 512
D_SSM = 512
POOL_WINDOWS = (2, 4, 8, 16)
POOL_GROUP = 128
HIST_ROWS = 16
POOL_BUF = 15
SSM_GROUP = 16
N_SSM_GROUPS = 32
SSM_STATE = 64
N_STATE = N_SSM_GROUPS * SSM_STATE
CHUNK = 64
N_MEM = 256
N_XHEADS = 4
XHEAD_DIM = 256
N_EXPERTS = 32
TOP_K = 4
SWIGLU_LIMIT = 7.0
SWIGLU_ALPHA = 1.702
MOE_BLOCK = 256
EPS = 1e-6
PAST_LEN = 1024

LANES = 128
SUBLANES = 8
SLAB = 256
N_SLABS = N_STATE // SLAB
SSM_PACK = LANES // SSM_GROUP
N_PACKS = N_SSM_GROUPS // SSM_PACK
PACK_STATES = SSM_PACK * SSM_STATE
SLABS_PER_PACK = PACK_STATES // SLAB
ROW_TILE = D_MODEL // LANES
N_DMA_PRIORITIES = 2
PUSH_UNROLL = 4
VMEM_LIMIT = 56 * 1024 * 1024

BF16 = jnp.bfloat16
F32 = jnp.float32


def _rms(x, g):
    return x * lax.rsqrt(jnp.mean(x * x, axis=-1, keepdims=True) + EPS) * g


def _dot(a, b):
    return jnp.dot(a, b, preferred_element_type=F32)


def _scan_slab(s, tre_ref, tim_ref, hre_scr, him_scr, bure_scr, buim_scr, tt):
    t_re = tre_ref[s]
    t_im = tim_ref[s]
    row = lax.broadcasted_iota(jnp.int32, (SUBLANES, SLAB), 0)
    levels = []
    for shift in (1, 2, 4):
        bcast = lambda t: jnp.where(row >= shift, jnp.broadcast_to(t[shift - 1:shift, :], (SUBLANES, SLAB)), 0.0)
        levels.append((shift, bcast(t_re), bcast(t_im)))
    c_re = hre_scr[s]
    c_im = him_scr[s]
    for v in range(tt // SUBLANES):
        rows = pl.ds(v * SUBLANES, SUBLANES)
        re = bure_scr[s, rows, :]
        im = buim_scr[s, rows, :]
        for shift, pr, pi in levels:
            sre = pltpu.roll(re, shift, 0)
            sim = pltpu.roll(im, shift, 0)
            re, im = re + (pr * sre - pi * sim), im + (pr * sim + pi * sre)
        re, im = re + (t_re * c_re - t_im * c_im), im + (t_re * c_im + t_im * c_re)
        bure_scr[s, rows, :] = re
        buim_scr[s, rows, :] = im
        c_re = re[SUBLANES - 1:SUBLANES, :]
        c_im = im[SUBLANES - 1:SUBLANES, :]
    hre_scr[s] = c_re
    him_scr[s] = c_im


def _mixer_kernel(x_ref, hist_ref, h0re_ref, h0im_ref, gmix_ref, win_ref, wpool_ref, pscale_ref,
                  bmat_ref, cre_ref, cim_ref, tre_ref, tim_ref, dskip_ref, wglu_ref, bglu_ref, wout_ref,
                  h_ref, poolnew_ref, ssmre_ref, ssmim_ref,
                  hist_scr, hre_scr, him_scr, bure_scr, buim_scr, *, tt, pos0):
    j = pl.program_id(1)

    @pl.when(j == 0)
    def _():
        hist_scr[...] = hist_ref[0]
        hre_scr[...] = h0re_ref[0]
        him_scr[...] = h0im_ref[0]

    x = x_ref[0]
    xn = _rms(x, gmix_ref[...])
    z = _dot(xn.astype(BF16), win_ref[...])
    zp = z[:, :D_POOL]
    u = z[:, D_POOL:]

    ext = jnp.concatenate([hist_scr[...], zp], axis=0)
    pos = pos0 + j * tt + lax.broadcasted_iota(jnp.int32, (tt, 1), 0)
    acc = ext
    outs = []
    for gi, w in enumerate(POOL_WINDOWS):
        lo = gi * POOL_GROUP
        acc = acc[:, POOL_GROUP * (1 if gi else 0):]
        acc = acc + pltpu.roll(acc, w // 2, 0)
        wsum = acc[HIST_ROWS:, :POOL_GROUP]
        cnt = jnp.minimum(pos + 1, w).astype(F32)
        d = wsum / cnt - zp[:, lo:lo + POOL_GROUP]
        outs.append(_dot(d.astype(BF16), wpool_ref[gi]))
    y_pool = jnp.concatenate(outs, axis=1) * pscale_ref[...]
    hist_scr[...] = ext[tt:tt + HIST_ROWS]
    poolnew_ref[0] = ext[tt:tt + HIST_ROWS]

    ub = u.astype(BF16)
    ys = []
    for c in range(N_PACKS):
        slabs = range(c * SLABS_PER_PACK, (c + 1) * SLABS_PER_PACK)
        bu = _dot(ub[:, c * LANES:(c + 1) * LANES], bmat_ref[c])
        for i, s in enumerate(slabs):
            bure_scr[s] = bu[:, i * SLAB:(i + 1) * SLAB]
            buim_scr[s] = bu[:, PACK_STATES + i * SLAB:PACK_STATES + (i + 1) * SLAB]
        for s in slabs:
            _scan_slab(s, tre_ref, tim_ref, hre_scr, him_scr, bure_scr, buim_scr, tt)
        hs_re = jnp.concatenate([bure_scr[s] for s in slabs], axis=1)
        hs_im = jnp.concatenate([buim_scr[s] for s in slabs], axis=1)
        ys.append(_dot(hs_re.astype(BF16), cre_ref[c]) + _dot(hs_im.astype(BF16), cim_ref[c]))
    ssmre_ref[0] = hre_scr[...]
    ssmim_ref[0] = him_scr[...]
    y = jnp.concatenate(ys, axis=1) + dskip_ref[...] * u
    g = 0.5 * y * (1.0 + jnp.tanh(math.sqrt(2.0 / math.pi) * (y + 0.044715 * (y * y * y))))
    y_ssm = g * jax.nn.sigmoid(_dot(g.astype(BF16), wglu_ref[...]) + bglu_ref[...])

    mix = jnp.concatenate([y_pool, y_ssm], axis=1)
    h_ref[0] = x + _dot(mix.astype(BF16), wout_ref[...])


def _const_spec(shape):
    return pl.BlockSpec(shape, lambda *_: (0,) * len(shape))


def _mixer(x, hist, h0re, h0im, prm, *, tt, pos0):
    b, t, d = x.shape
    kern = functools.partial(_mixer_kernel, tt=tt, pos0=pos0)
    per_b3 = lambda shp: pl.BlockSpec((1,) + shp, lambda i, j: (i, 0, 0))
    per_b4 = lambda shp: pl.BlockSpec((1,) + shp, lambda i, j: (i, 0, 0, 0))
    return pl.pallas_call(
        kern,
        grid=(b, t // tt),
        in_specs=[
            pl.BlockSpec((1, tt, d), lambda i, j: (i, j, 0)),
            per_b3((HIST_ROWS, D_POOL)),
            per_b4((N_SLABS, 1, SLAB)),
            per_b4((N_SLABS, 1, SLAB)),
            _const_spec((1, d)),
            _const_spec((d, d)),
            _const_spec((len(POOL_WINDOWS), POOL_GROUP, POOL_GROUP)),
            _const_spec((1, D_POOL)),
            _const_spec((N_PACKS, LANES, 2 * PACK_STATES)),
            _const_spec((N_PACKS, PACK_STATES, LANES)),
            _const_spec((N_PACKS, PACK_STATES, LANES)),
            _const_spec((N_SLABS, SUBLANES, SLAB)),
            _const_spec((N_SLABS, SUBLANES, SLAB)),
            _const_spec((1, D_SSM)),
            _const_spec((D_SSM, D_SSM)),
            _const_spec((1, D_SSM)),
            _const_spec((d, d)),
        ],
        out_specs=[
            pl.BlockSpec((1, tt, d), lambda i, j: (i, j, 0)),
            per_b3((HIST_ROWS, D_POOL)),
            per_b4((N_SLABS, 1, SLAB)),
            per_b4((N_SLABS, 1, SLAB)),
        ],
        out_shape=[
            jax.ShapeDtypeStruct((b, t, d), F32),
            jax.ShapeDtypeStruct((b, HIST_ROWS, D_POOL), F32),
            jax.ShapeDtypeStruct((b, N_SLABS, 1, SLAB), F32),
            jax.ShapeDtypeStruct((b, N_SLABS, 1, SLAB), F32),
        ],
        scratch_shapes=[
            pltpu.VMEM((HIST_ROWS, D_POOL), F32),
            pltpu.VMEM((N_SLABS, 1, SLAB), F32),
            pltpu.VMEM((N_SLABS, 1, SLAB), F32),
            pltpu.VMEM((N_SLABS, tt, SLAB), F32),
            pltpu.VMEM((N_SLABS, tt, SLAB), F32),
        ],
        compiler_params=pltpu.CompilerParams(
            dimension_semantics=("arbitrary", "arbitrary"), vmem_limit_bytes=VMEM_LIMIT),
    )(x, hist, h0re, h0im, prm['g_mix'], prm['w_in'], prm['w_pool'], prm['pool_scale'],
      prm['bmat'], prm['c_re'], prm['c_im'], prm['t_re'], prm['t_im'], prm['d_skip'],
      prm['w_glu'], prm['b_glu'], prm['w_out'])


def _memkv_kernel(m_ref, g_ref, wk_ref, wv_ref, k_ref, v_ref):
    m = _rms(m_ref[0], g_ref[...]).astype(BF16)
    k_ref[0] = _dot(m, wk_ref[...])
    v_ref[0] = _dot(m, wv_ref[...])


def _memkv(mem, g, wk, wv):
    b, n, d = mem.shape
    blk = pl.BlockSpec((1, n, d), lambda i: (i, 0, 0))
    return pl.pallas_call(
        _memkv_kernel,
        grid=(b,),
        in_specs=[blk, _const_spec((1, d)), _const_spec((d, d)), _const_spec((d, d))],
        out_specs=[blk, blk],
        out_shape=[jax.ShapeDtypeStruct((b, n, d), F32)] * 2,
        compiler_params=pltpu.CompilerParams(vmem_limit_bytes=VMEM_LIMIT),
    )(mem, g, wk, wv)


def _attn_kernel(h_ref, kt_ref, v_ref, gx_ref, wq_ref, wo_ref, gf_ref, wr_ref, br_ref,
                 h2_ref, xn_ref, e_ref, gate_ref, cnt_ref, cnt_scr, *, tt):
    h = h_ref[0]
    hn = _rms(h, gx_ref[...])
    q = _dot(hn.astype(BF16), wq_ref[...])
    outs = []
    for hd in range(N_XHEADS):
        cols = slice(hd * XHEAD_DIM, (hd + 1) * XHEAD_DIM)
        s = _dot(q[:, cols].astype(BF16), kt_ref[0, cols, :]) * (XHEAD_DIM ** -0.5)
        p = jnp.exp(s - jnp.max(s, axis=-1, keepdims=True))
        p = p / jnp.sum(p, axis=-1, keepdims=True)
        outs.append(_dot(p.astype(BF16), v_ref[0, :, cols]))
    o = jnp.concatenate(outs, axis=1)
    h2 = h + _dot(o.astype(BF16), wo_ref[...])
    h2_ref[0] = h2

    xn = _rms(h2, gf_ref[...])
    for s in range(ROW_TILE):
        xn_ref[pl.ds(s, tt, stride=ROW_TILE), :] = xn[:, s * LANES:(s + 1) * LANES]
    logits = _dot(xn.astype(BF16), wr_ref[...]) + br_ref[...]
    lane = lax.broadcasted_iota(jnp.int32, logits.shape, 1)
    lane_f = lane.astype(F32)
    e_out = jnp.zeros(logits.shape, jnp.int32)
    hits = jnp.zeros(logits.shape, F32)
    top = []
    for k in range(TOP_K):
        m = jnp.max(logits, axis=-1, keepdims=True)
        idx = jnp.min(jnp.where(logits == m, lane_f, float(LANES)), axis=-1, keepdims=True)
        e_out = jnp.where(lane == k, idx.astype(jnp.int32), e_out)
        top.append(m)
        chosen = lane_f == idx
        hits = hits + chosen.astype(F32)
        logits = jnp.where(chosen, -jnp.inf, logits)

    @pl.when(jnp.logical_and(pl.program_id(0) == 0, pl.program_id(1) == 0))
    def _():
        cnt_scr[...] = jnp.zeros_like(cnt_scr)
    cnt_scr[...] = cnt_scr[...] + jnp.sum(hits, axis=0, keepdims=True)
    cnt_ref[...] = cnt_scr[...]
    ex = [jnp.exp(m - top[0]) for m in top]
    tot = ex[0] + ex[1] + ex[2] + ex[3]
    g_out = jnp.zeros(logits.shape, F32)
    for k in range(TOP_K):
        g_out = jnp.where(lane == k, ex[k] / tot, g_out)
    e_ref[...] = e_out
    gate_ref[...] = g_out


def _attn(h, k, v, prm, *, tt):
    b, t, d = h.shape
    n = b * t
    kern = functools.partial(_attn_kernel, tt=tt)
    nt = t // tt
    kv = pl.BlockSpec((1, N_MEM, d), lambda i, j: (i, 0, 0))
    kt = pl.BlockSpec((1, d, N_MEM), lambda i, j: (i, 0, 0))
    tok = lambda width: pl.BlockSpec((tt, width), lambda i, j: (i * nt + j, 0))
    return pl.pallas_call(
        kern,
        grid=(b, nt),
        in_specs=[
            pl.BlockSpec((1, tt, d), lambda i, j: (i, j, 0)), kt, kv,
            _const_spec((1, d)), _const_spec((d, d)), _const_spec((d, d)),
            _const_spec((1, d)), _const_spec((d, LANES)), _const_spec((1, LANES)),
        ],
        out_specs=[
            pl.BlockSpec((1, tt, d), lambda i, j: (i, j, 0)),
            pl.BlockSpec((tt * ROW_TILE, LANES), lambda i, j: (i * nt + j, 0)),
            tok(LANES), tok(LANES), _const_spec((1, LANES)),
        ],
        out_shape=[
            jax.ShapeDtypeStruct((b, t, d), F32),
            jax.ShapeDtypeStruct((n * ROW_TILE, LANES), F32),
            jax.ShapeDtypeStruct((n, LANES), jnp.int32),
            jax.ShapeDtypeStruct((n, LANES), F32),
            jax.ShapeDtypeStruct((1, LANES), F32),
        ],
        scratch_shapes=[pltpu.VMEM((1, LANES), F32)],
        compiler_params=pltpu.CompilerParams(
            dimension_semantics=("arbitrary", "arbitrary"), vmem_limit_bytes=VMEM_LIMIT),
    )(h, k.transpose(0, 2, 1), v, prm['g_xattn'], prm['w_q'], prm['w_o'], prm['g_ffn'], prm['w_router'], prm['b_router'])


RANK_TILE = 512


def _rank_kernel(ep_ref, es_ref, start_ref, dest_ref, carry_scr, *, steps_p):
    i = pl.program_id(0)

    @pl.when(i == 0)
    def _():
        carry_scr[...] = start_ref[...]

    e = jnp.where(i < steps_p, ep_ref[...], es_ref[...])
    lane = lax.broadcasted_iota(jnp.int32, e.shape, 1)
    onehot = [lane == e[:, k:k + 1] for k in range(TOP_K)]
    hits = jnp.zeros(e.shape, F32)
    for oh in onehot:
        hits = hits + oh.astype(F32)
    r = lax.broadcasted_iota(jnp.int32, (RANK_TILE, RANK_TILE), 0)
    c = lax.broadcasted_iota(jnp.int32, (RANK_TILE, RANK_TILE), 1)
    before = (c < r).astype(BF16)
    base = _dot(before, hits.astype(BF16)) + carry_scr[...]
    rank = jnp.zeros(e.shape, jnp.int32)
    for k, oh in enumerate(onehot):
        rk = jnp.sum(jnp.where(oh, base, 0.0), axis=-1, keepdims=True).astype(jnp.int32)
        rank = jnp.where(lane == k, rk, rank)
    dest_ref[...] = rank
    carry_scr[...] = carry_scr[...] + jnp.sum(hits, axis=0, keepdims=True)


def _rank(e_p, e_s, start):
    steps_p = e_p.shape[0] // RANK_TILE
    blk = (RANK_TILE, LANES)
    return pl.pallas_call(
        functools.partial(_rank_kernel, steps_p=steps_p),
        grid=(steps_p + 1,),
        in_specs=[pl.BlockSpec(blk, lambda i: (jnp.minimum(i, steps_p - 1), 0)), _const_spec(blk),
                  _const_spec((1, LANES))],
        out_specs=pl.BlockSpec(blk, lambda i: (i, 0)),
        out_shape=jax.ShapeDtypeStruct(((steps_p + 1) * RANK_TILE, LANES), jnp.int32),
        scratch_shapes=[pltpu.VMEM((1, LANES), F32)],
        compiler_params=pltpu.CompilerParams(dimension_semantics=("arbitrary",)),
    )(e_p, e_s, start)


BLOCK_ROWS = MOE_BLOCK * ROW_TILE


def _row_copy_wait(src_ref, dst_ref, sem, rows):
    pltpu.make_async_copy(src_ref.at[pl.ds(0, rows * ROW_TILE)], dst_ref.at[pl.ds(0, rows * ROW_TILE)], sem).wait()


def _dispatch_kernel(last_blk_ref, nvalid_ref, dest_p_ref, dest_s_ref, xp_ref, xs_ref, xpad_ref,
                     zero_buf, sem, zero_sem, *, tile_p, tile_s, n_blocks):
    i = pl.program_id(0)
    last = pl.num_programs(0) - 1

    def zero_copy(blk):
        dst = xpad_ref.at[pl.ds(pl.multiple_of(blk * BLOCK_ROWS, BLOCK_ROWS), BLOCK_ROWS)]
        return pltpu.make_async_copy(zero_buf, dst, zero_sem)

    @pl.when(i == 0)
    def _():
        zero_buf[...] = jnp.zeros_like(zero_buf)

        def on_pad_blocks(fn):
            for e in range(N_EXPERTS):
                if e == 0:
                    fn(last_blk_ref[0])
                else:
                    pl.when(last_blk_ref[e] != last_blk_ref[e - 1])(functools.partial(fn, last_blk_ref[e]))
            lax.fori_loop(nvalid_ref[0], n_blocks, lambda blk, c: (fn(blk), c)[1], 0)

        on_pad_blocks(lambda blk: zero_copy(blk).start())
        on_pad_blocks(lambda blk: zero_copy(blk).wait())

    def push(dest_ref, x_ref, tile):
        def body(it, carry):
            for j in range(PUSH_UNROLL):
                t = it * PUSH_UNROLL + j
                src = x_ref.at[pl.ds(pl.multiple_of(t * ROW_TILE, ROW_TILE), ROW_TILE)]
                for k in range(TOP_K):
                    row = dest_ref[0, 0, t * TOP_K + k]
                    dst = xpad_ref.at[pl.ds(pl.multiple_of(row * ROW_TILE, ROW_TILE), ROW_TILE)]
                    pltpu.make_async_copy(src, dst, sem).start(priority=k % N_DMA_PRIORITIES)
            return carry

        lax.fori_loop(0, tile // PUSH_UNROLL, body, 0)
        for _ in range(TOP_K):
            _row_copy_wait(x_ref, xpad_ref, sem, tile)

    pl.when(i < last)(lambda: push(dest_p_ref, xp_ref, tile_p))
    pl.when(i == last)(lambda: push(dest_s_ref, xs_ref, tile_s))


def _dispatch(last_blk, nvalid, dest_p, dest_s, xn_p, xn_s, n_blocks, *, tile_p):
    n_p = xn_p.shape[0] // ROW_TILE
    tile_s = xn_s.shape[0] // ROW_TILE
    steps_p = n_p // tile_p
    kern = functools.partial(_dispatch_kernel, tile_p=tile_p, tile_s=tile_s, n_blocks=n_blocks)
    return pl.pallas_call(
        kern,
        grid_spec=pltpu.PrefetchScalarGridSpec(
            num_scalar_prefetch=2,
            grid=(steps_p + 1,),
            in_specs=[
                pl.BlockSpec((1, 1, tile_p * TOP_K), lambda i, lb, nv: (jnp.minimum(i, steps_p - 1), 0, 0),
                             memory_space=pltpu.SMEM),
                pl.BlockSpec((1, 1, tile_s * TOP_K), lambda i, lb, nv: (0, 0, 0), memory_space=pltpu.SMEM),
                pl.BlockSpec((tile_p * ROW_TILE, LANES), lambda i, lb, nv: (jnp.minimum(i, steps_p - 1), 0)),
                pl.BlockSpec((tile_s * ROW_TILE, LANES), lambda i, lb, nv: (0, 0)),
            ],
            out_specs=pl.BlockSpec(memory_space=pl.ANY),
            scratch_shapes=[pltpu.VMEM((BLOCK_ROWS, LANES), F32), pltpu.SemaphoreType.DMA, pltpu.SemaphoreType.DMA],
        ),
        out_shape=jax.ShapeDtypeStruct((n_blocks * BLOCK_ROWS, LANES), F32),
        compiler_params=pltpu.CompilerParams(dimension_semantics=("arbitrary",)),
    )(last_blk, nvalid, dest_p.reshape(steps_p, 1, tile_p * TOP_K), dest_s.reshape(1, 1, tile_s * TOP_K), xn_p, xn_s)


def _expert_kernel(blk_e_ref, nvalid_ref, next_e_ref, x_ref, bg_ref, bu_ref, bd_ref, wg_hbm, wu_hbm, wd_hbm, y_ref,
                   stage, w_bf, sems):
    b = pl.program_id(0)
    valid = b < nvalid_ref[0]
    e = blk_e_ref[b]
    weights = (wg_hbm, wu_hbm, wd_hbm)

    def fetch(expert):
        return [pltpu.make_async_copy(w.at[expert], stage.at[i], sems.at[i]) for i, w in enumerate(weights)]

    @pl.when(b == 0)
    def _():
        for cp in fetch(e):
            cp.start()

    @pl.when(jnp.logical_and(valid, jnp.logical_or(b == 0, blk_e_ref[jnp.maximum(b - 1, 0)] != e)))
    def _():
        for i, cp in enumerate(fetch(e)):
            cp.wait()
            w_bf[i] = stage[i].astype(BF16)

        @pl.when(next_e_ref[b] >= 0)
        def _():
            for cp in fetch(next_e_ref[b]):
                cp.start()

    @pl.when(valid)
    def _():
        x = jnp.concatenate([x_ref[pl.ds(s, MOE_BLOCK, stride=ROW_TILE), :] for s in range(ROW_TILE)], axis=1)
        x = x.astype(BF16)
        g = _dot(x, w_bf[0]) + bg_ref[0]
        u = _dot(x, w_bf[1]) + bu_ref[0]
        g = jnp.minimum(g, SWIGLU_LIMIT)
        u = jnp.clip(u, -SWIGLU_LIMIT, SWIGLU_LIMIT)
        hdn = g * jax.nn.sigmoid(SWIGLU_ALPHA * g) * (u + 1.0)
        y = _dot(hdn.astype(BF16), w_bf[2]) + bd_ref[0]
        for s in range(ROW_TILE):
            y_ref[pl.ds(s, MOE_BLOCK, stride=ROW_TILE), :] = y[:, s * LANES:(s + 1) * LANES]

    @pl.when(b >= nvalid_ref[0])
    def _():
        y_ref[...] = jnp.zeros_like(y_ref)


def _experts(blk_e, nvalid, next_e, xpad, w_gate, b_gate, w_up, b_up, w_down, b_down, n_blocks):
    d = D_MODEL
    n_w = 3
    bspec = pl.BlockSpec((1, 1, d), lambda b, be, nv, ne: (be[b], 0, 0))
    xspec = pl.BlockSpec((BLOCK_ROWS, LANES), lambda b, be, nv, ne: (jnp.minimum(b, nv[0] - 1), 0))
    hbm = pl.BlockSpec(memory_space=pl.ANY)
    return pl.pallas_call(
        _expert_kernel,
        grid_spec=pltpu.PrefetchScalarGridSpec(
            num_scalar_prefetch=3,
            grid=(n_blocks,),
            in_specs=[xspec, bspec, bspec, bspec, hbm, hbm, hbm],
            out_specs=pl.BlockSpec((BLOCK_ROWS, LANES), lambda b, be, nv, ne: (b, 0)),
            scratch_shapes=[pltpu.VMEM((n_w, d, d), F32), pltpu.VMEM((n_w, d, d), BF16),
                            pltpu.SemaphoreType.DMA((n_w,))],
        ),
        out_shape=jax.ShapeDtypeStruct((n_blocks * BLOCK_ROWS, LANES), F32),
        compiler_params=pltpu.CompilerParams(dimension_semantics=("arbitrary",), vmem_limit_bytes=VMEM_LIMIT),
    )(blk_e, nvalid, next_e, xpad, b_gate.reshape(N_EXPERTS, 1, d), b_up.reshape(N_EXPERTS, 1, d),
      b_down.reshape(N_EXPERTS, 1, d), w_gate, w_up, w_down)


def _combine_kernel(dest_ref, dest_next_ref, h_ref, gate_ref, gfin_ref, ypad_ref, y_ref, buf_a, buf_b, sems, *, tile):
    i = pl.program_id(0)
    n_steps = pl.num_programs(0)

    def gather(dref, buf, sem, t, k):
        row = dref[0, 0, t * TOP_K + k]
        slot = (k * tile + t) * ROW_TILE
        if not isinstance(slot, int):
            slot = pl.multiple_of(slot, ROW_TILE)
        src = ypad_ref.at[pl.ds(pl.multiple_of(row * ROW_TILE, ROW_TILE), ROW_TILE)]
        pltpu.make_async_copy(src, buf.at[pl.ds(slot, ROW_TILE)], sem).start(priority=k % N_DMA_PRIORITIES)

    def drain(buf, sem):
        for _ in range(TOP_K):
            _row_copy_wait(ypad_ref, buf, sem, tile)

    @pl.when(i == 0)
    def _():
        def body(t, carry):
            for k in range(TOP_K):
                gather(dest_ref, buf_a, sems.at[0], t, k)
            return carry
        lax.fori_loop(0, tile, body, 0)

    def step(cur, cur_sem, nxt, nxt_sem):
        drain(cur, cur_sem)
        gates = gate_ref[...]
        h = h_ref[...]
        per_slab = tile // ROW_TILE
        cols = []
        for s in range(ROW_TILE):
            for t in range(s * per_slab, (s + 1) * per_slab):
                for k in range(TOP_K):
                    gather(dest_next_ref, nxt, nxt_sem, t, k)
            acc = h[:, s * LANES:(s + 1) * LANES]
            for k in range(TOP_K):
                acc = acc + gates[:, k:k + 1] * cur[pl.ds(k * tile * ROW_TILE + s, tile, stride=ROW_TILE), :]
            cols.append(acc)
        y_ref[...] = _rms(jnp.concatenate(cols, axis=1), gfin_ref[...])

    pl.when(i % 2 == 0)(lambda: step(buf_a, sems.at[0], buf_b, sems.at[1]))
    pl.when(i % 2 == 1)(lambda: step(buf_b, sems.at[1], buf_a, sems.at[0]))

    @pl.when(i == n_steps - 1)
    def _():
        pl.when(i % 2 == 0)(lambda: drain(buf_b, sems.at[1]))
        pl.when(i % 2 == 1)(lambda: drain(buf_a, sems.at[0]))


def _combine(dest, h2, gates, g_final, ypad, *, tile):
    n, d = h2.shape
    steps = n // tile
    kern = functools.partial(_combine_kernel, tile=tile)
    dest3 = dest.reshape(steps, 1, tile * TOP_K)
    buf = pltpu.VMEM((TOP_K * tile * ROW_TILE, LANES), F32)
    return pl.pallas_call(
        kern,
        grid=(steps,),
        in_specs=[
            pl.BlockSpec((1, 1, tile * TOP_K), lambda i: (i, 0, 0), memory_space=pltpu.SMEM),
            pl.BlockSpec((1, 1, tile * TOP_K), lambda i: (jnp.minimum(i + 1, steps - 1), 0, 0),
                         memory_space=pltpu.SMEM),
            pl.BlockSpec((tile, d), lambda i: (i, 0)),
            pl.BlockSpec((tile, LANES), lambda i: (i, 0)),
            _const_spec((1, d)),
            pl.BlockSpec(memory_space=pl.ANY),
        ],
        out_specs=pl.BlockSpec((tile, d), lambda i: (i, 0)),
        out_shape=jax.ShapeDtypeStruct((n, d), F32),
        scratch_shapes=[buf, buf, pltpu.SemaphoreType.DMA((2,))],
        compiler_params=pltpu.CompilerParams(dimension_semantics=("arbitrary",), vmem_limit_bytes=VMEM_LIMIT),
    )(dest3, dest3, h2, gates, g_final, ypad)


def _ssm_params(a_re, a_im, log_dt, b_re, b_im, c_re, c_im):
    dt = jnp.exp(log_dt)[:, None]
    mag = jnp.exp(a_re * dt)
    lb_re = mag * jnp.cos(a_im * dt)
    lb_im = mag * jnp.sin(a_im * dt)
    den = a_re * a_re + a_im * a_im
    q_re = ((lb_re - 1.0) * a_re + lb_im * a_im) / den
    q_im = (lb_im * a_re - (lb_re - 1.0) * a_im) / den
    bb_re = q_re[:, :, None] * b_re - q_im[:, :, None] * b_im
    bb_im = q_re[:, :, None] * b_im + q_im[:, :, None] * b_re
    eye = jnp.eye(SSM_PACK, dtype=F32)

    def pack_blockdiag(m):
        m = m.reshape(N_PACKS, SSM_PACK, m.shape[1], m.shape[2])
        return jnp.einsum('ngab,gh->ngahb', m, eye).reshape(N_PACKS, SSM_PACK * m.shape[2], SSM_PACK * m.shape[3])

    bmat = jnp.concatenate([pack_blockdiag(bb_re.transpose(0, 2, 1)), pack_blockdiag(bb_im.transpose(0, 2, 1))],
                           axis=2)
    cmat_re = pack_blockdiag(c_re.transpose(0, 2, 1))
    cmat_im = pack_blockdiag(-c_im.transpose(0, 2, 1))
    t_re, t_im = [lb_re.reshape(-1)], [lb_im.reshape(-1)]
    for _ in range(SUBLANES - 1):
        r, i = t_re[-1], t_im[-1]
        t_re.append(r * t_re[0] - i * t_im[0])
        t_im.append(r * t_im[0] + i * t_re[0])
    slabbed = lambda rows: jnp.stack(rows).reshape(SUBLANES, N_SLABS, SLAB).transpose(1, 0, 2)
    return bmat.astype(BF16), cmat_re.astype(BF16), cmat_im.astype(BF16), slabbed(t_re), slabbed(t_im)


def kernel(x_prompt, x_sample, mem_prompt, cache_mem_k, cache_mem_v, cache_pool, state_ssm_re, state_ssm_im, norm_mix, w_in, w_pool, pool_scale, ssm_a_re, ssm_a_im, ssm_log_dt, ssm_b_re, ssm_b_im, ssm_c_re, ssm_c_im, ssm_d, w_glu, b_glu, w_out, norm_xattn, norm_mem, w_q, w_k, w_v, w_o, norm_ffn, w_router, b_router, w_gate, b_gate, w_up, b_up, w_down, b_down, norm_final):
    assert x_prompt.shape[2] == D_MODEL and norm_mix.shape[0] == 1
    bp, tp, d = x_prompt.shape
    bs, ts, _ = x_sample.shape
    n_p, n_s = bp * tp, bs * ts
    row = lambda v: v.reshape(1, -1)

    bmat, c_re, c_im, t_re, t_im = _ssm_params(ssm_a_re[0], ssm_a_im[0], ssm_log_dt[0], ssm_b_re[0], ssm_b_im[0],
                                               ssm_c_re[0], ssm_c_im[0])
    mix_prm = dict(g_mix=row(norm_mix[0]), w_in=w_in[0].astype(BF16), w_pool=w_pool[0].astype(BF16),
                   pool_scale=row(pool_scale[0]), bmat=bmat, c_re=c_re, c_im=c_im, t_re=t_re, t_im=t_im,
                   d_skip=row(ssm_d[0]), w_glu=w_glu[0].astype(BF16), b_glu=row(b_glu[0]),
                   w_out=w_out[0].astype(BF16))

    attn_prm = dict(g_xattn=row(norm_xattn[0]), w_q=w_q[0].astype(BF16), w_o=w_o[0].astype(BF16),
                    g_ffn=row(norm_ffn[0]),
                    w_router=jnp.pad(w_router[0], ((0, 0), (0, LANES - N_EXPERTS))).astype(BF16),
                    b_router=jnp.pad(row(b_router[0]), ((0, 0), (0, LANES - N_EXPERTS)), constant_values=-jnp.inf))
    slab_state = lambda s: s.reshape(s.shape[0], N_SLABS, 1, SLAB)

    zeros_state = jnp.zeros((bp, N_SLABS, 1, SLAB), F32)
    h1_p, pool_p, sre_p, sim_p = _mixer(x_prompt, jnp.zeros((bp, HIST_ROWS, D_POOL), F32), zeros_state, zeros_state,
                                        mix_prm, tt=512, pos0=0)
    mk, mv = _memkv(mem_prompt, row(norm_mem[0]), w_k[0].astype(BF16), w_v[0].astype(BF16))
    h2_p, xn_p, e_p, g_p, cnt_p = _attn(h1_p, mk.astype(BF16), mv.astype(BF16), attn_prm, tt=1024)

    hist_s = jnp.pad(cache_pool[0], ((0, 0), (HIST_ROWS - POOL_BUF, 0), (0, 0)))
    h1_s, pool_s, sre_s, sim_s = _mixer(x_sample, hist_s, slab_state(state_ssm_re[0]), slab_state(state_ssm_im[0]),
                                        mix_prm, tt=ts, pos0=PAST_LEN)
    ck = cache_mem_k[0].reshape(bs, N_MEM, d).astype(BF16)
    cv = cache_mem_v[0].reshape(bs, N_MEM, d).astype(BF16)
    h2_s, xn_s, e_s, g_s, cnt_s = _attn(h1_s, ck, cv, attn_prm, tt=ts)

    n_all = n_p + n_s
    counts = (cnt_p + cnt_s)[0, :N_EXPERTS].astype(jnp.int32)
    padded = (counts + MOE_BLOCK - 1) // MOE_BLOCK * MOE_BLOCK
    pend = jnp.cumsum(padded)
    pstart = pend - padded
    n_blocks = -(-(n_all * TOP_K + N_EXPERTS * (MOE_BLOCK - 1)) // MOE_BLOCK)
    blk_e = jnp.minimum(jnp.sum(pend[None, :] <= (jnp.arange(n_blocks) * MOE_BLOCK)[:, None], axis=1),
                        N_EXPERTS - 1).astype(jnp.int32)
    nvalid = (pend[-1:] // MOE_BLOCK).astype(jnp.int32)
    last_blk = jnp.maximum(pend // MOE_BLOCK - 1, 0).astype(jnp.int32)
    after = (pend // MOE_BLOCK)[blk_e]
    next_e = jnp.where(after < nvalid[0], blk_e[jnp.minimum(after, n_blocks - 1)], -1).astype(jnp.int32)
    start = jnp.pad(pstart.astype(F32), (0, LANES - N_EXPERTS)).reshape(1, LANES)
    e_s_tile = jnp.pad(e_s, ((0, RANK_TILE - n_s), (0, 0)), constant_values=-1)
    dest = _rank(e_p, e_s_tile, start)[:n_all, :TOP_K].reshape(-1)
    dest_p, dest_s = dest[:n_p * TOP_K], dest[n_p * TOP_K:]

    xpad = _dispatch(last_blk, nvalid, dest_p, dest_s, xn_p, xn_s, n_blocks, tile_p=256)
    ypad = _experts(blk_e, nvalid, next_e, xpad, w_gate[0], b_gate[0], w_up[0], b_up[0], w_down[0], b_down[0], n_blocks)
    g_fin = row(norm_final)
    y_p = _combine(dest_p, h2_p.reshape(n_p, d), g_p, g_fin, ypad, tile=256)
    y_s = _combine(dest_s, h2_s.reshape(n_s, d), g_s, g_fin, ypad, tile=n_s)

    unslab = lambda s: s.reshape(1, s.shape[0], N_SSM_GROUPS, SSM_STATE)
    kv5 = lambda a: a.reshape(1, bp, N_MEM, N_XHEADS, XHEAD_DIM)
    return (y_p.reshape(bp, tp, d), y_s.reshape(bs, ts, d), kv5(mk), kv5(mv),
            pool_p[None, :, HIST_ROWS - POOL_BUF:], pool_s[None, :, HIST_ROWS - POOL_BUF:],
            unslab(sre_p), unslab(sim_p), unslab(sre_s), unslab(sim_s))
```

```python
import functools
import math

import jax
import jax.numpy as jnp
from jax import lax
from jax.experimental import pallas as pl
from jax.experimental.pallas import tpu as pltpu

D_MODEL = 1024
D_POOL = 512
D_SSM = 512
POOL_WINDOWS = (2, 4, 8, 16)
POOL_GROUP = 128
HIST_ROWS = 16
POOL_BUF = 15
SSM_GROUP = 16
N_SSM_GROUPS = 32
SSM_STATE = 64
N_STATE = N_SSM_GROUPS * SSM_STATE
CHUNK = 64
N_MEM = 256
N_XHEADS = 4
XHEAD_DIM = 256
N_EXPERTS = 32
TOP_K = 4
SWIGLU_LIMIT = 7.0
SWIGLU_ALPHA = 1.702
MOE_BLOCK = 512
EPS = 1e-6
PAST_LEN = 1024

LANES = 128
SUBLANES = 8
SLAB = 256
N_SLABS = N_STATE // SLAB
SSM_PACK = LANES // SSM_GROUP
N_PACKS = N_SSM_GROUPS // SSM_PACK
PACK_STATES = SSM_PACK * SSM_STATE
SLABS_PER_PACK = PACK_STATES // SLAB
RUN = 4
ROW_TILE = D_MODEL // LANES
N_DMA_PRIORITIES = 2
PUSH_UNROLL = 4
VMEM_LIMIT = 56 * 1024 * 1024

BF16 = jnp.bfloat16
F32 = jnp.float32


def _rms(x, g):
    return x * lax.rsqrt(jnp.mean(x * x, axis=-1, keepdims=True) + EPS) * g


def _dot(a, b):
    return jnp.dot(a, b, preferred_element_type=F32)


def _chain_slab(s, tre_ref, tim_ref, hre_scr, him_scr, re_scr, im_scr, tt):
    t_re = tre_ref[s]
    t_im = tim_ref[s]
    upper = lax.broadcasted_iota(jnp.int32, (SUBLANES, SLAB), 0) < RUN
    ta_re, ta_im = jnp.where(upper, t_re, 0.0), jnp.where(upper, t_im, 0.0)
    tb_re, tb_im = jnp.where(upper, 0.0, t_re), jnp.where(upper, 0.0, t_im)
    c_re = hre_scr[s]
    c_im = him_scr[s]
    for v in range(tt // SUBLANES):
        rows = pl.ds(v * SUBLANES, SUBLANES)
        re = re_scr[rows, :]
        im = im_scr[rows, :]
        re, im = re + (ta_re * c_re - ta_im * c_im), im + (ta_re * c_im + ta_im * c_re)
        m_re = re[RUN - 1:RUN, :]
        m_im = im[RUN - 1:RUN, :]
        re, im = re + (tb_re * m_re - tb_im * m_im), im + (tb_re * m_im + tb_im * m_re)
        re_scr[rows, :] = re
        im_scr[rows, :] = im
        c_re = re[SUBLANES - 1:SUBLANES, :]
        c_im = im[SUBLANES - 1:SUBLANES, :]
    hre_scr[s] = c_re
    him_scr[s] = c_im


def _mixer_kernel(x_ref, hist_ref, h0re_ref, h0im_ref, gmix_ref, win_ref, wpool_ref, pscale_ref,
                  bmat_ref, cre_ref, cim_ref, tre_ref, tim_ref, dskip_ref, wglu_ref, bglu_ref, wout_ref,
                  h_ref, poolnew_ref, ssmre_ref, ssmim_ref,
                  hist_scr, hre_scr, him_scr, *slab_scr, tt, pos0):
    bure_scr, buim_scr = slab_scr[:N_SLABS], slab_scr[N_SLABS:]
    j = pl.program_id(1)

    @pl.when(j == 0)
    def _():
        hist_scr[...] = hist_ref[0]
        hre_scr[...] = h0re_ref[0]
        him_scr[...] = h0im_ref[0]

    x = x_ref[0]
    xn = _rms(x, gmix_ref[...])
    z = _dot(xn.astype(BF16), win_ref[...])
    zp = z[:, :D_POOL]
    u = z[:, D_POOL:]

    ext = jnp.concatenate([hist_scr[...], zp], axis=0)
    pos = pos0 + j * tt + lax.broadcasted_iota(jnp.int32, (tt, 1), 0)
    acc = ext
    outs = []
    for gi, w in enumerate(POOL_WINDOWS):
        lo = gi * POOL_GROUP
        acc = acc[:, POOL_GROUP * (1 if gi else 0):]
        acc = acc + pltpu.roll(acc, w // 2, 0)
        wsum = acc[HIST_ROWS:, :POOL_GROUP]
        cnt = jnp.minimum(pos + 1, w).astype(F32)
        d = wsum / cnt - zp[:, lo:lo + POOL_GROUP]
        outs.append(_dot(d.astype(BF16), wpool_ref[gi]))
    y_pool = jnp.concatenate(outs, axis=1) * pscale_ref[...]
    hist_scr[...] = ext[tt:tt + HIST_ROWS]
    poolnew_ref[0] = ext[tt:tt + HIST_ROWS]

    run_row = lax.broadcasted_iota(jnp.int32, (tt, 1), 0) % RUN
    shifted = [u.astype(BF16)]
    for dd in range(1, RUN):
        shifted.append(jnp.where(run_row >= dd, pltpu.roll(u, dd, 0), 0.0).astype(BF16))
    def project(c):
        lhs = jnp.concatenate([sh[:, c * LANES:(c + 1) * LANES] for sh in shifted], axis=1)
        bu = _dot(lhs, bmat_ref[c])
        for i in range(SLABS_PER_PACK):
            s = c * SLABS_PER_PACK + i
            bure_scr[s][...] = bu[:, i * SLAB:(i + 1) * SLAB]
            buim_scr[s][...] = bu[:, PACK_STATES + i * SLAB:PACK_STATES + (i + 1) * SLAB]

    ys = []
    project(0)
    for c in range(N_PACKS):
        slabs = range(c * SLABS_PER_PACK, (c + 1) * SLABS_PER_PACK)
        if c + 1 < N_PACKS:
            project(c + 1)
        for s in slabs:
            _chain_slab(s, tre_ref, tim_ref, hre_scr, him_scr, bure_scr[s], buim_scr[s], tt)
        hs_re = jnp.concatenate([bure_scr[s][...] for s in slabs], axis=1)
        hs_im = jnp.concatenate([buim_scr[s][...] for s in slabs], axis=1)
        ys.append(_dot(hs_re.astype(BF16), cre_ref[c]) + _dot(hs_im.astype(BF16), cim_ref[c]))
    ssmre_ref[0] = hre_scr[...]
    ssmim_ref[0] = him_scr[...]
    y = jnp.concatenate(ys, axis=1) + dskip_ref[...] * u
    g = 0.5 * y * (1.0 + jnp.tanh(math.sqrt(2.0 / math.pi) * (y + 0.044715 * (y * y * y))))
    y_ssm = g * jax.nn.sigmoid(_dot(g.astype(BF16), wglu_ref[...]) + bglu_ref[...])

    mix = jnp.concatenate([y_pool, y_ssm], axis=1)
    h_ref[0] = x + _dot(mix.astype(BF16), wout_ref[...])


def _const_spec(shape):
    return pl.BlockSpec(shape, lambda *_: (0,) * len(shape))


def _mixer(x, hist, h0re, h0im, prm, *, tt, pos0):
    b, t, d = x.shape
    kern = functools.partial(_mixer_kernel, tt=tt, pos0=pos0)
    per_b3 = lambda shp: pl.BlockSpec((1,) + shp, lambda i, j: (i, 0, 0))
    per_b4 = lambda shp: pl.BlockSpec((1,) + shp, lambda i, j: (i, 0, 0, 0))
    return pl.pallas_call(
        kern,
        grid=(b, t // tt),
        in_specs=[
            pl.BlockSpec((1, tt, d), lambda i, j: (i, j, 0)),
            per_b3((HIST_ROWS, D_POOL)),
            per_b4((N_SLABS, 1, SLAB)),
            per_b4((N_SLABS, 1, SLAB)),
            _const_spec((1, d)),
            _const_spec((d, d)),
            _const_spec((len(POOL_WINDOWS), POOL_GROUP, POOL_GROUP)),
            _const_spec((1, D_POOL)),
            _const_spec((N_PACKS, RUN * LANES, 2 * PACK_STATES)),
            _const_spec((N_PACKS, PACK_STATES, LANES)),
            _const_spec((N_PACKS, PACK_STATES, LANES)),
            _const_spec((N_SLABS, SUBLANES, SLAB)),
            _const_spec((N_SLABS, SUBLANES, SLAB)),
            _const_spec((1, D_SSM)),
            _const_spec((D_SSM, D_SSM)),
            _const_spec((1, D_SSM)),
            _const_spec((d, d)),
        ],
        out_specs=[
            pl.BlockSpec((1, tt, d), lambda i, j: (i, j, 0)),
            per_b3((HIST_ROWS, D_POOL)),
            per_b4((N_SLABS, 1, SLAB)),
            per_b4((N_SLABS, 1, SLAB)),
        ],
        out_shape=[
            jax.ShapeDtypeStruct((b, t, d), F32),
            jax.ShapeDtypeStruct((b, HIST_ROWS, D_POOL), F32),
            jax.ShapeDtypeStruct((b, N_SLABS, 1, SLAB), F32),
            jax.ShapeDtypeStruct((b, N_SLABS, 1, SLAB), F32),
        ],
        scratch_shapes=[
            pltpu.VMEM((HIST_ROWS, D_POOL), F32),
            pltpu.VMEM((N_SLABS, 1, SLAB), F32),
            pltpu.VMEM((N_SLABS, 1, SLAB), F32),
        ] + [pltpu.VMEM((tt, SLAB), F32)] * (2 * N_SLABS),
        compiler_params=pltpu.CompilerParams(
            dimension_semantics=("arbitrary", "arbitrary"), vmem_limit_bytes=VMEM_LIMIT),
    )(x, hist, h0re, h0im, prm['g_mix'], prm['w_in'], prm['w_pool'], prm['pool_scale'],
      prm['bmat'], prm['c_re'], prm['c_im'], prm['t_re'], prm['t_im'], prm['d_skip'],
      prm['w_glu'], prm['b_glu'], prm['w_out'])


def _memkv_kernel(m_ref, g_ref, wk_ref, wv_ref, k_ref, v_ref):
    m = _rms(m_ref[0], g_ref[...]).astype(BF16)
    k_ref[0] = _dot(m, wk_ref[...])
    v_ref[0] = _dot(m, wv_ref[...])


def _memkv(mem, g, wk, wv):
    b, n, d = mem.shape
    blk = pl.BlockSpec((1, n, d), lambda i: (i, 0, 0))
    return pl.pallas_call(
        _memkv_kernel,
        grid=(b,),
        in_specs=[blk, _const_spec((1, d)), _const_spec((d, d)), _const_spec((d, d))],
        out_specs=[blk, blk],
        out_shape=[jax.ShapeDtypeStruct((b, n, d), F32)] * 2,
        compiler_params=pltpu.CompilerParams(vmem_limit_bytes=VMEM_LIMIT),
    )(mem, g, wk, wv)


def _attn_kernel(h_ref, k_ref, v_ref, gx_ref, wq_ref, wo_ref, gf_ref, wr_ref, br_ref,
                 h2_ref, xn_ref, e_ref, gate_ref, cnt_ref, cnt_scr, kt_scr, v_scr, *, tt):
    @pl.when(pl.program_id(1) == 0)
    def _():
        kt_scr[...] = k_ref[0].T.astype(BF16)
        v_scr[...] = v_ref[0].astype(BF16)

    h = h_ref[0]
    hn = _rms(h, gx_ref[...])
    q = _dot(hn.astype(BF16), wq_ref[...])
    outs = []
    for hd in range(N_XHEADS):
        cols = slice(hd * XHEAD_DIM, (hd + 1) * XHEAD_DIM)
        s = _dot(q[:, cols].astype(BF16), kt_scr[cols, :]) * (XHEAD_DIM ** -0.5)
        p = jnp.exp(s - jnp.max(s, axis=-1, keepdims=True))
        p = p / jnp.sum(p, axis=-1, keepdims=True)
        outs.append(_dot(p.astype(BF16), v_scr[:, cols]))
    o = jnp.concatenate(outs, axis=1)
    h2 = h + _dot(o.astype(BF16), wo_ref[...])
    h2_ref[0] = h2

    xn = _rms(h2, gf_ref[...])
    for s in range(ROW_TILE):
        xn_ref[pl.ds(s, tt, stride=ROW_TILE), :] = xn[:, s * LANES:(s + 1) * LANES]
    logits = _dot(xn.astype(BF16), wr_ref[...]) + br_ref[...]
    lane = lax.broadcasted_iota(jnp.int32, logits.shape, 1)
    lane_f = lane.astype(F32)
    e_out = jnp.zeros(logits.shape, jnp.int32)
    hits = jnp.zeros(logits.shape, F32)
    top = []
    for k in range(TOP_K):
        m = jnp.max(logits, axis=-1, keepdims=True)
        idx = jnp.min(jnp.where(logits == m, lane_f, float(LANES)), axis=-1, keepdims=True)
        e_out = jnp.where(lane == k, idx.astype(jnp.int32), e_out)
        top.append(m)
        chosen = lane_f == idx
        hits = hits + chosen.astype(F32)
        logits = jnp.where(chosen, -jnp.inf, logits)

    @pl.when(jnp.logical_and(pl.program_id(0) == 0, pl.program_id(1) == 0))
    def _():
        cnt_scr[...] = jnp.zeros_like(cnt_scr)
    cnt_scr[...] = cnt_scr[...] + jnp.sum(hits, axis=0, keepdims=True)
    cnt_ref[...] = cnt_scr[...]
    ex = [jnp.exp(m - top[0]) for m in top]
    tot = ex[0] + ex[1] + ex[2] + ex[3]
    g_out = jnp.zeros(logits.shape, F32)
    for k in range(TOP_K):
        g_out = jnp.where(lane == k, ex[k] / tot, g_out)
    e_ref[...] = e_out
    gate_ref[...] = g_out


def _attn(h, k, v, prm, *, tt):
    b, t, d = h.shape
    n = b * t
    kern = functools.partial(_attn_kernel, tt=tt)
    nt = t // tt
    kv = pl.BlockSpec((1, N_MEM, d), lambda i, j: (i, 0, 0))
    tok = lambda width: pl.BlockSpec((tt, width), lambda i, j: (i * nt + j, 0))
    return pl.pallas_call(
        kern,
        grid=(b, nt),
        in_specs=[
            pl.BlockSpec((1, tt, d), lambda i, j: (i, j, 0)), kv, kv,
            _const_spec((1, d)), _const_spec((d, d)), _const_spec((d, d)),
            _const_spec((1, d)), _const_spec((d, LANES)), _const_spec((1, LANES)),
        ],
        out_specs=[
            pl.BlockSpec((1, tt, d), lambda i, j: (i, j, 0)),
            pl.BlockSpec((tt * ROW_TILE, LANES), lambda i, j: (i * nt + j, 0)),
            tok(LANES), tok(LANES), _const_spec((1, LANES)),
        ],
        out_shape=[
            jax.ShapeDtypeStruct((b, t, d), F32),
            jax.ShapeDtypeStruct((n * ROW_TILE, LANES), F32),
            jax.ShapeDtypeStruct((n, LANES), jnp.int32),
            jax.ShapeDtypeStruct((n, LANES), F32),
            jax.ShapeDtypeStruct((1, LANES), F32),
        ],
        scratch_shapes=[pltpu.VMEM((1, LANES), F32), pltpu.VMEM((d, N_MEM), BF16), pltpu.VMEM((N_MEM, d), BF16)],
        compiler_params=pltpu.CompilerParams(
            dimension_semantics=("arbitrary", "arbitrary"), vmem_limit_bytes=VMEM_LIMIT),
    )(h, k, v, prm['g_xattn'], prm['w_q'], prm['w_o'], prm['g_ffn'], prm['w_router'], prm['b_router'])


RANK_TILE = 512


def _rank_kernel(ep_ref, es_ref, start_ref, dest_ref, carry_scr, *, steps_p):
    i = pl.program_id(0)

    @pl.when(i == 0)
    def _():
        carry_scr[...] = start_ref[...]

    e = jnp.where(i < steps_p, ep_ref[...], es_ref[...])
    lane = lax.broadcasted_iota(jnp.int32, e.shape, 1)
    onehot = [lane == e[:, k:k + 1] for k in range(TOP_K)]
    hits = jnp.zeros(e.shape, F32)
    for oh in onehot:
        hits = hits + oh.astype(F32)
    r = lax.broadcasted_iota(jnp.int32, (RANK_TILE, RANK_TILE), 0)
    c = lax.broadcasted_iota(jnp.int32, (RANK_TILE, RANK_TILE), 1)
    before = (c < r).astype(BF16)
    base = _dot(before, hits.astype(BF16)) + carry_scr[...]
    rank = jnp.zeros(e.shape, jnp.int32)
    for k, oh in enumerate(onehot):
        rk = jnp.sum(jnp.where(oh, base, 0.0), axis=-1, keepdims=True).astype(jnp.int32)
        rank = jnp.where(lane == k, rk, rank)
    dest_ref[...] = rank
    carry_scr[...] = carry_scr[...] + jnp.sum(hits, axis=0, keepdims=True)


def _rank(e_p, e_s, start):
    steps_p = e_p.shape[0] // RANK_TILE
    blk = (RANK_TILE, LANES)
    return pl.pallas_call(
        functools.partial(_rank_kernel, steps_p=steps_p),
        grid=(steps_p + 1,),
        in_specs=[pl.BlockSpec(blk, lambda i: (jnp.minimum(i, steps_p - 1), 0)), _const_spec(blk),
                  _const_spec((1, LANES))],
        out_specs=pl.BlockSpec(blk, lambda i: (i, 0)),
        out_shape=jax.ShapeDtypeStruct(((steps_p + 1) * RANK_TILE, LANES), jnp.int32),
        scratch_shapes=[pltpu.VMEM((1, LANES), F32)],
        compiler_params=pltpu.CompilerParams(dimension_semantics=("arbitrary",)),
    )(e_p, e_s, start)


BLOCK_ROWS = MOE_BLOCK * ROW_TILE


def _row_copy_wait(src_ref, dst_ref, sem, rows):
    pltpu.make_async_copy(src_ref.at[pl.ds(0, rows * ROW_TILE)], dst_ref.at[pl.ds(0, rows * ROW_TILE)], sem).wait()


def _dispatch_kernel(last_blk_ref, nvalid_ref, dest_p_ref, dest_s_ref, xp_ref, xs_ref, xpad_ref,
                     zero_buf, sem, zero_sem, *, tile_p, tile_s, n_blocks):
    i = pl.program_id(0)
    last = pl.num_programs(0) - 1

    def zero_copy(blk):
        dst = xpad_ref.at[pl.ds(pl.multiple_of(blk * BLOCK_ROWS, BLOCK_ROWS), BLOCK_ROWS)]
        return pltpu.make_async_copy(zero_buf, dst, zero_sem)

    @pl.when(i == 0)
    def _():
        zero_buf[...] = jnp.zeros_like(zero_buf)

        def on_pad_blocks(fn):
            for e in range(N_EXPERTS):
                if e == 0:
                    fn(last_blk_ref[0])
                else:
                    pl.when(last_blk_ref[e] != last_blk_ref[e - 1])(functools.partial(fn, last_blk_ref[e]))
            lax.fori_loop(nvalid_ref[0], n_blocks, lambda blk, c: (fn(blk), c)[1], 0)

        on_pad_blocks(lambda blk: zero_copy(blk).start())
        on_pad_blocks(lambda blk: zero_copy(blk).wait())

    def push(dest_ref, x_ref, tile):
        def body(it, carry):
            for j in range(PUSH_UNROLL):
                t = it * PUSH_UNROLL + j
                src = x_ref.at[pl.ds(pl.multiple_of(t * ROW_TILE, ROW_TILE), ROW_TILE)]
                for k in range(TOP_K):
                    row = dest_ref[0, 0, t * TOP_K + k]
                    dst = xpad_ref.at[pl.ds(pl.multiple_of(row * ROW_TILE, ROW_TILE), ROW_TILE)]
                    pltpu.make_async_copy(src, dst, sem).start(priority=k % N_DMA_PRIORITIES)
            return carry

        lax.fori_loop(0, tile // PUSH_UNROLL, body, 0)
        for _ in range(TOP_K):
            _row_copy_wait(x_ref, xpad_ref, sem, tile)

    pl.when(i < last)(lambda: push(dest_p_ref, xp_ref, tile_p))
    pl.when(i == last)(lambda: push(dest_s_ref, xs_ref, tile_s))


def _dispatch(last_blk, nvalid, dest_p, dest_s, xn_p, xn_s, n_blocks, *, tile_p):
    n_p = xn_p.shape[0] // ROW_TILE
    tile_s = xn_s.shape[0] // ROW_TILE
    steps_p = n_p // tile_p
    kern = functools.partial(_dispatch_kernel, tile_p=tile_p, tile_s=tile_s, n_blocks=n_blocks)
    return pl.pallas_call(
        kern,
        grid_spec=pltpu.PrefetchScalarGridSpec(
            num_scalar_prefetch=2,
            grid=(steps_p + 1,),
            in_specs=[
                pl.BlockSpec((1, 1, tile_p * TOP_K), lambda i, lb, nv: (jnp.minimum(i, steps_p - 1), 0, 0),
                             memory_space=pltpu.SMEM),
                pl.BlockSpec((1, 1, tile_s * TOP_K), lambda i, lb, nv: (0, 0, 0), memory_space=pltpu.SMEM),
                pl.BlockSpec((tile_p * ROW_TILE, LANES), lambda i, lb, nv: (jnp.minimum(i, steps_p - 1), 0)),
                pl.BlockSpec((tile_s * ROW_TILE, LANES), lambda i, lb, nv: (0, 0)),
            ],
            out_specs=pl.BlockSpec(memory_space=pl.ANY),
            scratch_shapes=[pltpu.VMEM((BLOCK_ROWS, LANES), F32), pltpu.SemaphoreType.DMA, pltpu.SemaphoreType.DMA],
        ),
        out_shape=jax.ShapeDtypeStruct((n_blocks * BLOCK_ROWS, LANES), F32),
        compiler_params=pltpu.CompilerParams(dimension_semantics=("arbitrary",)),
    )(last_blk, nvalid, dest_p.reshape(steps_p, 1, tile_p * TOP_K), dest_s.reshape(1, 1, tile_s * TOP_K), xn_p, xn_s)


def _expert_kernel(blk_e_ref, nvalid_ref, next_e_ref, x_ref, bg_ref, bu_ref, bd_ref, wg_hbm, wu_hbm, wd_hbm, y_ref,
                   stage, w_bf, sems):
    b = pl.program_id(0)
    valid = b < nvalid_ref[0]
    e = blk_e_ref[b]
    weights = (wg_hbm, wu_hbm, wd_hbm)

    def fetch(expert):
        return [pltpu.make_async_copy(w.at[expert], stage.at[i], sems.at[i]) for i, w in enumerate(weights)]

    @pl.when(b == 0)
    def _():
        for cp in fetch(e):
            cp.start()

    @pl.when(jnp.logical_and(valid, jnp.logical_or(b == 0, blk_e_ref[jnp.maximum(b - 1, 0)] != e)))
    def _():
        for i, cp in enumerate(fetch(e)):
            cp.wait()
            w_bf[i] = stage[i].astype(BF16)

        @pl.when(next_e_ref[b] >= 0)
        def _():
            for cp in fetch(next_e_ref[b]):
                cp.start()

    @pl.when(valid)
    def _():
        x = jnp.concatenate([x_ref[pl.ds(s, MOE_BLOCK, stride=ROW_TILE), :] for s in range(ROW_TILE)], axis=1)
        x = x.astype(BF16)
        g = _dot(x, w_bf[0]) + bg_ref[0]
        u = _dot(x, w_bf[1]) + bu_ref[0]
        g = jnp.minimum(g, SWIGLU_LIMIT)
        u = jnp.clip(u, -SWIGLU_LIMIT, SWIGLU_LIMIT)
        hdn = g * jax.nn.sigmoid(SWIGLU_ALPHA * g) * (u + 1.0)
        y = _dot(hdn.astype(BF16), w_bf[2]) + bd_ref[0]
        for s in range(ROW_TILE):
            y_ref[pl.ds(s, MOE_BLOCK, stride=ROW_TILE), :] = y[:, s * LANES:(s + 1) * LANES]

    @pl.when(b >= nvalid_ref[0])
    def _():
        y_ref[...] = jnp.zeros_like(y_ref)


def _experts(blk_e, nvalid, next_e, xpad, w_gate, b_gate, w_up, b_up, w_down, b_down, n_blocks):
    d = D_MODEL
    n_w = 3
    bspec = pl.BlockSpec((1, 1, d), lambda b, be, nv, ne: (be[b], 0, 0))
    xspec = pl.BlockSpec((BLOCK_ROWS, LANES), lambda b, be, nv, ne: (jnp.minimum(b, nv[0] - 1), 0))
    hbm = pl.BlockSpec(memory_space=pl.ANY)
    return pl.pallas_call(
        _expert_kernel,
        grid_spec=pltpu.PrefetchScalarGridSpec(
            num_scalar_prefetch=3,
            grid=(n_blocks,),
            in_specs=[xspec, bspec, bspec, bspec, hbm, hbm, hbm],
            out_specs=pl.BlockSpec((BLOCK_ROWS, LANES), lambda b, be, nv, ne: (b, 0)),
            scratch_shapes=[pltpu.VMEM((n_w, d, d), F32), pltpu.VMEM((n_w, d, d), BF16),
                            pltpu.SemaphoreType.DMA((n_w,))],
        ),
        out_shape=jax.ShapeDtypeStruct((n_blocks * BLOCK_ROWS, LANES), F32),
        compiler_params=pltpu.CompilerParams(dimension_semantics=("arbitrary",), vmem_limit_bytes=VMEM_LIMIT),
    )(blk_e, nvalid, next_e, xpad, b_gate.reshape(N_EXPERTS, 1, d), b_up.reshape(N_EXPERTS, 1, d),
      b_down.reshape(N_EXPERTS, 1, d), w_gate, w_up, w_down)


def _combine_kernel(dest_ref, dest_next_ref, h_ref, gate_ref, gfin_ref, ypad_ref, y_ref, buf_a, buf_b, sems, *, tile):
    i = pl.program_id(0)
    n_steps = pl.num_programs(0)

    def gather(dref, buf, sem, t, k):
        row = dref[0, 0, t * TOP_K + k]
        slot = (k * tile + t) * ROW_TILE
        if not isinstance(slot, int):
            slot = pl.multiple_of(slot, ROW_TILE)
        src = ypad_ref.at[pl.ds(pl.multiple_of(row * ROW_TILE, ROW_TILE), ROW_TILE)]
        pltpu.make_async_copy(src, buf.at[pl.ds(slot, ROW_TILE)], sem).start(priority=k % N_DMA_PRIORITIES)

    def drain(buf, sem):
        for _ in range(TOP_K):
            _row_copy_wait(ypad_ref, buf, sem, tile)

    @pl.when(i == 0)
    def _():
        def body(t, carry):
            for k in range(TOP_K):
                gather(dest_ref, buf_a, sems.at[0], t, k)
            return carry
        lax.fori_loop(0, tile, body, 0)

    def step(cur, cur_sem, nxt, nxt_sem):
        drain(cur, cur_sem)
        gates = gate_ref[...]
        h = h_ref[...]
        per_slab = tile // ROW_TILE
        cols = []
        for s in range(ROW_TILE):
            for t in range(s * per_slab, (s + 1) * per_slab):
                for k in range(TOP_K):
                    gather(dest_next_ref, nxt, nxt_sem, t, k)
            acc = h[:, s * LANES:(s + 1) * LANES]
            for k in range(TOP_K):
                acc = acc + gates[:, k:k + 1] * cur[pl.ds(k * tile * ROW_TILE + s, tile, stride=ROW_TILE), :]
            cols.append(acc)
        y_ref[...] = _rms(jnp.concatenate(cols, axis=1), gfin_ref[...])

    pl.when(i % 2 == 0)(lambda: step(buf_a, sems.at[0], buf_b, sems.at[1]))
    pl.when(i % 2 == 1)(lambda: step(buf_b, sems.at[1], buf_a, sems.at[0]))

    @pl.when(i == n_steps - 1)
    def _():
        pl.when(i % 2 == 0)(lambda: drain(buf_b, sems.at[1]))
        pl.when(i % 2 == 1)(lambda: drain(buf_a, sems.at[0]))


def _combine(dest, h2, gates, g_final, ypad, *, tile):
    n, d = h2.shape
    steps = n // tile
    kern = functools.partial(_combine_kernel, tile=tile)
    dest3 = dest.reshape(steps, 1, tile * TOP_K)
    buf = pltpu.VMEM((TOP_K * tile * ROW_TILE, LANES), F32)
    return pl.pallas_call(
        kern,
        grid=(steps,),
        in_specs=[
            pl.BlockSpec((1, 1, tile * TOP_K), lambda i: (i, 0, 0), memory_space=pltpu.SMEM),
            pl.BlockSpec((1, 1, tile * TOP_K), lambda i: (jnp.minimum(i + 1, steps - 1), 0, 0),
                         memory_space=pltpu.SMEM),
            pl.BlockSpec((tile, d), lambda i: (i, 0)),
            pl.BlockSpec((tile, LANES), lambda i: (i, 0)),
            _const_spec((1, d)),
            pl.BlockSpec(memory_space=pl.ANY),
        ],
        out_specs=pl.BlockSpec((tile, d), lambda i: (i, 0)),
        out_shape=jax.ShapeDtypeStruct((n, d), F32),
        scratch_shapes=[buf, buf, pltpu.SemaphoreType.DMA((2,))],
        compiler_params=pltpu.CompilerParams(dimension_semantics=("arbitrary",), vmem_limit_bytes=VMEM_LIMIT),
    )(dest3, dest3, h2, gates, g_final, ypad)


def _ssm_params(a_re, a_im, log_dt, b_re, b_im, c_re, c_im):
    dt = jnp.exp(log_dt)[:, None]
    mag = jnp.exp(a_re * dt)
    lb_re = mag * jnp.cos(a_im * dt)
    lb_im = mag * jnp.sin(a_im * dt)
    den = a_re * a_re + a_im * a_im
    q_re = ((lb_re - 1.0) * a_re + lb_im * a_im) / den
    q_im = (lb_im * a_re - (lb_re - 1.0) * a_im) / den
    bb_re = q_re[:, :, None] * b_re - q_im[:, :, None] * b_im
    bb_im = q_re[:, :, None] * b_im + q_im[:, :, None] * b_re
    eye = jnp.eye(SSM_PACK, dtype=F32)

    def pack_blockdiag(m):
        m = m.reshape(N_PACKS, SSM_PACK, m.shape[1], m.shape[2])
        return jnp.einsum('ngab,gh->ngahb', m, eye).reshape(N_PACKS, SSM_PACK * m.shape[2], SSM_PACK * m.shape[3])

    pw_re, pw_im = [jnp.ones_like(lb_re), lb_re], [jnp.zeros_like(lb_im), lb_im]
    for _ in range(RUN - 1):
        r, i = pw_re[-1], pw_im[-1]
        pw_re.append(r * lb_re - i * lb_im)
        pw_im.append(r * lb_im + i * lb_re)
    b_rows = []
    for dd in range(RUN):
        s_re = bb_re * pw_re[dd][:, :, None] - bb_im * pw_im[dd][:, :, None]
        s_im = bb_re * pw_im[dd][:, :, None] + bb_im * pw_re[dd][:, :, None]
        b_rows.append(jnp.concatenate([pack_blockdiag(s_re.transpose(0, 2, 1)),
                                       pack_blockdiag(s_im.transpose(0, 2, 1))], axis=2))
    bmat = jnp.concatenate(b_rows, axis=1)
    cmat_re = pack_blockdiag(c_re.transpose(0, 2, 1))
    cmat_im = pack_blockdiag(-c_im.transpose(0, 2, 1))
    rows_of = lambda pw: [pw[(r % RUN) + 1].reshape(-1) for r in range(SUBLANES)]
    slabbed = lambda rows: jnp.stack(rows).reshape(SUBLANES, N_SLABS, SLAB).transpose(1, 0, 2)
    return (bmat.astype(BF16), cmat_re.astype(BF16), cmat_im.astype(BF16),
            slabbed(rows_of(pw_re)), slabbed(rows_of(pw_im)))


def kernel(x_prompt, x_sample, mem_prompt, cache_mem_k, cache_mem_v, cache_pool, state_ssm_re, state_ssm_im, norm_mix, w_in, w_pool, pool_scale, ssm_a_re, ssm_a_im, ssm_log_dt, ssm_b_re, ssm_b_im, ssm_c_re, ssm_c_im, ssm_d, w_glu, b_glu, w_out, norm_xattn, norm_mem, w_q, w_k, w_v, w_o, norm_ffn, w_router, b_router, w_gate, b_gate, w_up, b_up, w_down, b_down, norm_final):
    assert x_prompt.shape[2] == D_MODEL and norm_mix.shape[0] == 1
    bp, tp, d = x_prompt.shape
    bs, ts, _ = x_sample.shape
    n_p, n_s = bp * tp, bs * ts
    row = lambda v: v.reshape(1, -1)

    bmat, c_re, c_im, t_re, t_im = _ssm_params(ssm_a_re[0], ssm_a_im[0], ssm_log_dt[0], ssm_b_re[0], ssm_b_im[0],
                                               ssm_c_re[0], ssm_c_im[0])
    mix_prm = dict(g_mix=row(norm_mix[0]), w_in=w_in[0].astype(BF16), w_pool=w_pool[0].astype(BF16),
                   pool_scale=row(pool_scale[0]), bmat=bmat, c_re=c_re, c_im=c_im, t_re=t_re, t_im=t_im,
                   d_skip=row(ssm_d[0]), w_glu=w_glu[0].astype(BF16), b_glu=row(b_glu[0]),
                   w_out=w_out[0].astype(BF16))

    attn_prm = dict(g_xattn=row(norm_xattn[0]), w_q=w_q[0].astype(BF16), w_o=w_o[0].astype(BF16),
                    g_ffn=row(norm_ffn[0]),
                    w_router=jnp.pad(w_router[0], ((0, 0), (0, LANES - N_EXPERTS))).astype(BF16),
                    b_router=jnp.pad(row(b_router[0]), ((0, 0), (0, LANES - N_EXPERTS)), constant_values=-jnp.inf))
    slab_state = lambda s: s.reshape(s.shape[0], N_SLABS, 1, SLAB)

    zeros_state = jnp.zeros((bp, N_SLABS, 1, SLAB), F32)
    h1_p, pool_p, sre_p, sim_p = _mixer(x_prompt, jnp.zeros((bp, HIST_ROWS, D_POOL), F32), zeros_state, zeros_state,
                                        mix_prm, tt=512, pos0=0)
    mk, mv = _memkv(mem_prompt, row(norm_mem[0]), w_k[0].astype(BF16), w_v[0].astype(BF16))
    h2_p, xn_p, e_p, g_p, cnt_p = _attn(h1_p, mk, mv, attn_prm, tt=1024)

    hist_s = jnp.pad(cache_pool[0], ((0, 0), (HIST_ROWS - POOL_BUF, 0), (0, 0)))
    h1_s, pool_s, sre_s, sim_s = _mixer(x_sample, hist_s, slab_state(state_ssm_re[0]), slab_state(state_ssm_im[0]),
                                        mix_prm, tt=ts, pos0=PAST_LEN)
    ck = cache_mem_k[0].reshape(bs, N_MEM, d)
    cv = cache_mem_v[0].reshape(bs, N_MEM, d)
    h2_s, xn_s, e_s, g_s, cnt_s = _attn(h1_s, ck, cv, attn_prm, tt=ts)

    n_all = n_p + n_s
    counts = (cnt_p + cnt_s)[0, :N_EXPERTS].astype(jnp.int32)
    padded = (counts + MOE_BLOCK - 1) // MOE_BLOCK * MOE_BLOCK
    pend = jnp.cumsum(padded)
    pstart = pend - padded
    n_blocks = -(-(n_all * TOP_K + N_EXPERTS * (MOE_BLOCK - 1)) // MOE_BLOCK)
    blk_e = jnp.minimum(jnp.sum(pend[None, :] <= (jnp.arange(n_blocks) * MOE_BLOCK)[:, None], axis=1),
                        N_EXPERTS - 1).astype(jnp.int32)
    nvalid = (pend[-1:] // MOE_BLOCK).astype(jnp.int32)
    last_blk = jnp.maximum(pend // MOE_BLOCK - 1, 0).astype(jnp.int32)
    after = (pend // MOE_BLOCK)[blk_e]
    next_e = jnp.where(after < nvalid[0], blk_e[jnp.minimum(after, n_blocks - 1)], -1).astype(jnp.int32)
    start = jnp.pad(pstart.astype(F32), (0, LANES - N_EXPERTS)).reshape(1, LANES)
    e_s_tile = jnp.pad(e_s, ((0, RANK_TILE - n_s), (0, 0)), constant_values=-1)
    dest = _rank(e_p, e_s_tile, start)[:n_all, :TOP_K].reshape(-1)
    dest_p, dest_s = dest[:n_p * TOP_K], dest[n_p * TOP_K:]

    xpad = _dispatch(last_blk, nvalid, dest_p, dest_s, xn_p, xn_s, n_blocks, tile_p=256)
    ypad = _experts(blk_e, nvalid, next_e, xpad, w_gate[0], b_gate[0], w_up[0], b_up[0], w_down[0], b_down[0], n_blocks)
    g_fin = row(norm_final)
    y_p = _combine(dest_p, h2_p.reshape(n_p, d), g_p, g_fin, ypad, tile=256)
    y_s = _combine(dest_s, h2_s.reshape(n_s, d), g_s, g_fin, ypad, tile=n_s)

    unslab = lambda s: s.reshape(1, s.shape[0], N_SSM_GROUPS, SSM_STATE)
    kv5 = lambda a: a.reshape(1, bp, N_MEM, N_XHEADS, XHEAD_DIM)
    return (y_p.reshape(bp, tp, d), y_s.reshape(bs, ts, d), kv5(mk), kv5(mv),
            pool_p[None, :, HIST_ROWS - POOL_BUF:], pool_s[None, :, HIST_ROWS - POOL_BUF:],
            unslab(sre_p), unslab(sim_p), unslab(sre_s), unslab(sim_s))
```

```python
import functools
import math

import jax
import jax.numpy as jnp
from jax import lax
from jax.experimental import pallas as pl
from jax.experimental.pallas import tpu as pltpu

D_MODEL = 1024
D_POOL = 512
D_SSM = 512
POOL_WINDOWS = (2, 4, 8, 16)
POOL_GROUP = 128
HIST_ROWS = 16
POOL_BUF = 15
SSM_GROUP = 16
N_SSM_GROUPS = 32
SSM_STATE = 64
N_STATE = N_SSM_GROUPS * SSM_STATE
N_MEM = 256
N_XHEADS = 4
XHEAD_DIM = 256
N_EXPERTS = 32
TOP_K = 4
SWIGLU_LIMIT = 7.0
SWIGLU_ALPHA = 1.702
MOE_BLOCK = 512
EPS = 1e-6
PAST_LEN = 1024

LANES = 128
SUBLANES = 8
SLAB = 256
N_SLABS = N_STATE // SLAB
SSM_PACK = LANES // SSM_GROUP
N_PACKS = N_SSM_GROUPS // SSM_PACK
PACK_STATES = SSM_PACK * SSM_STATE
SLABS_PER_PACK = PACK_STATES // SLAB
RUN = 4
ROW_TILE = D_MODEL // LANES
N_DMA_PRIORITIES = 2
PUSH_UNROLL = 4
VMEM_LIMIT = 56 * 1024 * 1024

BF16 = jnp.bfloat16
F32 = jnp.float32


def _rms(x, g):
    return x * lax.rsqrt(jnp.mean(x * x, axis=-1, keepdims=True) + EPS) * g


def _dot(a, b):
    return jnp.dot(a, b, preferred_element_type=F32)


def _sigmoid(x):
    return 0.5 * (1.0 + jnp.tanh(0.5 * x))


def _chain_slab(s, tre_ref, tim_ref, hre_scr, him_scr, re_scr, im_scr, tt, fillers=()):
    n_groups = tt // SUBLANES
    every = n_groups // len(fillers) if fillers else 0
    t_re = tre_ref[s]
    t_im = tim_ref[s]
    upper = lax.broadcasted_iota(jnp.int32, (SUBLANES, SLAB), 0) < RUN
    ta_re, ta_im = jnp.where(upper, t_re, 0.0), jnp.where(upper, t_im, 0.0)
    tb_re, tb_im = jnp.where(upper, 0.0, t_re), jnp.where(upper, 0.0, t_im)
    c_re = hre_scr[s]
    c_im = him_scr[s]
    for v in range(n_groups):
        if fillers and v % every == 0 and v // every < len(fillers):
            fillers[v // every]()
        rows = pl.ds(v * SUBLANES, SUBLANES)
        re = re_scr[rows, :]
        im = im_scr[rows, :]
        re, im = re + (ta_re * c_re - ta_im * c_im), im + (ta_re * c_im + ta_im * c_re)
        m_re = re[RUN - 1:RUN, :]
        m_im = im[RUN - 1:RUN, :]
        re, im = re + (tb_re * m_re - tb_im * m_im), im + (tb_re * m_im + tb_im * m_re)
        re_scr[rows, :] = re
        im_scr[rows, :] = im
        c_re = re[SUBLANES - 1:SUBLANES, :]
        c_im = im[SUBLANES - 1:SUBLANES, :]
    hre_scr[s] = c_re
    him_scr[s] = c_im


def _mixer_kernel(x_ref, hist_ref, h0re_ref, h0im_ref, gmix_ref, win_ref, wpool_ref, pscale_ref,
                  bmat_ref, cre_ref, cim_ref, tre_ref, tim_ref, dskip_ref, wglu_ref, bglu_ref, wout_ref,
                  h_ref, poolnew_ref, ssmre_ref, ssmim_ref,
                  hist_scr, hre_scr, him_scr, lhs_scr, u_scr, mix_scr, *slab_scr, tt, pos0):
    bure_scr, buim_scr = slab_scr[:N_SLABS], slab_scr[N_SLABS:]
    j = pl.program_id(1)

    @pl.when(j == 0)
    def _():
        hist_scr[...] = hist_ref[0]
        hre_scr[...] = h0re_ref[0]
        him_scr[...] = h0im_ref[0]

    xn = _rms(x_ref[0], gmix_ref[...])
    z = _dot(xn.astype(BF16), win_ref[...])
    zp = z[:, :D_POOL]
    u = z[:, D_POOL:]
    u_scr[...] = u

    def pool_mixer():
        ext = jnp.concatenate([hist_scr[...], zp], axis=0)
        pos = pos0 + j * tt + lax.broadcasted_iota(jnp.int32, (tt, 1), 0)
        acc = ext
        outs = []
        for gi, w in enumerate(POOL_WINDOWS):
            lo = gi * POOL_GROUP
            acc = acc[:, POOL_GROUP * (1 if gi else 0):]
            acc = acc + pltpu.roll(acc, w // 2, 0)
            wsum = acc[HIST_ROWS:, :POOL_GROUP]
            cnt = jnp.minimum(pos + 1, w).astype(F32)
            d = wsum / cnt - zp[:, lo:lo + POOL_GROUP]
            outs.append(_dot(d.astype(BF16), wpool_ref[gi]))
        hist_scr[...] = ext[tt:tt + HIST_ROWS]
        poolnew_ref[0] = ext[tt:tt + HIST_ROWS]
        mix_scr[:, :D_POOL] = (jnp.concatenate(outs, axis=1) * pscale_ref[...]).astype(BF16)

    run_row = lax.broadcasted_iota(jnp.int32, (tt, 1), 0) % RUN
    for dd in range(RUN):
        sh = u if dd == 0 else jnp.where(run_row >= dd, pltpu.roll(u, dd, 0), 0.0)
        for c in range(N_PACKS):
            lhs_scr[c, :, dd * LANES:(dd + 1) * LANES] = sh[:, c * LANES:(c + 1) * LANES].astype(BF16)

    def projections(c):
        work = []
        for i in range(SLABS_PER_PACK):
            for part, dst in ((0, bure_scr), (1, buim_scr)):
                lo = part * PACK_STATES + i * SLAB

                def piece(lo=lo, ref=dst[c * SLABS_PER_PACK + i]):
                    ref[...] = _dot(lhs_scr[c], bmat_ref[c, :, lo:lo + SLAB])
                work.append(piece)
        return work

    ys = []
    pool_mixer()
    for piece in projections(0):
        piece()
    for c in range(N_PACKS):
        slabs = list(range(c * SLABS_PER_PACK, (c + 1) * SLABS_PER_PACK))
        ahead = projections(c + 1) if c + 1 < N_PACKS else []
        per_slab = len(ahead) // len(slabs)
        for i, s in enumerate(slabs):
            _chain_slab(s, tre_ref, tim_ref, hre_scr, him_scr, bure_scr[s], buim_scr[s], tt,
                        fillers=ahead[i * per_slab:(i + 1) * per_slab])
        hs_re = jnp.concatenate([bure_scr[s][...] for s in slabs], axis=1)
        hs_im = jnp.concatenate([buim_scr[s][...] for s in slabs], axis=1)
        ys.append(_dot(hs_re.astype(BF16), cre_ref[c]) + _dot(hs_im.astype(BF16), cim_ref[c]))
    ssmre_ref[0] = hre_scr[...]
    ssmim_ref[0] = him_scr[...]
    y = jnp.concatenate(ys, axis=1) + dskip_ref[...] * u_scr[...]
    g = 0.5 * y * (1.0 + jnp.tanh(math.sqrt(2.0 / math.pi) * (y + 0.044715 * (y * y * y))))
    y_ssm = g * _sigmoid(_dot(g.astype(BF16), wglu_ref[...]) + bglu_ref[...])
    mix_scr[:, D_POOL:] = y_ssm.astype(BF16)
    h_ref[0] = x_ref[0] + _dot(mix_scr[...], wout_ref[...])


def _const_spec(shape):
    return pl.BlockSpec(shape, lambda *_: (0,) * len(shape))


def _mixer(x, hist, h0re, h0im, prm, *, tt, pos0):
    b, t, d = x.shape
    kern = functools.partial(_mixer_kernel, tt=tt, pos0=pos0)
    per_b3 = lambda shp: pl.BlockSpec((1,) + shp, lambda i, j: (i, 0, 0))
    per_b4 = lambda shp: pl.BlockSpec((1,) + shp, lambda i, j: (i, 0, 0, 0))
    return pl.pallas_call(
        kern,
        grid=(b, t // tt),
        in_specs=[
            pl.BlockSpec((1, tt, d), lambda i, j: (i, j, 0)),
            per_b3((HIST_ROWS, D_POOL)),
            per_b4((N_SLABS, 1, SLAB)),
            per_b4((N_SLABS, 1, SLAB)),
            _const_spec((1, d)),
            _const_spec((d, d)),
            _const_spec((len(POOL_WINDOWS), POOL_GROUP, POOL_GROUP)),
            _const_spec((1, D_POOL)),
            _const_spec((N_PACKS, RUN * LANES, 2 * PACK_STATES)),
            _const_spec((N_PACKS, PACK_STATES, LANES)),
            _const_spec((N_PACKS, PACK_STATES, LANES)),
            _const_spec((N_SLABS, SUBLANES, SLAB)),
            _const_spec((N_SLABS, SUBLANES, SLAB)),
            _const_spec((1, D_SSM)),
            _const_spec((D_SSM, D_SSM)),
            _const_spec((1, D_SSM)),
            _const_spec((d, d)),
        ],
        out_specs=[
            pl.BlockSpec((1, tt, d), lambda i, j: (i, j, 0)),
            per_b3((HIST_ROWS, D_POOL)),
            per_b4((N_SLABS, 1, SLAB)),
            per_b4((N_SLABS, 1, SLAB)),
        ],
        out_shape=[
            jax.ShapeDtypeStruct((b, t, d), F32),
            jax.ShapeDtypeStruct((b, HIST_ROWS, D_POOL), F32),
            jax.ShapeDtypeStruct((b, N_SLABS, 1, SLAB), F32),
            jax.ShapeDtypeStruct((b, N_SLABS, 1, SLAB), F32),
        ],
        scratch_shapes=[
            pltpu.VMEM((HIST_ROWS, D_POOL), F32),
            pltpu.VMEM((N_SLABS, 1, SLAB), F32),
            pltpu.VMEM((N_SLABS, 1, SLAB), F32),
            pltpu.VMEM((N_PACKS, tt, RUN * LANES), BF16),
            pltpu.VMEM((tt, D_SSM), F32),
            pltpu.VMEM((tt, d), BF16),
        ] + [pltpu.VMEM((tt, SLAB), F32)] * (2 * N_SLABS),
        compiler_params=pltpu.CompilerParams(
            dimension_semantics=("arbitrary", "arbitrary"), vmem_limit_bytes=VMEM_LIMIT),
    )(x, hist, h0re, h0im, prm['g_mix'], prm['w_in'], prm['w_pool'], prm['pool_scale'],
      prm['bmat'], prm['c_re'], prm['c_im'], prm['t_re'], prm['t_im'], prm['d_skip'],
      prm['w_glu'], prm['b_glu'], prm['w_out'])


def _memkv_kernel(m_ref, g_ref, wk_ref, wv_ref, k_ref, v_ref):
    m = _rms(m_ref[0], g_ref[...]).astype(BF16)
    k_ref[0] = _dot(m, wk_ref[...])
    v_ref[0] = _dot(m, wv_ref[...])


def _memkv(mem, g, wk, wv):
    b, n, d = mem.shape
    blk = pl.BlockSpec((1, n, d), lambda i: (i, 0, 0))
    return pl.pallas_call(
        _memkv_kernel,
        grid=(b,),
        in_specs=[blk, _const_spec((1, d)), _const_spec((d, d)), _const_spec((d, d))],
        out_specs=[blk, blk],
        out_shape=[jax.ShapeDtypeStruct((b, n, d), F32)] * 2,
        compiler_params=pltpu.CompilerParams(vmem_limit_bytes=VMEM_LIMIT),
    )(mem, g, wk, wv)


def _attn_kernel(h_ref, k_ref, v_ref, gx_ref, wq_ref, wo_ref, gf_ref, wr_ref, br_ref,
                 h2_ref, xn_ref, e_ref, gate_ref, cnt_ref, cnt_scr, kt_scr, v_scr, *, tt):
    @pl.when(pl.program_id(1) == 0)
    def _():
        kt_scr[...] = k_ref[0].T.astype(BF16)
        v_scr[...] = v_ref[0].astype(BF16)

    h = h_ref[0]
    hn = _rms(h, gx_ref[...])
    q = _dot(hn.astype(BF16), wq_ref[...])
    outs = []
    for hd in range(N_XHEADS):
        cols = slice(hd * XHEAD_DIM, (hd + 1) * XHEAD_DIM)
        s = _dot(q[:, cols].astype(BF16), kt_scr[cols, :]) * (XHEAD_DIM ** -0.5)
        p = jnp.exp(s - jnp.max(s, axis=-1, keepdims=True))
        p = p * (1.0 / jnp.sum(p, axis=-1, keepdims=True))
        outs.append(_dot(p.astype(BF16), v_scr[:, cols]))
    o = jnp.concatenate(outs, axis=1)
    h2 = h + _dot(o.astype(BF16), wo_ref[...])
    h2_ref[0] = h2

    xn = _rms(h2, gf_ref[...])
    for s in range(ROW_TILE):
        xn_ref[pl.ds(s, tt, stride=ROW_TILE), :] = xn[:, s * LANES:(s + 1) * LANES]
    logits = _dot(xn.astype(BF16), wr_ref[...]) + br_ref[...]
    lane = lax.broadcasted_iota(jnp.int32, logits.shape, 1)
    lane_f = lane.astype(F32)
    e_out = jnp.zeros(logits.shape, jnp.int32)
    hits = jnp.zeros(logits.shape, F32)
    top = []
    for k in range(TOP_K):
        m = jnp.max(logits, axis=-1, keepdims=True)
        idx = jnp.min(jnp.where(logits == m, lane_f, float(LANES)), axis=-1, keepdims=True)
        e_out = jnp.where(lane == k, idx.astype(jnp.int32), e_out)
        top.append(m)
        chosen = lane_f == idx
        hits = hits + chosen.astype(F32)
        logits = jnp.where(chosen, -jnp.inf, logits)

    @pl.when(jnp.logical_and(pl.program_id(0) == 0, pl.program_id(1) == 0))
    def _():
        cnt_scr[...] = jnp.zeros_like(cnt_scr)
    cnt_scr[...] = cnt_scr[...] + jnp.sum(hits, axis=0, keepdims=True)
    cnt_ref[...] = cnt_scr[...]
    ex = [jnp.exp(m - top[0]) for m in top]
    tot = ex[0] + ex[1] + ex[2] + ex[3]
    g_out = jnp.zeros(logits.shape, F32)
    for k in range(TOP_K):
        g_out = jnp.where(lane == k, ex[k] / tot, g_out)
    e_ref[...] = e_out
    gate_ref[...] = g_out


def _attn(h, k, v, prm, *, tt):
    b, t, d = h.shape
    n = b * t
    kern = functools.partial(_attn_kernel, tt=tt)
    nt = t // tt
    kv = pl.BlockSpec((1, N_MEM, d), lambda i, j: (i, 0, 0))
    tok = lambda width: pl.BlockSpec((tt, width), lambda i, j: (i * nt + j, 0))
    return pl.pallas_call(
        kern,
        grid=(b, nt),
        in_specs=[
            pl.BlockSpec((1, tt, d), lambda i, j: (i, j, 0)), kv, kv,
            _const_spec((1, d)), _const_spec((d, d)), _const_spec((d, d)),
            _const_spec((1, d)), _const_spec((d, LANES)), _const_spec((1, LANES)),
        ],
        out_specs=[
            pl.BlockSpec((1, tt, d), lambda i, j: (i, j, 0)),
            pl.BlockSpec((tt * ROW_TILE, LANES), lambda i, j: (i * nt + j, 0)),
            tok(LANES), tok(LANES), _const_spec((1, LANES)),
        ],
        out_shape=[
            jax.ShapeDtypeStruct((b, t, d), F32),
            jax.ShapeDtypeStruct((n * ROW_TILE, LANES), F32),
            jax.ShapeDtypeStruct((n, LANES), jnp.int32),
            jax.ShapeDtypeStruct((n, LANES), F32),
            jax.ShapeDtypeStruct((1, LANES), F32),
        ],
        scratch_shapes=[pltpu.VMEM((1, LANES), F32), pltpu.VMEM((d, N_MEM), BF16), pltpu.VMEM((N_MEM, d), BF16)],
        compiler_params=pltpu.CompilerParams(
            dimension_semantics=("arbitrary", "arbitrary"), vmem_limit_bytes=VMEM_LIMIT),
    )(h, k, v, prm['g_xattn'], prm['w_q'], prm['w_o'], prm['g_ffn'], prm['w_router'], prm['b_router'])


RANK_TILE = 512


def _rank_kernel(ep_ref, es_ref, start_ref, dest_ref, carry_scr, *, steps_p):
    i = pl.program_id(0)

    @pl.when(i == 0)
    def _():
        carry_scr[...] = start_ref[...]

    e = jnp.where(i < steps_p, ep_ref[...], es_ref[...])
    lane = lax.broadcasted_iota(jnp.int32, e.shape, 1)
    onehot = [lane == e[:, k:k + 1] for k in range(TOP_K)]
    hits = jnp.zeros(e.shape, F32)
    for oh in onehot:
        hits = hits + oh.astype(F32)
    r = lax.broadcasted_iota(jnp.int32, (RANK_TILE, RANK_TILE), 0)
    c = lax.broadcasted_iota(jnp.int32, (RANK_TILE, RANK_TILE), 1)
    before = (c < r).astype(BF16)
    base = _dot(before, hits.astype(BF16)) + carry_scr[...]
    rank = jnp.zeros(e.shape, jnp.int32)
    for k, oh in enumerate(onehot):
        rk = jnp.sum(jnp.where(oh, base, 0.0), axis=-1, keepdims=True).astype(jnp.int32)
        rank = jnp.where(lane == k, rk, rank)
    dest_ref[...] = rank
    carry_scr[...] = carry_scr[...] + jnp.sum(hits, axis=0, keepdims=True)


def _rank(e_p, e_s, start):
    steps_p = e_p.shape[0] // RANK_TILE
    blk = (RANK_TILE, LANES)
    return pl.pallas_call(
        functools.partial(_rank_kernel, steps_p=steps_p),
        grid=(steps_p + 1,),
        in_specs=[pl.BlockSpec(blk, lambda i: (jnp.minimum(i, steps_p - 1), 0)), _const_spec(blk),
                  _const_spec((1, LANES))],
        out_specs=pl.BlockSpec(blk, lambda i: (i, 0)),
        out_shape=jax.ShapeDtypeStruct(((steps_p + 1) * RANK_TILE, LANES), jnp.int32),
        scratch_shapes=[pltpu.VMEM((1, LANES), F32)],
        compiler_params=pltpu.CompilerParams(dimension_semantics=("arbitrary",)),
    )(e_p, e_s, start)


BLOCK_ROWS = MOE_BLOCK * ROW_TILE


def _row_copy_wait(src_ref, dst_ref, sem, rows):
    pltpu.make_async_copy(src_ref.at[pl.ds(0, rows * ROW_TILE)], dst_ref.at[pl.ds(0, rows * ROW_TILE)], sem).wait()


def _dispatch_kernel(last_blk_ref, nvalid_ref, dest_p_ref, dest_s_ref, xp_ref, xs_ref, xpad_ref,
                     zero_buf, sem, zero_sem, *, tile_p, tile_s, n_blocks):
    i = pl.program_id(0)
    last = pl.num_programs(0) - 1

    def zero_copy(blk):
        dst = xpad_ref.at[pl.ds(pl.multiple_of(blk * BLOCK_ROWS, BLOCK_ROWS), BLOCK_ROWS)]
        return pltpu.make_async_copy(zero_buf, dst, zero_sem)

    @pl.when(i == 0)
    def _():
        zero_buf[...] = jnp.zeros_like(zero_buf)

        def on_pad_blocks(fn):
            for e in range(N_EXPERTS):
                if e == 0:
                    fn(last_blk_ref[0])
                else:
                    pl.when(last_blk_ref[e] != last_blk_ref[e - 1])(functools.partial(fn, last_blk_ref[e]))
            lax.fori_loop(nvalid_ref[0], n_blocks, lambda blk, c: (fn(blk), c)[1], 0)

        on_pad_blocks(lambda blk: zero_copy(blk).start())
        on_pad_blocks(lambda blk: zero_copy(blk).wait())

    def push(dest_ref, x_ref, tile):
        def body(it, carry):
            for j in range(PUSH_UNROLL):
                t = it * PUSH_UNROLL + j
                src = x_ref.at[pl.ds(pl.multiple_of(t * ROW_TILE, ROW_TILE), ROW_TILE)]
                for k in range(TOP_K):
                    row = dest_ref[0, 0, t * TOP_K + k]
                    dst = xpad_ref.at[pl.ds(pl.multiple_of(row * ROW_TILE, ROW_TILE), ROW_TILE)]
                    pltpu.make_async_copy(src, dst, sem).start(priority=k % N_DMA_PRIORITIES)
            return carry

        lax.fori_loop(0, tile // PUSH_UNROLL, body, 0)
        for _ in range(TOP_K):
            _row_copy_wait(x_ref, xpad_ref, sem, tile)

    pl.when(i < last)(lambda: push(dest_p_ref, xp_ref, tile_p))
    pl.when(i == last)(lambda: push(dest_s_ref, xs_ref, tile_s))


def _dispatch(last_blk, nvalid, dest_p, dest_s, xn_p, xn_s, n_blocks, *, tile_p):
    n_p = xn_p.shape[0] // ROW_TILE
    tile_s = xn_s.shape[0] // ROW_TILE
    steps_p = n_p // tile_p
    kern = functools.partial(_dispatch_kernel, tile_p=tile_p, tile_s=tile_s, n_blocks=n_blocks)
    return pl.pallas_call(
        kern,
        grid_spec=pltpu.PrefetchScalarGridSpec(
            num_scalar_prefetch=2,
            grid=(steps_p + 1,),
            in_specs=[
                pl.BlockSpec((1, 1, tile_p * TOP_K), lambda i, lb, nv: (jnp.minimum(i, steps_p - 1), 0, 0),
                             memory_space=pltpu.SMEM),
                pl.BlockSpec((1, 1, tile_s * TOP_K), lambda i, lb, nv: (0, 0, 0), memory_space=pltpu.SMEM),
                pl.BlockSpec((tile_p * ROW_TILE, LANES), lambda i, lb, nv: (jnp.minimum(i, steps_p - 1), 0)),
                pl.BlockSpec((tile_s * ROW_TILE, LANES), lambda i, lb, nv: (0, 0)),
            ],
            out_specs=pl.BlockSpec(memory_space=pl.ANY),
            scratch_shapes=[pltpu.VMEM((BLOCK_ROWS, LANES), F32), pltpu.SemaphoreType.DMA, pltpu.SemaphoreType.DMA],
        ),
        out_shape=jax.ShapeDtypeStruct((n_blocks * BLOCK_ROWS, LANES), F32),
        compiler_params=pltpu.CompilerParams(dimension_semantics=("arbitrary",)),
    )(last_blk, nvalid, dest_p.reshape(steps_p, 1, tile_p * TOP_K), dest_s.reshape(1, 1, tile_s * TOP_K), xn_p, xn_s)


def _expert_kernel(blk_e_ref, nvalid_ref, next_e_ref, x_ref, bg_ref, bu_ref, bd_ref, wg_hbm, wu_hbm, wd_hbm, y_ref,
                   stage, w_bf, sems):
    b = pl.program_id(0)
    valid = b < nvalid_ref[0]
    e = blk_e_ref[b]
    weights = (wg_hbm, wu_hbm, wd_hbm)

    def fetch(expert):
        return [pltpu.make_async_copy(w.at[expert], stage.at[i], sems.at[i]) for i, w in enumerate(weights)]

    @pl.when(b == 0)
    def _():
        for cp in fetch(e):
            cp.start()

    @pl.when(jnp.logical_and(valid, jnp.logical_or(b == 0, blk_e_ref[jnp.maximum(b - 1, 0)] != e)))
    def _():
        for i, cp in enumerate(fetch(e)):
            cp.wait()
            w_bf[i] = stage[i].astype(BF16)

        @pl.when(next_e_ref[b] >= 0)
        def _():
            for cp in fetch(next_e_ref[b]):
                cp.start()

    @pl.when(valid)
    def _():
        x = jnp.concatenate([x_ref[pl.ds(s, MOE_BLOCK, stride=ROW_TILE), :] for s in range(ROW_TILE)], axis=1)
        x = x.astype(BF16)
        g = _dot(x, w_bf[0]) + bg_ref[0]
        u = _dot(x, w_bf[1]) + bu_ref[0]
        g = jnp.minimum(g, SWIGLU_LIMIT)
        u = jnp.clip(u, -SWIGLU_LIMIT, SWIGLU_LIMIT)
        hdn = g * _sigmoid(SWIGLU_ALPHA * g) * (u + 1.0)
        y = _dot(hdn.astype(BF16), w_bf[2]) + bd_ref[0]
        for s in range(ROW_TILE):
            y_ref[pl.ds(s, MOE_BLOCK, stride=ROW_TILE), :] = y[:, s * LANES:(s + 1) * LANES]

    @pl.when(b >= nvalid_ref[0])
    def _():
        y_ref[...] = jnp.zeros_like(y_ref)


def _experts(blk_e, nvalid, next_e, xpad, w_gate, b_gate, w_up, b_up, w_down, b_down, n_blocks):
    d = D_MODEL
    n_w = 3
    bspec = pl.BlockSpec((1, 1, d), lambda b, be, nv, ne: (be[b], 0, 0))
    xspec = pl.BlockSpec((BLOCK_ROWS, LANES), lambda b, be, nv, ne: (jnp.minimum(b, nv[0] - 1), 0))
    hbm = pl.BlockSpec(memory_space=pl.ANY)
    return pl.pallas_call(
        _expert_kernel,
        grid_spec=pltpu.PrefetchScalarGridSpec(
            num_scalar_prefetch=3,
            grid=(n_blocks,),
            in_specs=[xspec, bspec, bspec, bspec, hbm, hbm, hbm],
            out_specs=pl.BlockSpec((BLOCK_ROWS, LANES), lambda b, be, nv, ne: (b, 0)),
            scratch_shapes=[pltpu.VMEM((n_w, d, d), F32), pltpu.VMEM((n_w, d, d), BF16),
                            pltpu.SemaphoreType.DMA((n_w,))],
        ),
        out_shape=jax.ShapeDtypeStruct((n_blocks * BLOCK_ROWS, LANES), F32),
        compiler_params=pltpu.CompilerParams(dimension_semantics=("arbitrary",), vmem_limit_bytes=VMEM_LIMIT),
    )(blk_e, nvalid, next_e, xpad, b_gate.reshape(N_EXPERTS, 1, d), b_up.reshape(N_EXPERTS, 1, d),
      b_down.reshape(N_EXPERTS, 1, d), w_gate, w_up, w_down)


def _combine_kernel(dest_ref, dest_next_ref, h_ref, gate_ref, gfin_ref, ypad_ref, y_ref, buf_a, buf_b, sems, *, tile):
    i = pl.program_id(0)
    n_steps = pl.num_programs(0)

    def gather(dref, buf, sem, t, k):
        row = dref[0, 0, t * TOP_K + k]
        slot = (k * tile + t) * ROW_TILE
        if not isinstance(slot, int):
            slot = pl.multiple_of(slot, ROW_TILE)
        src = ypad_ref.at[pl.ds(pl.multiple_of(row * ROW_TILE, ROW_TILE), ROW_TILE)]
        pltpu.make_async_copy(src, buf.at[pl.ds(slot, ROW_TILE)], sem).start(priority=k % N_DMA_PRIORITIES)

    def drain(buf, sem):
        for _ in range(TOP_K):
            _row_copy_wait(ypad_ref, buf, sem, tile)

    @pl.when(i == 0)
    def _():
        def body(t, carry):
            for k in range(TOP_K):
                gather(dest_ref, buf_a, sems.at[0], t, k)
            return carry
        lax.fori_loop(0, tile, body, 0)

    def step(cur, cur_sem, nxt, nxt_sem):
        drain(cur, cur_sem)
        gates = gate_ref[...]
        h = h_ref[...]
        per_slab = tile // ROW_TILE
        cols = []
        for s in range(ROW_TILE):
            for t in range(s * per_slab, (s + 1) * per_slab):
                for k in range(TOP_K):
                    gather(dest_next_ref, nxt, nxt_sem, t, k)
            acc = h[:, s * LANES:(s + 1) * LANES]
            for k in range(TOP_K):
                acc = acc + gates[:, k:k + 1] * cur[pl.ds(k * tile * ROW_TILE + s, tile, stride=ROW_TILE), :]
            cols.append(acc)
        y_ref[...] = _rms(jnp.concatenate(cols, axis=1), gfin_ref[...])

    pl.when(i % 2 == 0)(lambda: step(buf_a, sems.at[0], buf_b, sems.at[1]))
    pl.when(i % 2 == 1)(lambda: step(buf_b, sems.at[1], buf_a, sems.at[0]))

    @pl.when(i == n_steps - 1)
    def _():
        pl.when(i % 2 == 0)(lambda: drain(buf_b, sems.at[1]))
        pl.when(i % 2 == 1)(lambda: drain(buf_a, sems.at[0]))


def _combine(dest, h2, gates, g_final, ypad, *, tile):
    n, d = h2.shape
    steps = n // tile
    kern = functools.partial(_combine_kernel, tile=tile)
    dest3 = dest.reshape(steps, 1, tile * TOP_K)
    buf = pltpu.VMEM((TOP_K * tile * ROW_TILE, LANES), F32)
    return pl.pallas_call(
        kern,
        grid=(steps,),
        in_specs=[
            pl.BlockSpec((1, 1, tile * TOP_K), lambda i: (i, 0, 0), memory_space=pltpu.SMEM),
            pl.BlockSpec((1, 1, tile * TOP_K), lambda i: (jnp.minimum(i + 1, steps - 1), 0, 0),
                         memory_space=pltpu.SMEM),
            pl.BlockSpec((tile, d), lambda i: (i, 0)),
            pl.BlockSpec((tile, LANES), lambda i: (i, 0)),
            _const_spec((1, d)),
            pl.BlockSpec(memory_space=pl.ANY),
        ],
        out_specs=pl.BlockSpec((tile, d), lambda i: (i, 0)),
        out_shape=jax.ShapeDtypeStruct((n, d), F32),
        scratch_shapes=[buf, buf, pltpu.SemaphoreType.DMA((2,))],
        compiler_params=pltpu.CompilerParams(dimension_semantics=("arbitrary",), vmem_limit_bytes=VMEM_LIMIT),
    )(dest3, dest3, h2, gates, g_final, ypad)


def _ssm_params(a_re, a_im, log_dt, b_re, b_im, c_re, c_im):
    dt = jnp.exp(log_dt)[:, None]
    mag = jnp.exp(a_re * dt)
    lb_re = mag * jnp.cos(a_im * dt)
    lb_im = mag * jnp.sin(a_im * dt)
    den = a_re * a_re + a_im * a_im
    q_re = ((lb_re - 1.0) * a_re + lb_im * a_im) / den
    q_im = (lb_im * a_re - (lb_re - 1.0) * a_im) / den
    bb_re = q_re[:, :, None] * b_re - q_im[:, :, None] * b_im
    bb_im = q_re[:, :, None] * b_im + q_im[:, :, None] * b_re
    eye = jnp.eye(SSM_PACK, dtype=F32)

    def pack_blockdiag(m):
        m = m.reshape(N_PACKS, SSM_PACK, m.shape[1], m.shape[2])
        return jnp.einsum('ngab,gh->ngahb', m, eye).reshape(N_PACKS, SSM_PACK * m.shape[2], SSM_PACK * m.shape[3])

    pw_re, pw_im = [jnp.ones_like(lb_re), lb_re], [jnp.zeros_like(lb_im), lb_im]
    for _ in range(RUN - 1):
        r, i = pw_re[-1], pw_im[-1]
        pw_re.append(r * lb_re - i * lb_im)
        pw_im.append(r * lb_im + i * lb_re)
    b_rows = []
    for dd in range(RUN):
        s_re = bb_re * pw_re[dd][:, :, None] - bb_im * pw_im[dd][:, :, None]
        s_im = bb_re * pw_im[dd][:, :, None] + bb_im * pw_re[dd][:, :, None]
        b_rows.append(jnp.concatenate([pack_blockdiag(s_re.transpose(0, 2, 1)),
                                       pack_blockdiag(s_im.transpose(0, 2, 1))], axis=2))
    bmat = jnp.concatenate(b_rows, axis=1)
    cmat_re = pack_blockdiag(c_re.transpose(0, 2, 1))
    cmat_im = pack_blockdiag(-c_im.transpose(0, 2, 1))
    rows_of = lambda pw: [pw[(r % RUN) + 1].reshape(-1) for r in range(SUBLANES)]
    slabbed = lambda rows: jnp.stack(rows).reshape(SUBLANES, N_SLABS, SLAB).transpose(1, 0, 2)
    return (bmat.astype(BF16), cmat_re.astype(BF16), cmat_im.astype(BF16),
            slabbed(rows_of(pw_re)), slabbed(rows_of(pw_im)))


def kernel(x_prompt, x_sample, mem_prompt, cache_mem_k, cache_mem_v, cache_pool, state_ssm_re, state_ssm_im, norm_mix, w_in, w_pool, pool_scale, ssm_a_re, ssm_a_im, ssm_log_dt, ssm_b_re, ssm_b_im, ssm_c_re, ssm_c_im, ssm_d, w_glu, b_glu, w_out, norm_xattn, norm_mem, w_q, w_k, w_v, w_o, norm_ffn, w_router, b_router, w_gate, b_gate, w_up, b_up, w_down, b_down, norm_final):
    assert x_prompt.shape[2] == D_MODEL and norm_mix.shape[0] == 1
    bp, tp, d = x_prompt.shape
    bs, ts, _ = x_sample.shape
    n_p, n_s = bp * tp, bs * ts
    row = lambda v: v.reshape(1, -1)

    bmat, c_re, c_im, t_re, t_im = _ssm_params(ssm_a_re[0], ssm_a_im[0], ssm_log_dt[0], ssm_b_re[0], ssm_b_im[0],
                                               ssm_c_re[0], ssm_c_im[0])
    mix_prm = dict(g_mix=row(norm_mix[0]), w_in=w_in[0].astype(BF16), w_pool=w_pool[0].astype(BF16),
                   pool_scale=row(pool_scale[0]), bmat=bmat, c_re=c_re, c_im=c_im, t_re=t_re, t_im=t_im,
                   d_skip=row(ssm_d[0]), w_glu=w_glu[0].astype(BF16), b_glu=row(b_glu[0]),
                   w_out=w_out[0].astype(BF16))

    attn_prm = dict(g_xattn=row(norm_xattn[0]), w_q=w_q[0].astype(BF16), w_o=w_o[0].astype(BF16),
                    g_ffn=row(norm_ffn[0]),
                    w_router=jnp.pad(w_router[0], ((0, 0), (0, LANES - N_EXPERTS))).astype(BF16),
                    b_router=jnp.pad(row(b_router[0]), ((0, 0), (0, LANES - N_EXPERTS)), constant_values=-jnp.inf))
    slab_state = lambda s: s.reshape(s.shape[0], N_SLABS, 1, SLAB)

    zeros_state = jnp.zeros((bp, N_SLABS, 1, SLAB), F32)
    h1_p, pool_p, sre_p, sim_p = _mixer(x_prompt, jnp.zeros((bp, HIST_ROWS, D_POOL), F32), zeros_state, zeros_state,
                                        mix_prm, tt=512, pos0=0)
    mk, mv = _memkv(mem_prompt, row(norm_mem[0]), w_k[0].astype(BF16), w_v[0].astype(BF16))
    h2_p, xn_p, e_p, g_p, cnt_p = _attn(h1_p, mk, mv, attn_prm, tt=1024)

    hist_s = jnp.pad(cache_pool[0], ((0, 0), (HIST_ROWS - POOL_BUF, 0), (0, 0)))
    h1_s, pool_s, sre_s, sim_s = _mixer(x_sample, hist_s, slab_state(state_ssm_re[0]), slab_state(state_ssm_im[0]),
                                        mix_prm, tt=ts, pos0=PAST_LEN)
    ck = cache_mem_k[0].reshape(bs, N_MEM, d)
    cv = cache_mem_v[0].reshape(bs, N_MEM, d)
    h2_s, xn_s, e_s, g_s, cnt_s = _attn(h1_s, ck, cv, attn_prm, tt=ts)

    n_all = n_p + n_s
    counts = (cnt_p + cnt_s)[0, :N_EXPERTS].astype(jnp.int32)
    padded = (counts + MOE_BLOCK - 1) // MOE_BLOCK * MOE_BLOCK
    pend = jnp.cumsum(padded)
    pstart = pend - padded
    n_blocks = -(-(n_all * TOP_K + N_EXPERTS * (MOE_BLOCK - 1)) // MOE_BLOCK)
    blk_e = jnp.minimum(jnp.sum(pend[None, :] <= (jnp.arange(n_blocks) * MOE_BLOCK)[:, None], axis=1),
                        N_EXPERTS - 1).astype(jnp.int32)
    nvalid = (pend[-1:] // MOE_BLOCK).astype(jnp.int32)
    last_blk = jnp.maximum(pend // MOE_BLOCK - 1, 0).astype(jnp.int32)
    after = (pend // MOE_BLOCK)[blk_e]
    next_e = jnp.where(after < nvalid[0], blk_e[jnp.minimum(after, n_blocks - 1)], -1).astype(jnp.int32)
    start = jnp.pad(pstart.astype(F32), (0, LANES - N_EXPERTS)).reshape(1, LANES)
    e_s_tile = jnp.pad(e_s, ((0, RANK_TILE - n_s), (0, 0)), constant_values=-1)
    dest = _rank(e_p, e_s_tile, start)[:n_all, :TOP_K].reshape(-1)
    dest_p, dest_s = dest[:n_p * TOP_K], dest[n_p * TOP_K:]

    xpad = _dispatch(last_blk, nvalid, dest_p, dest_s, xn_p, xn_s, n_blocks, tile_p=256)
    ypad = _experts(blk_e, nvalid, next_e, xpad, w_gate[0], b_gate[0], w_up[0], b_up[0], w_down[0], b_down[0], n_blocks)
    g_fin = row(norm_final)
    y_p = _combine(dest_p, h2_p.reshape(n_p, d), g_p, g_fin, ypad, tile=256)
    y_s = _combine(dest_s, h2_s.reshape(n_s, d), g_s, g_fin, ypad, tile=n_s)

    unslab = lambda s: s.reshape(1, s.shape[0], N_SSM_GROUPS, SSM_STATE)
    kv5 = lambda a: a.reshape(1, bp, N_MEM, N_XHEADS, XHEAD_DIM)
    return (y_p.reshape(bp, tp, d), y_s.reshape(bs, ts, d), kv5(mk), kv5(mv),
            pool_p[None, :, HIST_ROWS - POOL_BUF:], pool_s[None, :, HIST_ROWS - POOL_BUF:],
            unslab(sre_p), unslab(sim_p), unslab(sre_s), unslab(sim_s))
```

```python
import functools
import math

import jax
import jax.numpy as jnp
from jax import lax
from jax.experimental import pallas as pl
from jax.experimental.pallas import tpu as pltpu

D_MODEL = 1024
D_POOL = 512
D_SSM = 512
POOL_WINDOWS = (2, 4, 8, 16)
POOL_GROUP = 128
HIST_ROWS = 16
POOL_BUF = 15
SSM_GROUP = 16
N_SSM_GROUPS = 32
SSM_STATE = 64
N_STATE = N_SSM_GROUPS * SSM_STATE
N_MEM = 256
N_XHEADS = 4
XHEAD_DIM = 256
N_EXPERTS = 32
TOP_K = 4
SWIGLU_LIMIT = 7.0
SWIGLU_ALPHA = 1.702
MOE_BLOCK = 512
EPS = 1e-6
PAST_LEN = 1024

LANES = 128
SUBLANES = 8
SLAB = 256
N_SLABS = N_STATE // SLAB
SSM_PACK = LANES // SSM_GROUP
N_PACKS = N_SSM_GROUPS // SSM_PACK
PACK_STATES = SSM_PACK * SSM_STATE
SLABS_PER_PACK = PACK_STATES // SLAB
RUN = 4
ROW_TILE = D_MODEL // LANES
N_DMA_PRIORITIES = 2
VMEM_LIMIT = 56 * 1024 * 1024

BF16 = jnp.bfloat16
F32 = jnp.float32


def _rms(x, g):
    return x * lax.rsqrt(jnp.mean(x * x, axis=-1, keepdims=True) + EPS) * g


def _dot(a, b):
    return jnp.dot(a, b, preferred_element_type=F32)


def _sigmoid(x):
    return 0.5 * (1.0 + jnp.tanh(0.5 * x))


def _chain_slab(s, tre_ref, tim_ref, hre_scr, him_scr, re_scr, im_scr, tt, fillers=()):
    n_groups = tt // SUBLANES
    every = n_groups // len(fillers) if fillers else 0
    t_re = tre_ref[s]
    t_im = tim_ref[s]
    upper = lax.broadcasted_iota(jnp.int32, (SUBLANES, SLAB), 0) < RUN
    ta_re, ta_im = jnp.where(upper, t_re, 0.0), jnp.where(upper, t_im, 0.0)
    tb_re, tb_im = jnp.where(upper, 0.0, t_re), jnp.where(upper, 0.0, t_im)
    c_re = hre_scr[s]
    c_im = him_scr[s]
    for v in range(n_groups):
        if fillers and v % every == 0 and v // every < len(fillers):
            fillers[v // every]()
        rows = pl.ds(v * SUBLANES, SUBLANES)
        re = re_scr[rows, :]
        im = im_scr[rows, :]
        re, im = re + (ta_re * c_re - ta_im * c_im), im + (ta_re * c_im + ta_im * c_re)
        m_re = re[RUN - 1:RUN, :]
        m_im = im[RUN - 1:RUN, :]
        re, im = re + (tb_re * m_re - tb_im * m_im), im + (tb_re * m_im + tb_im * m_re)
        re_scr[rows, :] = re
        im_scr[rows, :] = im
        c_re = re[SUBLANES - 1:SUBLANES, :]
        c_im = im[SUBLANES - 1:SUBLANES, :]
    hre_scr[s] = c_re
    him_scr[s] = c_im


def _mixer_kernel(x_ref, hist_ref, h0re_ref, h0im_ref, gmix_ref, win_ref, wpool_ref, pscale_ref,
                  bmat_ref, cre_ref, cim_ref, tre_ref, tim_ref, dskip_ref, wglu_ref, bglu_ref, wout_ref,
                  h_ref, poolnew_ref, ssmre_ref, ssmim_ref,
                  hist_scr, hre_scr, him_scr, lhs_scr, u_scr, mix_scr, *slab_scr, tt, pos0):
    bure_scr, buim_scr = slab_scr[:N_SLABS], slab_scr[N_SLABS:]
    j = pl.program_id(1)

    @pl.when(j == 0)
    def _():
        hist_scr[...] = hist_ref[0]
        hre_scr[...] = h0re_ref[0]
        him_scr[...] = h0im_ref[0]

    xn = _rms(x_ref[0], gmix_ref[...])
    z = _dot(xn.astype(BF16), win_ref[...])
    zp = z[:, :D_POOL]
    u = z[:, D_POOL:]
    u_scr[...] = u

    def pool_mixer():
        ext = jnp.concatenate([hist_scr[...], zp], axis=0)
        pos = pos0 + j * tt + lax.broadcasted_iota(jnp.int32, (tt, 1), 0)
        acc = ext
        outs = []
        for gi, w in enumerate(POOL_WINDOWS):
            lo = gi * POOL_GROUP
            acc = acc[:, POOL_GROUP * (1 if gi else 0):]
            acc = acc + pltpu.roll(acc, w // 2, 0)
            wsum = acc[HIST_ROWS:, :POOL_GROUP]
            cnt = jnp.minimum(pos + 1, w).astype(F32)
            d = wsum / cnt - zp[:, lo:lo + POOL_GROUP]
            outs.append(_dot(d.astype(BF16), wpool_ref[gi]))
        hist_scr[...] = ext[tt:tt + HIST_ROWS]
        poolnew_ref[0] = ext[tt:tt + HIST_ROWS]
        mix_scr[:, :D_POOL] = (jnp.concatenate(outs, axis=1) * pscale_ref[...]).astype(BF16)

    run_row = lax.broadcasted_iota(jnp.int32, (tt, 1), 0) % RUN
    for dd in range(RUN):
        sh = u if dd == 0 else jnp.where(run_row >= dd, pltpu.roll(u, dd, 0), 0.0)
        for c in range(N_PACKS):
            lhs_scr[c, :, dd * LANES:(dd + 1) * LANES] = sh[:, c * LANES:(c + 1) * LANES].astype(BF16)

    def projections(c):
        work = []
        for i in range(SLABS_PER_PACK):
            for part, dst in ((0, bure_scr), (1, buim_scr)):
                lo = part * PACK_STATES + i * SLAB

                def piece(lo=lo, ref=dst[c * SLABS_PER_PACK + i]):
                    ref[...] = _dot(lhs_scr[c], bmat_ref[c, :, lo:lo + SLAB])
                work.append(piece)
        return work

    ys = []
    pool_mixer()
    for piece in projections(0):
        piece()
    for c in range(N_PACKS):
        slabs = list(range(c * SLABS_PER_PACK, (c + 1) * SLABS_PER_PACK))
        ahead = projections(c + 1) if c + 1 < N_PACKS else []
        per_slab = len(ahead) // len(slabs)
        for i, s in enumerate(slabs):
            _chain_slab(s, tre_ref, tim_ref, hre_scr, him_scr, bure_scr[s], buim_scr[s], tt,
                        fillers=ahead[i * per_slab:(i + 1) * per_slab])
        hs_re = jnp.concatenate([bure_scr[s][...] for s in slabs], axis=1)
        hs_im = jnp.concatenate([buim_scr[s][...] for s in slabs], axis=1)
        ys.append(_dot(hs_re.astype(BF16), cre_ref[c]) + _dot(hs_im.astype(BF16), cim_ref[c]))
    ssmre_ref[0] = hre_scr[...]
    ssmim_ref[0] = him_scr[...]
    y = jnp.concatenate(ys, axis=1) + dskip_ref[...] * u_scr[...]
    g = 0.5 * y * (1.0 + jnp.tanh(math.sqrt(2.0 / math.pi) * (y + 0.044715 * (y * y * y))))
    y_ssm = g * _sigmoid(_dot(g.astype(BF16), wglu_ref[...]) + bglu_ref[...])
    mix_scr[:, D_POOL:] = y_ssm.astype(BF16)
    h_ref[0] = x_ref[0] + _dot(mix_scr[...], wout_ref[...])


def _const_spec(shape):
    return pl.BlockSpec(shape, lambda *_: (0,) * len(shape))


def _mixer(x, hist, h0re, h0im, prm, *, tt, pos0):
    b, t, d = x.shape
    kern = functools.partial(_mixer_kernel, tt=tt, pos0=pos0)
    per_b3 = lambda shp: pl.BlockSpec((1,) + shp, lambda i, j: (i, 0, 0))
    per_b4 = lambda shp: pl.BlockSpec((1,) + shp, lambda i, j: (i, 0, 0, 0))
    return pl.pallas_call(
        kern,
        grid=(b, t // tt),
        in_specs=[
            pl.BlockSpec((1, tt, d), lambda i, j: (i, j, 0)),
            per_b3((HIST_ROWS, D_POOL)),
            per_b4((N_SLABS, 1, SLAB)),
            per_b4((N_SLABS, 1, SLAB)),
            _const_spec((1, d)),
            _const_spec((d, d)),
            _const_spec((len(POOL_WINDOWS), POOL_GROUP, POOL_GROUP)),
            _const_spec((1, D_POOL)),
            _const_spec((N_PACKS, RUN * LANES, 2 * PACK_STATES)),
            _const_spec((N_PACKS, PACK_STATES, LANES)),
            _const_spec((N_PACKS, PACK_STATES, LANES)),
            _const_spec((N_SLABS, SUBLANES, SLAB)),
            _const_spec((N_SLABS, SUBLANES, SLAB)),
            _const_spec((1, D_SSM)),
            _const_spec((D_SSM, D_SSM)),
            _const_spec((1, D_SSM)),
            _const_spec((d, d)),
        ],
        out_specs=[
            pl.BlockSpec((1, tt, d), lambda i, j: (i, j, 0)),
            per_b3((HIST_ROWS, D_POOL)),
            per_b4((N_SLABS, 1, SLAB)),
            per_b4((N_SLABS, 1, SLAB)),
        ],
        out_shape=[
            jax.ShapeDtypeStruct((b, t, d), F32),
            jax.ShapeDtypeStruct((b, HIST_ROWS, D_POOL), F32),
            jax.ShapeDtypeStruct((b, N_SLABS, 1, SLAB), F32),
            jax.ShapeDtypeStruct((b, N_SLABS, 1, SLAB), F32),
        ],
        scratch_shapes=[
            pltpu.VMEM((HIST_ROWS, D_POOL), F32),
            pltpu.VMEM((N_SLABS, 1, SLAB), F32),
            pltpu.VMEM((N_SLABS, 1, SLAB), F32),
            pltpu.VMEM((N_PACKS, tt, RUN * LANES), BF16),
            pltpu.VMEM((tt, D_SSM), F32),
            pltpu.VMEM((tt, d), BF16),
        ] + [pltpu.VMEM((tt, SLAB), F32)] * (2 * N_SLABS),
        compiler_params=pltpu.CompilerParams(
            dimension_semantics=("arbitrary", "arbitrary"), vmem_limit_bytes=VMEM_LIMIT),
    )(x, hist, h0re, h0im, prm['g_mix'], prm['w_in'], prm['w_pool'], prm['pool_scale'],
      prm['bmat'], prm['c_re'], prm['c_im'], prm['t_re'], prm['t_im'], prm['d_skip'],
      prm['w_glu'], prm['b_glu'], prm['w_out'])


def _memkv_kernel(m_ref, g_ref, wk_ref, wv_ref, k_ref, v_ref):
    m = _rms(m_ref[0], g_ref[...]).astype(BF16)
    k_ref[0] = _dot(m, wk_ref[...])
    v_ref[0] = _dot(m, wv_ref[...])


def _memkv(mem, g, wk, wv):
    b, n, d = mem.shape
    blk = pl.BlockSpec((1, n, d), lambda i: (i, 0, 0))
    return pl.pallas_call(
        _memkv_kernel,
        grid=(b,),
        in_specs=[blk, _const_spec((1, d)), _const_spec((d, d)), _const_spec((d, d))],
        out_specs=[blk, blk],
        out_shape=[jax.ShapeDtypeStruct((b, n, d), F32)] * 2,
        compiler_params=pltpu.CompilerParams(vmem_limit_bytes=VMEM_LIMIT),
    )(mem, g, wk, wv)


def _attn_kernel(h_ref, k_ref, v_ref, gx_ref, wq_ref, wo_ref, gf_ref, wr_ref, br_ref, tail_ref,
                 h2_ref, xn_ref, e_ref, gate_ref, cnt_ref, cnt_scr, kt_scr, v_scr, *, tt, nt, n_tiles):
    step = pl.program_id(0)
    pl.when(step < n_tiles)(lambda: _attn_tile(
        step, h_ref, k_ref, v_ref, gx_ref, wq_ref, wo_ref, gf_ref, wr_ref, br_ref,
        h2_ref, xn_ref, e_ref, gate_ref, cnt_ref, cnt_scr, kt_scr, v_scr, tt=tt, nt=nt))

    @pl.when(step == n_tiles)
    def _():
        xn_ref[pl.ds(0, tail_ref.shape[0]), :] = tail_ref[...]


def _attn_tile(step, h_ref, k_ref, v_ref, gx_ref, wq_ref, wo_ref, gf_ref, wr_ref, br_ref,
               h2_ref, xn_ref, e_ref, gate_ref, cnt_ref, cnt_scr, kt_scr, v_scr, *, tt, nt):
    @pl.when(step % nt == 0)
    def _():
        kt_scr[...] = k_ref[0].T.astype(BF16)
        v_scr[...] = v_ref[0].astype(BF16)

    h = h_ref[0]
    hn = _rms(h, gx_ref[...])
    q = _dot(hn.astype(BF16), wq_ref[...])
    outs = []
    for hd in range(N_XHEADS):
        cols = slice(hd * XHEAD_DIM, (hd + 1) * XHEAD_DIM)
        s = _dot(q[:, cols].astype(BF16), kt_scr[cols, :]) * (XHEAD_DIM ** -0.5)
        p = jnp.exp(s - jnp.max(s, axis=-1, keepdims=True))
        p = p * (1.0 / jnp.sum(p, axis=-1, keepdims=True))
        outs.append(_dot(p.astype(BF16), v_scr[:, cols]))
    o = jnp.concatenate(outs, axis=1)
    h2 = h + _dot(o.astype(BF16), wo_ref[...])
    h2_ref[0] = h2

    xn = _rms(h2, gf_ref[...])
    for s in range(ROW_TILE):
        xn_ref[pl.ds(s, tt, stride=ROW_TILE), :] = xn[:, s * LANES:(s + 1) * LANES]
    logits = _dot(xn.astype(BF16), wr_ref[...]) + br_ref[...]
    lane = lax.broadcasted_iota(jnp.int32, logits.shape, 1)
    lane_f = lane.astype(F32)
    e_out = jnp.zeros(logits.shape, jnp.int32)
    hits = jnp.zeros(logits.shape, F32)
    top = []
    for k in range(TOP_K):
        m = jnp.max(logits, axis=-1, keepdims=True)
        idx = jnp.min(jnp.where(logits == m, lane_f, float(LANES)), axis=-1, keepdims=True)
        e_out = jnp.where(lane == k, idx.astype(jnp.int32), e_out)
        top.append(m)
        chosen = lane_f == idx
        hits = hits + chosen.astype(F32)
        logits = jnp.where(chosen, -jnp.inf, logits)

    @pl.when(step == 0)
    def _():
        cnt_scr[...] = jnp.zeros_like(cnt_scr)
    cnt_scr[...] = cnt_scr[...] + jnp.sum(hits, axis=0, keepdims=True)
    cnt_ref[...] = cnt_scr[...]
    ex = [jnp.exp(m - top[0]) for m in top]
    tot = ex[0] + ex[1] + ex[2] + ex[3]
    g_out = jnp.zeros(logits.shape, F32)
    for k in range(TOP_K):
        g_out = jnp.where(lane == k, ex[k] / tot, g_out)
    e_ref[...] = e_out
    gate_ref[...] = g_out


def _attn(h, k, v, prm, *, tt, tail=None):
    b, t, d = h.shape
    n = b * t
    nt = t // tt
    n_tiles = b * nt
    has_tail = tail is not None
    if not has_tail:
        tail = jnp.zeros((ROW_TILE, LANES), F32)
    kern = functools.partial(_attn_kernel, tt=tt, nt=nt, n_tiles=n_tiles)
    tile_of = lambda i: jnp.minimum(i, n_tiles - 1)
    kv = pl.BlockSpec((1, N_MEM, d), lambda i: (tile_of(i) // nt, 0, 0))
    act = pl.BlockSpec((1, tt, d), lambda i: (tile_of(i) // nt, tile_of(i) % nt, 0))
    tok = pl.BlockSpec((tt, LANES), lambda i: (tile_of(i), 0))
    return pl.pallas_call(
        kern,
        grid=(n_tiles + (1 if has_tail else 0),),
        in_specs=[
            act, kv, kv,
            _const_spec((1, d)), _const_spec((d, d)), _const_spec((d, d)),
            _const_spec((1, d)), _const_spec((d, LANES)), _const_spec((1, LANES)),
            _const_spec(tail.shape),
        ],
        out_specs=[
            act,
            pl.BlockSpec((tt * ROW_TILE, LANES), lambda i: (i, 0)),
            tok, tok, _const_spec((1, LANES)),
        ],
        out_shape=[
            jax.ShapeDtypeStruct((b, t, d), F32),
            jax.ShapeDtypeStruct((n * ROW_TILE + (tail.shape[0] if has_tail else 0), LANES), F32),
            jax.ShapeDtypeStruct((n, LANES), jnp.int32),
            jax.ShapeDtypeStruct((n, LANES), F32),
            jax.ShapeDtypeStruct((1, LANES), F32),
        ],
        scratch_shapes=[pltpu.VMEM((1, LANES), F32), pltpu.VMEM((d, N_MEM), BF16), pltpu.VMEM((N_MEM, d), BF16)],
        compiler_params=pltpu.CompilerParams(dimension_semantics=("arbitrary",), vmem_limit_bytes=VMEM_LIMIT),
    )(h, k, v, prm['g_xattn'], prm['w_q'], prm['w_o'], prm['g_ffn'], prm['w_router'], prm['b_router'], tail)


RANK_TILE = 512


def _rank_kernel(ep_ref, es_ref, start_ref, dest_ref, carry_scr, *, steps_p):
    i = pl.program_id(0)

    @pl.when(i == 0)
    def _():
        carry_scr[...] = start_ref[...]

    e = jnp.where(i < steps_p, ep_ref[...], es_ref[...])
    lane = lax.broadcasted_iota(jnp.int32, e.shape, 1)
    onehot = [lane == e[:, k:k + 1] for k in range(TOP_K)]
    hits = jnp.zeros(e.shape, F32)
    for oh in onehot:
        hits = hits + oh.astype(F32)
    r = lax.broadcasted_iota(jnp.int32, (RANK_TILE, RANK_TILE), 0)
    c = lax.broadcasted_iota(jnp.int32, (RANK_TILE, RANK_TILE), 1)
    before = (c < r).astype(BF16)
    base = _dot(before, hits.astype(BF16)) + carry_scr[...]
    rank = jnp.zeros(e.shape, jnp.int32)
    for k, oh in enumerate(onehot):
        rk = jnp.sum(jnp.where(oh, base, 0.0), axis=-1, keepdims=True).astype(jnp.int32)
        rank = jnp.where(lane == k, rk, rank)
    dest_ref[...] = rank
    carry_scr[...] = carry_scr[...] + jnp.sum(hits, axis=0, keepdims=True)


def _rank(e_p, e_s, start):
    steps_p = e_p.shape[0] // RANK_TILE
    blk = (RANK_TILE, LANES)
    return pl.pallas_call(
        functools.partial(_rank_kernel, steps_p=steps_p),
        grid=(steps_p + 1,),
        in_specs=[pl.BlockSpec(blk, lambda i: (jnp.minimum(i, steps_p - 1), 0)), _const_spec(blk),
                  _const_spec((1, LANES))],
        out_specs=pl.BlockSpec(blk, lambda i: (i, 0)),
        out_shape=jax.ShapeDtypeStruct(((steps_p + 1) * RANK_TILE, LANES), jnp.int32),
        scratch_shapes=[pltpu.VMEM((1, LANES), F32)],
        compiler_params=pltpu.CompilerParams(dimension_semantics=("arbitrary",)),
    )(e_p, e_s, start)


BLOCK_ROWS = MOE_BLOCK * ROW_TILE


def _row_copy_wait(src_ref, dst_ref, sem, rows):
    pltpu.make_async_copy(src_ref.at[pl.ds(0, rows * ROW_TILE)], dst_ref.at[pl.ds(0, rows * ROW_TILE)], sem).wait()


INVERT_CHUNK = 8192
INVERT_UNROLL = 8


def _invert_kernel(gap_lo_ref, gap_hi_ref, dest_ref, src_ref):
    i = pl.program_id(0)

    @pl.when(i == 0)
    def _():
        def clear(r, carry):
            src_ref[r] = 0
            return carry
        for g in range(N_EXPERTS + 1):
            lax.fori_loop(gap_lo_ref[g], gap_hi_ref[g], clear, 0)

    tokens_per_iter = INVERT_UNROLL // TOP_K
    first_token = i * (INVERT_CHUNK // TOP_K)

    def body(it, carry):
        token = first_token + it * tokens_per_iter
        for j in range(INVERT_UNROLL):
            src_ref[dest_ref[0, 0, it * INVERT_UNROLL + j]] = token + j // TOP_K
        return carry

    lax.fori_loop(0, INVERT_CHUNK // INVERT_UNROLL, body, 0)


def _invert(dest, gap_lo, gap_hi, n_rows):
    n = dest.shape[0]
    steps = -(-n // INVERT_CHUNK)
    dest = jnp.pad(dest, (0, steps * INVERT_CHUNK - n), constant_values=n_rows)
    return pl.pallas_call(
        _invert_kernel,
        grid_spec=pltpu.PrefetchScalarGridSpec(
            num_scalar_prefetch=2,
            grid=(steps,),
            in_specs=[pl.BlockSpec((1, 1, INVERT_CHUNK), lambda i, lo, hi: (i, 0, 0), memory_space=pltpu.SMEM)],
            out_specs=pl.BlockSpec(memory_space=pltpu.SMEM),
        ),
        out_shape=jax.ShapeDtypeStruct((n_rows + 1,), jnp.int32),
        compiler_params=pltpu.CompilerParams(dimension_semantics=("arbitrary",)),
    )(gap_lo, gap_hi, dest.reshape(steps, 1, INVERT_CHUNK))


GATHER_CHUNKS = 4


def _expert_kernel(blk_e_ref, nvalid_ref, next_e_ref, nrows_ref, src_ref, src_next_ref, bg_ref, bu_ref, bd_ref,
                   x_hbm, wg_hbm, wu_hbm, wd_hbm, y_ref, stage, w_bf, xbuf_a, xbuf_b, sems, xsems):
    b = pl.program_id(0)
    n_valid = nvalid_ref[0]
    valid = b < n_valid
    e = blk_e_ref[b]
    weights = (wg_hbm, wu_hbm, wd_hbm)

    def gather(sref, buf, sem, r):
        tok = sref[0, 0, r]
        slot = r * ROW_TILE
        if not isinstance(slot, int):
            slot = pl.multiple_of(slot, ROW_TILE)
        src = x_hbm.at[pl.ds(pl.multiple_of(tok * ROW_TILE, ROW_TILE), ROW_TILE)]
        priority = r % N_DMA_PRIORITIES if isinstance(r, int) else 0
        pltpu.make_async_copy(src, buf.at[pl.ds(slot, ROW_TILE)], sem).start(priority=priority)

    @pl.when(b == 0)
    def _():
        lax.fori_loop(0, MOE_BLOCK, lambda r, c: (gather(src_ref, xbuf_a, xsems.at[0], r), c)[1], 0)

    def fetch(expert):
        return [pltpu.make_async_copy(w.at[expert], stage.at[i], sems.at[i]) for i, w in enumerate(weights)]

    @pl.when(b == 0)
    def _():
        for cp in fetch(e):
            cp.start()

    @pl.when(jnp.logical_and(valid, jnp.logical_or(b == 0, blk_e_ref[jnp.maximum(b - 1, 0)] != e)))
    def _():
        for i, cp in enumerate(fetch(e)):
            cp.wait()
            w_bf[i] = stage[i].astype(BF16)

        @pl.when(next_e_ref[b] >= 0)
        def _():
            for cp in fetch(next_e_ref[b]):
                cp.start()

    def step(cur, cur_sem, nxt, nxt_sem):
        _row_copy_wait(x_hbm, cur, cur_sem, MOE_BLOCK)
        x = jnp.concatenate([cur[pl.ds(s, MOE_BLOCK, stride=ROW_TILE), :] for s in range(ROW_TILE)], axis=1)
        routed = lax.broadcasted_iota(jnp.int32, (MOE_BLOCK, 1), 0) < nrows_ref[b]
        x = jnp.where(routed, x, 0.0).astype(BF16)
        per = MOE_BLOCK // GATHER_CHUNKS
        width = D_MODEL // GATHER_CHUNKS
        gs, us = [], []
        for c in range(GATHER_CHUNKS):
            for r in range(c * per, (c + 1) * per):
                gather(src_next_ref, nxt, nxt_sem, r)
            cols = slice(c * width, (c + 1) * width)
            gs.append(_dot(x, w_bf[0, :, cols]))
            us.append(_dot(x, w_bf[1, :, cols]))
        g = jnp.concatenate(gs, axis=1) + bg_ref[0]
        u = jnp.concatenate(us, axis=1) + bu_ref[0]
        g = jnp.minimum(g, SWIGLU_LIMIT)
        u = jnp.clip(u, -SWIGLU_LIMIT, SWIGLU_LIMIT)
        hdn = g * _sigmoid(SWIGLU_ALPHA * g) * (u + 1.0)
        y = _dot(hdn.astype(BF16), w_bf[2]) + bd_ref[0]
        for s in range(ROW_TILE):
            y_ref[pl.ds(s, MOE_BLOCK, stride=ROW_TILE), :] = y[:, s * LANES:(s + 1) * LANES]

    even = b % 2 == 0
    pl.when(jnp.logical_and(valid, even))(lambda: step(xbuf_a, xsems.at[0], xbuf_b, xsems.at[1]))
    pl.when(jnp.logical_and(valid, jnp.logical_not(even)))(lambda: step(xbuf_b, xsems.at[1], xbuf_a, xsems.at[0]))

    @pl.when(b == n_valid - 1)
    def _():
        pl.when(even)(lambda: _row_copy_wait(x_hbm, xbuf_b, xsems.at[1], MOE_BLOCK))
        pl.when(jnp.logical_not(even))(lambda: _row_copy_wait(x_hbm, xbuf_a, xsems.at[0], MOE_BLOCK))

    @pl.when(b >= n_valid)
    def _():
        y_ref[...] = jnp.zeros_like(y_ref)


def _experts(blk_e, nvalid, next_e, nrows, src, x_rows, w_gate, b_gate, w_up, b_up, w_down, b_down, n_blocks):
    d = D_MODEL
    n_w = 3
    bspec = pl.BlockSpec((1, 1, d), lambda b, be, nv, ne, nr: (be[b], 0, 0))
    rows_of = lambda shift: pl.BlockSpec(
        (1, 1, MOE_BLOCK), lambda b, be, nv, ne, nr: (jnp.minimum(b + shift, nv[0] - 1), 0, 0), memory_space=pltpu.SMEM)
    hbm = pl.BlockSpec(memory_space=pl.ANY)
    xbuf = pltpu.VMEM((BLOCK_ROWS, LANES), F32)
    return pl.pallas_call(
        _expert_kernel,
        grid_spec=pltpu.PrefetchScalarGridSpec(
            num_scalar_prefetch=4,
            grid=(n_blocks,),
            in_specs=[rows_of(0), rows_of(1), bspec, bspec, bspec, hbm, hbm, hbm, hbm],
            out_specs=pl.BlockSpec((BLOCK_ROWS, LANES), lambda b, be, nv, ne, nr: (b, 0)),
            scratch_shapes=[pltpu.VMEM((n_w, d, d), F32), pltpu.VMEM((n_w, d, d), BF16), xbuf, xbuf,
                            pltpu.SemaphoreType.DMA((n_w,)), pltpu.SemaphoreType.DMA((2,))],
        ),
        out_shape=jax.ShapeDtypeStruct((n_blocks * BLOCK_ROWS, LANES), F32),
        compiler_params=pltpu.CompilerParams(dimension_semantics=("arbitrary",), vmem_limit_bytes=VMEM_LIMIT),
    )(blk_e, nvalid, next_e, nrows, src, src, b_gate.reshape(N_EXPERTS, 1, d), b_up.reshape(N_EXPERTS, 1, d),
      b_down.reshape(N_EXPERTS, 1, d), x_rows, w_gate, w_up, w_down)


def _combine_kernel(dest_ref, dest_next_ref, h_ref, gate_ref, gfin_ref, ypad_ref, y_ref, buf_a, buf_b, sems, *, tile):
    i = pl.program_id(0)
    n_steps = pl.num_programs(0)

    def gather(dref, buf, sem, t, k):
        row = dref[0, 0, t * TOP_K + k]
        slot = (k * tile + t) * ROW_TILE
        if not isinstance(slot, int):
            slot = pl.multiple_of(slot, ROW_TILE)
        src = ypad_ref.at[pl.ds(pl.multiple_of(row * ROW_TILE, ROW_TILE), ROW_TILE)]
        pltpu.make_async_copy(src, buf.at[pl.ds(slot, ROW_TILE)], sem).start(priority=k % N_DMA_PRIORITIES)

    def drain(buf, sem):
        for _ in range(TOP_K):
            _row_copy_wait(ypad_ref, buf, sem, tile)

    @pl.when(i == 0)
    def _():
        def body(t, carry):
            for k in range(TOP_K):
                gather(dest_ref, buf_a, sems.at[0], t, k)
            return carry
        lax.fori_loop(0, tile, body, 0)

    def step(cur, cur_sem, nxt, nxt_sem):
        drain(cur, cur_sem)
        gates = gate_ref[...]
        h = h_ref[...]
        per_slab = tile // ROW_TILE
        cols = []
        for s in range(ROW_TILE):
            for t in range(s * per_slab, (s + 1) * per_slab):
                for k in range(TOP_K):
                    gather(dest_next_ref, nxt, nxt_sem, t, k)
            acc = h[:, s * LANES:(s + 1) * LANES]
            for k in range(TOP_K):
                acc = acc + gates[:, k:k + 1] * cur[pl.ds(k * tile * ROW_TILE + s, tile, stride=ROW_TILE), :]
            cols.append(acc)
        y_ref[...] = _rms(jnp.concatenate(cols, axis=1), gfin_ref[...])

    pl.when(i % 2 == 0)(lambda: step(buf_a, sems.at[0], buf_b, sems.at[1]))
    pl.when(i % 2 == 1)(lambda: step(buf_b, sems.at[1], buf_a, sems.at[0]))

    @pl.when(i == n_steps - 1)
    def _():
        pl.when(i % 2 == 0)(lambda: drain(buf_b, sems.at[1]))
        pl.when(i % 2 == 1)(lambda: drain(buf_a, sems.at[0]))


def _combine(dest, h2, gates, g_final, ypad, *, tile):
    n, d = h2.shape
    steps = n // tile
    kern = functools.partial(_combine_kernel, tile=tile)
    dest3 = dest.reshape(steps, 1, tile * TOP_K)
    buf = pltpu.VMEM((TOP_K * tile * ROW_TILE, LANES), F32)
    return pl.pallas_call(
        kern,
        grid=(steps,),
        in_specs=[
            pl.BlockSpec((1, 1, tile * TOP_K), lambda i: (i, 0, 0), memory_space=pltpu.SMEM),
            pl.BlockSpec((1, 1, tile * TOP_K), lambda i: (jnp.minimum(i + 1, steps - 1), 0, 0),
                         memory_space=pltpu.SMEM),
            pl.BlockSpec((tile, d), lambda i: (i, 0)),
            pl.BlockSpec((tile, LANES), lambda i: (i, 0)),
            _const_spec((1, d)),
            pl.BlockSpec(memory_space=pl.ANY),
        ],
        out_specs=pl.BlockSpec((tile, d), lambda i: (i, 0)),
        out_shape=jax.ShapeDtypeStruct((n, d), F32),
        scratch_shapes=[buf, buf, pltpu.SemaphoreType.DMA((2,))],
        compiler_params=pltpu.CompilerParams(dimension_semantics=("arbitrary",), vmem_limit_bytes=VMEM_LIMIT),
    )(dest3, dest3, h2, gates, g_final, ypad)


def _ssm_params(a_re, a_im, log_dt, b_re, b_im, c_re, c_im):
    dt = jnp.exp(log_dt)[:, None]
    mag = jnp.exp(a_re * dt)
    lb_re = mag * jnp.cos(a_im * dt)
    lb_im = mag * jnp.sin(a_im * dt)
    den = a_re * a_re + a_im * a_im
    q_re = ((lb_re - 1.0) * a_re + lb_im * a_im) / den
    q_im = (lb_im * a_re - (lb_re - 1.0) * a_im) / den
    bb_re = q_re[:, :, None] * b_re - q_im[:, :, None] * b_im
    bb_im = q_re[:, :, None] * b_im + q_im[:, :, None] * b_re
    eye = jnp.eye(SSM_PACK, dtype=F32)

    def pack_blockdiag(m):
        m = m.reshape(N_PACKS, SSM_PACK, m.shape[1], m.shape[2])
        return jnp.einsum('ngab,gh->ngahb', m, eye).reshape(N_PACKS, SSM_PACK * m.shape[2], SSM_PACK * m.shape[3])

    pw_re, pw_im = [jnp.ones_like(lb_re), lb_re], [jnp.zeros_like(lb_im), lb_im]
    for _ in range(RUN - 1):
        r, i = pw_re[-1], pw_im[-1]
        pw_re.append(r * lb_re - i * lb_im)
        pw_im.append(r * lb_im + i * lb_re)
    b_rows = []
    for dd in range(RUN):
        s_re = bb_re * pw_re[dd][:, :, None] - bb_im * pw_im[dd][:, :, None]
        s_im = bb_re * pw_im[dd][:, :, None] + bb_im * pw_re[dd][:, :, None]
        b_rows.append(jnp.concatenate([pack_blockdiag(s_re.transpose(0, 2, 1)),
                                       pack_blockdiag(s_im.transpose(0, 2, 1))], axis=2))
    bmat = jnp.concatenate(b_rows, axis=1)
    cmat_re = pack_blockdiag(c_re.transpose(0, 2, 1))
    cmat_im = pack_blockdiag(-c_im.transpose(0, 2, 1))
    rows_of = lambda pw: [pw[(r % RUN) + 1].reshape(-1) for r in range(SUBLANES)]
    slabbed = lambda rows: jnp.stack(rows).reshape(SUBLANES, N_SLABS, SLAB).transpose(1, 0, 2)
    return (bmat.astype(BF16), cmat_re.astype(BF16), cmat_im.astype(BF16),
            slabbed(rows_of(pw_re)), slabbed(rows_of(pw_im)))


def kernel(x_prompt, x_sample, mem_prompt, cache_mem_k, cache_mem_v, cache_pool, state_ssm_re, state_ssm_im, norm_mix, w_in, w_pool, pool_scale, ssm_a_re, ssm_a_im, ssm_log_dt, ssm_b_re, ssm_b_im, ssm_c_re, ssm_c_im, ssm_d, w_glu, b_glu, w_out, norm_xattn, norm_mem, w_q, w_k, w_v, w_o, norm_ffn, w_router, b_router, w_gate, b_gate, w_up, b_up, w_down, b_down, norm_final):
    assert x_prompt.shape[2] == D_MODEL and norm_mix.shape[0] == 1
    bp, tp, d = x_prompt.shape
    bs, ts, _ = x_sample.shape
    n_p, n_s = bp * tp, bs * ts
    row = lambda v: v.reshape(1, -1)

    bmat, c_re, c_im, t_re, t_im = _ssm_params(ssm_a_re[0], ssm_a_im[0], ssm_log_dt[0], ssm_b_re[0], ssm_b_im[0],
                                               ssm_c_re[0], ssm_c_im[0])
    mix_prm = dict(g_mix=row(norm_mix[0]), w_in=w_in[0].astype(BF16), w_pool=w_pool[0].astype(BF16),
                   pool_scale=row(pool_scale[0]), bmat=bmat, c_re=c_re, c_im=c_im, t_re=t_re, t_im=t_im,
                   d_skip=row(ssm_d[0]), w_glu=w_glu[0].astype(BF16), b_glu=row(b_glu[0]),
                   w_out=w_out[0].astype(BF16))

    attn_prm = dict(g_xattn=row(norm_xattn[0]), w_q=w_q[0].astype(BF16), w_o=w_o[0].astype(BF16),
                    g_ffn=row(norm_ffn[0]),
                    w_router=jnp.pad(w_router[0], ((0, 0), (0, LANES - N_EXPERTS))).astype(BF16),
                    b_router=jnp.pad(row(b_router[0]), ((0, 0), (0, LANES - N_EXPERTS)), constant_values=-jnp.inf))
    slab_state = lambda s: s.reshape(s.shape[0], N_SLABS, 1, SLAB)

    hist_s = jnp.pad(cache_pool[0], ((0, 0), (HIST_ROWS - POOL_BUF, 0), (0, 0)))
    h1_s, pool_s, sre_s, sim_s = _mixer(x_sample, hist_s, slab_state(state_ssm_re[0]), slab_state(state_ssm_im[0]),
                                        mix_prm, tt=ts, pos0=PAST_LEN)
    ck = cache_mem_k[0].reshape(bs, N_MEM, d)
    cv = cache_mem_v[0].reshape(bs, N_MEM, d)
    h2_s, xn_s, e_s, g_s, cnt_s = _attn(h1_s, ck, cv, attn_prm, tt=ts)

    zeros_state = jnp.zeros((bp, N_SLABS, 1, SLAB), F32)
    h1_p, pool_p, sre_p, sim_p = _mixer(x_prompt, jnp.zeros((bp, HIST_ROWS, D_POOL), F32), zeros_state, zeros_state,
                                        mix_prm, tt=512, pos0=0)
    mk, mv = _memkv(mem_prompt, row(norm_mem[0]), w_k[0].astype(BF16), w_v[0].astype(BF16))
    h2_p, xn_all, e_p, g_p, cnt_p = _attn(h1_p, mk, mv, attn_prm, tt=1024, tail=xn_s)

    n_all = n_p + n_s
    counts = (cnt_p + cnt_s)[0, :N_EXPERTS].astype(jnp.int32)
    padded = (counts + MOE_BLOCK - 1) // MOE_BLOCK * MOE_BLOCK
    pend = jnp.cumsum(padded)
    pstart = pend - padded
    n_blocks = -(-(n_all * TOP_K + N_EXPERTS * (MOE_BLOCK - 1)) // MOE_BLOCK)
    blk_e = jnp.minimum(jnp.sum(pend[None, :] <= (jnp.arange(n_blocks) * MOE_BLOCK)[:, None], axis=1),
                        N_EXPERTS - 1).astype(jnp.int32)
    nvalid = (pend[-1:] // MOE_BLOCK).astype(jnp.int32)
    in_expert = jnp.arange(n_blocks) * MOE_BLOCK - pstart[blk_e]
    nrows = jnp.clip(counts[blk_e] - in_expert, 0, MOE_BLOCK).astype(jnp.int32)
    after = (pend // MOE_BLOCK)[blk_e]
    next_e = jnp.where(after < nvalid[0], blk_e[jnp.minimum(after, n_blocks - 1)], -1).astype(jnp.int32)
    start = jnp.pad(pstart.astype(F32), (0, LANES - N_EXPERTS)).reshape(1, LANES)
    e_s_tile = jnp.pad(e_s, ((0, RANK_TILE - n_s), (0, 0)), constant_values=-1)
    dest = _rank(e_p, e_s_tile, start)[:n_all, :TOP_K].reshape(-1)
    dest_p, dest_s = dest[:n_p * TOP_K], dest[n_p * TOP_K:]

    n_rows = n_blocks * MOE_BLOCK
    gap_lo = jnp.concatenate([pstart + counts, pend[-1:]]).astype(jnp.int32)
    gap_hi = jnp.concatenate([pend, jnp.full((1,), n_rows + 1)]).astype(jnp.int32)
    src = _invert(dest, gap_lo, gap_hi, n_rows)[:n_rows].reshape(n_blocks, 1, MOE_BLOCK)
    ypad = _experts(blk_e, nvalid, next_e, nrows, src, xn_all, w_gate[0], b_gate[0], w_up[0], b_up[0], w_down[0], b_down[0], n_blocks)
    g_fin = row(norm_final)
    y_p = _combine(dest_p, h2_p.reshape(n_p, d), g_p, g_fin, ypad, tile=256)
    y_s = _combine(dest_s, h2_s.reshape(n_s, d), g_s, g_fin, ypad, tile=n_s)

    unslab = lambda s: s.reshape(1, s.shape[0], N_SSM_GROUPS, SSM_STATE)
    kv5 = lambda a: a.reshape(1, bp, N_MEM, N_XHEADS, XHEAD_DIM)
    return (y_p.reshape(bp, tp, d), y_s.reshape(bs, ts, d), kv5(mk), kv5(mv),
            pool_p[None, :, HIST_ROWS - POOL_BUF:], pool_s[None, :, HIST_ROWS - POOL_BUF:],
            unslab(sre_p), unslab(sim_p), unslab(sre_s), unslab(sim_s))
```

```python
import functools
import math

import jax
import jax.numpy as jnp
from jax import lax
from jax.experimental import pallas as pl
from jax.experimental.pallas import tpu as pltpu

D_MODEL = 1024
D_POOL = 512
D_SSM = 512
POOL_WINDOWS = (2, 4, 8, 16)
POOL_GROUP = 128
HIST_ROWS = 16
POOL_BUF = 15
SSM_GROUP = 16
N_SSM_GROUPS = 32
SSM_STATE = 64
N_STATE = N_SSM_GROUPS * SSM_STATE
N_MEM = 256
N_XHEADS = 4
XHEAD_DIM = 256
N_EXPERTS = 32
TOP_K = 4
SWIGLU_LIMIT = 7.0
SWIGLU_ALPHA = 1.702
MOE_BLOCK = 512
EPS = 1e-6
PAST_LEN = 1024

LANES = 128
SUBLANES = 8
SLAB = 256
N_SLABS = N_STATE // SLAB
SSM_PACK = LANES // SSM_GROUP
N_PACKS = N_SSM_GROUPS // SSM_PACK
PACK_STATES = SSM_PACK * SSM_STATE
SLABS_PER_PACK = PACK_STATES // SLAB
RUN = 4
ROW_TILE = D_MODEL // LANES
N_DMA_PRIORITIES = 2
PUSH_UNROLL = 4
VMEM_LIMIT = 56 * 1024 * 1024

MIXER_TILE = 512
ATTN_TILE = 1024
ROUTE_TILE = 256

BF16 = jnp.bfloat16
F32 = jnp.float32


def _rms(x, g):
    return x * lax.rsqrt(jnp.mean(x * x, axis=-1, keepdims=True) + EPS) * g


def _dot(a, b):
    return jnp.dot(a, b, preferred_element_type=F32)


def _chain_slab(s, tre_ref, tim_ref, hre_scr, him_scr, re_scr, im_scr, tt):
    t_re = tre_ref[s]
    t_im = tim_ref[s]
    upper = lax.broadcasted_iota(jnp.int32, (SUBLANES, SLAB), 0) < RUN
    ta_re, ta_im = jnp.where(upper, t_re, 0.0), jnp.where(upper, t_im, 0.0)
    tb_re, tb_im = jnp.where(upper, 0.0, t_re), jnp.where(upper, 0.0, t_im)
    c_re = hre_scr[s]
    c_im = him_scr[s]
    for v in range(tt // SUBLANES):
        rows = pl.ds(v * SUBLANES, SUBLANES)
        re = re_scr[rows, :]
        im = im_scr[rows, :]
        re, im = re + (ta_re * c_re - ta_im * c_im), im + (ta_re * c_im + ta_im * c_re)
        m_re = re[RUN - 1:RUN, :]
        m_im = im[RUN - 1:RUN, :]
        re, im = re + (tb_re * m_re - tb_im * m_im), im + (tb_re * m_im + tb_im * m_re)
        re_scr[rows, :] = re
        im_scr[rows, :] = im
        c_re = re[SUBLANES - 1:SUBLANES, :]
        c_im = im[SUBLANES - 1:SUBLANES, :]
    hre_scr[s] = c_re
    him_scr[s] = c_im


def _mixer_kernel(x_ref, hist_ref, h0re_ref, h0im_ref, gmix_ref, win_ref, wpool_ref, pscale_ref,
                  bmat_ref, cre_ref, cim_ref, tre_ref, tim_ref, dskip_ref, wglu_ref, bglu_ref, wout_ref,
                  h_ref, poolnew_ref, ssmre_ref, ssmim_ref,
                  hist_scr, hre_scr, him_scr, *slab_scr, tt, pos0):
    bure_scr, buim_scr = slab_scr[:N_SLABS], slab_scr[N_SLABS:]
    j = pl.program_id(1)

    @pl.when(j == 0)
    def _():
        hist_scr[...] = hist_ref[0]
        hre_scr[...] = h0re_ref[0]
        him_scr[...] = h0im_ref[0]

    x = x_ref[0]
    xn = _rms(x, gmix_ref[...])
    z = _dot(xn.astype(BF16), win_ref[...])
    zp = z[:, :D_POOL]
    u = z[:, D_POOL:]

    ext = jnp.concatenate([hist_scr[...], zp], axis=0)
    pos = pos0 + j * tt + lax.broadcasted_iota(jnp.int32, (tt, 1), 0)
    acc = ext
    outs = []
    for gi, w in enumerate(POOL_WINDOWS):
        lo = gi * POOL_GROUP
        acc = acc[:, POOL_GROUP * (1 if gi else 0):]
        acc = acc + pltpu.roll(acc, w // 2, 0)
        wsum = acc[HIST_ROWS:, :POOL_GROUP]
        cnt = jnp.minimum(pos + 1, w).astype(F32)
        d = wsum / cnt - zp[:, lo:lo + POOL_GROUP]
        outs.append(_dot(d.astype(BF16), wpool_ref[gi]))
    y_pool = jnp.concatenate(outs, axis=1) * pscale_ref[...]
    hist_scr[...] = ext[tt:tt + HIST_ROWS]
    poolnew_ref[0] = ext[tt:tt + HIST_ROWS]

    run_row = lax.broadcasted_iota(jnp.int32, (tt, 1), 0) % RUN
    shifted = [u.astype(BF16)]
    for dd in range(1, RUN):
        shifted.append(jnp.where(run_row >= dd, pltpu.roll(u, dd, 0), 0.0).astype(BF16))
    def project(c):
        lhs = jnp.concatenate([sh[:, c * LANES:(c + 1) * LANES] for sh in shifted], axis=1)
        bu = _dot(lhs, bmat_ref[c])
        for i in range(SLABS_PER_PACK):
            s = c * SLABS_PER_PACK + i
            bure_scr[s][...] = bu[:, i * SLAB:(i + 1) * SLAB]
            buim_scr[s][...] = bu[:, PACK_STATES + i * SLAB:PACK_STATES + (i + 1) * SLAB]

    ys = []
    project(0)
    for c in range(N_PACKS):
        slabs = range(c * SLABS_PER_PACK, (c + 1) * SLABS_PER_PACK)
        if c + 1 < N_PACKS:
            project(c + 1)
        for s in slabs:
            _chain_slab(s, tre_ref, tim_ref, hre_scr, him_scr, bure_scr[s], buim_scr[s], tt)
        hs_re = jnp.concatenate([bure_scr[s][...] for s in slabs], axis=1)
        hs_im = jnp.concatenate([buim_scr[s][...] for s in slabs], axis=1)
        ys.append(_dot(hs_re.astype(BF16), cre_ref[c]) + _dot(hs_im.astype(BF16), cim_ref[c]))
    ssmre_ref[0] = hre_scr[...]
    ssmim_ref[0] = him_scr[...]
    y = jnp.concatenate(ys, axis=1) + dskip_ref[...] * u
    g = 0.5 * y * (1.0 + jnp.tanh(math.sqrt(2.0 / math.pi) * (y + 0.044715 * (y * y * y))))
    y_ssm = g * jax.nn.sigmoid(_dot(g.astype(BF16), wglu_ref[...]) + bglu_ref[...])

    mix = jnp.concatenate([y_pool, y_ssm], axis=1)
    h_ref[0] = x + _dot(mix.astype(BF16), wout_ref[...])


def _const_spec(shape):
    return pl.BlockSpec(shape, lambda *_: (0,) * len(shape))


def _mixer(x, hist, h0re, h0im, prm, *, tt, pos0):
    b, t, d = x.shape
    kern = functools.partial(_mixer_kernel, tt=tt, pos0=pos0)
    per_b3 = lambda shp: pl.BlockSpec((1,) + shp, lambda i, j: (i, 0, 0))
    per_b4 = lambda shp: pl.BlockSpec((1,) + shp, lambda i, j: (i, 0, 0, 0))
    return pl.pallas_call(
        kern,
        grid=(b, t // tt),
        in_specs=[
            pl.BlockSpec((1, tt, d), lambda i, j: (i, j, 0)),
            per_b3((HIST_ROWS, D_POOL)),
            per_b4((N_SLABS, 1, SLAB)),
            per_b4((N_SLABS, 1, SLAB)),
            _const_spec((1, d)),
            _const_spec((d, d)),
            _const_spec((len(POOL_WINDOWS), POOL_GROUP, POOL_GROUP)),
            _const_spec((1, D_POOL)),
            _const_spec((N_PACKS, RUN * LANES, 2 * PACK_STATES)),
            _const_spec((N_PACKS, PACK_STATES, LANES)),
            _const_spec((N_PACKS, PACK_STATES, LANES)),
            _const_spec((N_SLABS, SUBLANES, SLAB)),
            _const_spec((N_SLABS, SUBLANES, SLAB)),
            _const_spec((1, D_SSM)),
            _const_spec((D_SSM, D_SSM)),
            _const_spec((1, D_SSM)),
            _const_spec((d, d)),
        ],
        out_specs=[
            pl.BlockSpec((1, tt, d), lambda i, j: (i, j, 0)),
            per_b3((HIST_ROWS, D_POOL)),
            per_b4((N_SLABS, 1, SLAB)),
            per_b4((N_SLABS, 1, SLAB)),
        ],
        out_shape=[
            jax.ShapeDtypeStruct((b, t, d), F32),
            jax.ShapeDtypeStruct((b, HIST_ROWS, D_POOL), F32),
            jax.ShapeDtypeStruct((b, N_SLABS, 1, SLAB), F32),
            jax.ShapeDtypeStruct((b, N_SLABS, 1, SLAB), F32),
        ],
        scratch_shapes=[
            pltpu.VMEM((HIST_ROWS, D_POOL), F32),
            pltpu.VMEM((N_SLABS, 1, SLAB), F32),
            pltpu.VMEM((N_SLABS, 1, SLAB), F32),
        ] + [pltpu.VMEM((tt, SLAB), F32)] * (2 * N_SLABS),
        compiler_params=pltpu.CompilerParams(
            dimension_semantics=("arbitrary", "arbitrary"), vmem_limit_bytes=VMEM_LIMIT),
    )(x, hist, h0re, h0im, prm['g_mix'], prm['w_in'], prm['w_pool'], prm['pool_scale'],
      prm['bmat'], prm['c_re'], prm['c_im'], prm['t_re'], prm['t_im'], prm['d_skip'],
      prm['w_glu'], prm['b_glu'], prm['w_out'])


def _memkv_kernel(m_ref, g_ref, wk_ref, wv_ref, k_ref, v_ref):
    m = _rms(m_ref[0], g_ref[...]).astype(BF16)
    k_ref[0] = _dot(m, wk_ref[...])
    v_ref[0] = _dot(m, wv_ref[...])


def _memkv(mem, g, wk, wv):
    b, n, d = mem.shape
    blk = pl.BlockSpec((1, n, d), lambda i: (i, 0, 0))
    return pl.pallas_call(
        _memkv_kernel,
        grid=(b,),
        in_specs=[blk, _const_spec((1, d)), _const_spec((d, d)), _const_spec((d, d))],
        out_specs=[blk, blk],
        out_shape=[jax.ShapeDtypeStruct((b, n, d), F32)] * 2,
        compiler_params=pltpu.CompilerParams(vmem_limit_bytes=VMEM_LIMIT),
    )(mem, g, wk, wv)


def _attn_kernel(h_ref, k_ref, v_ref, gx_ref, wq_ref, wo_ref, gf_ref, wr_ref, br_ref,
                 h2_ref, xn_ref, e_ref, gate_ref, cnt_ref, cnt_scr, kt_scr, v_scr, *, tt):
    @pl.when(pl.program_id(1) == 0)
    def _():
        kt_scr[...] = k_ref[0].T.astype(BF16)
        v_scr[...] = v_ref[0].astype(BF16)

    h = h_ref[0]
    hn = _rms(h, gx_ref[...])
    q = _dot(hn.astype(BF16), wq_ref[...])
    outs = []
    for hd in range(N_XHEADS):
        cols = slice(hd * XHEAD_DIM, (hd + 1) * XHEAD_DIM)
        s = _dot(q[:, cols].astype(BF16), kt_scr[cols, :]) * (XHEAD_DIM ** -0.5)
        p = jnp.exp(s - jnp.max(s, axis=-1, keepdims=True))
        p = p / jnp.sum(p, axis=-1, keepdims=True)
        outs.append(_dot(p.astype(BF16), v_scr[:, cols]))
    o = jnp.concatenate(outs, axis=1)
    h2 = h + _dot(o.astype(BF16), wo_ref[...])
    h2_ref[0] = h2

    xn = _rms(h2, gf_ref[...])
    for s in range(ROW_TILE):
        xn_ref[pl.ds(s, tt, stride=ROW_TILE), :] = xn[:, s * LANES:(s + 1) * LANES]
    logits = _dot(xn.astype(BF16), wr_ref[...]) + br_ref[...]
    lane = lax.broadcasted_iota(jnp.int32, logits.shape, 1)
    lane_f = lane.astype(F32)
    e_out = jnp.zeros(logits.shape, jnp.int32)
    hits = jnp.zeros(logits.shape, F32)
    top = []
    for k in range(TOP_K):
        m = jnp.max(logits, axis=-1, keepdims=True)
        idx = jnp.min(jnp.where(logits == m, lane_f, float(LANES)), axis=-1, keepdims=True)
        e_out = jnp.where(lane == k, idx.astype(jnp.int32), e_out)
        top.append(m)
        chosen = lane_f == idx
        hits = hits + chosen.astype(F32)
        logits = jnp.where(chosen, -jnp.inf, logits)

    @pl.when(jnp.logical_and(pl.program_id(0) == 0, pl.program_id(1) == 0))
    def _():
        cnt_scr[...] = jnp.zeros_like(cnt_scr)
    cnt_scr[...] = cnt_scr[...] + jnp.sum(hits, axis=0, keepdims=True)
    cnt_ref[...] = cnt_scr[...]
    ex = [jnp.exp(m - top[0]) for m in top]
    tot = ex[0] + ex[1] + ex[2] + ex[3]
    g_out = jnp.zeros(logits.shape, F32)
    for k in range(TOP_K):
        g_out = jnp.where(lane == k, ex[k] / tot, g_out)
    e_ref[...] = e_out
    gate_ref[...] = g_out


def _attn(h, k, v, prm, *, tt):
    b, t, d = h.shape
    n = b * t
    kern = functools.partial(_attn_kernel, tt=tt)
    nt = t // tt
    kv = pl.BlockSpec((1, N_MEM, d), lambda i, j: (i, 0, 0))
    tok = lambda width: pl.BlockSpec((tt, width), lambda i, j: (i * nt + j, 0))
    return pl.pallas_call(
        kern,
        grid=(b, nt),
        in_specs=[
            pl.BlockSpec((1, tt, d), lambda i, j: (i, j, 0)), kv, kv,
            _const_spec((1, d)), _const_spec((d, d)), _const_spec((d, d)),
            _const_spec((1, d)), _const_spec((d, LANES)), _const_spec((1, LANES)),
        ],
        out_specs=[
            pl.BlockSpec((1, tt, d), lambda i, j: (i, j, 0)),
            pl.BlockSpec((tt * ROW_TILE, LANES), lambda i, j: (i * nt + j, 0)),
            tok(LANES), tok(LANES), _const_spec((1, LANES)),
        ],
        out_shape=[
            jax.ShapeDtypeStruct((b, t, d), F32),
            jax.ShapeDtypeStruct((n * ROW_TILE, LANES), F32),
            jax.ShapeDtypeStruct((n, LANES), jnp.int32),
            jax.ShapeDtypeStruct((n, LANES), F32),
            jax.ShapeDtypeStruct((1, LANES), F32),
        ],
        scratch_shapes=[pltpu.VMEM((1, LANES), F32), pltpu.VMEM((d, N_MEM), BF16), pltpu.VMEM((N_MEM, d), BF16)],
        compiler_params=pltpu.CompilerParams(
            dimension_semantics=("arbitrary", "arbitrary"), vmem_limit_bytes=VMEM_LIMIT),
    )(h, k, v, prm['g_xattn'], prm['w_q'], prm['w_o'], prm['g_ffn'], prm['w_router'], prm['b_router'])


RANK_TILE = 512


def _rank_kernel(ep_ref, es_ref, start_ref, dest_ref, carry_scr, before_scr, *, steps_p):
    i = pl.program_id(0)

    @pl.when(i == 0)
    def _():
        carry_scr[...] = start_ref[...]
        r = lax.broadcasted_iota(jnp.int32, (RANK_TILE, RANK_TILE), 0)
        c = lax.broadcasted_iota(jnp.int32, (RANK_TILE, RANK_TILE), 1)
        before_scr[...] = (c < r).astype(BF16)

    e = jnp.where(i < steps_p, ep_ref[...], es_ref[...])
    lane = lax.broadcasted_iota(jnp.int32, e.shape, 1)
    onehot = [lane == e[:, k:k + 1] for k in range(TOP_K)]
    hits = jnp.zeros(e.shape, F32)
    for oh in onehot:
        hits = hits + oh.astype(F32)
    base = _dot(before_scr[...], hits.astype(BF16)) + carry_scr[...]
    rank = jnp.zeros(e.shape, jnp.int32)
    for k, oh in enumerate(onehot):
        rk = jnp.sum(jnp.where(oh, base, 0.0), axis=-1, keepdims=True).astype(jnp.int32)
        rank = jnp.where(lane == k, rk, rank)
    dest_ref[...] = rank
    carry_scr[...] = carry_scr[...] + jnp.sum(hits, axis=0, keepdims=True)


def _rank(e_p, e_s, start):
    steps_p = e_p.shape[0] // RANK_TILE
    blk = (RANK_TILE, LANES)
    return pl.pallas_call(
        functools.partial(_rank_kernel, steps_p=steps_p),
        grid=(steps_p + 1,),
        in_specs=[pl.BlockSpec(blk, lambda i: (jnp.minimum(i, steps_p - 1), 0)), _const_spec(blk),
                  _const_spec((1, LANES))],
        out_specs=pl.BlockSpec(blk, lambda i: (i, 0)),
        out_shape=jax.ShapeDtypeStruct(((steps_p + 1) * RANK_TILE, LANES), jnp.int32),
        scratch_shapes=[pltpu.VMEM((1, LANES), F32), pltpu.VMEM((RANK_TILE, RANK_TILE), BF16)],
        compiler_params=pltpu.CompilerParams(dimension_semantics=("arbitrary",)),
    )(e_p, e_s, start)


BLOCK_ROWS = MOE_BLOCK * ROW_TILE


def _row_copy_wait(src_ref, dst_ref, sem, rows):
    pltpu.make_async_copy(src_ref.at[pl.ds(0, rows * ROW_TILE)], dst_ref.at[pl.ds(0, rows * ROW_TILE)], sem).wait()


def _dispatch_kernel(last_blk_ref, nvalid_ref, dest_p_ref, dest_s_ref, xp_ref, xs_ref, xpad_ref,
                     zero_buf, sem, zero_sem, *, tile_p, tile_s, n_blocks):
    i = pl.program_id(0)
    last = pl.num_programs(0) - 1

    def zero_copy(blk):
        dst = xpad_ref.at[pl.ds(pl.multiple_of(blk * BLOCK_ROWS, BLOCK_ROWS), BLOCK_ROWS)]
        return pltpu.make_async_copy(zero_buf, dst, zero_sem)

    @pl.when(i == 0)
    def _():
        zero_buf[...] = jnp.zeros_like(zero_buf)

        def on_pad_blocks(fn):
            for e in range(N_EXPERTS):
                if e == 0:
                    fn(last_blk_ref[0])
                else:
                    pl.when(last_blk_ref[e] != last_blk_ref[e - 1])(functools.partial(fn, last_blk_ref[e]))
            lax.fori_loop(nvalid_ref[0], n_blocks, lambda blk, c: (fn(blk), c)[1], 0)

        on_pad_blocks(lambda blk: zero_copy(blk).start())
        on_pad_blocks(lambda blk: zero_copy(blk).wait())

    def push(dest_ref, x_ref, tile):
        def body(it, carry):
            for j in range(PUSH_UNROLL):
                t = it * PUSH_UNROLL + j
                src = x_ref.at[pl.ds(pl.multiple_of(t * ROW_TILE, ROW_TILE), ROW_TILE)]
                for k in range(TOP_K):
                    row = dest_ref[0, 0, t * TOP_K + k]
                    dst = xpad_ref.at[pl.ds(pl.multiple_of(row * ROW_TILE, ROW_TILE), ROW_TILE)]
                    pltpu.make_async_copy(src, dst, sem).start(priority=k % N_DMA_PRIORITIES)
            return carry

        lax.fori_loop(0, tile // PUSH_UNROLL, body, 0)
        for _ in range(TOP_K):
            _row_copy_wait(x_ref, xpad_ref, sem, tile)

    pl.when(i < last)(lambda: push(dest_p_ref, xp_ref, tile_p))
    pl.when(i == last)(lambda: push(dest_s_ref, xs_ref, tile_s))


def _dispatch(last_blk, nvalid, dest_p, dest_s, xn_p, xn_s, n_blocks, *, tile_p):
    n_p = xn_p.shape[0] // ROW_TILE
    tile_s = xn_s.shape[0] // ROW_TILE
    steps_p = n_p // tile_p
    kern = functools.partial(_dispatch_kernel, tile_p=tile_p, tile_s=tile_s, n_blocks=n_blocks)
    return pl.pallas_call(
        kern,
        grid_spec=pltpu.PrefetchScalarGridSpec(
            num_scalar_prefetch=2,
            grid=(steps_p + 1,),
            in_specs=[
                pl.BlockSpec((1, 1, tile_p * TOP_K), lambda i, lb, nv: (jnp.minimum(i, steps_p - 1), 0, 0),
                             memory_space=pltpu.SMEM),
                pl.BlockSpec((1, 1, tile_s * TOP_K), lambda i, lb, nv: (0, 0, 0), memory_space=pltpu.SMEM),
                pl.BlockSpec((tile_p * ROW_TILE, LANES), lambda i, lb, nv: (jnp.minimum(i, steps_p - 1), 0)),
                pl.BlockSpec((tile_s * ROW_TILE, LANES), lambda i, lb, nv: (0, 0)),
            ],
            out_specs=pl.BlockSpec(memory_space=pl.ANY),
            scratch_shapes=[pltpu.VMEM((BLOCK_ROWS, LANES), F32), pltpu.SemaphoreType.DMA, pltpu.SemaphoreType.DMA],
        ),
        out_shape=jax.ShapeDtypeStruct((n_blocks * BLOCK_ROWS, LANES), F32),
        compiler_params=pltpu.CompilerParams(dimension_semantics=("arbitrary",)),
    )(last_blk, nvalid, dest_p.reshape(steps_p, 1, tile_p * TOP_K), dest_s.reshape(1, 1, tile_s * TOP_K), xn_p, xn_s)


def _expert_kernel(blk_e_ref, nvalid_ref, next_e_ref, x_ref, bg_ref, bu_ref, bd_ref, wg_hbm, wu_hbm, wd_hbm, y_ref,
                   stage, w_bf, sems):
    b = pl.program_id(0)
    valid = b < nvalid_ref[0]
    e = blk_e_ref[b]
    weights = (wg_hbm, wu_hbm, wd_hbm)

    def fetch(expert):
        return [pltpu.make_async_copy(w.at[expert], stage.at[i], sems.at[i]) for i, w in enumerate(weights)]

    @pl.when(b == 0)
    def _():
        for cp in fetch(e):
            cp.start()

    @pl.when(jnp.logical_and(valid, jnp.logical_or(b == 0, blk_e_ref[jnp.maximum(b - 1, 0)] != e)))
    def _():
        for i, cp in enumerate(fetch(e)):
            cp.wait()
            w_bf[i] = stage[i].astype(BF16)

        @pl.when(next_e_ref[b] >= 0)
        def _():
            for cp in fetch(next_e_ref[b]):
                cp.start()

    @pl.when(valid)
    def _():
        x = jnp.concatenate([x_ref[pl.ds(s, MOE_BLOCK, stride=ROW_TILE), :] for s in range(ROW_TILE)], axis=1)
        x = x.astype(BF16)
        g = _dot(x, w_bf[0]) + bg_ref[0]
        u = _dot(x, w_bf[1]) + bu_ref[0]
        g = jnp.minimum(g, SWIGLU_LIMIT)
        u = jnp.clip(u, -SWIGLU_LIMIT, SWIGLU_LIMIT)
        hdn = g * (0.5 * (1.0 + jnp.tanh((0.5 * SWIGLU_ALPHA) * g))) * (u + 1.0)
        y = _dot(hdn.astype(BF16), w_bf[2]) + bd_ref[0]
        for s in range(ROW_TILE):
            y_ref[pl.ds(s, MOE_BLOCK, stride=ROW_TILE), :] = y[:, s * LANES:(s + 1) * LANES]

    @pl.when(b >= nvalid_ref[0])
    def _():
        y_ref[...] = jnp.zeros_like(y_ref)


def _experts(blk_e, nvalid, next_e, xpad, w_gate, b_gate, w_up, b_up, w_down, b_down, n_blocks):
    d = D_MODEL
    n_w = 3
    bspec = pl.BlockSpec((1, 1, d), lambda b, be, nv, ne: (be[b], 0, 0))
    xspec = pl.BlockSpec((BLOCK_ROWS, LANES), lambda b, be, nv, ne: (jnp.minimum(b, nv[0] - 1), 0))
    hbm = pl.BlockSpec(memory_space=pl.ANY)
    return pl.pallas_call(
        _expert_kernel,
        grid_spec=pltpu.PrefetchScalarGridSpec(
            num_scalar_prefetch=3,
            grid=(n_blocks,),
            in_specs=[xspec, bspec, bspec, bspec, hbm, hbm, hbm],
            out_specs=pl.BlockSpec((BLOCK_ROWS, LANES), lambda b, be, nv, ne: (b, 0)),
            scratch_shapes=[pltpu.VMEM((n_w, d, d), F32), pltpu.VMEM((n_w, d, d), BF16),
                            pltpu.SemaphoreType.DMA((n_w,))],
        ),
        out_shape=jax.ShapeDtypeStruct((n_blocks * BLOCK_ROWS, LANES), F32),
        compiler_params=pltpu.CompilerParams(dimension_semantics=("arbitrary",), vmem_limit_bytes=VMEM_LIMIT),
    )(blk_e, nvalid, next_e, xpad, b_gate.reshape(N_EXPERTS, 1, d), b_up.reshape(N_EXPERTS, 1, d),
      b_down.reshape(N_EXPERTS, 1, d), w_gate, w_up, w_down)


def _combine_kernel(dest_ref, dest_next_ref, h_ref, gate_ref, gfin_ref, ypad_ref, y_ref, buf_a, buf_b, sems, *, tile):
    i = pl.program_id(0)
    n_steps = pl.num_programs(0)

    def gather(dref, buf, sem, t, k):
        row = dref[0, 0, t * TOP_K + k]
        slot = (k * tile + t) * ROW_TILE
        if not isinstance(slot, int):
            slot = pl.multiple_of(slot, ROW_TILE)
        src = ypad_ref.at[pl.ds(pl.multiple_of(row * ROW_TILE, ROW_TILE), ROW_TILE)]
        pltpu.make_async_copy(src, buf.at[pl.ds(slot, ROW_TILE)], sem).start(priority=k % N_DMA_PRIORITIES)

    def drain(buf, sem):
        for _ in range(TOP_K):
            _row_copy_wait(ypad_ref, buf, sem, tile)

    @pl.when(i == 0)
    def _():
        def body(t, carry):
            for k in range(TOP_K):
                gather(dest_ref, buf_a, sems.at[0], t, k)
            return carry
        lax.fori_loop(0, tile, body, 0)

    def step(cur, cur_sem, nxt, nxt_sem):
        drain(cur, cur_sem)
        gates = gate_ref[...]
        h = h_ref[...]
        per_slab = tile // ROW_TILE
        cols = []
        for s in range(ROW_TILE):
            for t in range(s * per_slab, (s + 1) * per_slab):
                for k in range(TOP_K):
                    gather(dest_next_ref, nxt, nxt_sem, t, k)
            acc = h[:, s * LANES:(s + 1) * LANES]
            for k in range(TOP_K):
                acc = acc + gates[:, k:k + 1] * cur[pl.ds(k * tile * ROW_TILE + s, tile, stride=ROW_TILE), :]
            cols.append(acc)
        y_ref[...] = _rms(jnp.concatenate(cols, axis=1), gfin_ref[...])

    pl.when(i % 2 == 0)(lambda: step(buf_a, sems.at[0], buf_b, sems.at[1]))
    pl.when(i % 2 == 1)(lambda: step(buf_b, sems.at[1], buf_a, sems.at[0]))

    @pl.when(i == n_steps - 1)
    def _():
        pl.when(i % 2 == 0)(lambda: drain(buf_b, sems.at[1]))
        pl.when(i % 2 == 1)(lambda: drain(buf_a, sems.at[0]))


def _combine(dest, h2, gates, g_final, ypad, *, tile):
    n, d = h2.shape
    steps = n // tile
    kern = functools.partial(_combine_kernel, tile=tile)
    dest3 = dest.reshape(steps, 1, tile * TOP_K)
    buf = pltpu.VMEM((TOP_K * tile * ROW_TILE, LANES), F32)
    return pl.pallas_call(
        kern,
        grid=(steps,),
        in_specs=[
            pl.BlockSpec((1, 1, tile * TOP_K), lambda i: (i, 0, 0), memory_space=pltpu.SMEM),
            pl.BlockSpec((1, 1, tile * TOP_K), lambda i: (jnp.minimum(i + 1, steps - 1), 0, 0),
                         memory_space=pltpu.SMEM),
            pl.BlockSpec((tile, d), lambda i: (i, 0)),
            pl.BlockSpec((tile, LANES), lambda i: (i, 0)),
            _const_spec((1, d)),
            pl.BlockSpec(memory_space=pl.ANY),
        ],
        out_specs=pl.BlockSpec((tile, d), lambda i: (i, 0)),
        out_shape=jax.ShapeDtypeStruct((n, d), F32),
        scratch_shapes=[buf, buf, pltpu.SemaphoreType.DMA((2,))],
        compiler_params=pltpu.CompilerParams(dimension_semantics=("arbitrary",), vmem_limit_bytes=VMEM_LIMIT),
    )(dest3, dest3, h2, gates, g_final, ypad)


def _ssm_params(a_re, a_im, log_dt, b_re, b_im, c_re, c_im):
    dt = jnp.exp(log_dt)[:, None]
    mag = jnp.exp(a_re * dt)
    lb_re = mag * jnp.cos(a_im * dt)
    lb_im = mag * jnp.sin(a_im * dt)
    den = a_re * a_re + a_im * a_im
    q_re = ((lb_re - 1.0) * a_re + lb_im * a_im) / den
    q_im = (lb_im * a_re - (lb_re - 1.0) * a_im) / den
    bb_re = q_re[:, :, None] * b_re - q_im[:, :, None] * b_im
    bb_im = q_re[:, :, None] * b_im + q_im[:, :, None] * b_re
    eye = jnp.eye(SSM_PACK, dtype=F32)

    def pack_blockdiag(m):
        m = m.reshape(N_PACKS, SSM_PACK, m.shape[1], m.shape[2])
        return jnp.einsum('ngab,gh->ngahb', m, eye).reshape(N_PACKS, SSM_PACK * m.shape[2], SSM_PACK * m.shape[3])

    pw_re, pw_im = [jnp.ones_like(lb_re), lb_re], [jnp.zeros_like(lb_im), lb_im]
    for _ in range(RUN - 1):
        r, i = pw_re[-1], pw_im[-1]
        pw_re.append(r * lb_re - i * lb_im)
        pw_im.append(r * lb_im + i * lb_re)
    b_rows = []
    for dd in range(RUN):
        s_re = bb_re * pw_re[dd][:, :, None] - bb_im * pw_im[dd][:, :, None]
        s_im = bb_re * pw_im[dd][:, :, None] + bb_im * pw_re[dd][:, :, None]
        b_rows.append(jnp.concatenate([pack_blockdiag(s_re.transpose(0, 2, 1)),
                                       pack_blockdiag(s_im.transpose(0, 2, 1))], axis=2))
    bmat = jnp.concatenate(b_rows, axis=1)
    cmat_re = pack_blockdiag(c_re.transpose(0, 2, 1))
    cmat_im = pack_blockdiag(-c_im.transpose(0, 2, 1))
    rows_of = lambda pw: [pw[(r % RUN) + 1].reshape(-1) for r in range(SUBLANES)]
    slabbed = lambda rows: jnp.stack(rows).reshape(SUBLANES, N_SLABS, SLAB).transpose(1, 0, 2)
    return (bmat.astype(BF16), cmat_re.astype(BF16), cmat_im.astype(BF16),
            slabbed(rows_of(pw_re)), slabbed(rows_of(pw_im)))


def kernel(x_prompt, x_sample, mem_prompt, cache_mem_k, cache_mem_v, cache_pool, state_ssm_re, state_ssm_im, norm_mix, w_in, w_pool, pool_scale, ssm_a_re, ssm_a_im, ssm_log_dt, ssm_b_re, ssm_b_im, ssm_c_re, ssm_c_im, ssm_d, w_glu, b_glu, w_out, norm_xattn, norm_mem, w_q, w_k, w_v, w_o, norm_ffn, w_router, b_router, w_gate, b_gate, w_up, b_up, w_down, b_down, norm_final):
    assert x_prompt.shape[2] == D_MODEL and norm_mix.shape[0] == 1
    bp, tp, d = x_prompt.shape
    bs, ts, _ = x_sample.shape
    n_p, n_s = bp * tp, bs * ts
    row = lambda v: v.reshape(1, -1)

    bmat, c_re, c_im, t_re, t_im = _ssm_params(ssm_a_re[0], ssm_a_im[0], ssm_log_dt[0], ssm_b_re[0], ssm_b_im[0],
                                               ssm_c_re[0], ssm_c_im[0])
    mix_prm = dict(g_mix=row(norm_mix[0]), w_in=w_in[0].astype(BF16), w_pool=w_pool[0].astype(BF16),
                   pool_scale=row(pool_scale[0]), bmat=bmat, c_re=c_re, c_im=c_im, t_re=t_re, t_im=t_im,
                   d_skip=row(ssm_d[0]), w_glu=w_glu[0].astype(BF16), b_glu=row(b_glu[0]),
                   w_out=w_out[0].astype(BF16))

    attn_prm = dict(g_xattn=row(norm_xattn[0]), w_q=w_q[0].astype(BF16), w_o=w_o[0].astype(BF16),
                    g_ffn=row(norm_ffn[0]),
                    w_router=jnp.pad(w_router[0], ((0, 0), (0, LANES - N_EXPERTS))).astype(BF16),
                    b_router=jnp.pad(row(b_router[0]), ((0, 0), (0, LANES - N_EXPERTS)), constant_values=-jnp.inf))
    slab_state = lambda s: s.reshape(s.shape[0], N_SLABS, 1, SLAB)

    zeros_state = jnp.zeros((bp, N_SLABS, 1, SLAB), F32)
    h1_p, pool_p, sre_p, sim_p = _mixer(x_prompt, jnp.zeros((bp, HIST_ROWS, D_POOL), F32), zeros_state, zeros_state,
                                        mix_prm, tt=MIXER_TILE, pos0=0)
    mk, mv = _memkv(mem_prompt, row(norm_mem[0]), w_k[0].astype(BF16), w_v[0].astype(BF16))
    h2_p, xn_p, e_p, g_p, cnt_p = _attn(h1_p, mk, mv, attn_prm, tt=ATTN_TILE)

    hist_s = jnp.pad(cache_pool[0], ((0, 0), (HIST_ROWS - POOL_BUF, 0), (0, 0)))
    h1_s, pool_s, sre_s, sim_s = _mixer(x_sample, hist_s, slab_state(state_ssm_re[0]), slab_state(state_ssm_im[0]),
                                        mix_prm, tt=ts, pos0=PAST_LEN)
    ck = cache_mem_k[0].reshape(bs, N_MEM, d)
    cv = cache_mem_v[0].reshape(bs, N_MEM, d)
    h2_s, xn_s, e_s, g_s, cnt_s = _attn(h1_s, ck, cv, attn_prm, tt=ts)

    n_all = n_p + n_s
    counts = (cnt_p + cnt_s)[0, :N_EXPERTS].astype(jnp.int32)
    padded = (counts + MOE_BLOCK - 1) // MOE_BLOCK * MOE_BLOCK
    pend = jnp.cumsum(padded)
    pstart = pend - padded
    n_blocks = -(-(n_all * TOP_K + N_EXPERTS * (MOE_BLOCK - 1)) // MOE_BLOCK)
    blk_e = jnp.minimum(jnp.sum(pend[None, :] <= (jnp.arange(n_blocks) * MOE_BLOCK)[:, None], axis=1),
                        N_EXPERTS - 1).astype(jnp.int32)
    nvalid = (pend[-1:] // MOE_BLOCK).astype(jnp.int32)
    last_blk = jnp.maximum(pend // MOE_BLOCK - 1, 0).astype(jnp.int32)
    after = (pend // MOE_BLOCK)[blk_e]
    next_e = jnp.where(after < nvalid[0], blk_e[jnp.minimum(after, n_blocks - 1)], -1).astype(jnp.int32)
    start = jnp.pad(pstart.astype(F32), (0, LANES - N_EXPERTS)).reshape(1, LANES)
    e_s_tile = jnp.pad(e_s, ((0, RANK_TILE - n_s), (0, 0)), constant_values=-1)
    dest = _rank(e_p, e_s_tile, start)[:n_all, :TOP_K].reshape(-1)
    dest_p, dest_s = dest[:n_p * TOP_K], dest[n_p * TOP_K:]

    xpad = _dispatch(last_blk, nvalid, dest_p, dest_s, xn_p, xn_s, n_blocks, tile_p=ROUTE_TILE)
    ypad = _experts(blk_e, nvalid, next_e, xpad, w_gate[0], b_gate[0], w_up[0], b_up[0], w_down[0], b_down[0], n_blocks)
    g_fin = row(norm_final)
    y_p = _combine(dest_p, h2_p.reshape(n_p, d), g_p, g_fin, ypad, tile=ROUTE_TILE)
    y_s = _combine(dest_s, h2_s.reshape(n_s, d), g_s, g_fin, ypad, tile=n_s)

    unslab = lambda s: s.reshape(1, s.shape[0], N_SSM_GROUPS, SSM_STATE)
    kv5 = lambda a: a.reshape(1, bp, N_MEM, N_XHEADS, XHEAD_DIM)
    return (y_p.reshape(bp, tp, d), y_s.reshape(bs, ts, d), kv5(mk), kv5(mv),
            pool_p[None, :, HIST_ROWS - POOL_BUF:], pool_s[None, :, HIST_ROWS - POOL_BUF:],
            unslab(sre_p), unslab(sim_p), unslab(sre_s), unslab(sim_s))
```

```python
import functools
import math

import jax
import jax.numpy as jnp
from jax import lax
from jax.experimental import pallas as pl
from jax.experimental.pallas import tpu as pltpu

D_MODEL = 1024
D_POOL = 512
D_SSM = 512
POOL_WINDOWS = (2, 4, 8, 16)
POOL_GROUP = 128
HIST_ROWS = 16
POOL_BUF = 15
SSM_GROUP = 16
N_SSM_GROUPS = 32
SSM_STATE = 64
N_STATE = N_SSM_GROUPS * SSM_STATE
N_MEM = 256
N_XHEADS = 4
XHEAD_DIM = 256
N_EXPERTS = 32
TOP_K = 4
SWIGLU_LIMIT = 7.0
SWIGLU_ALPHA = 1.702
MOE_BLOCK = 512
EPS = 1e-6
PAST_LEN = 1024

LANES = 128
SUBLANES = 8
SLAB = 256
N_SLABS = N_STATE // SLAB
SSM_PACK = LANES // SSM_GROUP
N_PACKS = N_SSM_GROUPS // SSM_PACK
PACK_STATES = SSM_PACK * SSM_STATE
SLABS_PER_PACK = PACK_STATES // SLAB
RUN = 4
ROW_TILE = D_MODEL // LANES
N_DMA_PRIORITIES = 2
PUSH_UNROLL = 4
VMEM_LIMIT = 56 * 1024 * 1024

MIXER_TILE = 512
ATTN_TILE = 1024
ROUTE_TILE = 256

BF16 = jnp.bfloat16
F32 = jnp.float32


def _rms(x, g):
    return x * lax.rsqrt(jnp.mean(x * x, axis=-1, keepdims=True) + EPS) * g


def _dot(a, b):
    return jnp.dot(a, b, preferred_element_type=F32)


def _chain_slab(s, tre_ref, tim_ref, hre_scr, him_scr, re_scr, im_scr, tt):
    t_re = tre_ref[s]
    t_im = tim_ref[s]
    upper = lax.broadcasted_iota(jnp.int32, (SUBLANES, SLAB), 0) < RUN
    ta_re, ta_im = jnp.where(upper, t_re, 0.0), jnp.where(upper, t_im, 0.0)
    tb_re, tb_im = jnp.where(upper, 0.0, t_re), jnp.where(upper, 0.0, t_im)
    c_re = hre_scr[s]
    c_im = him_scr[s]
    for v in range(tt // SUBLANES):
        rows = pl.ds(v * SUBLANES, SUBLANES)
        re = re_scr[rows, :]
        im = im_scr[rows, :]
        re, im = re + (ta_re * c_re - ta_im * c_im), im + (ta_re * c_im + ta_im * c_re)
        m_re = re[RUN - 1:RUN, :]
        m_im = im[RUN - 1:RUN, :]
        re, im = re + (tb_re * m_re - tb_im * m_im), im + (tb_re * m_im + tb_im * m_re)
        re_scr[rows, :] = re
        im_scr[rows, :] = im
        c_re = re[SUBLANES - 1:SUBLANES, :]
        c_im = im[SUBLANES - 1:SUBLANES, :]
    hre_scr[s] = c_re
    him_scr[s] = c_im


def _mixer_kernel(x_ref, hist_ref, h0re_ref, h0im_ref, gmix_ref, win_ref, wpool_ref, pscale_ref,
                  bmat_ref, cre_ref, cim_ref, tre_ref, tim_ref, dskip_ref, wglu_ref, bglu_ref, wout_ref,
                  h_ref, poolnew_ref, ssmre_ref, ssmim_ref,
                  hist_scr, hre_scr, him_scr, *slab_scr, tt, pos0):
    bure_scr, buim_scr = slab_scr[:N_SLABS], slab_scr[N_SLABS:]
    j = pl.program_id(1)

    @pl.when(j == 0)
    def _():
        hist_scr[...] = hist_ref[0]
        hre_scr[...] = h0re_ref[0]
        him_scr[...] = h0im_ref[0]

    x = x_ref[0]
    xn = _rms(x, gmix_ref[...])
    z = _dot(xn.astype(BF16), win_ref[...])
    zp = z[:, :D_POOL]
    u = z[:, D_POOL:]

    ext = jnp.concatenate([hist_scr[...], zp], axis=0)
    pos = pos0 + j * tt + lax.broadcasted_iota(jnp.int32, (tt, 1), 0)
    acc = ext
    outs = []
    for gi, w in enumerate(POOL_WINDOWS):
        lo = gi * POOL_GROUP
        acc = acc[:, POOL_GROUP * (1 if gi else 0):]
        acc = acc + pltpu.roll(acc, w // 2, 0)
        wsum = acc[HIST_ROWS:, :POOL_GROUP]
        cnt = jnp.minimum(pos + 1, w).astype(F32)
        d = wsum / cnt - zp[:, lo:lo + POOL_GROUP]
        outs.append(_dot(d.astype(BF16), wpool_ref[gi]))
    y_pool = jnp.concatenate(outs, axis=1) * pscale_ref[...]
    hist_scr[...] = ext[tt:tt + HIST_ROWS]
    poolnew_ref[0] = ext[tt:tt + HIST_ROWS]

    run_row = lax.broadcasted_iota(jnp.int32, (tt, 1), 0) % RUN
    shifted = [u.astype(BF16)]
    for dd in range(1, RUN):
        shifted.append(jnp.where(run_row >= dd, pltpu.roll(u, dd, 0), 0.0).astype(BF16))
    def project(c):
        lhs = jnp.concatenate([sh[:, c * LANES:(c + 1) * LANES] for sh in shifted], axis=1)
        bu = _dot(lhs, bmat_ref[c])
        for i in range(SLABS_PER_PACK):
            s = c * SLABS_PER_PACK + i
            bure_scr[s][...] = bu[:, i * SLAB:(i + 1) * SLAB]
            buim_scr[s][...] = bu[:, PACK_STATES + i * SLAB:PACK_STATES + (i + 1) * SLAB]

    ys = []
    project(0)
    for c in range(N_PACKS):
        slabs = range(c * SLABS_PER_PACK, (c + 1) * SLABS_PER_PACK)
        if c + 1 < N_PACKS:
            project(c + 1)
        for s in slabs:
            _chain_slab(s, tre_ref, tim_ref, hre_scr, him_scr, bure_scr[s], buim_scr[s], tt)
        hs_re = jnp.concatenate([bure_scr[s][...] for s in slabs], axis=1)
        hs_im = jnp.concatenate([buim_scr[s][...] for s in slabs], axis=1)
        ys.append(_dot(hs_re.astype(BF16), cre_ref[c]) + _dot(hs_im.astype(BF16), cim_ref[c]))
    ssmre_ref[0] = hre_scr[...]
    ssmim_ref[0] = him_scr[...]
    y = jnp.concatenate(ys, axis=1) + dskip_ref[...] * u
    g = 0.5 * y * (1.0 + jnp.tanh(math.sqrt(2.0 / math.pi) * (y + 0.044715 * (y * y * y))))
    y_ssm = g * jax.nn.sigmoid(_dot(g.astype(BF16), wglu_ref[...]) + bglu_ref[...])

    mix = jnp.concatenate([y_pool, y_ssm], axis=1)
    h_ref[0] = x + _dot(mix.astype(BF16), wout_ref[...])


def _const_spec(shape):
    return pl.BlockSpec(shape, lambda *_: (0,) * len(shape))


def _mixer(x, hist, h0re, h0im, prm, *, tt, pos0):
    b, t, d = x.shape
    kern = functools.partial(_mixer_kernel, tt=tt, pos0=pos0)
    per_b3 = lambda shp: pl.BlockSpec((1,) + shp, lambda i, j: (i, 0, 0))
    per_b4 = lambda shp: pl.BlockSpec((1,) + shp, lambda i, j: (i, 0, 0, 0))
    return pl.pallas_call(
        kern,
        grid=(b, t // tt),
        in_specs=[
            pl.BlockSpec((1, tt, d), lambda i, j: (i, j, 0)),
            per_b3((HIST_ROWS, D_POOL)),
            per_b4((N_SLABS, 1, SLAB)),
            per_b4((N_SLABS, 1, SLAB)),
            _const_spec((1, d)),
            _const_spec((d, d)),
            _const_spec((len(POOL_WINDOWS), POOL_GROUP, POOL_GROUP)),
            _const_spec((1, D_POOL)),
            _const_spec((N_PACKS, RUN * LANES, 2 * PACK_STATES)),
            _const_spec((N_PACKS, PACK_STATES, LANES)),
            _const_spec((N_PACKS, PACK_STATES, LANES)),
            _const_spec((N_SLABS, SUBLANES, SLAB)),
            _const_spec((N_SLABS, SUBLANES, SLAB)),
            _const_spec((1, D_SSM)),
            _const_spec((D_SSM, D_SSM)),
            _const_spec((1, D_SSM)),
            _const_spec((d, d)),
        ],
        out_specs=[
            pl.BlockSpec((1, tt, d), lambda i, j: (i, j, 0)),
            per_b3((HIST_ROWS, D_POOL)),
            per_b4((N_SLABS, 1, SLAB)),
            per_b4((N_SLABS, 1, SLAB)),
        ],
        out_shape=[
            jax.ShapeDtypeStruct((b, t, d), F32),
            jax.ShapeDtypeStruct((b, HIST_ROWS, D_POOL), F32),
            jax.ShapeDtypeStruct((b, N_SLABS, 1, SLAB), F32),
            jax.ShapeDtypeStruct((b, N_SLABS, 1, SLAB), F32),
        ],
        scratch_shapes=[
            pltpu.VMEM((HIST_ROWS, D_POOL), F32),
            pltpu.VMEM((N_SLABS, 1, SLAB), F32),
            pltpu.VMEM((N_SLABS, 1, SLAB), F32),
        ] + [pltpu.VMEM((tt, SLAB), F32)] * (2 * N_SLABS),
        compiler_params=pltpu.CompilerParams(
            dimension_semantics=("arbitrary", "arbitrary"), vmem_limit_bytes=VMEM_LIMIT),
    )(x, hist, h0re, h0im, prm['g_mix'], prm['w_in'], prm['w_pool'], prm['pool_scale'],
      prm['bmat'], prm['c_re'], prm['c_im'], prm['t_re'], prm['t_im'], prm['d_skip'],
      prm['w_glu'], prm['b_glu'], prm['w_out'])


def _memkv_kernel(m_ref, g_ref, wk_ref, wv_ref, k_ref, v_ref):
    m = _rms(m_ref[0], g_ref[...]).astype(BF16)
    k_ref[0] = _dot(m, wk_ref[...])
    v_ref[0] = _dot(m, wv_ref[...])


def _memkv(mem, g, wk, wv):
    b, n, d = mem.shape
    blk = pl.BlockSpec((1, n, d), lambda i: (i, 0, 0))
    return pl.pallas_call(
        _memkv_kernel,
        grid=(b,),
        in_specs=[blk, _const_spec((1, d)), _const_spec((d, d)), _const_spec((d, d))],
        out_specs=[blk, blk],
        out_shape=[jax.ShapeDtypeStruct((b, n, d), F32)] * 2,
        compiler_params=pltpu.CompilerParams(vmem_limit_bytes=VMEM_LIMIT),
    )(mem, g, wk, wv)


def _attn_kernel(h_ref, k_ref, v_ref, gx_ref, wq_ref, wo_ref, gf_ref, wr_ref, br_ref,
                 h2_ref, xn_ref, e_ref, gate_ref, cnt_ref, cnt_scr, kt_scr, v_scr, *, tt):
    @pl.when(pl.program_id(1) == 0)
    def _():
        kt_scr[...] = k_ref[0].T.astype(BF16)
        v_scr[...] = v_ref[0].astype(BF16)

    h = h_ref[0]
    hn = _rms(h, gx_ref[...])
    q = _dot(hn.astype(BF16), wq_ref[...])
    outs = []
    for hd in range(N_XHEADS):
        cols = slice(hd * XHEAD_DIM, (hd + 1) * XHEAD_DIM)
        s = _dot(q[:, cols].astype(BF16), kt_scr[cols, :]) * (XHEAD_DIM ** -0.5)
        p = jnp.exp(s - jnp.max(s, axis=-1, keepdims=True))
        p = p / jnp.sum(p, axis=-1, keepdims=True)
        outs.append(_dot(p.astype(BF16), v_scr[:, cols]))
    o = jnp.concatenate(outs, axis=1)
    h2 = h + _dot(o.astype(BF16), wo_ref[...])
    h2_ref[0] = h2

    xn = _rms(h2, gf_ref[...])
    for s in range(ROW_TILE):
        xn_ref[pl.ds(s, tt, stride=ROW_TILE), :] = xn[:, s * LANES:(s + 1) * LANES]
    logits = _dot(xn.astype(BF16), wr_ref[...]) + br_ref[...]
    lane = lax.broadcasted_iota(jnp.int32, logits.shape, 1)
    lane_f = lane.astype(F32)
    e_out = jnp.zeros(logits.shape, jnp.int32)
    hits = jnp.zeros(logits.shape, F32)
    top = []
    for k in range(TOP_K):
        m = jnp.max(logits, axis=-1, keepdims=True)
        idx = jnp.min(jnp.where(logits == m, lane_f, float(LANES)), axis=-1, keepdims=True)
        e_out = jnp.where(lane == k, idx.astype(jnp.int32), e_out)
        top.append(m)
        chosen = lane_f == idx
        hits = hits + chosen.astype(F32)
        logits = jnp.where(chosen, -jnp.inf, logits)

    @pl.when(jnp.logical_and(pl.program_id(0) == 0, pl.program_id(1) == 0))
    def _():
        cnt_scr[...] = jnp.zeros_like(cnt_scr)
    cnt_scr[...] = cnt_scr[...] + jnp.sum(hits, axis=0, keepdims=True)
    cnt_ref[...] = cnt_scr[...]
    ex = [jnp.exp(m - top[0]) for m in top]
    tot = ex[0] + ex[1] + ex[2] + ex[3]
    g_out = jnp.zeros(logits.shape, F32)
    for k in range(TOP_K):
        g_out = jnp.where(lane == k, ex[k] / tot, g_out)
    e_ref[...] = e_out
    gate_ref[...] = g_out


def _attn(h, k, v, prm, *, tt):
    b, t, d = h.shape
    n = b * t
    kern = functools.partial(_attn_kernel, tt=tt)
    nt = t // tt
    kv = pl.BlockSpec((1, N_MEM, d), lambda i, j: (i, 0, 0))
    tok = lambda width: pl.BlockSpec((tt, width), lambda i, j: (i * nt + j, 0))
    return pl.pallas_call(
        kern,
        grid=(b, nt),
        in_specs=[
            pl.BlockSpec((1, tt, d), lambda i, j: (i, j, 0)), kv, kv,
            _const_spec((1, d)), _const_spec((d, d)), _const_spec((d, d)),
            _const_spec((1, d)), _const_spec((d, LANES)), _const_spec((1, LANES)),
        ],
        out_specs=[
            pl.BlockSpec((1, tt, d), lambda i, j: (i, j, 0)),
            pl.BlockSpec((tt * ROW_TILE, LANES), lambda i, j: (i * nt + j, 0)),
            tok(LANES), tok(LANES), _const_spec((1, LANES)),
        ],
        out_shape=[
            jax.ShapeDtypeStruct((b, t, d), F32),
            jax.ShapeDtypeStruct((n * ROW_TILE, LANES), F32),
            jax.ShapeDtypeStruct((n, LANES), jnp.int32),
            jax.ShapeDtypeStruct((n, LANES), F32),
            jax.ShapeDtypeStruct((1, LANES), F32),
        ],
        scratch_shapes=[pltpu.VMEM((1, LANES), F32), pltpu.VMEM((d, N_MEM), BF16), pltpu.VMEM((N_MEM, d), BF16)],
        compiler_params=pltpu.CompilerParams(
            dimension_semantics=("arbitrary", "arbitrary"), vmem_limit_bytes=VMEM_LIMIT),
    )(h, k, v, prm['g_xattn'], prm['w_q'], prm['w_o'], prm['g_ffn'], prm['w_router'], prm['b_router'])


RANK_TILE = 512
BLOCK_ROWS = MOE_BLOCK * ROW_TILE


def _row_copy_wait(src_ref, dst_ref, sem, rows):
    pltpu.make_async_copy(src_ref.at[pl.ds(0, rows * ROW_TILE)], dst_ref.at[pl.ds(0, rows * ROW_TILE)], sem).wait()


def _route_kernel(last_blk_ref, nvalid_ref, ep_ref, es_ref, start_ref, xp_hbm, xs_hbm, dest_ref, xpad_ref,
                  carry_scr, before_scr, dest_smem, zero_buf, sem, zero_sem, smem_sem, *, steps_p, tile_s, n_blocks):
    i = pl.program_id(0)

    def zero_copy(blk):
        dst = xpad_ref.at[pl.ds(pl.multiple_of(blk * BLOCK_ROWS, BLOCK_ROWS), BLOCK_ROWS)]
        return pltpu.make_async_copy(zero_buf, dst, zero_sem)

    @pl.when(i == 0)
    def _():
        carry_scr[...] = start_ref[...]
        r = lax.broadcasted_iota(jnp.int32, (RANK_TILE, RANK_TILE), 0)
        c = lax.broadcasted_iota(jnp.int32, (RANK_TILE, RANK_TILE), 1)
        before_scr[...] = (c < r).astype(BF16)

        zero_buf[...] = jnp.zeros_like(zero_buf)

        def on_pad_blocks(fn):
            for e in range(N_EXPERTS):
                if e == 0:
                    fn(last_blk_ref[0])
                else:
                    pl.when(last_blk_ref[e] != last_blk_ref[e - 1])(functools.partial(fn, last_blk_ref[e]))
            lax.fori_loop(nvalid_ref[0], n_blocks, lambda blk, c: (fn(blk), c)[1], 0)

        on_pad_blocks(lambda blk: zero_copy(blk).start())
        on_pad_blocks(lambda blk: zero_copy(blk).wait())

    e = jnp.where(i < steps_p, ep_ref[...], es_ref[...])
    lane = lax.broadcasted_iota(jnp.int32, e.shape, 1)
    onehot = [lane == e[:, k:k + 1] for k in range(TOP_K)]
    hits = jnp.zeros(e.shape, F32)
    for oh in onehot:
        hits = hits + oh.astype(F32)
    base = _dot(before_scr[...], hits.astype(BF16)) + carry_scr[...]
    rank = jnp.zeros(e.shape, jnp.int32)
    for k, oh in enumerate(onehot):
        rk = jnp.sum(jnp.where(oh, base, 0.0), axis=-1, keepdims=True).astype(jnp.int32)
        rank = jnp.where(lane == k, rk, rank)
    dest_ref[...] = rank
    carry_scr[...] = carry_scr[...] + jnp.sum(hits, axis=0, keepdims=True)

    @pl.when(i > 0)
    def _():
        for _ in range(TOP_K):
            _row_copy_wait(xp_hbm, xpad_ref, sem, RANK_TILE)

    to_smem = pltpu.make_async_copy(dest_ref, dest_smem, smem_sem)
    to_smem.start()
    to_smem.wait()

    def push(x_hbm, first_token, tile):
        def body(it, carry):
            for j in range(PUSH_UNROLL):
                t = it * PUSH_UNROLL + j
                src = x_hbm.at[pl.ds(pl.multiple_of((first_token + t) * ROW_TILE, ROW_TILE), ROW_TILE)]
                for k in range(TOP_K):
                    row = dest_smem[t, k]
                    dst = xpad_ref.at[pl.ds(pl.multiple_of(row * ROW_TILE, ROW_TILE), ROW_TILE)]
                    pltpu.make_async_copy(src, dst, sem).start(priority=k % N_DMA_PRIORITIES)
            return carry

        lax.fori_loop(0, tile // PUSH_UNROLL, body, 0)

    pl.when(i < steps_p)(lambda: push(xp_hbm, i * RANK_TILE, RANK_TILE))

    @pl.when(i == steps_p)
    def _():
        push(xs_hbm, 0, tile_s)
        for _ in range(TOP_K):
            _row_copy_wait(xs_hbm, xpad_ref, sem, tile_s)


def _route(last_blk, nvalid, e_p, e_s, start, xn_p, xn_s, n_blocks):
    steps_p = e_p.shape[0] // RANK_TILE
    tile_s = xn_s.shape[0] // ROW_TILE
    blk = (RANK_TILE, LANES)
    const = lambda shape: pl.BlockSpec(shape, lambda i, lb, nv: (0,) * len(shape))
    hbm = pl.BlockSpec(memory_space=pl.ANY)
    return pl.pallas_call(
        functools.partial(_route_kernel, steps_p=steps_p, tile_s=tile_s, n_blocks=n_blocks),
        grid_spec=pltpu.PrefetchScalarGridSpec(
            num_scalar_prefetch=2,
            grid=(steps_p + 1,),
            in_specs=[pl.BlockSpec(blk, lambda i, lb, nv: (jnp.minimum(i, steps_p - 1), 0)), const(blk),
                      const((1, LANES)), hbm, hbm],
            out_specs=[pl.BlockSpec(blk, lambda i, lb, nv: (i, 0)), hbm],
            scratch_shapes=[pltpu.VMEM((1, LANES), F32), pltpu.VMEM((RANK_TILE, RANK_TILE), BF16),
                            pltpu.SMEM(blk, jnp.int32), pltpu.VMEM((BLOCK_ROWS, LANES), F32),
                            pltpu.SemaphoreType.DMA, pltpu.SemaphoreType.DMA, pltpu.SemaphoreType.DMA],
        ),
        out_shape=[jax.ShapeDtypeStruct(((steps_p + 1) * RANK_TILE, LANES), jnp.int32),
                   jax.ShapeDtypeStruct((n_blocks * BLOCK_ROWS, LANES), F32)],
        compiler_params=pltpu.CompilerParams(dimension_semantics=("arbitrary",)),
    )(last_blk, nvalid, e_p, e_s, start, xn_p, xn_s)


def _expert_kernel(blk_e_ref, nvalid_ref, next_e_ref, x_ref, bg_ref, bu_ref, bd_ref, wg_hbm, wu_hbm, wd_hbm, y_ref,
                   stage, w_bf, sems):
    b = pl.program_id(0)
    valid = b < nvalid_ref[0]
    e = blk_e_ref[b]
    weights = (wg_hbm, wu_hbm, wd_hbm)

    def fetch(expert):
        return [pltpu.make_async_copy(w.at[expert], stage.at[i], sems.at[i]) for i, w in enumerate(weights)]

    @pl.when(b == 0)
    def _():
        for cp in fetch(e):
            cp.start()

    @pl.when(jnp.logical_and(valid, jnp.logical_or(b == 0, blk_e_ref[jnp.maximum(b - 1, 0)] != e)))
    def _():
        for i, cp in enumerate(fetch(e)):
            cp.wait()
            w_bf[i] = stage[i].astype(BF16)

        @pl.when(next_e_ref[b] >= 0)
        def _():
            for cp in fetch(next_e_ref[b]):
                cp.start()

    @pl.when(valid)
    def _():
        x = jnp.concatenate([x_ref[pl.ds(s, MOE_BLOCK, stride=ROW_TILE), :] for s in range(ROW_TILE)], axis=1)
        x = x.astype(BF16)
        g = _dot(x, w_bf[0]) + bg_ref[0]
        u = _dot(x, w_bf[1]) + bu_ref[0]
        g = jnp.minimum(g, SWIGLU_LIMIT)
        u = jnp.clip(u, -SWIGLU_LIMIT, SWIGLU_LIMIT)
        hdn = g * (0.5 * (1.0 + jnp.tanh((0.5 * SWIGLU_ALPHA) * g))) * (u + 1.0)
        y = _dot(hdn.astype(BF16), w_bf[2]) + bd_ref[0]
        for s in range(ROW_TILE):
            y_ref[pl.ds(s, MOE_BLOCK, stride=ROW_TILE), :] = y[:, s * LANES:(s + 1) * LANES]

    @pl.when(b >= nvalid_ref[0])
    def _():
        y_ref[...] = jnp.zeros_like(y_ref)


def _experts(blk_e, nvalid, next_e, xpad, w_gate, b_gate, w_up, b_up, w_down, b_down, n_blocks):
    d = D_MODEL
    n_w = 3
    bspec = pl.BlockSpec((1, 1, d), lambda b, be, nv, ne: (be[b], 0, 0))
    xspec = pl.BlockSpec((BLOCK_ROWS, LANES), lambda b, be, nv, ne: (jnp.minimum(b, nv[0] - 1), 0))
    hbm = pl.BlockSpec(memory_space=pl.ANY)
    return pl.pallas_call(
        _expert_kernel,
        grid_spec=pltpu.PrefetchScalarGridSpec(
            num_scalar_prefetch=3,
            grid=(n_blocks,),
            in_specs=[xspec, bspec, bspec, bspec, hbm, hbm, hbm],
            out_specs=pl.BlockSpec((BLOCK_ROWS, LANES), lambda b, be, nv, ne: (b, 0)),
            scratch_shapes=[pltpu.VMEM((n_w, d, d), F32), pltpu.VMEM((n_w, d, d), BF16),
                            pltpu.SemaphoreType.DMA((n_w,))],
        ),
        out_shape=jax.ShapeDtypeStruct((n_blocks * BLOCK_ROWS, LANES), F32),
        compiler_params=pltpu.CompilerParams(dimension_semantics=("arbitrary",), vmem_limit_bytes=VMEM_LIMIT),
    )(blk_e, nvalid, next_e, xpad, b_gate.reshape(N_EXPERTS, 1, d), b_up.reshape(N_EXPERTS, 1, d),
      b_down.reshape(N_EXPERTS, 1, d), w_gate, w_up, w_down)


def _combine_kernel(dest_ref, dest_next_ref, h_ref, gate_ref, gfin_ref, ypad_ref, y_ref, buf_a, buf_b, sems, *, tile):
    i = pl.program_id(0)
    n_steps = pl.num_programs(0)

    def gather(dref, buf, sem, t, k):
        row = dref[0, 0, t * TOP_K + k]
        slot = (k * tile + t) * ROW_TILE
        if not isinstance(slot, int):
            slot = pl.multiple_of(slot, ROW_TILE)
        src = ypad_ref.at[pl.ds(pl.multiple_of(row * ROW_TILE, ROW_TILE), ROW_TILE)]
        pltpu.make_async_copy(src, buf.at[pl.ds(slot, ROW_TILE)], sem).start(priority=k % N_DMA_PRIORITIES)

    def drain(buf, sem):
        for _ in range(TOP_K):
            _row_copy_wait(ypad_ref, buf, sem, tile)

    @pl.when(i == 0)
    def _():
        def body(t, carry):
            for k in range(TOP_K):
                gather(dest_ref, buf_a, sems.at[0], t, k)
            return carry
        lax.fori_loop(0, tile, body, 0)

    def step(cur, cur_sem, nxt, nxt_sem):
        drain(cur, cur_sem)
        gates = gate_ref[...]
        h = h_ref[...]
        per_slab = tile // ROW_TILE
        cols = []
        for s in range(ROW_TILE):
            for t in range(s * per_slab, (s + 1) * per_slab):
                for k in range(TOP_K):
                    gather(dest_next_ref, nxt, nxt_sem, t, k)
            acc = h[:, s * LANES:(s + 1) * LANES]
            for k in range(TOP_K):
                acc = acc + gates[:, k:k + 1] * cur[pl.ds(k * tile * ROW_TILE + s, tile, stride=ROW_TILE), :]
            cols.append(acc)
        y_ref[...] = _rms(jnp.concatenate(cols, axis=1), gfin_ref[...])

    pl.when(i % 2 == 0)(lambda: step(buf_a, sems.at[0], buf_b, sems.at[1]))
    pl.when(i % 2 == 1)(lambda: step(buf_b, sems.at[1], buf_a, sems.at[0]))

    @pl.when(i == n_steps - 1)
    def _():
        pl.when(i % 2 == 0)(lambda: drain(buf_b, sems.at[1]))
        pl.when(i % 2 == 1)(lambda: drain(buf_a, sems.at[0]))


def _combine(dest, h2, gates, g_final, ypad, *, tile):
    n, d = h2.shape
    steps = n // tile
    kern = functools.partial(_combine_kernel, tile=tile)
    dest3 = dest.reshape(steps, 1, tile * TOP_K)
    buf = pltpu.VMEM((TOP_K * tile * ROW_TILE, LANES), F32)
    return pl.pallas_call(
        kern,
        grid=(steps,),
        in_specs=[
            pl.BlockSpec((1, 1, tile * TOP_K), lambda i: (i, 0, 0), memory_space=pltpu.SMEM),
            pl.BlockSpec((1, 1, tile * TOP_K), lambda i: (jnp.minimum(i + 1, steps - 1), 0, 0),
                         memory_space=pltpu.SMEM),
            pl.BlockSpec((tile, d), lambda i: (i, 0)),
            pl.BlockSpec((tile, LANES), lambda i: (i, 0)),
            _const_spec((1, d)),
            pl.BlockSpec(memory_space=pl.ANY),
        ],
        out_specs=pl.BlockSpec((tile, d), lambda i: (i, 0)),
        out_shape=jax.ShapeDtypeStruct((n, d), F32),
        scratch_shapes=[buf, buf, pltpu.SemaphoreType.DMA((2,))],
        compiler_params=pltpu.CompilerParams(dimension_semantics=("arbitrary",), vmem_limit_bytes=VMEM_LIMIT),
    )(dest3, dest3, h2, gates, g_final, ypad)


def _ssm_params(a_re, a_im, log_dt, b_re, b_im, c_re, c_im):
    dt = jnp.exp(log_dt)[:, None]
    mag = jnp.exp(a_re * dt)
    lb_re = mag * jnp.cos(a_im * dt)
    lb_im = mag * jnp.sin(a_im * dt)
    den = a_re * a_re + a_im * a_im
    q_re = ((lb_re - 1.0) * a_re + lb_im * a_im) / den
    q_im = (lb_im * a_re - (lb_re - 1.0) * a_im) / den
    bb_re = q_re[:, :, None] * b_re - q_im[:, :, None] * b_im
    bb_im = q_re[:, :, None] * b_im + q_im[:, :, None] * b_re
    eye = jnp.eye(SSM_PACK, dtype=F32)

    def pack_blockdiag(m):
        m = m.reshape(N_PACKS, SSM_PACK, m.shape[1], m.shape[2])
        return jnp.einsum('ngab,gh->ngahb', m, eye).reshape(N_PACKS, SSM_PACK * m.shape[2], SSM_PACK * m.shape[3])

    pw_re, pw_im = [jnp.ones_like(lb_re), lb_re], [jnp.zeros_like(lb_im), lb_im]
    for _ in range(RUN - 1):
        r, i = pw_re[-1], pw_im[-1]
        pw_re.append(r * lb_re - i * lb_im)
        pw_im.append(r * lb_im + i * lb_re)
    b_rows = []
    for dd in range(RUN):
        s_re = bb_re * pw_re[dd][:, :, None] - bb_im * pw_im[dd][:, :, None]
        s_im = bb_re * pw_im[dd][:, :, None] + bb_im * pw_re[dd][:, :, None]
        b_rows.append(jnp.concatenate([pack_blockdiag(s_re.transpose(0, 2, 1)),
                                       pack_blockdiag(s_im.transpose(0, 2, 1))], axis=2))
    bmat = jnp.concatenate(b_rows, axis=1)
    cmat_re = pack_blockdiag(c_re.transpose(0, 2, 1))
    cmat_im = pack_blockdiag(-c_im.transpose(0, 2, 1))
    rows_of = lambda pw: [pw[(r % RUN) + 1].reshape(-1) for r in range(SUBLANES)]
    slabbed = lambda rows: jnp.stack(rows).reshape(SUBLANES, N_SLABS, SLAB).transpose(1, 0, 2)
    return (bmat.astype(BF16), cmat_re.astype(BF16), cmat_im.astype(BF16),
            slabbed(rows_of(pw_re)), slabbed(rows_of(pw_im)))


def kernel(x_prompt, x_sample, mem_prompt, cache_mem_k, cache_mem_v, cache_pool, state_ssm_re, state_ssm_im, norm_mix, w_in, w_pool, pool_scale, ssm_a_re, ssm_a_im, ssm_log_dt, ssm_b_re, ssm_b_im, ssm_c_re, ssm_c_im, ssm_d, w_glu, b_glu, w_out, norm_xattn, norm_mem, w_q, w_k, w_v, w_o, norm_ffn, w_router, b_router, w_gate, b_gate, w_up, b_up, w_down, b_down, norm_final):
    assert x_prompt.shape[2] == D_MODEL and norm_mix.shape[0] == 1
    bp, tp, d = x_prompt.shape
    bs, ts, _ = x_sample.shape
    n_p, n_s = bp * tp, bs * ts
    row = lambda v: v.reshape(1, -1)

    bmat, c_re, c_im, t_re, t_im = _ssm_params(ssm_a_re[0], ssm_a_im[0], ssm_log_dt[0], ssm_b_re[0], ssm_b_im[0],
                                               ssm_c_re[0], ssm_c_im[0])
    mix_prm = dict(g_mix=row(norm_mix[0]), w_in=w_in[0].astype(BF16), w_pool=w_pool[0].astype(BF16),
                   pool_scale=row(pool_scale[0]), bmat=bmat, c_re=c_re, c_im=c_im, t_re=t_re, t_im=t_im,
                   d_skip=row(ssm_d[0]), w_glu=w_glu[0].astype(BF16), b_glu=row(b_glu[0]),
                   w_out=w_out[0].astype(BF16))

    attn_prm = dict(g_xattn=row(norm_xattn[0]), w_q=w_q[0].astype(BF16), w_o=w_o[0].astype(BF16),
                    g_ffn=row(norm_ffn[0]),
                    w_router=jnp.pad(w_router[0], ((0, 0), (0, LANES - N_EXPERTS))).astype(BF16),
                    b_router=jnp.pad(row(b_router[0]), ((0, 0), (0, LANES - N_EXPERTS)), constant_values=-jnp.inf))
    slab_state = lambda s: s.reshape(s.shape[0], N_SLABS, 1, SLAB)

    zeros_state = jnp.zeros((bp, N_SLABS, 1, SLAB), F32)
    h1_p, pool_p, sre_p, sim_p = _mixer(x_prompt, jnp.zeros((bp, HIST_ROWS, D_POOL), F32), zeros_state, zeros_state,
                                        mix_prm, tt=MIXER_TILE, pos0=0)
    mk, mv = _memkv(mem_prompt, row(norm_mem[0]), w_k[0].astype(BF16), w_v[0].astype(BF16))
    h2_p, xn_p, e_p, g_p, cnt_p = _attn(h1_p, mk, mv, attn_prm, tt=ATTN_TILE)

    hist_s = jnp.pad(cache_pool[0], ((0, 0), (HIST_ROWS - POOL_BUF, 0), (0, 0)))
    h1_s, pool_s, sre_s, sim_s = _mixer(x_sample, hist_s, slab_state(state_ssm_re[0]), slab_state(state_ssm_im[0]),
                                        mix_prm, tt=ts, pos0=PAST_LEN)
    ck = cache_mem_k[0].reshape(bs, N_MEM, d)
    cv = cache_mem_v[0].reshape(bs, N_MEM, d)
    h2_s, xn_s, e_s, g_s, cnt_s = _attn(h1_s, ck, cv, attn_prm, tt=ts)

    n_all = n_p + n_s
    counts = (cnt_p + cnt_s)[0, :N_EXPERTS].astype(jnp.int32)
    padded = (counts + MOE_BLOCK - 1) // MOE_BLOCK * MOE_BLOCK
    pend = jnp.cumsum(padded)
    pstart = pend - padded
    n_blocks = -(-(n_all * TOP_K + N_EXPERTS * (MOE_BLOCK - 1)) // MOE_BLOCK)
    blk_e = jnp.minimum(jnp.sum(pend[None, :] <= (jnp.arange(n_blocks) * MOE_BLOCK)[:, None], axis=1),
                        N_EXPERTS - 1).astype(jnp.int32)
    nvalid = (pend[-1:] // MOE_BLOCK).astype(jnp.int32)
    last_blk = jnp.maximum(pend // MOE_BLOCK - 1, 0).astype(jnp.int32)
    after = (pend // MOE_BLOCK)[blk_e]
    next_e = jnp.where(after < nvalid[0], blk_e[jnp.minimum(after, n_blocks - 1)], -1).astype(jnp.int32)
    start = jnp.pad(pstart.astype(F32), (0, LANES - N_EXPERTS)).reshape(1, LANES)
    e_s_tile = jnp.pad(e_s, ((0, RANK_TILE - n_s), (0, 0)), constant_values=-1)
    dest, xpad = _route(last_blk, nvalid, e_p, e_s_tile, start, xn_p, xn_s, n_blocks)
    dest = dest[:n_all, :TOP_K].reshape(-1)
    dest_p, dest_s = dest[:n_p * TOP_K], dest[n_p * TOP_K:]

    ypad = _experts(blk_e, nvalid, next_e, xpad, w_gate[0], b_gate[0], w_up[0], b_up[0], w_down[0], b_down[0], n_blocks)
    g_fin = row(norm_final)
    y_p = _combine(dest_p, h2_p.reshape(n_p, d), g_p, g_fin, ypad, tile=ROUTE_TILE)
    y_s = _combine(dest_s, h2_s.reshape(n_s, d), g_s, g_fin, ypad, tile=n_s)

    unslab = lambda s: s.reshape(1, s.shape[0], N_SSM_GROUPS, SSM_STATE)
    kv5 = lambda a: a.reshape(1, bp, N_MEM, N_XHEADS, XHEAD_DIM)
    return (y_p.reshape(bp, tp, d), y_s.reshape(bs, ts, d), kv5(mk), kv5(mv),
            pool_p[None, :, HIST_ROWS - POOL_BUF:], pool_s[None, :, HIST_ROWS - POOL_BUF:],
            unslab(sre_p), unslab(sim_p), unslab(sre_s), unslab(sim_s))
```

```python
import functools
import math

import jax
import jax.numpy as jnp
from jax import lax
from jax.experimental import pallas as pl
from jax.experimental.pallas import tpu as pltpu

D_MODEL = 1024
D_POOL = 512
D_SSM = 512
POOL_WINDOWS = (2, 4, 8, 16)
POOL_GROUP = 128
HIST_ROWS = 16
POOL_BUF = 15
SSM_GROUP = 16
N_SSM_GROUPS = 32
SSM_STATE = 64
N_STATE = N_SSM_GROUPS * SSM_STATE
N_MEM = 256
N_XHEADS = 4
XHEAD_DIM = 256
N_EXPERTS = 32
TOP_K = 4
SWIGLU_LIMIT = 7.0
SWIGLU_ALPHA = 1.702
MOE_BLOCK = 512
EPS = 1e-6
PAST_LEN = 1024

LANES = 128
SUBLANES = 8
SLAB = 256
N_SLABS = N_STATE // SLAB
SSM_PACK = LANES // SSM_GROUP
N_PACKS = N_SSM_GROUPS // SSM_PACK
PACK_STATES = SSM_PACK * SSM_STATE
SLABS_PER_PACK = PACK_STATES // SLAB
RUN = 4
ROW_TILE = D_MODEL // LANES
N_DMA_PRIORITIES = 2
PUSH_UNROLL = 4
VMEM_LIMIT = 56 * 1024 * 1024

MIXER_TILE = 512
ATTN_TILE = 1024
ROUTE_TILE = 256

BF16 = jnp.bfloat16
F32 = jnp.float32


def _rms(x, g):
    return x * lax.rsqrt(jnp.mean(x * x, axis=-1, keepdims=True) + EPS) * g


def _dot(a, b):
    return jnp.dot(a, b, preferred_element_type=F32)


def _chain_slab(s, tre_ref, tim_ref, hre_scr, him_scr, re_scr, im_scr, tt):
    t_re = tre_ref[s]
    t_im = tim_ref[s]
    upper = lax.broadcasted_iota(jnp.int32, (SUBLANES, SLAB), 0) < RUN
    ta_re, ta_im = jnp.where(upper, t_re, 0.0), jnp.where(upper, t_im, 0.0)
    tb_re, tb_im = jnp.where(upper, 0.0, t_re), jnp.where(upper, 0.0, t_im)
    c_re = hre_scr[s]
    c_im = him_scr[s]
    for v in range(tt // SUBLANES):
        rows = pl.ds(v * SUBLANES, SUBLANES)
        re = re_scr[rows, :]
        im = im_scr[rows, :]
        re, im = re + (ta_re * c_re - ta_im * c_im), im + (ta_re * c_im + ta_im * c_re)
        m_re = re[RUN - 1:RUN, :]
        m_im = im[RUN - 1:RUN, :]
        re, im = re + (tb_re * m_re - tb_im * m_im), im + (tb_re * m_im + tb_im * m_re)
        re_scr[rows, :] = re
        im_scr[rows, :] = im
        c_re = re[SUBLANES - 1:SUBLANES, :]
        c_im = im[SUBLANES - 1:SUBLANES, :]
    hre_scr[s] = c_re
    him_scr[s] = c_im


def _mixer_kernel(x_ref, hist_ref, h0re_ref, h0im_ref, gmix_ref, win_ref, wpool_ref, pscale_ref,
                  bmat_ref, cre_ref, cim_ref, tre_ref, tim_ref, dskip_ref, wglu_ref, bglu_ref, wout_ref,
                  h_ref, poolnew_ref, ssmre_ref, ssmim_ref,
                  hist_scr, hre_scr, him_scr, *slab_scr, tt, pos0):
    bure_scr, buim_scr = slab_scr[:N_SLABS], slab_scr[N_SLABS:]
    j = pl.program_id(1)

    @pl.when(j == 0)
    def _():
        hist_scr[...] = hist_ref[0]
        hre_scr[...] = h0re_ref[0]
        him_scr[...] = h0im_ref[0]

    x = x_ref[0]
    xn = _rms(x, gmix_ref[...])
    z = _dot(xn.astype(BF16), win_ref[...])
    zp = z[:, :D_POOL]
    u = z[:, D_POOL:]

    ext = jnp.concatenate([hist_scr[...], zp], axis=0)
    pos = pos0 + j * tt + lax.broadcasted_iota(jnp.int32, (tt, 1), 0)
    acc = ext
    outs = []
    for gi, w in enumerate(POOL_WINDOWS):
        lo = gi * POOL_GROUP
        acc = acc[:, POOL_GROUP * (1 if gi else 0):]
        acc = acc + pltpu.roll(acc, w // 2, 0)
        wsum = acc[HIST_ROWS:, :POOL_GROUP]
        cnt = jnp.minimum(pos + 1, w).astype(F32)
        d = wsum / cnt - zp[:, lo:lo + POOL_GROUP]
        outs.append(_dot(d.astype(BF16), wpool_ref[gi]))
    y_pool = jnp.concatenate(outs, axis=1) * pscale_ref[...]
    hist_scr[...] = ext[tt:tt + HIST_ROWS]
    poolnew_ref[0] = ext[tt:tt + HIST_ROWS]

    run_row = lax.broadcasted_iota(jnp.int32, (tt, 1), 0) % RUN
    shifted = [u.astype(BF16)]
    for dd in range(1, RUN):
        shifted.append(jnp.where(run_row >= dd, pltpu.roll(u, dd, 0), 0.0).astype(BF16))
    def project(c):
        lhs = jnp.concatenate([sh[:, c * LANES:(c + 1) * LANES] for sh in shifted], axis=1)
        bu = _dot(lhs, bmat_ref[c])
        for i in range(SLABS_PER_PACK):
            s = c * SLABS_PER_PACK + i
            bure_scr[s][...] = bu[:, i * SLAB:(i + 1) * SLAB]
            buim_scr[s][...] = bu[:, PACK_STATES + i * SLAB:PACK_STATES + (i + 1) * SLAB]

    ys = []
    project(0)
    for c in range(N_PACKS):
        slabs = range(c * SLABS_PER_PACK, (c + 1) * SLABS_PER_PACK)
        if c + 1 < N_PACKS:
            project(c + 1)
        for s in slabs:
            _chain_slab(s, tre_ref, tim_ref, hre_scr, him_scr, bure_scr[s], buim_scr[s], tt)
        hs_re = jnp.concatenate([bure_scr[s][...] for s in slabs], axis=1)
        hs_im = jnp.concatenate([buim_scr[s][...] for s in slabs], axis=1)
        ys.append(_dot(hs_re.astype(BF16), cre_ref[c]) + _dot(hs_im.astype(BF16), cim_ref[c]))
    ssmre_ref[0] = hre_scr[...]
    ssmim_ref[0] = him_scr[...]
    y = jnp.concatenate(ys, axis=1) + dskip_ref[...] * u
    g = 0.5 * y * (1.0 + jnp.tanh(math.sqrt(2.0 / math.pi) * (y + 0.044715 * (y * y * y))))
    y_ssm = g * jax.nn.sigmoid(_dot(g.astype(BF16), wglu_ref[...]) + bglu_ref[...])

    mix = jnp.concatenate([y_pool, y_ssm], axis=1)
    h_ref[0] = x + _dot(mix.astype(BF16), wout_ref[...])


def _const_spec(shape):
    return pl.BlockSpec(shape, lambda *_: (0,) * len(shape))


def _mixer(x, hist, h0re, h0im, prm, *, tt, pos0):
    b, t, d = x.shape
    kern = functools.partial(_mixer_kernel, tt=tt, pos0=pos0)
    per_b3 = lambda shp: pl.BlockSpec((1,) + shp, lambda i, j: (i, 0, 0))
    per_b4 = lambda shp: pl.BlockSpec((1,) + shp, lambda i, j: (i, 0, 0, 0))
    return pl.pallas_call(
        kern,
        grid=(b, t // tt),
        in_specs=[
            pl.BlockSpec((1, tt, d), lambda i, j: (i, j, 0)),
            per_b3((HIST_ROWS, D_POOL)),
            per_b4((N_SLABS, 1, SLAB)),
            per_b4((N_SLABS, 1, SLAB)),
            _const_spec((1, d)),
            _const_spec((d, d)),
            _const_spec((len(POOL_WINDOWS), POOL_GROUP, POOL_GROUP)),
            _const_spec((1, D_POOL)),
            _const_spec((N_PACKS, RUN * LANES, 2 * PACK_STATES)),
            _const_spec((N_PACKS, PACK_STATES, LANES)),
            _const_spec((N_PACKS, PACK_STATES, LANES)),
            _const_spec((N_SLABS, SUBLANES, SLAB)),
            _const_spec((N_SLABS, SUBLANES, SLAB)),
            _const_spec((1, D_SSM)),
            _const_spec((D_SSM, D_SSM)),
            _const_spec((1, D_SSM)),
            _const_spec((d, d)),
        ],
        out_specs=[
            pl.BlockSpec((1, tt, d), lambda i, j: (i, j, 0)),
            per_b3((HIST_ROWS, D_POOL)),
            per_b4((N_SLABS, 1, SLAB)),
            per_b4((N_SLABS, 1, SLAB)),
        ],
        out_shape=[
            jax.ShapeDtypeStruct((b, t, d), F32),
            jax.ShapeDtypeStruct((b, HIST_ROWS, D_POOL), F32),
            jax.ShapeDtypeStruct((b, N_SLABS, 1, SLAB), F32),
            jax.ShapeDtypeStruct((b, N_SLABS, 1, SLAB), F32),
        ],
        scratch_shapes=[
            pltpu.VMEM((HIST_ROWS, D_POOL), F32),
            pltpu.VMEM((N_SLABS, 1, SLAB), F32),
            pltpu.VMEM((N_SLABS, 1, SLAB), F32),
        ] + [pltpu.VMEM((tt, SLAB), F32)] * (2 * N_SLABS),
        compiler_params=pltpu.CompilerParams(
            dimension_semantics=("arbitrary", "arbitrary"), vmem_limit_bytes=VMEM_LIMIT),
    )(x, hist, h0re, h0im, prm['g_mix'], prm['w_in'], prm['w_pool'], prm['pool_scale'],
      prm['bmat'], prm['c_re'], prm['c_im'], prm['t_re'], prm['t_im'], prm['d_skip'],
      prm['w_glu'], prm['b_glu'], prm['w_out'])


def _memkv_kernel(m_ref, g_ref, wk_ref, wv_ref, k_ref, v_ref):
    m = _rms(m_ref[0], g_ref[...]).astype(BF16)
    k_ref[0] = _dot(m, wk_ref[...])
    v_ref[0] = _dot(m, wv_ref[...])


def _memkv(mem, g, wk, wv):
    b, n, d = mem.shape
    blk = pl.BlockSpec((1, n, d), lambda i: (i, 0, 0))
    return pl.pallas_call(
        _memkv_kernel,
        grid=(b,),
        in_specs=[blk, _const_spec((1, d)), _const_spec((d, d)), _const_spec((d, d))],
        out_specs=[blk, blk],
        out_shape=[jax.ShapeDtypeStruct((b, n, d), F32)] * 2,
        compiler_params=pltpu.CompilerParams(vmem_limit_bytes=VMEM_LIMIT),
    )(mem, g, wk, wv)


def _attn_kernel(h_ref, k_ref, v_ref, gx_ref, wq_ref, wo_ref, gf_ref, wr_ref, br_ref,
                 h2_ref, xn_ref, e_ref, gate_ref, cnt_ref, cnt_scr, kt_scr, v_scr, *, tt):
    @pl.when(pl.program_id(1) == 0)
    def _():
        kt_scr[...] = k_ref[0].T.astype(BF16)
        v_scr[...] = v_ref[0].astype(BF16)

    h = h_ref[0]
    hn = _rms(h, gx_ref[...])
    q = _dot(hn.astype(BF16), wq_ref[...])
    outs = []
    for hd in range(N_XHEADS):
        cols = slice(hd * XHEAD_DIM, (hd + 1) * XHEAD_DIM)
        s = _dot(q[:, cols].astype(BF16), kt_scr[cols, :]) * (XHEAD_DIM ** -0.5)
        p = jnp.exp(s - jnp.max(s, axis=-1, keepdims=True))
        p = p / jnp.sum(p, axis=-1, keepdims=True)
        outs.append(_dot(p.astype(BF16), v_scr[:, cols]))
    o = jnp.concatenate(outs, axis=1)
    h2 = h + _dot(o.astype(BF16), wo_ref[...])
    h2_ref[0] = h2

    xn = _rms(h2, gf_ref[...])
    for s in range(ROW_TILE):
        xn_ref[pl.ds(s, tt, stride=ROW_TILE), :] = xn[:, s * LANES:(s + 1) * LANES]
    logits = _dot(xn.astype(BF16), wr_ref[...]) + br_ref[...]
    lane = lax.broadcasted_iota(jnp.int32, logits.shape, 1)
    lane_f = lane.astype(F32)
    e_out = jnp.zeros(logits.shape, jnp.int32)
    hits = jnp.zeros(logits.shape, F32)
    top = []
    for k in range(TOP_K):
        m = jnp.max(logits, axis=-1, keepdims=True)
        idx = jnp.min(jnp.where(logits == m, lane_f, float(LANES)), axis=-1, keepdims=True)
        e_out = jnp.where(lane == k, idx.astype(jnp.int32), e_out)
        top.append(m)
        chosen = lane_f == idx
        hits = hits + chosen.astype(F32)
        logits = jnp.where(chosen, -jnp.inf, logits)

    @pl.when(jnp.logical_and(pl.program_id(0) == 0, pl.program_id(1) == 0))
    def _():
        cnt_scr[...] = jnp.zeros_like(cnt_scr)
    cnt_scr[...] = cnt_scr[...] + jnp.sum(hits, axis=0, keepdims=True)
    cnt_ref[...] = cnt_scr[...]
    ex = [jnp.exp(m - top[0]) for m in top]
    tot = ex[0] + ex[1] + ex[2] + ex[3]
    g_out = jnp.zeros(logits.shape, F32)
    for k in range(TOP_K):
        g_out = jnp.where(lane == k, ex[k] / tot, g_out)
    e_ref[...] = e_out
    gate_ref[...] = g_out


def _attn(h, k, v, prm, *, tt):
    b, t, d = h.shape
    n = b * t
    kern = functools.partial(_attn_kernel, tt=tt)
    nt = t // tt
    kv = pl.BlockSpec((1, N_MEM, d), lambda i, j: (i, 0, 0))
    tok = lambda width: pl.BlockSpec((tt, width), lambda i, j: (i * nt + j, 0))
    return pl.pallas_call(
        kern,
        grid=(b, nt),
        in_specs=[
            pl.BlockSpec((1, tt, d), lambda i, j: (i, j, 0)), kv, kv,
            _const_spec((1, d)), _const_spec((d, d)), _const_spec((d, d)),
            _const_spec((1, d)), _const_spec((d, LANES)), _const_spec((1, LANES)),
        ],
        out_specs=[
            pl.BlockSpec((1, tt, d), lambda i, j: (i, j, 0)),
            pl.BlockSpec((tt * ROW_TILE, LANES), lambda i, j: (i * nt + j, 0)),
            tok(LANES), tok(LANES), _const_spec((1, LANES)),
        ],
        out_shape=[
            jax.ShapeDtypeStruct((b, t, d), F32),
            jax.ShapeDtypeStruct((n * ROW_TILE, LANES), F32),
            jax.ShapeDtypeStruct((n, LANES), jnp.int32),
            jax.ShapeDtypeStruct((n, LANES), F32),
            jax.ShapeDtypeStruct((1, LANES), F32),
        ],
        scratch_shapes=[pltpu.VMEM((1, LANES), F32), pltpu.VMEM((d, N_MEM), BF16), pltpu.VMEM((N_MEM, d), BF16)],
        compiler_params=pltpu.CompilerParams(
            dimension_semantics=("arbitrary", "arbitrary"), vmem_limit_bytes=VMEM_LIMIT),
    )(h, k, v, prm['g_xattn'], prm['w_q'], prm['w_o'], prm['g_ffn'], prm['w_router'], prm['b_router'])


RANK_TILE = 512
BLOCK_ROWS = MOE_BLOCK * ROW_TILE


def _row_copy_wait(src_ref, dst_ref, sem, rows):
    pltpu.make_async_copy(src_ref.at[pl.ds(0, rows * ROW_TILE)], dst_ref.at[pl.ds(0, rows * ROW_TILE)], sem).wait()


def _route_kernel(last_blk_ref, nvalid_ref, ep_ref, es_ref, start_ref, xp_hbm, xs_ref, dest_ref, xpad_ref,
                  carry_scr, before_scr, dest_smem, zero_buf, xbuf, sem, zero_sem, smem_sem, load_sems,
                  *, steps_p, tile_s, n_blocks):
    i = pl.program_id(0)

    def zero_copy(blk):
        dst = xpad_ref.at[pl.ds(pl.multiple_of(blk * BLOCK_ROWS, BLOCK_ROWS), BLOCK_ROWS)]
        return pltpu.make_async_copy(zero_buf, dst, zero_sem)

    @pl.when(i == 0)
    def _():
        carry_scr[...] = start_ref[...]
        r = lax.broadcasted_iota(jnp.int32, (RANK_TILE, RANK_TILE), 0)
        c = lax.broadcasted_iota(jnp.int32, (RANK_TILE, RANK_TILE), 1)
        before_scr[...] = (c < r).astype(BF16)

        zero_buf[...] = jnp.zeros_like(zero_buf)

        def on_pad_blocks(fn):
            for e in range(N_EXPERTS):
                if e == 0:
                    fn(last_blk_ref[0])
                else:
                    pl.when(last_blk_ref[e] != last_blk_ref[e - 1])(functools.partial(fn, last_blk_ref[e]))
            lax.fori_loop(nvalid_ref[0], n_blocks, lambda blk, c: (fn(blk), c)[1], 0)

        on_pad_blocks(lambda blk: zero_copy(blk).start())
        on_pad_blocks(lambda blk: zero_copy(blk).wait())

    e = jnp.where(i < steps_p, ep_ref[...], es_ref[...])
    lane = lax.broadcasted_iota(jnp.int32, e.shape, 1)
    onehot = [lane == e[:, k:k + 1] for k in range(TOP_K)]
    hits = jnp.zeros(e.shape, F32)
    for oh in onehot:
        hits = hits + oh.astype(F32)
    base = _dot(before_scr[...], hits.astype(BF16)) + carry_scr[...]
    rank = jnp.zeros(e.shape, jnp.int32)
    for k, oh in enumerate(onehot):
        rk = jnp.sum(jnp.where(oh, base, 0.0), axis=-1, keepdims=True).astype(jnp.int32)
        rank = jnp.where(lane == k, rk, rank)
    dest_ref[...] = rank
    carry_scr[...] = carry_scr[...] + jnp.sum(hits, axis=0, keepdims=True)

    tile_rows = RANK_TILE * ROW_TILE
    slot = i % 2

    def load(tile, into):
        src = xp_hbm.at[pl.ds(pl.multiple_of(tile * tile_rows, tile_rows), tile_rows)]
        return pltpu.make_async_copy(src, xbuf.at[into], load_sems.at[into])

    pl.when(i == 0)(lambda: load(0, 0).start())

    @pl.when(i > 0)
    def _():
        for _ in range(TOP_K):
            _row_copy_wait(xbuf.at[0], xpad_ref, sem, RANK_TILE)

    pl.when(i + 1 < steps_p)(lambda: load(i + 1, 1 - slot).start())

    to_smem = pltpu.make_async_copy(dest_ref, dest_smem, smem_sem)
    to_smem.start()
    to_smem.wait()

    def push(rows_ref, tile):
        def body(it, carry):
            for j in range(PUSH_UNROLL):
                t = it * PUSH_UNROLL + j
                src = rows_ref.at[pl.ds(pl.multiple_of(t * ROW_TILE, ROW_TILE), ROW_TILE)]
                for k in range(TOP_K):
                    row = dest_smem[t, k]
                    dst = xpad_ref.at[pl.ds(pl.multiple_of(row * ROW_TILE, ROW_TILE), ROW_TILE)]
                    pltpu.make_async_copy(src, dst, sem).start(priority=k % N_DMA_PRIORITIES)
            return carry

        lax.fori_loop(0, tile // PUSH_UNROLL, body, 0)

    @pl.when(i < steps_p)
    def _():
        load(i, slot).wait()
        push(xbuf.at[slot], RANK_TILE)

    @pl.when(i == steps_p)
    def _():
        push(xs_ref, tile_s)
        for _ in range(TOP_K):
            _row_copy_wait(xs_ref, xpad_ref, sem, tile_s)


def _route(last_blk, nvalid, e_p, e_s, start, xn_p, xn_s, n_blocks):
    steps_p = e_p.shape[0] // RANK_TILE
    tile_s = xn_s.shape[0] // ROW_TILE
    blk = (RANK_TILE, LANES)
    const = lambda shape: pl.BlockSpec(shape, lambda i, lb, nv: (0,) * len(shape))
    hbm = pl.BlockSpec(memory_space=pl.ANY)
    return pl.pallas_call(
        functools.partial(_route_kernel, steps_p=steps_p, tile_s=tile_s, n_blocks=n_blocks),
        grid_spec=pltpu.PrefetchScalarGridSpec(
            num_scalar_prefetch=2,
            grid=(steps_p + 1,),
            in_specs=[pl.BlockSpec(blk, lambda i, lb, nv: (jnp.minimum(i, steps_p - 1), 0)), const(blk),
                      const((1, LANES)), hbm, const(xn_s.shape)],
            out_specs=[pl.BlockSpec(blk, lambda i, lb, nv: (i, 0)), hbm],
            scratch_shapes=[pltpu.VMEM((1, LANES), F32), pltpu.VMEM((RANK_TILE, RANK_TILE), BF16),
                            pltpu.SMEM(blk, jnp.int32), pltpu.VMEM((BLOCK_ROWS, LANES), F32),
                            pltpu.VMEM((2, RANK_TILE * ROW_TILE, LANES), F32),
                            pltpu.SemaphoreType.DMA, pltpu.SemaphoreType.DMA, pltpu.SemaphoreType.DMA,
                            pltpu.SemaphoreType.DMA((2,))],
        ),
        out_shape=[jax.ShapeDtypeStruct(((steps_p + 1) * RANK_TILE, LANES), jnp.int32),
                   jax.ShapeDtypeStruct((n_blocks * BLOCK_ROWS, LANES), F32)],
        compiler_params=pltpu.CompilerParams(dimension_semantics=("arbitrary",)),
    )(last_blk, nvalid, e_p, e_s, start, xn_p, xn_s)


def _expert_kernel(blk_e_ref, nvalid_ref, next_e_ref, x_ref, bg_ref, bu_ref, bd_ref, wg_hbm, wu_hbm, wd_hbm, y_ref,
                   stage, w_bf, sems):
    b = pl.program_id(0)
    valid = b < nvalid_ref[0]
    e = blk_e_ref[b]
    weights = (wg_hbm, wu_hbm, wd_hbm)

    def fetch(expert):
        return [pltpu.make_async_copy(w.at[expert], stage.at[i], sems.at[i]) for i, w in enumerate(weights)]

    @pl.when(b == 0)
    def _():
        for cp in fetch(e):
            cp.start()

    @pl.when(jnp.logical_and(valid, jnp.logical_or(b == 0, blk_e_ref[jnp.maximum(b - 1, 0)] != e)))
    def _():
        for i, cp in enumerate(fetch(e)):
            cp.wait()
            w_bf[i] = stage[i].astype(BF16)

        @pl.when(next_e_ref[b] >= 0)
        def _():
            for cp in fetch(next_e_ref[b]):
                cp.start()

    @pl.when(valid)
    def _():
        x = jnp.concatenate([x_ref[pl.ds(s, MOE_BLOCK, stride=ROW_TILE), :] for s in range(ROW_TILE)], axis=1)
        x = x.astype(BF16)
        g = _dot(x, w_bf[0]) + bg_ref[0]
        u = _dot(x, w_bf[1]) + bu_ref[0]
        g = jnp.minimum(g, SWIGLU_LIMIT)
        u = jnp.clip(u, -SWIGLU_LIMIT, SWIGLU_LIMIT)
        hdn = g * (0.5 * (1.0 + jnp.tanh((0.5 * SWIGLU_ALPHA) * g))) * (u + 1.0)
        y = _dot(hdn.astype(BF16), w_bf[2]) + bd_ref[0]
        for s in range(ROW_TILE):
            y_ref[pl.ds(s, MOE_BLOCK, stride=ROW_TILE), :] = y[:, s * LANES:(s + 1) * LANES]

    @pl.when(b >= nvalid_ref[0])
    def _():
        y_ref[...] = jnp.zeros_like(y_ref)


def _experts(blk_e, nvalid, next_e, xpad, w_gate, b_gate, w_up, b_up, w_down, b_down, n_blocks):
    d = D_MODEL
    n_w = 3
    bspec = pl.BlockSpec((1, 1, d), lambda b, be, nv, ne: (be[b], 0, 0))
    xspec = pl.BlockSpec((BLOCK_ROWS, LANES), lambda b, be, nv, ne: (jnp.minimum(b, nv[0] - 1), 0))
    hbm = pl.BlockSpec(memory_space=pl.ANY)
    return pl.pallas_call(
        _expert_kernel,
        grid_spec=pltpu.PrefetchScalarGridSpec(
            num_scalar_prefetch=3,
            grid=(n_blocks,),
            in_specs=[xspec, bspec, bspec, bspec, hbm, hbm, hbm],
            out_specs=pl.BlockSpec((BLOCK_ROWS, LANES), lambda b, be, nv, ne: (b, 0)),
            scratch_shapes=[pltpu.VMEM((n_w, d, d), F32), pltpu.VMEM((n_w, d, d), BF16),
                            pltpu.SemaphoreType.DMA((n_w,))],
        ),
        out_shape=jax.ShapeDtypeStruct((n_blocks * BLOCK_ROWS, LANES), F32),
        compiler_params=pltpu.CompilerParams(dimension_semantics=("arbitrary",), vmem_limit_bytes=VMEM_LIMIT),
    )(blk_e, nvalid, next_e, xpad, b_gate.reshape(N_EXPERTS, 1, d), b_up.reshape(N_EXPERTS, 1, d),
      b_down.reshape(N_EXPERTS, 1, d), w_gate, w_up, w_down)


def _combine_kernel(dest_ref, dest_next_ref, h_ref, gate_ref, gfin_ref, ypad_ref, y_ref, buf_a, buf_b, sems, *, tile):
    i = pl.program_id(0)
    n_steps = pl.num_programs(0)

    def gather(dref, buf, sem, t, k):
        row = dref[0, 0, t * TOP_K + k]
        slot = (k * tile + t) * ROW_TILE
        if not isinstance(slot, int):
            slot = pl.multiple_of(slot, ROW_TILE)
        src = ypad_ref.at[pl.ds(pl.multiple_of(row * ROW_TILE, ROW_TILE), ROW_TILE)]
        pltpu.make_async_copy(src, buf.at[pl.ds(slot, ROW_TILE)], sem).start(priority=k % N_DMA_PRIORITIES)

    def drain(buf, sem):
        for _ in range(TOP_K):
            _row_copy_wait(ypad_ref, buf, sem, tile)

    @pl.when(i == 0)
    def _():
        def body(t, carry):
            for k in range(TOP_K):
                gather(dest_ref, buf_a, sems.at[0], t, k)
            return carry
        lax.fori_loop(0, tile, body, 0)

    def step(cur, cur_sem, nxt, nxt_sem):
        drain(cur, cur_sem)
        gates = gate_ref[...]
        h = h_ref[...]
        per_slab = tile // ROW_TILE
        cols = []
        for s in range(ROW_TILE):
            for t in range(s * per_slab, (s + 1) * per_slab):
                for k in range(TOP_K):
                    gather(dest_next_ref, nxt, nxt_sem, t, k)
            acc = h[:, s * LANES:(s + 1) * LANES]
            for k in range(TOP_K):
                acc = acc + gates[:, k:k + 1] * cur[pl.ds(k * tile * ROW_TILE + s, tile, stride=ROW_TILE), :]
            cols.append(acc)
        y_ref[...] = _rms(jnp.concatenate(cols, axis=1), gfin_ref[...])

    pl.when(i % 2 == 0)(lambda: step(buf_a, sems.at[0], buf_b, sems.at[1]))
    pl.when(i % 2 == 1)(lambda: step(buf_b, sems.at[1], buf_a, sems.at[0]))

    @pl.when(i == n_steps - 1)
    def _():
        pl.when(i % 2 == 0)(lambda: drain(buf_b, sems.at[1]))
        pl.when(i % 2 == 1)(lambda: drain(buf_a, sems.at[0]))


def _combine(dest, h2, gates, g_final, ypad, *, tile):
    n, d = h2.shape
    steps = n // tile
    kern = functools.partial(_combine_kernel, tile=tile)
    dest3 = dest.reshape(steps, 1, tile * TOP_K)
    buf = pltpu.VMEM((TOP_K * tile * ROW_TILE, LANES), F32)
    return pl.pallas_call(
        kern,
        grid=(steps,),
        in_specs=[
            pl.BlockSpec((1, 1, tile * TOP_K), lambda i: (i, 0, 0), memory_space=pltpu.SMEM),
            pl.BlockSpec((1, 1, tile * TOP_K), lambda i: (jnp.minimum(i + 1, steps - 1), 0, 0),
                         memory_space=pltpu.SMEM),
            pl.BlockSpec((tile, d), lambda i: (i, 0)),
            pl.BlockSpec((tile, LANES), lambda i: (i, 0)),
            _const_spec((1, d)),
            pl.BlockSpec(memory_space=pl.ANY),
        ],
        out_specs=pl.BlockSpec((tile, d), lambda i: (i, 0)),
        out_shape=jax.ShapeDtypeStruct((n, d), F32),
        scratch_shapes=[buf, buf, pltpu.SemaphoreType.DMA((2,))],
        compiler_params=pltpu.CompilerParams(dimension_semantics=("arbitrary",), vmem_limit_bytes=VMEM_LIMIT),
    )(dest3, dest3, h2, gates, g_final, ypad)


def _ssm_params(a_re, a_im, log_dt, b_re, b_im, c_re, c_im):
    dt = jnp.exp(log_dt)[:, None]
    mag = jnp.exp(a_re * dt)
    lb_re = mag * jnp.cos(a_im * dt)
    lb_im = mag * jnp.sin(a_im * dt)
    den = a_re * a_re + a_im * a_im
    q_re = ((lb_re - 1.0) * a_re + lb_im * a_im) / den
    q_im = (lb_im * a_re - (lb_re - 1.0) * a_im) / den
    bb_re = q_re[:, :, None] * b_re - q_im[:, :, None] * b_im
    bb_im = q_re[:, :, None] * b_im + q_im[:, :, None] * b_re
    eye = jnp.eye(SSM_PACK, dtype=F32)

    def pack_blockdiag(m):
        m = m.reshape(N_PACKS, SSM_PACK, m.shape[1], m.shape[2])
        return jnp.einsum('ngab,gh->ngahb', m, eye).reshape(N_PACKS, SSM_PACK * m.shape[2], SSM_PACK * m.shape[3])

    pw_re, pw_im = [jnp.ones_like(lb_re), lb_re], [jnp.zeros_like(lb_im), lb_im]
    for _ in range(RUN - 1):
        r, i = pw_re[-1], pw_im[-1]
        pw_re.append(r * lb_re - i * lb_im)
        pw_im.append(r * lb_im + i * lb_re)
    b_rows = []
    for dd in range(RUN):
        s_re = bb_re * pw_re[dd][:, :, None] - bb_im * pw_im[dd][:, :, None]
        s_im = bb_re * pw_im[dd][:, :, None] + bb_im * pw_re[dd][:, :, None]
        b_rows.append(jnp.concatenate([pack_blockdiag(s_re.transpose(0, 2, 1)),
                                       pack_blockdiag(s_im.transpose(0, 2, 1))], axis=2))
    bmat = jnp.concatenate(b_rows, axis=1)
    cmat_re = pack_blockdiag(c_re.transpose(0, 2, 1))
    cmat_im = pack_blockdiag(-c_im.transpose(0, 2, 1))
    rows_of = lambda pw: [pw[(r % RUN) + 1].reshape(-1) for r in range(SUBLANES)]
    slabbed = lambda rows: jnp.stack(rows).reshape(SUBLANES, N_SLABS, SLAB).transpose(1, 0, 2)
    return (bmat.astype(BF16), cmat_re.astype(BF16), cmat_im.astype(BF16),
            slabbed(rows_of(pw_re)), slabbed(rows_of(pw_im)))


def kernel(x_prompt, x_sample, mem_prompt, cache_mem_k, cache_mem_v, cache_pool, state_ssm_re, state_ssm_im, norm_mix, w_in, w_pool, pool_scale, ssm_a_re, ssm_a_im, ssm_log_dt, ssm_b_re, ssm_b_im, ssm_c_re, ssm_c_im, ssm_d, w_glu, b_glu, w_out, norm_xattn, norm_mem, w_q, w_k, w_v, w_o, norm_ffn, w_router, b_router, w_gate, b_gate, w_up, b_up, w_down, b_down, norm_final):
    assert x_prompt.shape[2] == D_MODEL and norm_mix.shape[0] == 1
    bp, tp, d = x_prompt.shape
    bs, ts, _ = x_sample.shape
    n_p, n_s = bp * tp, bs * ts
    row = lambda v: v.reshape(1, -1)

    bmat, c_re, c_im, t_re, t_im = _ssm_params(ssm_a_re[0], ssm_a_im[0], ssm_log_dt[0], ssm_b_re[0], ssm_b_im[0],
                                               ssm_c_re[0], ssm_c_im[0])
    mix_prm = dict(g_mix=row(norm_mix[0]), w_in=w_in[0].astype(BF16), w_pool=w_pool[0].astype(BF16),
                   pool_scale=row(pool_scale[0]), bmat=bmat, c_re=c_re, c_im=c_im, t_re=t_re, t_im=t_im,
                   d_skip=row(ssm_d[0]), w_glu=w_glu[0].astype(BF16), b_glu=row(b_glu[0]),
                   w_out=w_out[0].astype(BF16))

    attn_prm = dict(g_xattn=row(norm_xattn[0]), w_q=w_q[0].astype(BF16), w_o=w_o[0].astype(BF16),
                    g_ffn=row(norm_ffn[0]),
                    w_router=jnp.pad(w_router[0], ((0, 0), (0, LANES - N_EXPERTS))).astype(BF16),
                    b_router=jnp.pad(row(b_router[0]), ((0, 0), (0, LANES - N_EXPERTS)), constant_values=-jnp.inf))
    slab_state = lambda s: s.reshape(s.shape[0], N_SLABS, 1, SLAB)

    zeros_state = jnp.zeros((bp, N_SLABS, 1, SLAB), F32)
    h1_p, pool_p, sre_p, sim_p = _mixer(x_prompt, jnp.zeros((bp, HIST_ROWS, D_POOL), F32), zeros_state, zeros_state,
                                        mix_prm, tt=MIXER_TILE, pos0=0)
    mk, mv = _memkv(mem_prompt, row(norm_mem[0]), w_k[0].astype(BF16), w_v[0].astype(BF16))
    h2_p, xn_p, e_p, g_p, cnt_p = _attn(h1_p, mk, mv, attn_prm, tt=ATTN_TILE)

    hist_s = jnp.pad(cache_pool[0], ((0, 0), (HIST_ROWS - POOL_BUF, 0), (0, 0)))
    h1_s, pool_s, sre_s, sim_s = _mixer(x_sample, hist_s, slab_state(state_ssm_re[0]), slab_state(state_ssm_im[0]),
                                        mix_prm, tt=ts, pos0=PAST_LEN)
    ck = cache_mem_k[0].reshape(bs, N_MEM, d)
    cv = cache_mem_v[0].reshape(bs, N_MEM, d)
    h2_s, xn_s, e_s, g_s, cnt_s = _attn(h1_s, ck, cv, attn_prm, tt=ts)

    n_all = n_p + n_s
    counts = (cnt_p + cnt_s)[0, :N_EXPERTS].astype(jnp.int32)
    padded = (counts + MOE_BLOCK - 1) // MOE_BLOCK * MOE_BLOCK
    pend = jnp.cumsum(padded)
    pstart = pend - padded
    n_blocks = -(-(n_all * TOP_K + N_EXPERTS * (MOE_BLOCK - 1)) // MOE_BLOCK)
    blk_e = jnp.minimum(jnp.sum(pend[None, :] <= (jnp.arange(n_blocks) * MOE_BLOCK)[:, None], axis=1),
                        N_EXPERTS - 1).astype(jnp.int32)
    nvalid = (pend[-1:] // MOE_BLOCK).astype(jnp.int32)
    last_blk = jnp.maximum(pend // MOE_BLOCK - 1, 0).astype(jnp.int32)
    after = (pend // MOE_BLOCK)[blk_e]
    next_e = jnp.where(after < nvalid[0], blk_e[jnp.minimum(after, n_blocks - 1)], -1).astype(jnp.int32)
    start = jnp.pad(pstart.astype(F32), (0, LANES - N_EXPERTS)).reshape(1, LANES)
    e_s_tile = jnp.pad(e_s, ((0, RANK_TILE - n_s), (0, 0)), constant_values=-1)
    dest, xpad = _route(last_blk, nvalid, e_p, e_s_tile, start, xn_p, xn_s, n_blocks)
    dest = dest[:n_all, :TOP_K].reshape(-1)
    dest_p, dest_s = dest[:n_p * TOP_K], dest[n_p * TOP_K:]

    ypad = _experts(blk_e, nvalid, next_e, xpad, w_gate[0], b_gate[0], w_up[0], b_up[0], w_down[0], b_down[0], n_blocks)
    g_fin = row(norm_final)
    y_p = _combine(dest_p, h2_p.reshape(n_p, d), g_p, g_fin, ypad, tile=ROUTE_TILE)
    y_s = _combine(dest_s, h2_s.reshape(n_s, d), g_s, g_fin, ypad, tile=n_s)

    unslab = lambda s: s.reshape(1, s.shape[0], N_SSM_GROUPS, SSM_STATE)
    kv5 = lambda a: a.reshape(1, bp, N_MEM, N_XHEADS, XHEAD_DIM)
    return (y_p.reshape(bp, tp, d), y_s.reshape(bs, ts, d), kv5(mk), kv5(mv),
            pool_p[None, :, HIST_ROWS - POOL_BUF:], pool_s[None, :, HIST_ROWS - POOL_BUF:],
            unslab(sre_p), unslab(sim_p), unslab(sre_s), unslab(sim_s))
```

```python
import functools
import math

import jax
import jax.numpy as jnp
from jax import lax
from jax.experimental import pallas as pl
from jax.experimental.pallas import tpu as pltpu

D_MODEL = 1024
D_POOL = 512
D_SSM = 512
POOL_WINDOWS = (2, 4, 8, 16)
POOL_GROUP = 128
HIST_ROWS = 16
POOL_BUF = 15
SSM_GROUP = 16
N_SSM_GROUPS = 32
SSM_STATE = 64
N_STATE = N_SSM_GROUPS * SSM_STATE
N_MEM = 256
N_XHEADS = 4
XHEAD_DIM = 256
N_EXPERTS = 32
TOP_K = 4
SWIGLU_LIMIT = 7.0
SWIGLU_ALPHA = 1.702
MOE_BLOCK = 512
EPS = 1e-6
PAST_LEN = 1024

LANES = 128
SUBLANES = 8
SLAB = 256
N_SLABS = N_STATE // SLAB
SSM_PACK = LANES // SSM_GROUP
N_PACKS = N_SSM_GROUPS // SSM_PACK
PACK_STATES = SSM_PACK * SSM_STATE
SLABS_PER_PACK = PACK_STATES // SLAB
RUN = 4
ROW_TILE = D_MODEL // LANES
N_DMA_PRIORITIES = 2
PUSH_UNROLL = 4
VMEM_LIMIT = 56 * 1024 * 1024

MIXER_TILE = 512
ATTN_TILE = 1024
ROUTE_TILE = 256

BF16 = jnp.bfloat16
F32 = jnp.float32


def _rms(x, g):
    return x * lax.rsqrt(jnp.mean(x * x, axis=-1, keepdims=True) + EPS) * g


def _dot(a, b):
    return jnp.dot(a, b, preferred_element_type=F32)


def _chain_slab(s, tre_ref, tim_ref, hre_scr, him_scr, re_scr, im_scr, tt):
    t_re = tre_ref[s]
    t_im = tim_ref[s]
    upper = lax.broadcasted_iota(jnp.int32, (SUBLANES, SLAB), 0) < RUN
    ta_re, ta_im = jnp.where(upper, t_re, 0.0), jnp.where(upper, t_im, 0.0)
    tb_re, tb_im = jnp.where(upper, 0.0, t_re), jnp.where(upper, 0.0, t_im)
    c_re = hre_scr[s]
    c_im = him_scr[s]
    for v in range(tt // SUBLANES):
        rows = pl.ds(v * SUBLANES, SUBLANES)
        re = re_scr[rows, :]
        im = im_scr[rows, :]
        re, im = re + (ta_re * c_re - ta_im * c_im), im + (ta_re * c_im + ta_im * c_re)
        m_re = re[RUN - 1:RUN, :]
        m_im = im[RUN - 1:RUN, :]
        re, im = re + (tb_re * m_re - tb_im * m_im), im + (tb_re * m_im + tb_im * m_re)
        re_scr[rows, :] = re
        im_scr[rows, :] = im
        c_re = re[SUBLANES - 1:SUBLANES, :]
        c_im = im[SUBLANES - 1:SUBLANES, :]
    hre_scr[s] = c_re
    him_scr[s] = c_im


def _mixer_kernel(x_ref, hist_ref, h0re_ref, h0im_ref, gmix_ref, win_ref, wpool_ref, pscale_ref,
                  bmat_ref, cre_ref, cim_ref, tre_ref, tim_ref, dskip_ref, wglu_ref, bglu_ref, wout_ref,
                  h_ref, poolnew_ref, ssmre_ref, ssmim_ref,
                  hist_scr, hre_scr, him_scr, *slab_scr, tt, pos0):
    bure_scr, buim_scr = slab_scr[:N_SLABS], slab_scr[N_SLABS:]
    j = pl.program_id(1)

    @pl.when(j == 0)
    def _():
        hist_scr[...] = hist_ref[0]
        hre_scr[...] = h0re_ref[0]
        him_scr[...] = h0im_ref[0]

    x = x_ref[0]
    xn = _rms(x, gmix_ref[...])
    z = _dot(xn.astype(BF16), win_ref[...])
    zp = z[:, :D_POOL]
    u = z[:, D_POOL:]

    ext = jnp.concatenate([hist_scr[...], zp], axis=0)
    pos = pos0 + j * tt + lax.broadcasted_iota(jnp.int32, (tt, 1), 0)
    acc = ext
    outs = []
    for gi, w in enumerate(POOL_WINDOWS):
        lo = gi * POOL_GROUP
        acc = acc[:, POOL_GROUP * (1 if gi else 0):]
        acc = acc + pltpu.roll(acc, w // 2, 0)
        wsum = acc[HIST_ROWS:, :POOL_GROUP]
        cnt = jnp.minimum(pos + 1, w).astype(F32)
        d = wsum / cnt - zp[:, lo:lo + POOL_GROUP]
        outs.append(_dot(d.astype(BF16), wpool_ref[gi]))
    y_pool = jnp.concatenate(outs, axis=1) * pscale_ref[...]
    hist_scr[...] = ext[tt:tt + HIST_ROWS]
    poolnew_ref[0] = ext[tt:tt + HIST_ROWS]

    run_row = lax.broadcasted_iota(jnp.int32, (tt, 1), 0) % RUN
    shifted = [u.astype(BF16)]
    for dd in range(1, RUN):
        shifted.append(jnp.where(run_row >= dd, pltpu.roll(u, dd, 0), 0.0).astype(BF16))
    def project(c):
        lhs = jnp.concatenate([sh[:, c * LANES:(c + 1) * LANES] for sh in shifted], axis=1)
        bu = _dot(lhs, bmat_ref[c])
        for i in range(SLABS_PER_PACK):
            s = c * SLABS_PER_PACK + i
            bure_scr[s][...] = bu[:, i * SLAB:(i + 1) * SLAB]
            buim_scr[s][...] = bu[:, PACK_STATES + i * SLAB:PACK_STATES + (i + 1) * SLAB]

    ys = []
    project(0)
    for c in range(N_PACKS):
        slabs = range(c * SLABS_PER_PACK, (c + 1) * SLABS_PER_PACK)
        if c + 1 < N_PACKS:
            project(c + 1)
        for s in slabs:
            _chain_slab(s, tre_ref, tim_ref, hre_scr, him_scr, bure_scr[s], buim_scr[s], tt)
        hs_re = jnp.concatenate([bure_scr[s][...] for s in slabs], axis=1)
        hs_im = jnp.concatenate([buim_scr[s][...] for s in slabs], axis=1)
        ys.append(_dot(hs_re.astype(BF16), cre_ref[c]) + _dot(hs_im.astype(BF16), cim_ref[c]))
    ssmre_ref[0] = hre_scr[...]
    ssmim_ref[0] = him_scr[...]
    y = jnp.concatenate(ys, axis=1) + dskip_ref[...] * u
    g = 0.5 * y * (1.0 + jnp.tanh(math.sqrt(2.0 / math.pi) * (y + 0.044715 * (y * y * y))))
    y_ssm = g * jax.nn.sigmoid(_dot(g.astype(BF16), wglu_ref[...]) + bglu_ref[...])

    mix = jnp.concatenate([y_pool, y_ssm], axis=1)
    h_ref[0] = x + _dot(mix.astype(BF16), wout_ref[...])


def _const_spec(shape):
    return pl.BlockSpec(shape, lambda *_: (0,) * len(shape))


def _mixer(x, hist, h0re, h0im, prm, *, tt, pos0):
    b, t, d = x.shape
    kern = functools.partial(_mixer_kernel, tt=tt, pos0=pos0)
    per_b3 = lambda shp: pl.BlockSpec((1,) + shp, lambda i, j: (i, 0, 0))
    per_b4 = lambda shp: pl.BlockSpec((1,) + shp, lambda i, j: (i, 0, 0, 0))
    return pl.pallas_call(
        kern,
        grid=(b, t // tt),
        in_specs=[
            pl.BlockSpec((1, tt, d), lambda i, j: (i, j, 0)),
            per_b3((HIST_ROWS, D_POOL)),
            per_b4((N_SLABS, 1, SLAB)),
            per_b4((N_SLABS, 1, SLAB)),
            _const_spec((1, d)),
            _const_spec((d, d)),
            _const_spec((len(POOL_WINDOWS), POOL_GROUP, POOL_GROUP)),
            _const_spec((1, D_POOL)),
            _const_spec((N_PACKS, RUN * LANES, 2 * PACK_STATES)),
            _const_spec((N_PACKS, PACK_STATES, LANES)),
            _const_spec((N_PACKS, PACK_STATES, LANES)),
            _const_spec((N_SLABS, SUBLANES, SLAB)),
            _const_spec((N_SLABS, SUBLANES, SLAB)),
            _const_spec((1, D_SSM)),
            _const_spec((D_SSM, D_SSM)),
            _const_spec((1, D_SSM)),
            _const_spec((d, d)),
        ],
        out_specs=[
            pl.BlockSpec((1, tt, d), lambda i, j: (i, j, 0)),
            per_b3((HIST_ROWS, D_POOL)),
            per_b4((N_SLABS, 1, SLAB)),
            per_b4((N_SLABS, 1, SLAB)),
        ],
        out_shape=[
            jax.ShapeDtypeStruct((b, t, d), F32),
            jax.ShapeDtypeStruct((b, HIST_ROWS, D_POOL), F32),
            jax.ShapeDtypeStruct((b, N_SLABS, 1, SLAB), F32),
            jax.ShapeDtypeStruct((b, N_SLABS, 1, SLAB), F32),
        ],
        scratch_shapes=[
            pltpu.VMEM((HIST_ROWS, D_POOL), F32),
            pltpu.VMEM((N_SLABS, 1, SLAB), F32),
            pltpu.VMEM((N_SLABS, 1, SLAB), F32),
        ] + [pltpu.VMEM((tt, SLAB), F32)] * (2 * N_SLABS),
        compiler_params=pltpu.CompilerParams(
            dimension_semantics=("arbitrary", "arbitrary"), vmem_limit_bytes=VMEM_LIMIT),
    )(x, hist, h0re, h0im, prm['g_mix'], prm['w_in'], prm['w_pool'], prm['pool_scale'],
      prm['bmat'], prm['c_re'], prm['c_im'], prm['t_re'], prm['t_im'], prm['d_skip'],
      prm['w_glu'], prm['b_glu'], prm['w_out'])


def _memkv_kernel(m_ref, g_ref, wk_ref, wv_ref, k_ref, v_ref):
    m = _rms(m_ref[0], g_ref[...]).astype(BF16)
    k_ref[0] = _dot(m, wk_ref[...])
    v_ref[0] = _dot(m, wv_ref[...])


def _memkv(mem, g, wk, wv):
    b, n, d = mem.shape
    blk = pl.BlockSpec((1, n, d), lambda i: (i, 0, 0))
    return pl.pallas_call(
        _memkv_kernel,
        grid=(b,),
        in_specs=[blk, _const_spec((1, d)), _const_spec((d, d)), _const_spec((d, d))],
        out_specs=[blk, blk],
        out_shape=[jax.ShapeDtypeStruct((b, n, d), F32)] * 2,
        compiler_params=pltpu.CompilerParams(vmem_limit_bytes=VMEM_LIMIT),
    )(mem, g, wk, wv)


def _attn_kernel(h_ref, k_ref, v_ref, gx_ref, wq_ref, wo_ref, gf_ref, wr_ref, br_ref,
                 h2_ref, xn_ref, e_ref, gate_ref, cnt_ref, cnt_scr, kt_scr, v_scr, *, tt):
    @pl.when(pl.program_id(1) == 0)
    def _():
        kt_scr[...] = k_ref[0].T.astype(BF16)
        v_scr[...] = v_ref[0].astype(BF16)

    h = h_ref[0]
    hn = _rms(h, gx_ref[...])
    q = _dot(hn.astype(BF16), wq_ref[...])
    outs = []
    for hd in range(N_XHEADS):
        cols = slice(hd * XHEAD_DIM, (hd + 1) * XHEAD_DIM)
        s = _dot(q[:, cols].astype(BF16), kt_scr[cols, :]) * (XHEAD_DIM ** -0.5)
        p = jnp.exp(s - jnp.max(s, axis=-1, keepdims=True))
        p = p / jnp.sum(p, axis=-1, keepdims=True)
        outs.append(_dot(p.astype(BF16), v_scr[:, cols]))
    o = jnp.concatenate(outs, axis=1)
    h2 = h + _dot(o.astype(BF16), wo_ref[...])
    h2_ref[0] = h2

    xn = _rms(h2, gf_ref[...])
    for s in range(ROW_TILE):
        xn_ref[pl.ds(s, tt, stride=ROW_TILE), :] = xn[:, s * LANES:(s + 1) * LANES]
    logits = _dot(xn.astype(BF16), wr_ref[...]) + br_ref[...]
    lane = lax.broadcasted_iota(jnp.int32, logits.shape, 1)
    lane_f = lane.astype(F32)
    e_out = jnp.zeros(logits.shape, jnp.int32)
    hits = jnp.zeros(logits.shape, F32)
    top = []
    for k in range(TOP_K):
        m = jnp.max(logits, axis=-1, keepdims=True)
        idx = jnp.min(jnp.where(logits == m, lane_f, float(LANES)), axis=-1, keepdims=True)
        e_out = jnp.where(lane == k, idx.astype(jnp.int32), e_out)
        top.append(m)
        chosen = lane_f == idx
        hits = hits + chosen.astype(F32)
        logits = jnp.where(chosen, -jnp.inf, logits)

    @pl.when(jnp.logical_and(pl.program_id(0) == 0, pl.program_id(1) == 0))
    def _():
        cnt_scr[...] = jnp.zeros_like(cnt_scr)
    cnt_scr[...] = cnt_scr[...] + jnp.sum(hits, axis=0, keepdims=True)
    cnt_ref[...] = cnt_scr[...]
    ex = [jnp.exp(m - top[0]) for m in top]
    tot = ex[0] + ex[1] + ex[2] + ex[3]
    g_out = jnp.zeros(logits.shape, F32)
    for k in range(TOP_K):
        g_out = jnp.where(lane == k, ex[k] / tot, g_out)
    e_ref[...] = e_out
    gate_ref[...] = g_out


def _attn(h, k, v, prm, *, tt):
    b, t, d = h.shape
    n = b * t
    kern = functools.partial(_attn_kernel, tt=tt)
    nt = t // tt
    kv = pl.BlockSpec((1, N_MEM, d), lambda i, j: (i, 0, 0))
    tok = lambda width: pl.BlockSpec((tt, width), lambda i, j: (i * nt + j, 0))
    return pl.pallas_call(
        kern,
        grid=(b, nt),
        in_specs=[
            pl.BlockSpec((1, tt, d), lambda i, j: (i, j, 0)), kv, kv,
            _const_spec((1, d)), _const_spec((d, d)), _const_spec((d, d)),
            _const_spec((1, d)), _const_spec((d, LANES)), _const_spec((1, LANES)),
        ],
        out_specs=[
            pl.BlockSpec((1, tt, d), lambda i, j: (i, j, 0)),
            pl.BlockSpec((tt * ROW_TILE, LANES), lambda i, j: (i * nt + j, 0)),
            tok(LANES), tok(LANES), _const_spec((1, LANES)),
        ],
        out_shape=[
            jax.ShapeDtypeStruct((b, t, d), F32),
            jax.ShapeDtypeStruct((n * ROW_TILE, LANES), F32),
            jax.ShapeDtypeStruct((n, LANES), jnp.int32),
            jax.ShapeDtypeStruct((n, LANES), F32),
            jax.ShapeDtypeStruct((1, LANES), F32),
        ],
        scratch_shapes=[pltpu.VMEM((1, LANES), F32), pltpu.VMEM((d, N_MEM), BF16), pltpu.VMEM((N_MEM, d), BF16)],
        compiler_params=pltpu.CompilerParams(
            dimension_semantics=("arbitrary", "arbitrary"), vmem_limit_bytes=VMEM_LIMIT),
    )(h, k, v, prm['g_xattn'], prm['w_q'], prm['w_o'], prm['g_ffn'], prm['w_router'], prm['b_router'])


RANK_TILE = 512


def _rank_kernel(ep_ref, es_ref, start_ref, dest_ref, carry_scr, before_scr, *, steps_p):
    i = pl.program_id(0)

    @pl.when(i == 0)
    def _():
        carry_scr[...] = start_ref[...]
        r = lax.broadcasted_iota(jnp.int32, (RANK_TILE, RANK_TILE), 0)
        c = lax.broadcasted_iota(jnp.int32, (RANK_TILE, RANK_TILE), 1)
        before_scr[...] = (c < r).astype(BF16)

    e = jnp.where(i < steps_p, ep_ref[...], es_ref[...])
    lane = lax.broadcasted_iota(jnp.int32, e.shape, 1)
    onehot = [lane == e[:, k:k + 1] for k in range(TOP_K)]
    hits = jnp.zeros(e.shape, F32)
    for oh in onehot:
        hits = hits + oh.astype(F32)
    base = _dot(before_scr[...], hits.astype(BF16)) + carry_scr[...]
    rank = jnp.zeros(e.shape, jnp.int32)
    for k, oh in enumerate(onehot):
        rk = jnp.sum(jnp.where(oh, base, 0.0), axis=-1, keepdims=True).astype(jnp.int32)
        rank = jnp.where(lane == k, rk, rank)
    dest_ref[...] = rank
    carry_scr[...] = carry_scr[...] + jnp.sum(hits, axis=0, keepdims=True)


def _rank(e_p, e_s, start):
    steps_p = e_p.shape[0] // RANK_TILE
    blk = (RANK_TILE, LANES)
    return pl.pallas_call(
        functools.partial(_rank_kernel, steps_p=steps_p),
        grid=(steps_p + 1,),
        in_specs=[pl.BlockSpec(blk, lambda i: (jnp.minimum(i, steps_p - 1), 0)), _const_spec(blk),
                  _const_spec((1, LANES))],
        out_specs=pl.BlockSpec(blk, lambda i: (i, 0)),
        out_shape=jax.ShapeDtypeStruct(((steps_p + 1) * RANK_TILE, LANES), jnp.int32),
        scratch_shapes=[pltpu.VMEM((1, LANES), F32), pltpu.VMEM((RANK_TILE, RANK_TILE), BF16)],
        compiler_params=pltpu.CompilerParams(dimension_semantics=("arbitrary",)),
    )(e_p, e_s, start)


BLOCK_ROWS = MOE_BLOCK * ROW_TILE


def _row_copy_wait(src_ref, dst_ref, sem, rows):
    pltpu.make_async_copy(src_ref.at[pl.ds(0, rows * ROW_TILE)], dst_ref.at[pl.ds(0, rows * ROW_TILE)], sem).wait()


def _dispatch_kernel(last_blk_ref, nvalid_ref, dest_p_ref, dest_s_ref, xp_ref, xs_ref, xpad_ref,
                     zero_buf, sem, zero_sem, *, tile_p, tile_s, n_blocks):
    i = pl.program_id(0)
    last = pl.num_programs(0) - 1

    def zero_copy(blk):
        dst = xpad_ref.at[pl.ds(pl.multiple_of(blk * BLOCK_ROWS, BLOCK_ROWS), BLOCK_ROWS)]
        return pltpu.make_async_copy(zero_buf, dst, zero_sem)

    @pl.when(i == 0)
    def _():
        zero_buf[...] = jnp.zeros_like(zero_buf)

        def on_pad_blocks(fn):
            for e in range(N_EXPERTS):
                if e == 0:
                    fn(last_blk_ref[0])
                else:
                    pl.when(last_blk_ref[e] != last_blk_ref[e - 1])(functools.partial(fn, last_blk_ref[e]))
            lax.fori_loop(nvalid_ref[0], n_blocks, lambda blk, c: (fn(blk), c)[1], 0)

        on_pad_blocks(lambda blk: zero_copy(blk).start())
        on_pad_blocks(lambda blk: zero_copy(blk).wait())

    def push(dest_ref, x_ref, tile):
        def body(it, carry):
            for j in range(PUSH_UNROLL):
                t = it * PUSH_UNROLL + j
                src = x_ref.at[pl.ds(pl.multiple_of(t * ROW_TILE, ROW_TILE), ROW_TILE)]
                for k in range(TOP_K):
                    row = dest_ref[0, 0, t * TOP_K + k]
                    dst = xpad_ref.at[pl.ds(pl.multiple_of(row * ROW_TILE, ROW_TILE), ROW_TILE)]
                    pltpu.make_async_copy(src, dst, sem).start(priority=k % N_DMA_PRIORITIES)
            return carry

        lax.fori_loop(0, tile // PUSH_UNROLL, body, 0)
        for _ in range(TOP_K):
            _row_copy_wait(x_ref, xpad_ref, sem, tile)

    pl.when(i < last)(lambda: push(dest_p_ref, xp_ref, tile_p))
    pl.when(i == last)(lambda: push(dest_s_ref, xs_ref, tile_s))


def _dispatch(last_blk, nvalid, dest_p, dest_s, xn_p, xn_s, n_blocks, *, tile_p):
    n_p = xn_p.shape[0] // ROW_TILE
    tile_s = xn_s.shape[0] // ROW_TILE
    steps_p = n_p // tile_p
    kern = functools.partial(_dispatch_kernel, tile_p=tile_p, tile_s=tile_s, n_blocks=n_blocks)
    return pl.pallas_call(
        kern,
        grid_spec=pltpu.PrefetchScalarGridSpec(
            num_scalar_prefetch=2,
            grid=(steps_p + 1,),
            in_specs=[
                pl.BlockSpec((1, 1, tile_p * TOP_K), lambda i, lb, nv: (jnp.minimum(i, steps_p - 1), 0, 0),
                             memory_space=pltpu.SMEM),
                pl.BlockSpec((1, 1, tile_s * TOP_K), lambda i, lb, nv: (0, 0, 0), memory_space=pltpu.SMEM),
                pl.BlockSpec((tile_p * ROW_TILE, LANES), lambda i, lb, nv: (jnp.minimum(i, steps_p - 1), 0)),
                pl.BlockSpec((tile_s * ROW_TILE, LANES), lambda i, lb, nv: (0, 0)),
            ],
            out_specs=pl.BlockSpec(memory_space=pl.ANY),
            scratch_shapes=[pltpu.VMEM((BLOCK_ROWS, LANES), F32), pltpu.SemaphoreType.DMA, pltpu.SemaphoreType.DMA],
        ),
        out_shape=jax.ShapeDtypeStruct((n_blocks * BLOCK_ROWS, LANES), F32),
        compiler_params=pltpu.CompilerParams(dimension_semantics=("arbitrary",)),
    )(last_blk, nvalid, dest_p.reshape(steps_p, 1, tile_p * TOP_K), dest_s.reshape(1, 1, tile_s * TOP_K), xn_p, xn_s)


def _expert_kernel(blk_e_ref, nvalid_ref, next_e_ref, x_ref, bg_ref, bu_ref, bd_ref, wg_hbm, wu_hbm, wd_hbm, y_ref,
                   stage, w_bf, sems):
    b = pl.program_id(0)
    valid = b < nvalid_ref[0]
    e = blk_e_ref[b]
    weights = (wg_hbm, wu_hbm, wd_hbm)

    def fetch(expert):
        return [pltpu.make_async_copy(w.at[expert], stage.at[i], sems.at[i]) for i, w in enumerate(weights)]

    @pl.when(b == 0)
    def _():
        for cp in fetch(e):
            cp.start()

    @pl.when(jnp.logical_and(valid, jnp.logical_or(b == 0, blk_e_ref[jnp.maximum(b - 1, 0)] != e)))
    def _():
        for i, cp in enumerate(fetch(e)):
            cp.wait()
            w_bf[i] = stage[i].astype(BF16)

        @pl.when(next_e_ref[b] >= 0)
        def _():
            for cp in fetch(next_e_ref[b]):
                cp.start()

    @pl.when(valid)
    def _():
        x = jnp.concatenate([x_ref[pl.ds(s, MOE_BLOCK, stride=ROW_TILE), :] for s in range(ROW_TILE)], axis=1)
        x = x.astype(BF16)
        g = _dot(x, w_bf[0]) + bg_ref[0]
        u = _dot(x, w_bf[1]) + bu_ref[0]
        g = jnp.minimum(g, SWIGLU_LIMIT)
        u = jnp.clip(u, -SWIGLU_LIMIT, SWIGLU_LIMIT)
        hdn = g * (0.5 * (1.0 + jnp.tanh((0.5 * SWIGLU_ALPHA) * g))) * (u + 1.0)
        y = _dot(hdn.astype(BF16), w_bf[2]) + bd_ref[0]
        for s in range(ROW_TILE):
            y_ref[pl.ds(s, MOE_BLOCK, stride=ROW_TILE), :] = y[:, s * LANES:(s + 1) * LANES]

    @pl.when(b >= nvalid_ref[0])
    def _():
        y_ref[...] = jnp.zeros_like(y_ref)


def _experts(blk_e, nvalid, next_e, xpad, w_gate, b_gate, w_up, b_up, w_down, b_down, n_blocks):
    d = D_MODEL
    n_w = 3
    bspec = pl.BlockSpec((1, 1, d), lambda b, be, nv, ne: (be[b], 0, 0))
    xspec = pl.BlockSpec((BLOCK_ROWS, LANES), lambda b, be, nv, ne: (jnp.minimum(b, nv[0] - 1), 0))
    hbm = pl.BlockSpec(memory_space=pl.ANY)
    return pl.pallas_call(
        _expert_kernel,
        grid_spec=pltpu.PrefetchScalarGridSpec(
            num_scalar_prefetch=3,
            grid=(n_blocks,),
            in_specs=[xspec, bspec, bspec, bspec, hbm, hbm, hbm],
            out_specs=pl.BlockSpec((BLOCK_ROWS, LANES), lambda b, be, nv, ne: (b, 0)),
            scratch_shapes=[pltpu.VMEM((n_w, d, d), F32), pltpu.VMEM((n_w, d, d), BF16),
                            pltpu.SemaphoreType.DMA((n_w,))],
        ),
        out_shape=jax.ShapeDtypeStruct((n_blocks * BLOCK_ROWS, LANES), F32),
        compiler_params=pltpu.CompilerParams(dimension_semantics=("arbitrary",), vmem_limit_bytes=VMEM_LIMIT),
    )(blk_e, nvalid, next_e, xpad, b_gate.reshape(N_EXPERTS, 1, d), b_up.reshape(N_EXPERTS, 1, d),
      b_down.reshape(N_EXPERTS, 1, d), w_gate, w_up, w_down)


COMBINE_SLOTS = 3


def _combine_kernel(dest_ref, dest_1_ref, dest_2_ref, h_ref, gate_ref, gfin_ref, ypad_ref, y_ref, *bufs_sems,
                    tile, n_steps):
    bufs, sems = bufs_sems[:COMBINE_SLOTS], bufs_sems[COMBINE_SLOTS]
    i = pl.program_id(0)

    def gather(dref, buf, sem, t, k):
        row = dref[0, 0, t * TOP_K + k]
        slot = (k * tile + t) * ROW_TILE
        if not isinstance(slot, int):
            slot = pl.multiple_of(slot, ROW_TILE)
        src = ypad_ref.at[pl.ds(pl.multiple_of(row * ROW_TILE, ROW_TILE), ROW_TILE)]
        pltpu.make_async_copy(src, buf.at[pl.ds(slot, ROW_TILE)], sem).start(priority=k % N_DMA_PRIORITIES)

    def drain(buf, sem):
        for _ in range(TOP_K):
            _row_copy_wait(ypad_ref, buf, sem, tile)

    @pl.when(i == 0)
    def _():
        def body(t, carry):
            for k in range(TOP_K):
                gather(dest_ref, bufs[0], sems.at[0], t, k)
                gather(dest_1_ref, bufs[1], sems.at[1], t, k)
            return carry
        lax.fori_loop(0, tile, body, 0)

    def step(cur, cur_sem, nxt, nxt_sem):
        drain(cur, cur_sem)
        gates = gate_ref[...]
        h = h_ref[...]
        per_slab = tile // ROW_TILE
        cols = []
        for s in range(ROW_TILE):
            for t in range(s * per_slab, (s + 1) * per_slab):
                for k in range(TOP_K):
                    gather(dest_2_ref, nxt, nxt_sem, t, k)
            acc = h[:, s * LANES:(s + 1) * LANES]
            for k in range(TOP_K):
                acc = acc + gates[:, k:k + 1] * cur[pl.ds(k * tile * ROW_TILE + s, tile, stride=ROW_TILE), :]
            cols.append(acc)
        y_ref[...] = _rms(jnp.concatenate(cols, axis=1), gfin_ref[...])

    for r in range(COMBINE_SLOTS):
        ahead = (r + COMBINE_SLOTS - 1) % COMBINE_SLOTS
        pl.when(i % COMBINE_SLOTS == r)(functools.partial(step, bufs[r], sems.at[r], bufs[ahead], sems.at[ahead]))

    @pl.when(i == n_steps - 1)
    def _():
        for extra in (n_steps, n_steps + 1):
            drain(bufs[extra % COMBINE_SLOTS], sems.at[extra % COMBINE_SLOTS])


def _combine(dest, h2, gates, g_final, ypad, *, tile):
    n, d = h2.shape
    steps = n // tile
    kern = functools.partial(_combine_kernel, tile=tile, n_steps=steps)
    dest3 = dest.reshape(steps, 1, tile * TOP_K)
    buf = pltpu.VMEM((TOP_K * tile * ROW_TILE, LANES), F32)
    tile_ahead = lambda k: pl.BlockSpec((1, 1, tile * TOP_K), lambda i: (jnp.minimum(i + k, steps - 1), 0, 0),
                                        memory_space=pltpu.SMEM)
    return pl.pallas_call(
        kern,
        grid=(steps,),
        in_specs=[
            tile_ahead(0), tile_ahead(1), tile_ahead(2),
            pl.BlockSpec((tile, d), lambda i: (i, 0)),
            pl.BlockSpec((tile, LANES), lambda i: (i, 0)),
            _const_spec((1, d)),
            pl.BlockSpec(memory_space=pl.ANY),
        ],
        out_specs=pl.BlockSpec((tile, d), lambda i: (i, 0)),
        out_shape=jax.ShapeDtypeStruct((n, d), F32),
        scratch_shapes=[buf] * COMBINE_SLOTS + [pltpu.SemaphoreType.DMA((COMBINE_SLOTS,))],
        compiler_params=pltpu.CompilerParams(dimension_semantics=("arbitrary",), vmem_limit_bytes=VMEM_LIMIT),
    )(dest3, dest3, dest3, h2, gates, g_final, ypad)


def _ssm_params(a_re, a_im, log_dt, b_re, b_im, c_re, c_im):
    dt = jnp.exp(log_dt)[:, None]
    mag = jnp.exp(a_re * dt)
    lb_re = mag * jnp.cos(a_im * dt)
    lb_im = mag * jnp.sin(a_im * dt)
    den = a_re * a_re + a_im * a_im
    q_re = ((lb_re - 1.0) * a_re + lb_im * a_im) / den
    q_im = (lb_im * a_re - (lb_re - 1.0) * a_im) / den
    bb_re = q_re[:, :, None] * b_re - q_im[:, :, None] * b_im
    bb_im = q_re[:, :, None] * b_im + q_im[:, :, None] * b_re
    eye = jnp.eye(SSM_PACK, dtype=F32)

    def pack_blockdiag(m):
        m = m.reshape(N_PACKS, SSM_PACK, m.shape[1], m.shape[2])
        return jnp.einsum('ngab,gh->ngahb', m, eye).reshape(N_PACKS, SSM_PACK * m.shape[2], SSM_PACK * m.shape[3])

    pw_re, pw_im = [jnp.ones_like(lb_re), lb_re], [jnp.zeros_like(lb_im), lb_im]
    for _ in range(RUN - 1):
        r, i = pw_re[-1], pw_im[-1]
        pw_re.append(r * lb_re - i * lb_im)
        pw_im.append(r * lb_im + i * lb_re)
    b_rows = []
    for dd in range(RUN):
        s_re = bb_re * pw_re[dd][:, :, None] - bb_im * pw_im[dd][:, :, None]
        s_im = bb_re * pw_im[dd][:, :, None] + bb_im * pw_re[dd][:, :, None]
        b_rows.append(jnp.concatenate([pack_blockdiag(s_re.transpose(0, 2, 1)),
                                       pack_blockdiag(s_im.transpose(0, 2, 1))], axis=2))
    bmat = jnp.concatenate(b_rows, axis=1)
    cmat_re = pack_blockdiag(c_re.transpose(0, 2, 1))
    cmat_im = pack_blockdiag(-c_im.transpose(0, 2, 1))
    rows_of = lambda pw: [pw[(r % RUN) + 1].reshape(-1) for r in range(SUBLANES)]
    slabbed = lambda rows: jnp.stack(rows).reshape(SUBLANES, N_SLABS, SLAB).transpose(1, 0, 2)
    return (bmat.astype(BF16), cmat_re.astype(BF16), cmat_im.astype(BF16),
            slabbed(rows_of(pw_re)), slabbed(rows_of(pw_im)))


def kernel(x_prompt, x_sample, mem_prompt, cache_mem_k, cache_mem_v, cache_pool, state_ssm_re, state_ssm_im, norm_mix, w_in, w_pool, pool_scale, ssm_a_re, ssm_a_im, ssm_log_dt, ssm_b_re, ssm_b_im, ssm_c_re, ssm_c_im, ssm_d, w_glu, b_glu, w_out, norm_xattn, norm_mem, w_q, w_k, w_v, w_o, norm_ffn, w_router, b_router, w_gate, b_gate, w_up, b_up, w_down, b_down, norm_final):
    assert x_prompt.shape[2] == D_MODEL and norm_mix.shape[0] == 1
    bp, tp, d = x_prompt.shape
    bs, ts, _ = x_sample.shape
    n_p, n_s = bp * tp, bs * ts
    row = lambda v: v.reshape(1, -1)

    bmat, c_re, c_im, t_re, t_im = _ssm_params(ssm_a_re[0], ssm_a_im[0], ssm_log_dt[0], ssm_b_re[0], ssm_b_im[0],
                                               ssm_c_re[0], ssm_c_im[0])
    mix_prm = dict(g_mix=row(norm_mix[0]), w_in=w_in[0].astype(BF16), w_pool=w_pool[0].astype(BF16),
                   pool_scale=row(pool_scale[0]), bmat=bmat, c_re=c_re, c_im=c_im, t_re=t_re, t_im=t_im,
                   d_skip=row(ssm_d[0]), w_glu=w_glu[0].astype(BF16), b_glu=row(b_glu[0]),
                   w_out=w_out[0].astype(BF16))

    attn_prm = dict(g_xattn=row(norm_xattn[0]), w_q=w_q[0].astype(BF16), w_o=w_o[0].astype(BF16),
                    g_ffn=row(norm_ffn[0]),
                    w_router=jnp.pad(w_router[0], ((0, 0), (0, LANES - N_EXPERTS))).astype(BF16),
                    b_router=jnp.pad(row(b_router[0]), ((0, 0), (0, LANES - N_EXPERTS)), constant_values=-jnp.inf))
    slab_state = lambda s: s.reshape(s.shape[0], N_SLABS, 1, SLAB)

    zeros_state = jnp.zeros((bp, N_SLABS, 1, SLAB), F32)
    h1_p, pool_p, sre_p, sim_p = _mixer(x_prompt, jnp.zeros((bp, HIST_ROWS, D_POOL), F32), zeros_state, zeros_state,
                                        mix_prm, tt=MIXER_TILE, pos0=0)
    mk, mv = _memkv(mem_prompt, row(norm_mem[0]), w_k[0].astype(BF16), w_v[0].astype(BF16))
    h2_p, xn_p, e_p, g_p, cnt_p = _attn(h1_p, mk, mv, attn_prm, tt=ATTN_TILE)

    hist_s = jnp.pad(cache_pool[0], ((0, 0), (HIST_ROWS - POOL_BUF, 0), (0, 0)))
    h1_s, pool_s, sre_s, sim_s = _mixer(x_sample, hist_s, slab_state(state_ssm_re[0]), slab_state(state_ssm_im[0]),
                                        mix_prm, tt=ts, pos0=PAST_LEN)
    ck = cache_mem_k[0].reshape(bs, N_MEM, d)
    cv = cache_mem_v[0].reshape(bs, N_MEM, d)
    h2_s, xn_s, e_s, g_s, cnt_s = _attn(h1_s, ck, cv, attn_prm, tt=ts)

    n_all = n_p + n_s
    counts = (cnt_p + cnt_s)[0, :N_EXPERTS].astype(jnp.int32)
    padded = (counts + MOE_BLOCK - 1) // MOE_BLOCK * MOE_BLOCK
    pend = jnp.cumsum(padded)
    pstart = pend - padded
    n_blocks = -(-(n_all * TOP_K + N_EXPERTS * (MOE_BLOCK - 1)) // MOE_BLOCK)
    blk_e = jnp.minimum(jnp.sum(pend[None, :] <= (jnp.arange(n_blocks) * MOE_BLOCK)[:, None], axis=1),
                        N_EXPERTS - 1).astype(jnp.int32)
    nvalid = (pend[-1:] // MOE_BLOCK).astype(jnp.int32)
    last_blk = jnp.maximum(pend // MOE_BLOCK - 1, 0).astype(jnp.int32)
    after = (pend // MOE_BLOCK)[blk_e]
    next_e = jnp.where(after < nvalid[0], blk_e[jnp.minimum(after, n_blocks - 1)], -1).astype(jnp.int32)
    start = jnp.pad(pstart.astype(F32), (0, LANES - N_EXPERTS)).reshape(1, LANES)
    e_s_tile = jnp.pad(e_s, ((0, RANK_TILE - n_s), (0, 0)), constant_values=-1)
    dest = _rank(e_p, e_s_tile, start)[:n_all, :TOP_K].reshape(-1)
    dest_p, dest_s = dest[:n_p * TOP_K], dest[n_p * TOP_K:]

    xpad = _dispatch(last_blk, nvalid, dest_p, dest_s, xn_p, xn_s, n_blocks, tile_p=ROUTE_TILE)
    ypad = _experts(blk_e, nvalid, next_e, xpad, w_gate[0], b_gate[0], w_up[0], b_up[0], w_down[0], b_down[0], n_blocks)
    g_fin = row(norm_final)
    y_p = _combine(dest_p, h2_p.reshape(n_p, d), g_p, g_fin, ypad, tile=ROUTE_TILE)
    y_s = _combine(dest_s, h2_s.reshape(n_s, d), g_s, g_fin, ypad, tile=n_s)

    unslab = lambda s: s.reshape(1, s.shape[0], N_SSM_GROUPS, SSM_STATE)
    kv5 = lambda a: a.reshape(1, bp, N_MEM, N_XHEADS, XHEAD_DIM)
    return (y_p.reshape(bp, tp, d), y_s.reshape(bs, ts, d), kv5(mk), kv5(mv),
            pool_p[None, :, HIST_ROWS - POOL_BUF:], pool_s[None, :, HIST_ROWS - POOL_BUF:],
            unslab(sre_p), unslab(sim_p), unslab(sre_s), unslab(sim_s))
```

```python
import functools
import math

import jax
import jax.numpy as jnp
from jax import lax
from jax.experimental import pallas as pl
from jax.experimental.pallas import tpu as pltpu

D_MODEL = 1024
D_POOL = 512
D_SSM = 512
POOL_WINDOWS = (2, 4, 8, 16)
POOL_GROUP = 128
HIST_ROWS = 16
POOL_BUF = 15
SSM_GROUP = 16
N_SSM_GROUPS = 32
SSM_STATE = 64
N_STATE = N_SSM_GROUPS * SSM_STATE
N_MEM = 256
N_XHEADS = 4
XHEAD_DIM = 256
N_EXPERTS = 32
TOP_K = 4
SWIGLU_LIMIT = 7.0
SWIGLU_ALPHA = 1.702
MOE_BLOCK = 512
EPS = 1e-6
PAST_LEN = 1024

LANES = 128
SUBLANES = 8
SLAB = 256
N_SLABS = N_STATE // SLAB
SSM_PACK = LANES // SSM_GROUP
N_PACKS = N_SSM_GROUPS // SSM_PACK
PACK_STATES = SSM_PACK * SSM_STATE
SLABS_PER_PACK = PACK_STATES // SLAB
RUN = 4
ROW_TILE = D_MODEL // LANES
N_DMA_PRIORITIES = 2
PUSH_UNROLL = 4
VMEM_LIMIT = 56 * 1024 * 1024

MIXER_TILE = 512
ATTN_TILE = 1024
ROUTE_TILE = 256

BF16 = jnp.bfloat16
F32 = jnp.float32


def _rms(x, g):
    return x * lax.rsqrt(jnp.mean(x * x, axis=-1, keepdims=True) + EPS) * g


def _dot(a, b):
    return jnp.dot(a, b, preferred_element_type=F32)


def _chain_slab(s, tre_ref, tim_ref, hre_scr, him_scr, re_scr, im_scr, tt):
    t_re = tre_ref[s]
    t_im = tim_ref[s]
    upper = lax.broadcasted_iota(jnp.int32, (SUBLANES, SLAB), 0) < RUN
    ta_re, ta_im = jnp.where(upper, t_re, 0.0), jnp.where(upper, t_im, 0.0)
    tb_re, tb_im = jnp.where(upper, 0.0, t_re), jnp.where(upper, 0.0, t_im)
    c_re = hre_scr[s]
    c_im = him_scr[s]
    for v in range(tt // SUBLANES):
        rows = pl.ds(v * SUBLANES, SUBLANES)
        re = re_scr[rows, :]
        im = im_scr[rows, :]
        re, im = re + (ta_re * c_re - ta_im * c_im), im + (ta_re * c_im + ta_im * c_re)
        m_re = re[RUN - 1:RUN, :]
        m_im = im[RUN - 1:RUN, :]
        re, im = re + (tb_re * m_re - tb_im * m_im), im + (tb_re * m_im + tb_im * m_re)
        re_scr[rows, :] = re
        im_scr[rows, :] = im
        c_re = re[SUBLANES - 1:SUBLANES, :]
        c_im = im[SUBLANES - 1:SUBLANES, :]
    hre_scr[s] = c_re
    him_scr[s] = c_im


def _mixer_kernel(x_ref, hist_ref, h0re_ref, h0im_ref, gmix_ref, win_ref, wpool_ref, pscale_ref,
                  bmat_ref, cre_ref, cim_ref, tre_ref, tim_ref, dskip_ref, wglu_ref, bglu_ref, wout_ref,
                  h_ref, poolnew_ref, ssmre_ref, ssmim_ref,
                  hist_scr, hre_scr, him_scr, *slab_scr, tt, pos0):
    bure_scr, buim_scr = slab_scr[:N_SLABS], slab_scr[N_SLABS:]
    j = pl.program_id(1)

    @pl.when(j == 0)
    def _():
        hist_scr[...] = hist_ref[0]
        hre_scr[...] = h0re_ref[0]
        him_scr[...] = h0im_ref[0]

    x = x_ref[0]
    xn = _rms(x, gmix_ref[...])
    z = _dot(xn.astype(BF16), win_ref[...])
    zp = z[:, :D_POOL]
    u = z[:, D_POOL:]

    ext = jnp.concatenate([hist_scr[...], zp], axis=0)
    pos = pos0 + j * tt + lax.broadcasted_iota(jnp.int32, (tt, 1), 0)
    acc = ext
    outs = []
    for gi, w in enumerate(POOL_WINDOWS):
        lo = gi * POOL_GROUP
        acc = acc[:, POOL_GROUP * (1 if gi else 0):]
        acc = acc + pltpu.roll(acc, w // 2, 0)
        wsum = acc[HIST_ROWS:, :POOL_GROUP]
        cnt = jnp.minimum(pos + 1, w).astype(F32)
        d = wsum / cnt - zp[:, lo:lo + POOL_GROUP]
        outs.append(_dot(d.astype(BF16), wpool_ref[gi]))
    y_pool = jnp.concatenate(outs, axis=1) * pscale_ref[...]
    hist_scr[...] = ext[tt:tt + HIST_ROWS]
    poolnew_ref[0] = ext[tt:tt + HIST_ROWS]

    run_row = lax.broadcasted_iota(jnp.int32, (tt, 1), 0) % RUN
    shifted = [u.astype(BF16)]
    for dd in range(1, RUN):
        shifted.append(jnp.where(run_row >= dd, pltpu.roll(u, dd, 0), 0.0).astype(BF16))
    def project(c):
        lhs = jnp.concatenate([sh[:, c * LANES:(c + 1) * LANES] for sh in shifted], axis=1)
        bu = _dot(lhs, bmat_ref[c])
        for i in range(SLABS_PER_PACK):
            s = c * SLABS_PER_PACK + i
            bure_scr[s][...] = bu[:, i * SLAB:(i + 1) * SLAB]
            buim_scr[s][...] = bu[:, PACK_STATES + i * SLAB:PACK_STATES + (i + 1) * SLAB]

    ys = []
    project(0)
    for c in range(N_PACKS):
        slabs = range(c * SLABS_PER_PACK, (c + 1) * SLABS_PER_PACK)
        if c + 1 < N_PACKS:
            project(c + 1)
        for s in slabs:
            _chain_slab(s, tre_ref, tim_ref, hre_scr, him_scr, bure_scr[s], buim_scr[s], tt)
        hs_re = jnp.concatenate([bure_scr[s][...] for s in slabs], axis=1)
        hs_im = jnp.concatenate([buim_scr[s][...] for s in slabs], axis=1)
        ys.append(_dot(hs_re.astype(BF16), cre_ref[c]) + _dot(hs_im.astype(BF16), cim_ref[c]))
    ssmre_ref[0] = hre_scr[...]
    ssmim_ref[0] = him_scr[...]
    y = jnp.concatenate(ys, axis=1) + dskip_ref[...] * u
    g = 0.5 * y * (1.0 + jnp.tanh(math.sqrt(2.0 / math.pi) * (y + 0.044715 * (y * y * y))))
    y_ssm = g * jax.nn.sigmoid(_dot(g.astype(BF16), wglu_ref[...]) + bglu_ref[...])

    mix = jnp.concatenate([y_pool, y_ssm], axis=1)
    h_ref[0] = x + _dot(mix.astype(BF16), wout_ref[...])


def _const_spec(shape):
    return pl.BlockSpec(shape, lambda *_: (0,) * len(shape))


def _mixer(x, hist, h0re, h0im, prm, *, tt, pos0):
    b, t, d = x.shape
    kern = functools.partial(_mixer_kernel, tt=tt, pos0=pos0)
    per_b3 = lambda shp: pl.BlockSpec((1,) + shp, lambda i, j: (i, 0, 0))
    per_b4 = lambda shp: pl.BlockSpec((1,) + shp, lambda i, j: (i, 0, 0, 0))
    return pl.pallas_call(
        kern,
        grid=(b, t // tt),
        in_specs=[
            pl.BlockSpec((1, tt, d), lambda i, j: (i, j, 0)),
            per_b3((HIST_ROWS, D_POOL)),
            per_b4((N_SLABS, 1, SLAB)),
            per_b4((N_SLABS, 1, SLAB)),
            _const_spec((1, d)),
            _const_spec((d, d)),
            _const_spec((len(POOL_WINDOWS), POOL_GROUP, POOL_GROUP)),
            _const_spec((1, D_POOL)),
            _const_spec((N_PACKS, RUN * LANES, 2 * PACK_STATES)),
            _const_spec((N_PACKS, PACK_STATES, LANES)),
            _const_spec((N_PACKS, PACK_STATES, LANES)),
            _const_spec((N_SLABS, SUBLANES, SLAB)),
            _const_spec((N_SLABS, SUBLANES, SLAB)),
            _const_spec((1, D_SSM)),
            _const_spec((D_SSM, D_SSM)),
            _const_spec((1, D_SSM)),
            _const_spec((d, d)),
        ],
        out_specs=[
            pl.BlockSpec((1, tt, d), lambda i, j: (i, j, 0)),
            per_b3((HIST_ROWS, D_POOL)),
            per_b4((N_SLABS, 1, SLAB)),
            per_b4((N_SLABS, 1, SLAB)),
        ],
        out_shape=[
            jax.ShapeDtypeStruct((b, t, d), F32),
            jax.ShapeDtypeStruct((b, HIST_ROWS, D_POOL), F32),
            jax.ShapeDtypeStruct((b, N_SLABS, 1, SLAB), F32),
            jax.ShapeDtypeStruct((b, N_SLABS, 1, SLAB), F32),
        ],
        scratch_shapes=[
            pltpu.VMEM((HIST_ROWS, D_POOL), F32),
            pltpu.VMEM((N_SLABS, 1, SLAB), F32),
            pltpu.VMEM((N_SLABS, 1, SLAB), F32),
        ] + [pltpu.VMEM((tt, SLAB), F32)] * (2 * N_SLABS),
        compiler_params=pltpu.CompilerParams(
            dimension_semantics=("arbitrary", "arbitrary"), vmem_limit_bytes=VMEM_LIMIT),
    )(x, hist, h0re, h0im, prm['g_mix'], prm['w_in'], prm['w_pool'], prm['pool_scale'],
      prm['bmat'], prm['c_re'], prm['c_im'], prm['t_re'], prm['t_im'], prm['d_skip'],
      prm['w_glu'], prm['b_glu'], prm['w_out'])


def _memkv_kernel(m_ref, g_ref, wk_ref, wv_ref, k_ref, v_ref):
    m = _rms(m_ref[0], g_ref[...]).astype(BF16)
    k_ref[0] = _dot(m, wk_ref[...])
    v_ref[0] = _dot(m, wv_ref[...])


def _memkv(mem, g, wk, wv):
    b, n, d = mem.shape
    blk = pl.BlockSpec((1, n, d), lambda i: (i, 0, 0))
    return pl.pallas_call(
        _memkv_kernel,
        grid=(b,),
        in_specs=[blk, _const_spec((1, d)), _const_spec((d, d)), _const_spec((d, d))],
        out_specs=[blk, blk],
        out_shape=[jax.ShapeDtypeStruct((b, n, d), F32)] * 2,
        compiler_params=pltpu.CompilerParams(vmem_limit_bytes=VMEM_LIMIT),
    )(mem, g, wk, wv)


def _attn_kernel(h_ref, k_ref, v_ref, gx_ref, wq_ref, wo_ref, gf_ref, wr_ref, br_ref,
                 h2_ref, xn_ref, e_ref, gate_ref, cnt_ref, cnt_scr, kt_scr, v_scr, *, tt):
    @pl.when(pl.program_id(1) == 0)
    def _():
        kt_scr[...] = k_ref[0].T.astype(BF16)
        v_scr[...] = v_ref[0].astype(BF16)

    h = h_ref[0]
    hn = _rms(h, gx_ref[...])
    q = _dot(hn.astype(BF16), wq_ref[...])
    outs = []
    for hd in range(N_XHEADS):
        cols = slice(hd * XHEAD_DIM, (hd + 1) * XHEAD_DIM)
        s = _dot(q[:, cols].astype(BF16), kt_scr[cols, :]) * (XHEAD_DIM ** -0.5)
        p = jnp.exp(s - jnp.max(s, axis=-1, keepdims=True))
        p = p / jnp.sum(p, axis=-1, keepdims=True)
        outs.append(_dot(p.astype(BF16), v_scr[:, cols]))
    o = jnp.concatenate(outs, axis=1)
    h2 = h + _dot(o.astype(BF16), wo_ref[...])
    h2_ref[0] = h2

    xn = _rms(h2, gf_ref[...])
    for s in range(ROW_TILE):
        xn_ref[pl.ds(s, tt, stride=ROW_TILE), :] = xn[:, s * LANES:(s + 1) * LANES]
    logits = _dot(xn.astype(BF16), wr_ref[...]) + br_ref[...]
    lane = lax.broadcasted_iota(jnp.int32, logits.shape, 1)
    lane_f = lane.astype(F32)
    e_out = jnp.zeros(logits.shape, jnp.int32)
    hits = jnp.zeros(logits.shape, F32)
    top = []
    for k in range(TOP_K):
        m = jnp.max(logits, axis=-1, keepdims=True)
        idx = jnp.min(jnp.where(logits == m, lane_f, float(LANES)), axis=-1, keepdims=True)
        e_out = jnp.where(lane == k, idx.astype(jnp.int32), e_out)
        top.append(m)
        chosen = lane_f == idx
        hits = hits + chosen.astype(F32)
        logits = jnp.where(chosen, -jnp.inf, logits)

    @pl.when(jnp.logical_and(pl.program_id(0) == 0, pl.program_id(1) == 0))
    def _():
        cnt_scr[...] = jnp.zeros_like(cnt_scr)
    cnt_scr[...] = cnt_scr[...] + jnp.sum(hits, axis=0, keepdims=True)
    cnt_ref[...] = cnt_scr[...]
    ex = [jnp.exp(m - top[0]) for m in top]
    tot = ex[0] + ex[1] + ex[2] + ex[3]
    g_out = jnp.zeros(logits.shape, F32)
    for k in range(TOP_K):
        g_out = jnp.where(lane == k, ex[k] / tot, g_out)
    e_ref[...] = e_out
    gate_ref[...] = g_out


def _attn(h, k, v, prm, *, tt):
    b, t, d = h.shape
    n = b * t
    kern = functools.partial(_attn_kernel, tt=tt)
    nt = t // tt
    kv = pl.BlockSpec((1, N_MEM, d), lambda i, j: (i, 0, 0))
    tok = lambda width: pl.BlockSpec((tt, width), lambda i, j: (i * nt + j, 0))
    return pl.pallas_call(
        kern,
        grid=(b, nt),
        in_specs=[
            pl.BlockSpec((1, tt, d), lambda i, j: (i, j, 0)), kv, kv,
            _const_spec((1, d)), _const_spec((d, d)), _const_spec((d, d)),
            _const_spec((1, d)), _const_spec((d, LANES)), _const_spec((1, LANES)),
        ],
        out_specs=[
            pl.BlockSpec((1, tt, d), lambda i, j: (i, j, 0)),
            pl.BlockSpec((tt * ROW_TILE, LANES), lambda i, j: (i * nt + j, 0)),
            tok(LANES), tok(LANES), _const_spec((1, LANES)),
        ],
        out_shape=[
            jax.ShapeDtypeStruct((b, t, d), F32),
            jax.ShapeDtypeStruct((n * ROW_TILE, LANES), F32),
            jax.ShapeDtypeStruct((n, LANES), jnp.int32),
            jax.ShapeDtypeStruct((n, LANES), F32),
            jax.ShapeDtypeStruct((1, LANES), F32),
        ],
        scratch_shapes=[pltpu.VMEM((1, LANES), F32), pltpu.VMEM((d, N_MEM), BF16), pltpu.VMEM((N_MEM, d), BF16)],
        compiler_params=pltpu.CompilerParams(
            dimension_semantics=("arbitrary", "arbitrary"), vmem_limit_bytes=VMEM_LIMIT),
    )(h, k, v, prm['g_xattn'], prm['w_q'], prm['w_o'], prm['g_ffn'], prm['w_router'], prm['b_router'])


RANK_TILE = 512


def _rank_kernel(ep_ref, es_ref, start_ref, dest_ref, carry_scr, before_scr, *, steps_p):
    i = pl.program_id(0)

    @pl.when(i == 0)
    def _():
        carry_scr[...] = start_ref[...]
        r = lax.broadcasted_iota(jnp.int32, (RANK_TILE, RANK_TILE), 0)
        c = lax.broadcasted_iota(jnp.int32, (RANK_TILE, RANK_TILE), 1)
        before_scr[...] = (c < r).astype(BF16)

    e = jnp.where(i < steps_p, ep_ref[...], es_ref[...])
    lane = lax.broadcasted_iota(jnp.int32, e.shape, 1)
    onehot = [lane == e[:, k:k + 1] for k in range(TOP_K)]
    hits = jnp.zeros(e.shape, F32)
    for oh in onehot:
        hits = hits + oh.astype(F32)
    base = _dot(before_scr[...], hits.astype(BF16)) + carry_scr[...]
    rank = jnp.zeros(e.shape, jnp.int32)
    for k, oh in enumerate(onehot):
        rk = jnp.sum(jnp.where(oh, base, 0.0), axis=-1, keepdims=True).astype(jnp.int32)
        rank = jnp.where(lane == k, rk, rank)
    dest_ref[...] = rank
    carry_scr[...] = carry_scr[...] + jnp.sum(hits, axis=0, keepdims=True)


def _rank(e_p, e_s, start):
    steps_p = e_p.shape[0] // RANK_TILE
    blk = (RANK_TILE, LANES)
    return pl.pallas_call(
        functools.partial(_rank_kernel, steps_p=steps_p),
        grid=(steps_p + 1,),
        in_specs=[pl.BlockSpec(blk, lambda i: (jnp.minimum(i, steps_p - 1), 0)), _const_spec(blk),
                  _const_spec((1, LANES))],
        out_specs=pl.BlockSpec(blk, lambda i: (i, 0)),
        out_shape=jax.ShapeDtypeStruct(((steps_p + 1) * RANK_TILE, LANES), jnp.int32),
        scratch_shapes=[pltpu.VMEM((1, LANES), F32), pltpu.VMEM((RANK_TILE, RANK_TILE), BF16)],
        compiler_params=pltpu.CompilerParams(dimension_semantics=("arbitrary",)),
    )(e_p, e_s, start)


BLOCK_ROWS = MOE_BLOCK * ROW_TILE


def _row_copy_wait(src_ref, dst_ref, sem, rows):
    pltpu.make_async_copy(src_ref.at[pl.ds(0, rows * ROW_TILE)], dst_ref.at[pl.ds(0, rows * ROW_TILE)], sem).wait()


DISPATCH_SLOTS = 3


def _dispatch_kernel(last_blk_ref, nvalid_ref, dest_p_ref, dest_s_ref, xp_hbm, xs_ref, xpad_ref,
                     zero_buf, xbuf, sems, load_sems, zero_sem, *, tile_p, tile_s, n_blocks, steps_p):
    i = pl.program_id(0)
    last = steps_p
    tile_rows = tile_p * ROW_TILE

    def zero_copy(blk):
        dst = xpad_ref.at[pl.ds(pl.multiple_of(blk * BLOCK_ROWS, BLOCK_ROWS), BLOCK_ROWS)]
        return pltpu.make_async_copy(zero_buf, dst, zero_sem)

    @pl.when(i == 0)
    def _():
        zero_buf[...] = jnp.zeros_like(zero_buf)

        def on_pad_blocks(fn):
            for e in range(N_EXPERTS):
                if e == 0:
                    fn(last_blk_ref[0])
                else:
                    pl.when(last_blk_ref[e] != last_blk_ref[e - 1])(functools.partial(fn, last_blk_ref[e]))
            lax.fori_loop(nvalid_ref[0], n_blocks, lambda blk, c: (fn(blk), c)[1], 0)

        on_pad_blocks(lambda blk: zero_copy(blk).start())
        on_pad_blocks(lambda blk: zero_copy(blk).wait())

    def load(tile, slot):
        src = xp_hbm.at[pl.ds(pl.multiple_of(tile * tile_rows, tile_rows), tile_rows)]
        return pltpu.make_async_copy(src, xbuf.at[slot], load_sems.at[slot])

    def push(dest_ref, x_ref, sem, tile):
        def body(it, carry):
            for j in range(PUSH_UNROLL):
                t = it * PUSH_UNROLL + j
                src = x_ref.at[pl.ds(pl.multiple_of(t * ROW_TILE, ROW_TILE), ROW_TILE)]
                for k in range(TOP_K):
                    row = dest_ref[0, 0, t * TOP_K + k]
                    dst = xpad_ref.at[pl.ds(pl.multiple_of(row * ROW_TILE, ROW_TILE), ROW_TILE)]
                    pltpu.make_async_copy(src, dst, sem).start(priority=k % N_DMA_PRIORITIES)
            return carry

        lax.fori_loop(0, tile // PUSH_UNROLL, body, 0)

    def landed(slot, tile):
        for _ in range(TOP_K):
            _row_copy_wait(xbuf.at[slot], xpad_ref, sems.at[slot], tile)

    pl.when(i == 0)(lambda: load(0, 0).start())

    @pl.when(i < last)
    def _():
        slot = i % DISPATCH_SLOTS
        load(i, slot).wait()
        push(dest_p_ref, xbuf.at[slot], sems.at[slot], tile_p)
        reuse = (i + 1) % DISPATCH_SLOTS
        pl.when(i >= DISPATCH_SLOTS - 1)(lambda: landed(reuse, tile_p))
        pl.when(i + 1 < last)(lambda: load(i + 1, reuse).start())

    @pl.when(i == last)
    def _():
        for tile in range(max(steps_p - (DISPATCH_SLOTS - 1), 0), steps_p):
            landed(tile % DISPATCH_SLOTS, tile_p)
        push(dest_s_ref, xs_ref, sems.at[0], tile_s)
        for _ in range(TOP_K):
            _row_copy_wait(xs_ref, xpad_ref, sems.at[0], tile_s)


def _dispatch(last_blk, nvalid, dest_p, dest_s, xn_p, xn_s, n_blocks, *, tile_p):
    n_p = xn_p.shape[0] // ROW_TILE
    tile_s = xn_s.shape[0] // ROW_TILE
    steps_p = n_p // tile_p
    kern = functools.partial(_dispatch_kernel, tile_p=tile_p, tile_s=tile_s, n_blocks=n_blocks, steps_p=steps_p)
    return pl.pallas_call(
        kern,
        grid_spec=pltpu.PrefetchScalarGridSpec(
            num_scalar_prefetch=2,
            grid=(steps_p + 1,),
            in_specs=[
                pl.BlockSpec((1, 1, tile_p * TOP_K), lambda i, lb, nv: (jnp.minimum(i, steps_p - 1), 0, 0),
                             memory_space=pltpu.SMEM),
                pl.BlockSpec((1, 1, tile_s * TOP_K), lambda i, lb, nv: (0, 0, 0), memory_space=pltpu.SMEM),
                pl.BlockSpec(memory_space=pl.ANY),
                pl.BlockSpec((tile_s * ROW_TILE, LANES), lambda i, lb, nv: (0, 0)),
            ],
            out_specs=pl.BlockSpec(memory_space=pl.ANY),
            scratch_shapes=[pltpu.VMEM((BLOCK_ROWS, LANES), F32),
                            pltpu.VMEM((DISPATCH_SLOTS, tile_p * ROW_TILE, LANES), F32),
                            pltpu.SemaphoreType.DMA((DISPATCH_SLOTS,)), pltpu.SemaphoreType.DMA((DISPATCH_SLOTS,)),
                            pltpu.SemaphoreType.DMA],
        ),
        out_shape=jax.ShapeDtypeStruct((n_blocks * BLOCK_ROWS, LANES), F32),
        compiler_params=pltpu.CompilerParams(dimension_semantics=("arbitrary",)),
    )(last_blk, nvalid, dest_p.reshape(steps_p, 1, tile_p * TOP_K), dest_s.reshape(1, 1, tile_s * TOP_K), xn_p, xn_s)


def _expert_kernel(blk_e_ref, nvalid_ref, next_e_ref, x_ref, bg_ref, bu_ref, bd_ref, wg_hbm, wu_hbm, wd_hbm, y_ref,
                   stage, w_bf, sems):
    b = pl.program_id(0)
    valid = b < nvalid_ref[0]
    e = blk_e_ref[b]
    weights = (wg_hbm, wu_hbm, wd_hbm)

    def fetch(expert):
        return [pltpu.make_async_copy(w.at[expert], stage.at[i], sems.at[i]) for i, w in enumerate(weights)]

    @pl.when(b == 0)
    def _():
        for cp in fetch(e):
            cp.start()

    @pl.when(jnp.logical_and(valid, jnp.logical_or(b == 0, blk_e_ref[jnp.maximum(b - 1, 0)] != e)))
    def _():
        for i, cp in enumerate(fetch(e)):
            cp.wait()
            w_bf[i] = stage[i].astype(BF16)

        @pl.when(next_e_ref[b] >= 0)
        def _():
            for cp in fetch(next_e_ref[b]):
                cp.start()

    @pl.when(valid)
    def _():
        x = jnp.concatenate([x_ref[pl.ds(s, MOE_BLOCK, stride=ROW_TILE), :] for s in range(ROW_TILE)], axis=1)
        x = x.astype(BF16)
        g = _dot(x, w_bf[0]) + bg_ref[0]
        u = _dot(x, w_bf[1]) + bu_ref[0]
        g = jnp.minimum(g, SWIGLU_LIMIT)
        u = jnp.clip(u, -SWIGLU_LIMIT, SWIGLU_LIMIT)
        hdn = g * (0.5 * (1.0 + jnp.tanh((0.5 * SWIGLU_ALPHA) * g))) * (u + 1.0)
        y = _dot(hdn.astype(BF16), w_bf[2]) + bd_ref[0]
        for s in range(ROW_TILE):
            y_ref[pl.ds(s, MOE_BLOCK, stride=ROW_TILE), :] = y[:, s * LANES:(s + 1) * LANES]

    @pl.when(b >= nvalid_ref[0])
    def _():
        y_ref[...] = jnp.zeros_like(y_ref)


def _experts(blk_e, nvalid, next_e, xpad, w_gate, b_gate, w_up, b_up, w_down, b_down, n_blocks):
    d = D_MODEL
    n_w = 3
    bspec = pl.BlockSpec((1, 1, d), lambda b, be, nv, ne: (be[b], 0, 0))
    xspec = pl.BlockSpec((BLOCK_ROWS, LANES), lambda b, be, nv, ne: (jnp.minimum(b, nv[0] - 1), 0))
    hbm = pl.BlockSpec(memory_space=pl.ANY)
    return pl.pallas_call(
        _expert_kernel,
        grid_spec=pltpu.PrefetchScalarGridSpec(
            num_scalar_prefetch=3,
            grid=(n_blocks,),
            in_specs=[xspec, bspec, bspec, bspec, hbm, hbm, hbm],
            out_specs=pl.BlockSpec((BLOCK_ROWS, LANES), lambda b, be, nv, ne: (b, 0)),
            scratch_shapes=[pltpu.VMEM((n_w, d, d), F32), pltpu.VMEM((n_w, d, d), BF16),
                            pltpu.SemaphoreType.DMA((n_w,))],
        ),
        out_shape=jax.ShapeDtypeStruct((n_blocks * BLOCK_ROWS, LANES), F32),
        compiler_params=pltpu.CompilerParams(dimension_semantics=("arbitrary",), vmem_limit_bytes=VMEM_LIMIT),
    )(blk_e, nvalid, next_e, xpad, b_gate.reshape(N_EXPERTS, 1, d), b_up.reshape(N_EXPERTS, 1, d),
      b_down.reshape(N_EXPERTS, 1, d), w_gate, w_up, w_down)


COMBINE_SLOTS = 3


def _combine_kernel(dest_ref, dest_1_ref, dest_2_ref, h_ref, gate_ref, gfin_ref, ypad_ref, y_ref, *bufs_sems,
                    tile, n_steps):
    bufs, sems = bufs_sems[:COMBINE_SLOTS], bufs_sems[COMBINE_SLOTS]
    i = pl.program_id(0)

    def gather(dref, buf, sem, t, k):
        row = dref[0, 0, t * TOP_K + k]
        slot = (k * tile + t) * ROW_TILE
        if not isinstance(slot, int):
            slot = pl.multiple_of(slot, ROW_TILE)
        src = ypad_ref.at[pl.ds(pl.multiple_of(row * ROW_TILE, ROW_TILE), ROW_TILE)]
        pltpu.make_async_copy(src, buf.at[pl.ds(slot, ROW_TILE)], sem).start(priority=k % N_DMA_PRIORITIES)

    def drain(buf, sem):
        for _ in range(TOP_K):
            _row_copy_wait(ypad_ref, buf, sem, tile)

    @pl.when(i == 0)
    def _():
        def body(t, carry):
            for k in range(TOP_K):
                gather(dest_ref, bufs[0], sems.at[0], t, k)
                gather(dest_1_ref, bufs[1], sems.at[1], t, k)
            return carry
        lax.fori_loop(0, tile, body, 0)

    def step(cur, cur_sem, nxt, nxt_sem):
        drain(cur, cur_sem)
        gates = gate_ref[...]
        h = h_ref[...]
        per_slab = tile // ROW_TILE
        cols = []
        for s in range(ROW_TILE):
            for t in range(s * per_slab, (s + 1) * per_slab):
                for k in range(TOP_K):
                    gather(dest_2_ref, nxt, nxt_sem, t, k)
            acc = h[:, s * LANES:(s + 1) * LANES]
            for k in range(TOP_K):
                acc = acc + gates[:, k:k + 1] * cur[pl.ds(k * tile * ROW_TILE + s, tile, stride=ROW_TILE), :]
            cols.append(acc)
        y_ref[...] = _rms(jnp.concatenate(cols, axis=1), gfin_ref[...])

    for r in range(COMBINE_SLOTS):
        ahead = (r + COMBINE_SLOTS - 1) % COMBINE_SLOTS
        pl.when(i % COMBINE_SLOTS == r)(functools.partial(step, bufs[r], sems.at[r], bufs[ahead], sems.at[ahead]))

    @pl.when(i == n_steps - 1)
    def _():
        for extra in (n_steps, n_steps + 1):
            drain(bufs[extra % COMBINE_SLOTS], sems.at[extra % COMBINE_SLOTS])


def _combine(dest, h2, gates, g_final, ypad, *, tile):
    n, d = h2.shape
    steps = n // tile
    kern = functools.partial(_combine_kernel, tile=tile, n_steps=steps)
    dest3 = dest.reshape(steps, 1, tile * TOP_K)
    buf = pltpu.VMEM((TOP_K * tile * ROW_TILE, LANES), F32)
    tile_ahead = lambda k: pl.BlockSpec((1, 1, tile * TOP_K), lambda i: (jnp.minimum(i + k, steps - 1), 0, 0),
                                        memory_space=pltpu.SMEM)
    return pl.pallas_call(
        kern,
        grid=(steps,),
        in_specs=[
            tile_ahead(0), tile_ahead(1), tile_ahead(2),
            pl.BlockSpec((tile, d), lambda i: (i, 0)),
            pl.BlockSpec((tile, LANES), lambda i: (i, 0)),
            _const_spec((1, d)),
            pl.BlockSpec(memory_space=pl.ANY),
        ],
        out_specs=pl.BlockSpec((tile, d), lambda i: (i, 0)),
        out_shape=jax.ShapeDtypeStruct((n, d), F32),
        scratch_shapes=[buf] * COMBINE_SLOTS + [pltpu.SemaphoreType.DMA((COMBINE_SLOTS,))],
        compiler_params=pltpu.CompilerParams(dimension_semantics=("arbitrary",), vmem_limit_bytes=VMEM_LIMIT),
    )(dest3, dest3, dest3, h2, gates, g_final, ypad)


def _ssm_params(a_re, a_im, log_dt, b_re, b_im, c_re, c_im):
    dt = jnp.exp(log_dt)[:, None]
    mag = jnp.exp(a_re * dt)
    lb_re = mag * jnp.cos(a_im * dt)
    lb_im = mag * jnp.sin(a_im * dt)
    den = a_re * a_re + a_im * a_im
    q_re = ((lb_re - 1.0) * a_re + lb_im * a_im) / den
    q_im = (lb_im * a_re - (lb_re - 1.0) * a_im) / den
    bb_re = q_re[:, :, None] * b_re - q_im[:, :, None] * b_im
    bb_im = q_re[:, :, None] * b_im + q_im[:, :, None] * b_re
    eye = jnp.eye(SSM_PACK, dtype=F32)

    def pack_blockdiag(m):
        m = m.reshape(N_PACKS, SSM_PACK, m.shape[1], m.shape[2])
        return jnp.einsum('ngab,gh->ngahb', m, eye).reshape(N_PACKS, SSM_PACK * m.shape[2], SSM_PACK * m.shape[3])

    pw_re, pw_im = [jnp.ones_like(lb_re), lb_re], [jnp.zeros_like(lb_im), lb_im]
    for _ in range(RUN - 1):
        r, i = pw_re[-1], pw_im[-1]
        pw_re.append(r * lb_re - i * lb_im)
        pw_im.append(r * lb_im + i * lb_re)
    b_rows = []
    for dd in range(RUN):
        s_re = bb_re * pw_re[dd][:, :, None] - bb_im * pw_im[dd][:, :, None]
        s_im = bb_re * pw_im[dd][:, :, None] + bb_im * pw_re[dd][:, :, None]
        b_rows.append(jnp.concatenate([pack_blockdiag(s_re.transpose(0, 2, 1)),
                                       pack_blockdiag(s_im.transpose(0, 2, 1))], axis=2))
    bmat = jnp.concatenate(b_rows, axis=1)
    cmat_re = pack_blockdiag(c_re.transpose(0, 2, 1))
    cmat_im = pack_blockdiag(-c_im.transpose(0, 2, 1))
    rows_of = lambda pw: [pw[(r % RUN) + 1].reshape(-1) for r in range(SUBLANES)]
    slabbed = lambda rows: jnp.stack(rows).reshape(SUBLANES, N_SLABS, SLAB).transpose(1, 0, 2)
    return (bmat.astype(BF16), cmat_re.astype(BF16), cmat_im.astype(BF16),
            slabbed(rows_of(pw_re)), slabbed(rows_of(pw_im)))


def kernel(x_prompt, x_sample, mem_prompt, cache_mem_k, cache_mem_v, cache_pool, state_ssm_re, state_ssm_im, norm_mix, w_in, w_pool, pool_scale, ssm_a_re, ssm_a_im, ssm_log_dt, ssm_b_re, ssm_b_im, ssm_c_re, ssm_c_im, ssm_d, w_glu, b_glu, w_out, norm_xattn, norm_mem, w_q, w_k, w_v, w_o, norm_ffn, w_router, b_router, w_gate, b_gate, w_up, b_up, w_down, b_down, norm_final):
    assert x_prompt.shape[2] == D_MODEL and norm_mix.shape[0] == 1
    bp, tp, d = x_prompt.shape
    bs, ts, _ = x_sample.shape
    n_p, n_s = bp * tp, bs * ts
    row = lambda v: v.reshape(1, -1)

    bmat, c_re, c_im, t_re, t_im = _ssm_params(ssm_a_re[0], ssm_a_im[0], ssm_log_dt[0], ssm_b_re[0], ssm_b_im[0],
                                               ssm_c_re[0], ssm_c_im[0])
    mix_prm = dict(g_mix=row(norm_mix[0]), w_in=w_in[0].astype(BF16), w_pool=w_pool[0].astype(BF16),
                   pool_scale=row(pool_scale[0]), bmat=bmat, c_re=c_re, c_im=c_im, t_re=t_re, t_im=t_im,
                   d_skip=row(ssm_d[0]), w_glu=w_glu[0].astype(BF16), b_glu=row(b_glu[0]),
                   w_out=w_out[0].astype(BF16))

    attn_prm = dict(g_xattn=row(norm_xattn[0]), w_q=w_q[0].astype(BF16), w_o=w_o[0].astype(BF16),
                    g_ffn=row(norm_ffn[0]),
                    w_router=jnp.pad(w_router[0], ((0, 0), (0, LANES - N_EXPERTS))).astype(BF16),
                    b_router=jnp.pad(row(b_router[0]), ((0, 0), (0, LANES - N_EXPERTS)), constant_values=-jnp.inf))
    slab_state = lambda s: s.reshape(s.shape[0], N_SLABS, 1, SLAB)

    zeros_state = jnp.zeros((bp, N_SLABS, 1, SLAB), F32)
    h1_p, pool_p, sre_p, sim_p = _mixer(x_prompt, jnp.zeros((bp, HIST_ROWS, D_POOL), F32), zeros_state, zeros_state,
                                        mix_prm, tt=MIXER_TILE, pos0=0)
    mk, mv = _memkv(mem_prompt, row(norm_mem[0]), w_k[0].astype(BF16), w_v[0].astype(BF16))
    h2_p, xn_p, e_p, g_p, cnt_p = _attn(h1_p, mk, mv, attn_prm, tt=ATTN_TILE)

    hist_s = jnp.pad(cache_pool[0], ((0, 0), (HIST_ROWS - POOL_BUF, 0), (0, 0)))
    h1_s, pool_s, sre_s, sim_s = _mixer(x_sample, hist_s, slab_state(state_ssm_re[0]), slab_state(state_ssm_im[0]),
                                        mix_prm, tt=ts, pos0=PAST_LEN)
    ck = cache_mem_k[0].reshape(bs, N_MEM, d)
    cv = cache_mem_v[0].reshape(bs, N_MEM, d)
    h2_s, xn_s, e_s, g_s, cnt_s = _attn(h1_s, ck, cv, attn_prm, tt=ts)

    n_all = n_p + n_s
    counts = (cnt_p + cnt_s)[0, :N_EXPERTS].astype(jnp.int32)
    padded = (counts + MOE_BLOCK - 1) // MOE_BLOCK * MOE_BLOCK
    pend = jnp.cumsum(padded)
    pstart = pend - padded
    n_blocks = -(-(n_all * TOP_K + N_EXPERTS * (MOE_BLOCK - 1)) // MOE_BLOCK)
    blk_e = jnp.minimum(jnp.sum(pend[None, :] <= (jnp.arange(n_blocks) * MOE_BLOCK)[:, None], axis=1),
                        N_EXPERTS - 1).astype(jnp.int32)
    nvalid = (pend[-1:] // MOE_BLOCK).astype(jnp.int32)
    last_blk = jnp.maximum(pend // MOE_BLOCK - 1, 0).astype(jnp.int32)
    after = (pend // MOE_BLOCK)[blk_e]
    next_e = jnp.where(after < nvalid[0], blk_e[jnp.minimum(after, n_blocks - 1)], -1).astype(jnp.int32)
    start = jnp.pad(pstart.astype(F32), (0, LANES - N_EXPERTS)).reshape(1, LANES)
    e_s_tile = jnp.pad(e_s, ((0, RANK_TILE - n_s), (0, 0)), constant_values=-1)
    dest = _rank(e_p, e_s_tile, start)[:n_all, :TOP_K].reshape(-1)
    dest_p, dest_s = dest[:n_p * TOP_K], dest[n_p * TOP_K:]

    xpad = _dispatch(last_blk, nvalid, dest_p, dest_s, xn_p, xn_s, n_blocks, tile_p=ROUTE_TILE)
    ypad = _experts(blk_e, nvalid, next_e, xpad, w_gate[0], b_gate[0], w_up[0], b_up[0], w_down[0], b_down[0], n_blocks)
    g_fin = row(norm_final)
    y_p = _combine(dest_p, h2_p.reshape(n_p, d), g_p, g_fin, ypad, tile=ROUTE_TILE)
    y_s = _combine(dest_s, h2_s.reshape(n_s, d), g_s, g_fin, ypad, tile=n_s)

    unslab = lambda s: s.reshape(1, s.shape[0], N_SSM_GROUPS, SSM_STATE)
    kv5 = lambda a: a.reshape(1, bp, N_MEM, N_XHEADS, XHEAD_DIM)
    return (y_p.reshape(bp, tp, d), y_s.reshape(bs, ts, d), kv5(mk), kv5(mv),
            pool_p[None, :, HIST_ROWS - POOL_BUF:], pool_s[None, :, HIST_ROWS - POOL_BUF:],
            unslab(sre_p), unslab(sim_p), unslab(sre_s), unslab(sim_s))
```

```python
import functools
import math

import jax
import jax.numpy as jnp
from jax import lax
from jax.experimental import pallas as pl
from jax.experimental.pallas import tpu as pltpu

D_MODEL = 1024
D_POOL = 512
D_SSM = 512
POOL_WINDOWS = (2, 4, 8, 16)
POOL_GROUP = 128
HIST_ROWS = 16
POOL_BUF = 15
SSM_GROUP = 16
N_SSM_GROUPS = 32
SSM_STATE = 64
N_STATE = N_SSM_GROUPS * SSM_STATE
N_MEM = 256
N_XHEADS = 4
XHEAD_DIM = 256
N_EXPERTS = 32
TOP_K = 4
SWIGLU_LIMIT = 7.0
SWIGLU_ALPHA = 1.702
MOE_BLOCK = 512
EPS = 1e-6
PAST_LEN = 1024

LANES = 128
SUBLANES = 8
SLAB = 256
N_SLABS = N_STATE // SLAB
SSM_PACK = LANES // SSM_GROUP
N_PACKS = N_SSM_GROUPS // SSM_PACK
PACK_STATES = SSM_PACK * SSM_STATE
SLABS_PER_PACK = PACK_STATES // SLAB
RUN = 4
ROW_TILE = D_MODEL // LANES
N_DMA_PRIORITIES = 2
PUSH_UNROLL = 4
VMEM_LIMIT = 56 * 1024 * 1024

MIXER_TILE = 512
ATTN_TILE = 1024
ROUTE_TILE = 256
DISPATCH_TILE = 512

BF16 = jnp.bfloat16
F32 = jnp.float32


def _rms(x, g):
    return x * lax.rsqrt(jnp.mean(x * x, axis=-1, keepdims=True) + EPS) * g


def _dot(a, b):
    return jnp.dot(a, b, preferred_element_type=F32)


def _chain_slab(s, tre_ref, tim_ref, hre_scr, him_scr, re_scr, im_scr, tt):
    t_re = tre_ref[s]
    t_im = tim_ref[s]
    upper = lax.broadcasted_iota(jnp.int32, (SUBLANES, SLAB), 0) < RUN
    ta_re, ta_im = jnp.where(upper, t_re, 0.0), jnp.where(upper, t_im, 0.0)
    tb_re, tb_im = jnp.where(upper, 0.0, t_re), jnp.where(upper, 0.0, t_im)
    c_re = hre_scr[s]
    c_im = him_scr[s]
    for v in range(tt // SUBLANES):
        rows = pl.ds(v * SUBLANES, SUBLANES)
        re = re_scr[rows, :]
        im = im_scr[rows, :]
        re, im = re + (ta_re * c_re - ta_im * c_im), im + (ta_re * c_im + ta_im * c_re)
        m_re = re[RUN - 1:RUN, :]
        m_im = im[RUN - 1:RUN, :]
        re, im = re + (tb_re * m_re - tb_im * m_im), im + (tb_re * m_im + tb_im * m_re)
        re_scr[rows, :] = re
        im_scr[rows, :] = im
        c_re = re[SUBLANES - 1:SUBLANES, :]
        c_im = im[SUBLANES - 1:SUBLANES, :]
    hre_scr[s] = c_re
    him_scr[s] = c_im


def _mixer_kernel(x_ref, hist_ref, h0re_ref, h0im_ref, gmix_ref, win_ref, wpool_ref, pscale_ref,
                  bmat_ref, cre_ref, cim_ref, tre_ref, tim_ref, dskip_ref, wglu_ref, bglu_ref, wout_ref,
                  h_ref, poolnew_ref, ssmre_ref, ssmim_ref,
                  hist_scr, hre_scr, him_scr, *slab_scr, tt, pos0):
    bure_scr, buim_scr = slab_scr[:N_SLABS], slab_scr[N_SLABS:]
    j = pl.program_id(1)

    @pl.when(j == 0)
    def _():
        hist_scr[...] = hist_ref[0]
        hre_scr[...] = h0re_ref[0]
        him_scr[...] = h0im_ref[0]

    x = x_ref[0]
    xn = _rms(x, gmix_ref[...])
    z = _dot(xn.astype(BF16), win_ref[...])
    zp = z[:, :D_POOL]
    u = z[:, D_POOL:]

    ext = jnp.concatenate([hist_scr[...], zp], axis=0)
    pos = pos0 + j * tt + lax.broadcasted_iota(jnp.int32, (tt, 1), 0)
    acc = ext
    outs = []
    for gi, w in enumerate(POOL_WINDOWS):
        lo = gi * POOL_GROUP
        acc = acc[:, POOL_GROUP * (1 if gi else 0):]
        acc = acc + pltpu.roll(acc, w // 2, 0)
        wsum = acc[HIST_ROWS:, :POOL_GROUP]
        cnt = jnp.minimum(pos + 1, w).astype(F32)
        d = wsum / cnt - zp[:, lo:lo + POOL_GROUP]
        outs.append(_dot(d.astype(BF16), wpool_ref[gi]))
    y_pool = jnp.concatenate(outs, axis=1) * pscale_ref[...]
    hist_scr[...] = ext[tt:tt + HIST_ROWS]
    poolnew_ref[0] = ext[tt:tt + HIST_ROWS]

    run_row = lax.broadcasted_iota(jnp.int32, (tt, 1), 0) % RUN
    shifted = [u.astype(BF16)]
    for dd in range(1, RUN):
        shifted.append(jnp.where(run_row >= dd, pltpu.roll(u, dd, 0), 0.0).astype(BF16))
    def project(c):
        lhs = jnp.concatenate([sh[:, c * LANES:(c + 1) * LANES] for sh in shifted], axis=1)
        bu = _dot(lhs, bmat_ref[c])
        for i in range(SLABS_PER_PACK):
            s = c * SLABS_PER_PACK + i
            bure_scr[s][...] = bu[:, i * SLAB:(i + 1) * SLAB]
            buim_scr[s][...] = bu[:, PACK_STATES + i * SLAB:PACK_STATES + (i + 1) * SLAB]

    ys = []
    project(0)
    for c in range(N_PACKS):
        slabs = range(c * SLABS_PER_PACK, (c + 1) * SLABS_PER_PACK)
        if c + 1 < N_PACKS:
            project(c + 1)
        for s in slabs:
            _chain_slab(s, tre_ref, tim_ref, hre_scr, him_scr, bure_scr[s], buim_scr[s], tt)
        hs_re = jnp.concatenate([bure_scr[s][...] for s in slabs], axis=1)
        hs_im = jnp.concatenate([buim_scr[s][...] for s in slabs], axis=1)
        ys.append(_dot(hs_re.astype(BF16), cre_ref[c]) + _dot(hs_im.astype(BF16), cim_ref[c]))
    ssmre_ref[0] = hre_scr[...]
    ssmim_ref[0] = him_scr[...]
    y = jnp.concatenate(ys, axis=1) + dskip_ref[...] * u
    g = 0.5 * y * (1.0 + jnp.tanh(math.sqrt(2.0 / math.pi) * (y + 0.044715 * (y * y * y))))
    y_ssm = g * jax.nn.sigmoid(_dot(g.astype(BF16), wglu_ref[...]) + bglu_ref[...])

    mix = jnp.concatenate([y_pool, y_ssm], axis=1)
    h_ref[0] = x + _dot(mix.astype(BF16), wout_ref[...])


def _const_spec(shape):
    return pl.BlockSpec(shape, lambda *_: (0,) * len(shape))


def _mixer(x, hist, h0re, h0im, prm, *, tt, pos0):
    b, t, d = x.shape
    kern = functools.partial(_mixer_kernel, tt=tt, pos0=pos0)
    per_b3 = lambda shp: pl.BlockSpec((1,) + shp, lambda i, j: (i, 0, 0))
    per_b4 = lambda shp: pl.BlockSpec((1,) + shp, lambda i, j: (i, 0, 0, 0))
    return pl.pallas_call(
        kern,
        grid=(b, t // tt),
        in_specs=[
            pl.BlockSpec((1, tt, d), lambda i, j: (i, j, 0)),
            per_b3((HIST_ROWS, D_POOL)),
            per_b4((N_SLABS, 1, SLAB)),
            per_b4((N_SLABS, 1, SLAB)),
            _const_spec((1, d)),
            _const_spec((d, d)),
            _const_spec((len(POOL_WINDOWS), POOL_GROUP, POOL_GROUP)),
            _const_spec((1, D_POOL)),
            _const_spec((N_PACKS, RUN * LANES, 2 * PACK_STATES)),
            _const_spec((N_PACKS, PACK_STATES, LANES)),
            _const_spec((N_PACKS, PACK_STATES, LANES)),
            _const_spec((N_SLABS, SUBLANES, SLAB)),
            _const_spec((N_SLABS, SUBLANES, SLAB)),
            _const_spec((1, D_SSM)),
            _const_spec((D_SSM, D_SSM)),
            _const_spec((1, D_SSM)),
            _const_spec((d, d)),
        ],
        out_specs=[
            pl.BlockSpec((1, tt, d), lambda i, j: (i, j, 0)),
            per_b3((HIST_ROWS, D_POOL)),
            per_b4((N_SLABS, 1, SLAB)),
            per_b4((N_SLABS, 1, SLAB)),
        ],
        out_shape=[
            jax.ShapeDtypeStruct((b, t, d), F32),
            jax.ShapeDtypeStruct((b, HIST_ROWS, D_POOL), F32),
            jax.ShapeDtypeStruct((b, N_SLABS, 1, SLAB), F32),
            jax.ShapeDtypeStruct((b, N_SLABS, 1, SLAB), F32),
        ],
        scratch_shapes=[
            pltpu.VMEM((HIST_ROWS, D_POOL), F32),
            pltpu.VMEM((N_SLABS, 1, SLAB), F32),
            pltpu.VMEM((N_SLABS, 1, SLAB), F32),
        ] + [pltpu.VMEM((tt, SLAB), F32)] * (2 * N_SLABS),
        compiler_params=pltpu.CompilerParams(
            dimension_semantics=("arbitrary", "arbitrary"), vmem_limit_bytes=VMEM_LIMIT),
    )(x, hist, h0re, h0im, prm['g_mix'], prm['w_in'], prm['w_pool'], prm['pool_scale'],
      prm['bmat'], prm['c_re'], prm['c_im'], prm['t_re'], prm['t_im'], prm['d_skip'],
      prm['w_glu'], prm['b_glu'], prm['w_out'])


def _memkv_kernel(m_ref, g_ref, wk_ref, wv_ref, k_ref, v_ref):
    m = _rms(m_ref[0], g_ref[...]).astype(BF16)
    k_ref[0] = _dot(m, wk_ref[...])
    v_ref[0] = _dot(m, wv_ref[...])


def _memkv(mem, g, wk, wv):
    b, n, d = mem.shape
    blk = pl.BlockSpec((1, n, d), lambda i: (i, 0, 0))
    return pl.pallas_call(
        _memkv_kernel,
        grid=(b,),
        in_specs=[blk, _const_spec((1, d)), _const_spec((d, d)), _const_spec((d, d))],
        out_specs=[blk, blk],
        out_shape=[jax.ShapeDtypeStruct((b, n, d), F32)] * 2,
        compiler_params=pltpu.CompilerParams(vmem_limit_bytes=VMEM_LIMIT),
    )(mem, g, wk, wv)


def _attn_kernel(h_ref, k_ref, v_ref, gx_ref, wq_ref, wo_ref, gf_ref, wr_ref, br_ref,
                 h2_ref, xn_ref, e_ref, gate_ref, cnt_ref, cnt_scr, kt_scr, v_scr, *, tt):
    @pl.when(pl.program_id(1) == 0)
    def _():
        kt_scr[...] = k_ref[0].T.astype(BF16)
        v_scr[...] = v_ref[0].astype(BF16)

    h = h_ref[0]
    hn = _rms(h, gx_ref[...])
    q = _dot(hn.astype(BF16), wq_ref[...])
    outs = []
    for hd in range(N_XHEADS):
        cols = slice(hd * XHEAD_DIM, (hd + 1) * XHEAD_DIM)
        s = _dot(q[:, cols].astype(BF16), kt_scr[cols, :]) * (XHEAD_DIM ** -0.5)
        p = jnp.exp(s - jnp.max(s, axis=-1, keepdims=True))
        p = p / jnp.sum(p, axis=-1, keepdims=True)
        outs.append(_dot(p.astype(BF16), v_scr[:, cols]))
    o = jnp.concatenate(outs, axis=1)
    h2 = h + _dot(o.astype(BF16), wo_ref[...])
    h2_ref[0] = h2

    xn = _rms(h2, gf_ref[...])
    for s in range(ROW_TILE):
        xn_ref[pl.ds(s, tt, stride=ROW_TILE), :] = xn[:, s * LANES:(s + 1) * LANES]
    logits = _dot(xn.astype(BF16), wr_ref[...]) + br_ref[...]
    lane = lax.broadcasted_iota(jnp.int32, logits.shape, 1)
    lane_f = lane.astype(F32)
    e_out = jnp.zeros(logits.shape, jnp.int32)
    hits = jnp.zeros(logits.shape, F32)
    top = []
    for k in range(TOP_K):
        m = jnp.max(logits, axis=-1, keepdims=True)
        idx = jnp.min(jnp.where(logits == m, lane_f, float(LANES)), axis=-1, keepdims=True)
        e_out = jnp.where(lane == k, idx.astype(jnp.int32), e_out)
        top.append(m)
        chosen = lane_f == idx
        hits = hits + chosen.astype(F32)
        logits = jnp.where(chosen, -jnp.inf, logits)

    @pl.when(jnp.logical_and(pl.program_id(0) == 0, pl.program_id(1) == 0))
    def _():
        cnt_scr[...] = jnp.zeros_like(cnt_scr)
    cnt_scr[...] = cnt_scr[...] + jnp.sum(hits, axis=0, keepdims=True)
    cnt_ref[...] = cnt_scr[...]
    ex = [jnp.exp(m - top[0]) for m in top]
    tot = ex[0] + ex[1] + ex[2] + ex[3]
    g_out = jnp.zeros(logits.shape, F32)
    for k in range(TOP_K):
        g_out = jnp.where(lane == k, ex[k] / tot, g_out)
    e_ref[...] = e_out
    gate_ref[...] = g_out


def _attn(h, k, v, prm, *, tt):
    b, t, d = h.shape
    n = b * t
    kern = functools.partial(_attn_kernel, tt=tt)
    nt = t // tt
    kv = pl.BlockSpec((1, N_MEM, d), lambda i, j: (i, 0, 0))
    tok = lambda width: pl.BlockSpec((tt, width), lambda i, j: (i * nt + j, 0))
    return pl.pallas_call(
        kern,
        grid=(b, nt),
        in_specs=[
            pl.BlockSpec((1, tt, d), lambda i, j: (i, j, 0)), kv, kv,
            _const_spec((1, d)), _const_spec((d, d)), _const_spec((d, d)),
            _const_spec((1, d)), _const_spec((d, LANES)), _const_spec((1, LANES)),
        ],
        out_specs=[
            pl.BlockSpec((1, tt, d), lambda i, j: (i, j, 0)),
            pl.BlockSpec((tt * ROW_TILE, LANES), lambda i, j: (i * nt + j, 0)),
            tok(LANES), tok(LANES), _const_spec((1, LANES)),
        ],
        out_shape=[
            jax.ShapeDtypeStruct((b, t, d), F32),
            jax.ShapeDtypeStruct((n * ROW_TILE, LANES), F32),
            jax.ShapeDtypeStruct((n, LANES), jnp.int32),
            jax.ShapeDtypeStruct((n, LANES), F32),
            jax.ShapeDtypeStruct((1, LANES), F32),
        ],
        scratch_shapes=[pltpu.VMEM((1, LANES), F32), pltpu.VMEM((d, N_MEM), BF16), pltpu.VMEM((N_MEM, d), BF16)],
        compiler_params=pltpu.CompilerParams(
            dimension_semantics=("arbitrary", "arbitrary"), vmem_limit_bytes=VMEM_LIMIT),
    )(h, k, v, prm['g_xattn'], prm['w_q'], prm['w_o'], prm['g_ffn'], prm['w_router'], prm['b_router'])


RANK_TILE = 512


def _rank_kernel(ep_ref, es_ref, start_ref, dest_ref, carry_scr, before_scr, *, steps_p):
    i = pl.program_id(0)

    @pl.when(i == 0)
    def _():
        carry_scr[...] = start_ref[...]
        r = lax.broadcasted_iota(jnp.int32, (RANK_TILE, RANK_TILE), 0)
        c = lax.broadcasted_iota(jnp.int32, (RANK_TILE, RANK_TILE), 1)
        before_scr[...] = (c < r).astype(BF16)

    e = jnp.where(i < steps_p, ep_ref[...], es_ref[...])
    lane = lax.broadcasted_iota(jnp.int32, e.shape, 1)
    onehot = [lane == e[:, k:k + 1] for k in range(TOP_K)]
    hits = jnp.zeros(e.shape, F32)
    for oh in onehot:
        hits = hits + oh.astype(F32)
    base = _dot(before_scr[...], hits.astype(BF16)) + carry_scr[...]
    rank = jnp.zeros(e.shape, jnp.int32)
    for k, oh in enumerate(onehot):
        rk = jnp.sum(jnp.where(oh, base, 0.0), axis=-1, keepdims=True).astype(jnp.int32)
        rank = jnp.where(lane == k, rk, rank)
    dest_ref[...] = rank
    carry_scr[...] = carry_scr[...] + jnp.sum(hits, axis=0, keepdims=True)


def _rank(e_p, e_s, start):
    steps_p = e_p.shape[0] // RANK_TILE
    blk = (RANK_TILE, LANES)
    return pl.pallas_call(
        functools.partial(_rank_kernel, steps_p=steps_p),
        grid=(steps_p + 1,),
        in_specs=[pl.BlockSpec(blk, lambda i: (jnp.minimum(i, steps_p - 1), 0)), _const_spec(blk),
                  _const_spec((1, LANES))],
        out_specs=pl.BlockSpec(blk, lambda i: (i, 0)),
        out_shape=jax.ShapeDtypeStruct(((steps_p + 1) * RANK_TILE, LANES), jnp.int32),
        scratch_shapes=[pltpu.VMEM((1, LANES), F32), pltpu.VMEM((RANK_TILE, RANK_TILE), BF16)],
        compiler_params=pltpu.CompilerParams(dimension_semantics=("arbitrary",)),
    )(e_p, e_s, start)


BLOCK_ROWS = MOE_BLOCK * ROW_TILE


def _row_copy_wait(src_ref, dst_ref, sem, rows):
    pltpu.make_async_copy(src_ref.at[pl.ds(0, rows * ROW_TILE)], dst_ref.at[pl.ds(0, rows * ROW_TILE)], sem).wait()


def _dispatch_kernel(last_blk_ref, nvalid_ref, dest_p_ref, dest_s_ref, xp_ref, xs_ref, xpad_ref,
                     zero_buf, sem, zero_sem, *, tile_p, tile_s, n_blocks):
    i = pl.program_id(0)
    last = pl.num_programs(0) - 1

    def zero_copy(blk):
        dst = xpad_ref.at[pl.ds(pl.multiple_of(blk * BLOCK_ROWS, BLOCK_ROWS), BLOCK_ROWS)]
        return pltpu.make_async_copy(zero_buf, dst, zero_sem)

    @pl.when(i == 0)
    def _():
        zero_buf[...] = jnp.zeros_like(zero_buf)

        def on_pad_blocks(fn):
            for e in range(N_EXPERTS):
                if e == 0:
                    fn(last_blk_ref[0])
                else:
                    pl.when(last_blk_ref[e] != last_blk_ref[e - 1])(functools.partial(fn, last_blk_ref[e]))
            lax.fori_loop(nvalid_ref[0], n_blocks, lambda blk, c: (fn(blk), c)[1], 0)

        on_pad_blocks(lambda blk: zero_copy(blk).start())
        on_pad_blocks(lambda blk: zero_copy(blk).wait())

    def push(dest_ref, x_ref, tile):
        def body(it, carry):
            for j in range(PUSH_UNROLL):
                t = it * PUSH_UNROLL + j
                src = x_ref.at[pl.ds(pl.multiple_of(t * ROW_TILE, ROW_TILE), ROW_TILE)]
                for k in range(TOP_K):
                    row = dest_ref[0, 0, t * TOP_K + k]
                    dst = xpad_ref.at[pl.ds(pl.multiple_of(row * ROW_TILE, ROW_TILE), ROW_TILE)]
                    pltpu.make_async_copy(src, dst, sem).start(priority=k % N_DMA_PRIORITIES)
            return carry

        lax.fori_loop(0, tile // PUSH_UNROLL, body, 0)
        for _ in range(TOP_K):
            _row_copy_wait(x_ref, xpad_ref, sem, tile)

    pl.when(i < last)(lambda: push(dest_p_ref, xp_ref, tile_p))
    pl.when(i == last)(lambda: push(dest_s_ref, xs_ref, tile_s))


def _dispatch(last_blk, nvalid, dest_p, dest_s, xn_p, xn_s, n_blocks, *, tile_p):
    n_p = xn_p.shape[0] // ROW_TILE
    tile_s = xn_s.shape[0] // ROW_TILE
    steps_p = n_p // tile_p
    kern = functools.partial(_dispatch_kernel, tile_p=tile_p, tile_s=tile_s, n_blocks=n_blocks)
    return pl.pallas_call(
        kern,
        grid_spec=pltpu.PrefetchScalarGridSpec(
            num_scalar_prefetch=2,
            grid=(steps_p + 1,),
            in_specs=[
                pl.BlockSpec((1, 1, tile_p * TOP_K), lambda i, lb, nv: (jnp.minimum(i, steps_p - 1), 0, 0),
                             memory_space=pltpu.SMEM),
                pl.BlockSpec((1, 1, tile_s * TOP_K), lambda i, lb, nv: (0, 0, 0), memory_space=pltpu.SMEM),
                pl.BlockSpec((tile_p * ROW_TILE, LANES), lambda i, lb, nv: (jnp.minimum(i, steps_p - 1), 0)),
                pl.BlockSpec((tile_s * ROW_TILE, LANES), lambda i, lb, nv: (0, 0)),
            ],
            out_specs=pl.BlockSpec(memory_space=pl.ANY),
            scratch_shapes=[pltpu.VMEM((BLOCK_ROWS, LANES), F32), pltpu.SemaphoreType.DMA, pltpu.SemaphoreType.DMA],
        ),
        out_shape=jax.ShapeDtypeStruct((n_blocks * BLOCK_ROWS, LANES), F32),
        compiler_params=pltpu.CompilerParams(dimension_semantics=("arbitrary",)),
    )(last_blk, nvalid, dest_p.reshape(steps_p, 1, tile_p * TOP_K), dest_s.reshape(1, 1, tile_s * TOP_K), xn_p, xn_s)


def _expert_kernel(blk_e_ref, nvalid_ref, next_e_ref, x_ref, bg_ref, bu_ref, bd_ref, wg_hbm, wu_hbm, wd_hbm, y_ref,
                   stage, w_bf, sems):
    b = pl.program_id(0)
    valid = b < nvalid_ref[0]
    e = blk_e_ref[b]
    weights = (wg_hbm, wu_hbm, wd_hbm)

    def fetch(expert):
        return [pltpu.make_async_copy(w.at[expert], stage.at[i], sems.at[i]) for i, w in enumerate(weights)]

    @pl.when(b == 0)
    def _():
        for cp in fetch(e):
            cp.start()

    @pl.when(jnp.logical_and(valid, jnp.logical_or(b == 0, blk_e_ref[jnp.maximum(b - 1, 0)] != e)))
    def _():
        for i, cp in enumerate(fetch(e)):
            cp.wait()
            w_bf[i] = stage[i].astype(BF16)

        @pl.when(next_e_ref[b] >= 0)
        def _():
            for cp in fetch(next_e_ref[b]):
                cp.start()

    @pl.when(valid)
    def _():
        x = jnp.concatenate([x_ref[pl.ds(s, MOE_BLOCK, stride=ROW_TILE), :] for s in range(ROW_TILE)], axis=1)
        x = x.astype(BF16)
        g = _dot(x, w_bf[0]) + bg_ref[0]
        u = _dot(x, w_bf[1]) + bu_ref[0]
        g = jnp.minimum(g, SWIGLU_LIMIT)
        u = jnp.clip(u, -SWIGLU_LIMIT, SWIGLU_LIMIT)
        hdn = g * (0.5 * (1.0 + jnp.tanh((0.5 * SWIGLU_ALPHA) * g))) * (u + 1.0)
        y = _dot(hdn.astype(BF16), w_bf[2]) + bd_ref[0]
        for s in range(ROW_TILE):
            y_ref[pl.ds(s, MOE_BLOCK, stride=ROW_TILE), :] = y[:, s * LANES:(s + 1) * LANES]

    @pl.when(b >= nvalid_ref[0])
    def _():
        y_ref[...] = jnp.zeros_like(y_ref)


def _experts(blk_e, nvalid, next_e, xpad, w_gate, b_gate, w_up, b_up, w_down, b_down, n_blocks):
    d = D_MODEL
    n_w = 3
    bspec = pl.BlockSpec((1, 1, d), lambda b, be, nv, ne: (be[b], 0, 0))
    xspec = pl.BlockSpec((BLOCK_ROWS, LANES), lambda b, be, nv, ne: (jnp.minimum(b, nv[0] - 1), 0))
    hbm = pl.BlockSpec(memory_space=pl.ANY)
    return pl.pallas_call(
        _expert_kernel,
        grid_spec=pltpu.PrefetchScalarGridSpec(
            num_scalar_prefetch=3,
            grid=(n_blocks,),
            in_specs=[xspec, bspec, bspec, bspec, hbm, hbm, hbm],
            out_specs=pl.BlockSpec((BLOCK_ROWS, LANES), lambda b, be, nv, ne: (b, 0)),
            scratch_shapes=[pltpu.VMEM((n_w, d, d), F32), pltpu.VMEM((n_w, d, d), BF16),
                            pltpu.SemaphoreType.DMA((n_w,))],
        ),
        out_shape=jax.ShapeDtypeStruct((n_blocks * BLOCK_ROWS, LANES), F32),
        compiler_params=pltpu.CompilerParams(dimension_semantics=("arbitrary",), vmem_limit_bytes=VMEM_LIMIT),
    )(blk_e, nvalid, next_e, xpad, b_gate.reshape(N_EXPERTS, 1, d), b_up.reshape(N_EXPERTS, 1, d),
      b_down.reshape(N_EXPERTS, 1, d), w_gate, w_up, w_down)


COMBINE_SLOTS = 3


def _combine_kernel(dest_ref, dest_1_ref, dest_2_ref, h_ref, gate_ref, gfin_ref, ypad_ref, y_ref, *bufs_sems,
                    tile, n_steps):
    bufs, sems = bufs_sems[:COMBINE_SLOTS], bufs_sems[COMBINE_SLOTS]
    i = pl.program_id(0)

    def gather(dref, buf, sem, t, k):
        row = dref[0, 0, t * TOP_K + k]
        slot = (k * tile + t) * ROW_TILE
        if not isinstance(slot, int):
            slot = pl.multiple_of(slot, ROW_TILE)
        src = ypad_ref.at[pl.ds(pl.multiple_of(row * ROW_TILE, ROW_TILE), ROW_TILE)]
        pltpu.make_async_copy(src, buf.at[pl.ds(slot, ROW_TILE)], sem).start(priority=k % N_DMA_PRIORITIES)

    def drain(buf, sem):
        for _ in range(TOP_K):
            _row_copy_wait(ypad_ref, buf, sem, tile)

    @pl.when(i == 0)
    def _():
        def body(t, carry):
            for k in range(TOP_K):
                gather(dest_ref, bufs[0], sems.at[0], t, k)
                gather(dest_1_ref, bufs[1], sems.at[1], t, k)
            return carry
        lax.fori_loop(0, tile, body, 0)

    def step(cur, cur_sem, nxt, nxt_sem):
        drain(cur, cur_sem)
        gates = gate_ref[...]
        h = h_ref[...]
        per_slab = tile // ROW_TILE
        cols = []
        for s in range(ROW_TILE):
            for t in range(s * per_slab, (s + 1) * per_slab):
                for k in range(TOP_K):
                    gather(dest_2_ref, nxt, nxt_sem, t, k)
            acc = h[:, s * LANES:(s + 1) * LANES]
            for k in range(TOP_K):
                acc = acc + gates[:, k:k + 1] * cur[pl.ds(k * tile * ROW_TILE + s, tile, stride=ROW_TILE), :]
            cols.append(acc)
        y_ref[...] = _rms(jnp.concatenate(cols, axis=1), gfin_ref[...])

    for r in range(COMBINE_SLOTS):
        ahead = (r + COMBINE_SLOTS - 1) % COMBINE_SLOTS
        pl.when(i % COMBINE_SLOTS == r)(functools.partial(step, bufs[r], sems.at[r], bufs[ahead], sems.at[ahead]))

    @pl.when(i == n_steps - 1)
    def _():
        for extra in (n_steps, n_steps + 1):
            drain(bufs[extra % COMBINE_SLOTS], sems.at[extra % COMBINE_SLOTS])


def _combine(dest, h2, gates, g_final, ypad, *, tile):
    n, d = h2.shape
    steps = n // tile
    kern = functools.partial(_combine_kernel, tile=tile, n_steps=steps)
    dest3 = dest.reshape(steps, 1, tile * TOP_K)
    buf = pltpu.VMEM((TOP_K * tile * ROW_TILE, LANES), F32)
    tile_ahead = lambda k: pl.BlockSpec((1, 1, tile * TOP_K), lambda i: (jnp.minimum(i + k, steps - 1), 0, 0),
                                        memory_space=pltpu.SMEM)
    return pl.pallas_call(
        kern,
        grid=(steps,),
        in_specs=[
            tile_ahead(0), tile_ahead(1), tile_ahead(2),
            pl.BlockSpec((tile, d), lambda i: (i, 0)),
            pl.BlockSpec((tile, LANES), lambda i: (i, 0)),
            _const_spec((1, d)),
            pl.BlockSpec(memory_space=pl.ANY),
        ],
        out_specs=pl.BlockSpec((tile, d), lambda i: (i, 0)),
        out_shape=jax.ShapeDtypeStruct((n, d), F32),
        scratch_shapes=[buf] * COMBINE_SLOTS + [pltpu.SemaphoreType.DMA((COMBINE_SLOTS,))],
        compiler_params=pltpu.CompilerParams(dimension_semantics=("arbitrary",), vmem_limit_bytes=VMEM_LIMIT),
    )(dest3, dest3, dest3, h2, gates, g_final, ypad)


def _ssm_params(a_re, a_im, log_dt, b_re, b_im, c_re, c_im):
    dt = jnp.exp(log_dt)[:, None]
    mag = jnp.exp(a_re * dt)
    lb_re = mag * jnp.cos(a_im * dt)
    lb_im = mag * jnp.sin(a_im * dt)
    den = a_re * a_re + a_im * a_im
    q_re = ((lb_re - 1.0) * a_re + lb_im * a_im) / den
    q_im = (lb_im * a_re - (lb_re - 1.0) * a_im) / den
    bb_re = q_re[:, :, None] * b_re - q_im[:, :, None] * b_im
    bb_im = q_re[:, :, None] * b_im + q_im[:, :, None] * b_re
    eye = jnp.eye(SSM_PACK, dtype=F32)

    def pack_blockdiag(m):
        m = m.reshape(N_PACKS, SSM_PACK, m.shape[1], m.shape[2])
        return jnp.einsum('ngab,gh->ngahb', m, eye).reshape(N_PACKS, SSM_PACK * m.shape[2], SSM_PACK * m.shape[3])

    pw_re, pw_im = [jnp.ones_like(lb_re), lb_re], [jnp.zeros_like(lb_im), lb_im]
    for _ in range(RUN - 1):
        r, i = pw_re[-1], pw_im[-1]
        pw_re.append(r * lb_re - i * lb_im)
        pw_im.append(r * lb_im + i * lb_re)
    b_rows = []
    for dd in range(RUN):
        s_re = bb_re * pw_re[dd][:, :, None] - bb_im * pw_im[dd][:, :, None]
        s_im = bb_re * pw_im[dd][:, :, None] + bb_im * pw_re[dd][:, :, None]
        b_rows.append(jnp.concatenate([pack_blockdiag(s_re.transpose(0, 2, 1)),
                                       pack_blockdiag(s_im.transpose(0, 2, 1))], axis=2))
    bmat = jnp.concatenate(b_rows, axis=1)
    cmat_re = pack_blockdiag(c_re.transpose(0, 2, 1))
    cmat_im = pack_blockdiag(-c_im.transpose(0, 2, 1))
    rows_of = lambda pw: [pw[(r % RUN) + 1].reshape(-1) for r in range(SUBLANES)]
    slabbed = lambda rows: jnp.stack(rows).reshape(SUBLANES, N_SLABS, SLAB).transpose(1, 0, 2)
    return (bmat.astype(BF16), cmat_re.astype(BF16), cmat_im.astype(BF16),
            slabbed(rows_of(pw_re)), slabbed(rows_of(pw_im)))


def kernel(x_prompt, x_sample, mem_prompt, cache_mem_k, cache_mem_v, cache_pool, state_ssm_re, state_ssm_im, norm_mix, w_in, w_pool, pool_scale, ssm_a_re, ssm_a_im, ssm_log_dt, ssm_b_re, ssm_b_im, ssm_c_re, ssm_c_im, ssm_d, w_glu, b_glu, w_out, norm_xattn, norm_mem, w_q, w_k, w_v, w_o, norm_ffn, w_router, b_router, w_gate, b_gate, w_up, b_up, w_down, b_down, norm_final):
    assert x_prompt.shape[2] == D_MODEL and norm_mix.shape[0] == 1
    bp, tp, d = x_prompt.shape
    bs, ts, _ = x_sample.shape
    n_p, n_s = bp * tp, bs * ts
    row = lambda v: v.reshape(1, -1)

    bmat, c_re, c_im, t_re, t_im = _ssm_params(ssm_a_re[0], ssm_a_im[0], ssm_log_dt[0], ssm_b_re[0], ssm_b_im[0],
                                               ssm_c_re[0], ssm_c_im[0])
    mix_prm = dict(g_mix=row(norm_mix[0]), w_in=w_in[0].astype(BF16), w_pool=w_pool[0].astype(BF16),
                   pool_scale=row(pool_scale[0]), bmat=bmat, c_re=c_re, c_im=c_im, t_re=t_re, t_im=t_im,
                   d_skip=row(ssm_d[0]), w_glu=w_glu[0].astype(BF16), b_glu=row(b_glu[0]),
                   w_out=w_out[0].astype(BF16))

    attn_prm = dict(g_xattn=row(norm_xattn[0]), w_q=w_q[0].astype(BF16), w_o=w_o[0].astype(BF16),
                    g_ffn=row(norm_ffn[0]),
                    w_router=jnp.pad(w_router[0], ((0, 0), (0, LANES - N_EXPERTS))).astype(BF16),
                    b_router=jnp.pad(row(b_router[0]), ((0, 0), (0, LANES - N_EXPERTS)), constant_values=-jnp.inf))
    slab_state = lambda s: s.reshape(s.shape[0], N_SLABS, 1, SLAB)

    zeros_state = jnp.zeros((bp, N_SLABS, 1, SLAB), F32)
    h1_p, pool_p, sre_p, sim_p = _mixer(x_prompt, jnp.zeros((bp, HIST_ROWS, D_POOL), F32), zeros_state, zeros_state,
                                        mix_prm, tt=MIXER_TILE, pos0=0)
    mk, mv = _memkv(mem_prompt, row(norm_mem[0]), w_k[0].astype(BF16), w_v[0].astype(BF16))
    h2_p, xn_p, e_p, g_p, cnt_p = _attn(h1_p, mk, mv, attn_prm, tt=ATTN_TILE)

    hist_s = jnp.pad(cache_pool[0], ((0, 0), (HIST_ROWS - POOL_BUF, 0), (0, 0)))
    h1_s, pool_s, sre_s, sim_s = _mixer(x_sample, hist_s, slab_state(state_ssm_re[0]), slab_state(state_ssm_im[0]),
                                        mix_prm, tt=ts, pos0=PAST_LEN)
    ck = cache_mem_k[0].reshape(bs, N_MEM, d)
    cv = cache_mem_v[0].reshape(bs, N_MEM, d)
    h2_s, xn_s, e_s, g_s, cnt_s = _attn(h1_s, ck, cv, attn_prm, tt=ts)

    n_all = n_p + n_s
    counts = (cnt_p + cnt_s)[0, :N_EXPERTS].astype(jnp.int32)
    padded = (counts + MOE_BLOCK - 1) // MOE_BLOCK * MOE_BLOCK
    pend = jnp.cumsum(padded)
    pstart = pend - padded
    n_blocks = -(-(n_all * TOP_K + N_EXPERTS * (MOE_BLOCK - 1)) // MOE_BLOCK)
    blk_e = jnp.minimum(jnp.sum(pend[None, :] <= (jnp.arange(n_blocks) * MOE_BLOCK)[:, None], axis=1),
                        N_EXPERTS - 1).astype(jnp.int32)
    nvalid = (pend[-1:] // MOE_BLOCK).astype(jnp.int32)
    last_blk = jnp.maximum(pend // MOE_BLOCK - 1, 0).astype(jnp.int32)
    after = (pend // MOE_BLOCK)[blk_e]
    next_e = jnp.where(after < nvalid[0], blk_e[jnp.minimum(after, n_blocks - 1)], -1).astype(jnp.int32)
    start = jnp.pad(pstart.astype(F32), (0, LANES - N_EXPERTS)).reshape(1, LANES)
    e_s_tile = jnp.pad(e_s, ((0, RANK_TILE - n_s), (0, 0)), constant_values=-1)
    dest = _rank(e_p, e_s_tile, start)[:n_all, :TOP_K].reshape(-1)
    dest_p, dest_s = dest[:n_p * TOP_K], dest[n_p * TOP_K:]

    xpad = _dispatch(last_blk, nvalid, dest_p, dest_s, xn_p, xn_s, n_blocks, tile_p=DISPATCH_TILE)
    ypad = _experts(blk_e, nvalid, next_e, xpad, w_gate[0], b_gate[0], w_up[0], b_up[0], w_down[0], b_down[0], n_blocks)
    g_fin = row(norm_final)
    y_p = _combine(dest_p, h2_p.reshape(n_p, d), g_p, g_fin, ypad, tile=ROUTE_TILE)
    y_s = _combine(dest_s, h2_s.reshape(n_s, d), g_s, g_fin, ypad, tile=n_s)

    unslab = lambda s: s.reshape(1, s.shape[0], N_SSM_GROUPS, SSM_STATE)
    kv5 = lambda a: a.reshape(1, bp, N_MEM, N_XHEADS, XHEAD_DIM)
    return (y_p.reshape(bp, tp, d), y_s.reshape(bs, ts, d), kv5(mk), kv5(mv),
            pool_p[None, :, HIST_ROWS - POOL_BUF:], pool_s[None, :, HIST_ROWS - POOL_BUF:],
            unslab(sre_p), unslab(sim_p), unslab(sre_s), unslab(sim_s))
```

```python
import functools
import math

import jax
import jax.numpy as jnp
from jax import lax
from jax.experimental import pallas as pl
from jax.experimental.pallas import tpu as pltpu

D_MODEL = 1024
D_POOL = 512
D_SSM = 512
POOL_WINDOWS = (2, 4, 8, 16)
POOL_GROUP = 128
HIST_ROWS = 16
POOL_BUF = 15
SSM_GROUP = 16
N_SSM_GROUPS = 32
SSM_STATE = 64
N_STATE = N_SSM_GROUPS * SSM_STATE
N_MEM = 256
N_XHEADS = 4
XHEAD_DIM = 256
N_EXPERTS = 32
TOP_K = 4
SWIGLU_LIMIT = 7.0
SWIGLU_ALPHA = 1.702
MOE_BLOCK = 512
EPS = 1e-6
PAST_LEN = 1024

LANES = 128
SUBLANES = 8
SLAB = 256
N_SLABS = N_STATE // SLAB
SSM_PACK = LANES // SSM_GROUP
N_PACKS = N_SSM_GROUPS // SSM_PACK
PACK_STATES = SSM_PACK * SSM_STATE
SLABS_PER_PACK = PACK_STATES // SLAB
RUN = 4
ROW_TILE = D_MODEL // LANES
N_DMA_PRIORITIES = 2
PUSH_UNROLL = 4
VMEM_LIMIT = 56 * 1024 * 1024

MIXER_TILE = 512
ATTN_TILE = 1024
ROUTE_TILE = 256
DISPATCH_TILE = 1024

BF16 = jnp.bfloat16
F32 = jnp.float32


def _rms(x, g):
    return x * lax.rsqrt(jnp.mean(x * x, axis=-1, keepdims=True) + EPS) * g


def _dot(a, b):
    return jnp.dot(a, b, preferred_element_type=F32)


def _chain_slab(s, tre_ref, tim_ref, hre_scr, him_scr, re_scr, im_scr, tt):
    t_re = tre_ref[s]
    t_im = tim_ref[s]
    upper = lax.broadcasted_iota(jnp.int32, (SUBLANES, SLAB), 0) < RUN
    ta_re, ta_im = jnp.where(upper, t_re, 0.0), jnp.where(upper, t_im, 0.0)
    tb_re, tb_im = jnp.where(upper, 0.0, t_re), jnp.where(upper, 0.0, t_im)
    c_re = hre_scr[s]
    c_im = him_scr[s]
    for v in range(tt // SUBLANES):
        rows = pl.ds(v * SUBLANES, SUBLANES)
        re = re_scr[rows, :]
        im = im_scr[rows, :]
        re, im = re + (ta_re * c_re - ta_im * c_im), im + (ta_re * c_im + ta_im * c_re)
        m_re = re[RUN - 1:RUN, :]
        m_im = im[RUN - 1:RUN, :]
        re, im = re + (tb_re * m_re - tb_im * m_im), im + (tb_re * m_im + tb_im * m_re)
        re_scr[rows, :] = re
        im_scr[rows, :] = im
        c_re = re[SUBLANES - 1:SUBLANES, :]
        c_im = im[SUBLANES - 1:SUBLANES, :]
    hre_scr[s] = c_re
    him_scr[s] = c_im


def _mixer_kernel(x_ref, hist_ref, h0re_ref, h0im_ref, gmix_ref, win_ref, wpool_ref, pscale_ref,
                  bmat_ref, cre_ref, cim_ref, tre_ref, tim_ref, dskip_ref, wglu_ref, bglu_ref, wout_ref,
                  h_ref, poolnew_ref, ssmre_ref, ssmim_ref,
                  hist_scr, hre_scr, him_scr, *slab_scr, tt, pos0):
    bure_scr, buim_scr = slab_scr[:N_SLABS], slab_scr[N_SLABS:]
    j = pl.program_id(1)

    @pl.when(j == 0)
    def _():
        hist_scr[...] = hist_ref[0]
        hre_scr[...] = h0re_ref[0]
        him_scr[...] = h0im_ref[0]

    x = x_ref[0]
    xn = _rms(x, gmix_ref[...])
    z = _dot(xn.astype(BF16), win_ref[...])
    zp = z[:, :D_POOL]
    u = z[:, D_POOL:]

    ext = jnp.concatenate([hist_scr[...], zp], axis=0)
    pos = pos0 + j * tt + lax.broadcasted_iota(jnp.int32, (tt, 1), 0)
    acc = ext
    outs = []
    for gi, w in enumerate(POOL_WINDOWS):
        lo = gi * POOL_GROUP
        acc = acc[:, POOL_GROUP * (1 if gi else 0):]
        acc = acc + pltpu.roll(acc, w // 2, 0)
        wsum = acc[HIST_ROWS:, :POOL_GROUP]
        cnt = jnp.minimum(pos + 1, w).astype(F32)
        d = wsum / cnt - zp[:, lo:lo + POOL_GROUP]
        outs.append(_dot(d.astype(BF16), wpool_ref[gi]))
    y_pool = jnp.concatenate(outs, axis=1) * pscale_ref[...]
    hist_scr[...] = ext[tt:tt + HIST_ROWS]
    poolnew_ref[0] = ext[tt:tt + HIST_ROWS]

    run_row = lax.broadcasted_iota(jnp.int32, (tt, 1), 0) % RUN
    shifted = [u.astype(BF16)]
    for dd in range(1, RUN):
        shifted.append(jnp.where(run_row >= dd, pltpu.roll(u, dd, 0), 0.0).astype(BF16))
    def project(c):
        lhs = jnp.concatenate([sh[:, c * LANES:(c + 1) * LANES] for sh in shifted], axis=1)
        bu = _dot(lhs, bmat_ref[c])
        for i in range(SLABS_PER_PACK):
            s = c * SLABS_PER_PACK + i
            bure_scr[s][...] = bu[:, i * SLAB:(i + 1) * SLAB]
            buim_scr[s][...] = bu[:, PACK_STATES + i * SLAB:PACK_STATES + (i + 1) * SLAB]

    ys = []
    project(0)
    for c in range(N_PACKS):
        slabs = range(c * SLABS_PER_PACK, (c + 1) * SLABS_PER_PACK)
        if c + 1 < N_PACKS:
            project(c + 1)
        for s in slabs:
            _chain_slab(s, tre_ref, tim_ref, hre_scr, him_scr, bure_scr[s], buim_scr[s], tt)
        hs_re = jnp.concatenate([bure_scr[s][...] for s in slabs], axis=1)
        hs_im = jnp.concatenate([buim_scr[s][...] for s in slabs], axis=1)
        ys.append(_dot(hs_re.astype(BF16), cre_ref[c]) + _dot(hs_im.astype(BF16), cim_ref[c]))
    ssmre_ref[0] = hre_scr[...]
    ssmim_ref[0] = him_scr[...]
    y = jnp.concatenate(ys, axis=1) + dskip_ref[...] * u
    g = 0.5 * y * (1.0 + jnp.tanh(math.sqrt(2.0 / math.pi) * (y + 0.044715 * (y * y * y))))
    y_ssm = g * jax.nn.sigmoid(_dot(g.astype(BF16), wglu_ref[...]) + bglu_ref[...])

    mix = jnp.concatenate([y_pool, y_ssm], axis=1)
    h_ref[0] = x + _dot(mix.astype(BF16), wout_ref[...])


def _const_spec(shape):
    return pl.BlockSpec(shape, lambda *_: (0,) * len(shape))


def _mixer(x, hist, h0re, h0im, prm, *, tt, pos0):
    b, t, d = x.shape
    kern = functools.partial(_mixer_kernel, tt=tt, pos0=pos0)
    per_b3 = lambda shp: pl.BlockSpec((1,) + shp, lambda i, j: (i, 0, 0))
    per_b4 = lambda shp: pl.BlockSpec((1,) + shp, lambda i, j: (i, 0, 0, 0))
    return pl.pallas_call(
        kern,
        grid=(b, t // tt),
        in_specs=[
            pl.BlockSpec((1, tt, d), lambda i, j: (i, j, 0)),
            per_b3((HIST_ROWS, D_POOL)),
            per_b4((N_SLABS, 1, SLAB)),
            per_b4((N_SLABS, 1, SLAB)),
            _const_spec((1, d)),
            _const_spec((d, d)),
            _const_spec((len(POOL_WINDOWS), POOL_GROUP, POOL_GROUP)),
            _const_spec((1, D_POOL)),
            _const_spec((N_PACKS, RUN * LANES, 2 * PACK_STATES)),
            _const_spec((N_PACKS, PACK_STATES, LANES)),
            _const_spec((N_PACKS, PACK_STATES, LANES)),
            _const_spec((N_SLABS, SUBLANES, SLAB)),
            _const_spec((N_SLABS, SUBLANES, SLAB)),
            _const_spec((1, D_SSM)),
            _const_spec((D_SSM, D_SSM)),
            _const_spec((1, D_SSM)),
            _const_spec((d, d)),
        ],
        out_specs=[
            pl.BlockSpec((1, tt, d), lambda i, j: (i, j, 0)),
            per_b3((HIST_ROWS, D_POOL)),
            per_b4((N_SLABS, 1, SLAB)),
            per_b4((N_SLABS, 1, SLAB)),
        ],
        out_shape=[
            jax.ShapeDtypeStruct((b, t, d), F32),
            jax.ShapeDtypeStruct((b, HIST_ROWS, D_POOL), F32),
            jax.ShapeDtypeStruct((b, N_SLABS, 1, SLAB), F32),
            jax.ShapeDtypeStruct((b, N_SLABS, 1, SLAB), F32),
        ],
        scratch_shapes=[
            pltpu.VMEM((HIST_ROWS, D_POOL), F32),
            pltpu.VMEM((N_SLABS, 1, SLAB), F32),
            pltpu.VMEM((N_SLABS, 1, SLAB), F32),
        ] + [pltpu.VMEM((tt, SLAB), F32)] * (2 * N_SLABS),
        compiler_params=pltpu.CompilerParams(
            dimension_semantics=("arbitrary", "arbitrary"), vmem_limit_bytes=VMEM_LIMIT),
    )(x, hist, h0re, h0im, prm['g_mix'], prm['w_in'], prm['w_pool'], prm['pool_scale'],
      prm['bmat'], prm['c_re'], prm['c_im'], prm['t_re'], prm['t_im'], prm['d_skip'],
      prm['w_glu'], prm['b_glu'], prm['w_out'])


def _memkv_kernel(m_ref, g_ref, wk_ref, wv_ref, k_ref, v_ref):
    m = _rms(m_ref[0], g_ref[...]).astype(BF16)
    k_ref[0] = _dot(m, wk_ref[...])
    v_ref[0] = _dot(m, wv_ref[...])


def _memkv(mem, g, wk, wv):
    b, n, d = mem.shape
    blk = pl.BlockSpec((1, n, d), lambda i: (i, 0, 0))
    return pl.pallas_call(
        _memkv_kernel,
        grid=(b,),
        in_specs=[blk, _const_spec((1, d)), _const_spec((d, d)), _const_spec((d, d))],
        out_specs=[blk, blk],
        out_shape=[jax.ShapeDtypeStruct((b, n, d), F32)] * 2,
        compiler_params=pltpu.CompilerParams(vmem_limit_bytes=VMEM_LIMIT),
    )(mem, g, wk, wv)


def _attn_kernel(h_ref, k_ref, v_ref, gx_ref, wq_ref, wo_ref, gf_ref, wr_ref, br_ref,
                 h2_ref, xn_ref, e_ref, gate_ref, cnt_ref, cnt_scr, kt_scr, v_scr, *, tt):
    @pl.when(pl.program_id(1) == 0)
    def _():
        kt_scr[...] = k_ref[0].T.astype(BF16)
        v_scr[...] = v_ref[0].astype(BF16)

    h = h_ref[0]
    hn = _rms(h, gx_ref[...])
    q = _dot(hn.astype(BF16), wq_ref[...])
    outs = []
    for hd in range(N_XHEADS):
        cols = slice(hd * XHEAD_DIM, (hd + 1) * XHEAD_DIM)
        s = _dot(q[:, cols].astype(BF16), kt_scr[cols, :]) * (XHEAD_DIM ** -0.5)
        p = jnp.exp(s - jnp.max(s, axis=-1, keepdims=True))
        p = p / jnp.sum(p, axis=-1, keepdims=True)
        outs.append(_dot(p.astype(BF16), v_scr[:, cols]))
    o = jnp.concatenate(outs, axis=1)
    h2 = h + _dot(o.astype(BF16), wo_ref[...])
    h2_ref[0] = h2

    xn = _rms(h2, gf_ref[...])
    for s in range(ROW_TILE):
        xn_ref[pl.ds(s, tt, stride=ROW_TILE), :] = xn[:, s * LANES:(s + 1) * LANES]
    logits = _dot(xn.astype(BF16), wr_ref[...]) + br_ref[...]
    lane = lax.broadcasted_iota(jnp.int32, logits.shape, 1)
    lane_f = lane.astype(F32)
    e_out = jnp.zeros(logits.shape, jnp.int32)
    hits = jnp.zeros(logits.shape, F32)
    top = []
    for k in range(TOP_K):
        m = jnp.max(logits, axis=-1, keepdims=True)
        idx = jnp.min(jnp.where(logits == m, lane_f, float(LANES)), axis=-1, keepdims=True)
        e_out = jnp.where(lane == k, idx.astype(jnp.int32), e_out)
        top.append(m)
        chosen = lane_f == idx
        hits = hits + chosen.astype(F32)
        logits = jnp.where(chosen, -jnp.inf, logits)

    @pl.when(jnp.logical_and(pl.program_id(0) == 0, pl.program_id(1) == 0))
    def _():
        cnt_scr[...] = jnp.zeros_like(cnt_scr)
    cnt_scr[...] = cnt_scr[...] + jnp.sum(hits, axis=0, keepdims=True)
    cnt_ref[...] = cnt_scr[...]
    ex = [jnp.exp(m - top[0]) for m in top]
    tot = ex[0] + ex[1] + ex[2] + ex[3]
    g_out = jnp.zeros(logits.shape, F32)
    for k in range(TOP_K):
        g_out = jnp.where(lane == k, ex[k] / tot, g_out)
    e_ref[...] = e_out
    gate_ref[...] = g_out


def _attn(h, k, v, prm, *, tt):
    b, t, d = h.shape
    n = b * t
    kern = functools.partial(_attn_kernel, tt=tt)
    nt = t // tt
    kv = pl.BlockSpec((1, N_MEM, d), lambda i, j: (i, 0, 0))
    tok = lambda width: pl.BlockSpec((tt, width), lambda i, j: (i * nt + j, 0))
    return pl.pallas_call(
        kern,
        grid=(b, nt),
        in_specs=[
            pl.BlockSpec((1, tt, d), lambda i, j: (i, j, 0)), kv, kv,
            _const_spec((1, d)), _const_spec((d, d)), _const_spec((d, d)),
            _const_spec((1, d)), _const_spec((d, LANES)), _const_spec((1, LANES)),
        ],
        out_specs=[
            pl.BlockSpec((1, tt, d), lambda i, j: (i, j, 0)),
            pl.BlockSpec((tt * ROW_TILE, LANES), lambda i, j: (i * nt + j, 0)),
            tok(LANES), tok(LANES), _const_spec((1, LANES)),
        ],
        out_shape=[
            jax.ShapeDtypeStruct((b, t, d), F32),
            jax.ShapeDtypeStruct((n * ROW_TILE, LANES), F32),
            jax.ShapeDtypeStruct((n, LANES), jnp.int32),
            jax.ShapeDtypeStruct((n, LANES), F32),
            jax.ShapeDtypeStruct((1, LANES), F32),
        ],
        scratch_shapes=[pltpu.VMEM((1, LANES), F32), pltpu.VMEM((d, N_MEM), BF16), pltpu.VMEM((N_MEM, d), BF16)],
        compiler_params=pltpu.CompilerParams(
            dimension_semantics=("arbitrary", "arbitrary"), vmem_limit_bytes=VMEM_LIMIT),
    )(h, k, v, prm['g_xattn'], prm['w_q'], prm['w_o'], prm['g_ffn'], prm['w_router'], prm['b_router'])


RANK_TILE = 512


def _rank_kernel(ep_ref, es_ref, start_ref, dest_ref, carry_scr, before_scr, *, steps_p):
    i = pl.program_id(0)

    @pl.when(i == 0)
    def _():
        carry_scr[...] = start_ref[...]
        r = lax.broadcasted_iota(jnp.int32, (RANK_TILE, RANK_TILE), 0)
        c = lax.broadcasted_iota(jnp.int32, (RANK_TILE, RANK_TILE), 1)
        before_scr[...] = (c < r).astype(BF16)

    e = jnp.where(i < steps_p, ep_ref[...], es_ref[...])
    lane = lax.broadcasted_iota(jnp.int32, e.shape, 1)
    onehot = [lane == e[:, k:k + 1] for k in range(TOP_K)]
    hits = jnp.zeros(e.shape, F32)
    for oh in onehot:
        hits = hits + oh.astype(F32)
    base = _dot(before_scr[...], hits.astype(BF16)) + carry_scr[...]
    rank = jnp.zeros(e.shape, jnp.int32)
    for k, oh in enumerate(onehot):
        rk = jnp.sum(jnp.where(oh, base, 0.0), axis=-1, keepdims=True).astype(jnp.int32)
        rank = jnp.where(lane == k, rk, rank)
    dest_ref[...] = rank
    carry_scr[...] = carry_scr[...] + jnp.sum(hits, axis=0, keepdims=True)


def _rank(e_p, e_s, start):
    steps_p = e_p.shape[0] // RANK_TILE
    blk = (RANK_TILE, LANES)
    return pl.pallas_call(
        functools.partial(_rank_kernel, steps_p=steps_p),
        grid=(steps_p + 1,),
        in_specs=[pl.BlockSpec(blk, lambda i: (jnp.minimum(i, steps_p - 1), 0)), _const_spec(blk),
                  _const_spec((1, LANES))],
        out_specs=pl.BlockSpec(blk, lambda i: (i, 0)),
        out_shape=jax.ShapeDtypeStruct(((steps_p + 1) * RANK_TILE, LANES), jnp.int32),
        scratch_shapes=[pltpu.VMEM((1, LANES), F32), pltpu.VMEM((RANK_TILE, RANK_TILE), BF16)],
        compiler_params=pltpu.CompilerParams(dimension_semantics=("arbitrary",)),
    )(e_p, e_s, start)


BLOCK_ROWS = MOE_BLOCK * ROW_TILE


def _row_copy_wait(src_ref, dst_ref, sem, rows):
    pltpu.make_async_copy(src_ref.at[pl.ds(0, rows * ROW_TILE)], dst_ref.at[pl.ds(0, rows * ROW_TILE)], sem).wait()


def _dispatch_kernel(last_blk_ref, nvalid_ref, dest_p_ref, dest_s_ref, xp_ref, xs_ref, xpad_ref,
                     zero_buf, sem, zero_sem, *, tile_p, tile_s, n_blocks):
    i = pl.program_id(0)
    last = pl.num_programs(0) - 1

    def zero_copy(blk):
        dst = xpad_ref.at[pl.ds(pl.multiple_of(blk * BLOCK_ROWS, BLOCK_ROWS), BLOCK_ROWS)]
        return pltpu.make_async_copy(zero_buf, dst, zero_sem)

    @pl.when(i == 0)
    def _():
        zero_buf[...] = jnp.zeros_like(zero_buf)

        def on_pad_blocks(fn):
            for e in range(N_EXPERTS):
                if e == 0:
                    fn(last_blk_ref[0])
                else:
                    pl.when(last_blk_ref[e] != last_blk_ref[e - 1])(functools.partial(fn, last_blk_ref[e]))
            lax.fori_loop(nvalid_ref[0], n_blocks, lambda blk, c: (fn(blk), c)[1], 0)

        on_pad_blocks(lambda blk: zero_copy(blk).start())
        on_pad_blocks(lambda blk: zero_copy(blk).wait())

    def push(dest_ref, x_ref, tile):
        def body(it, carry):
            for j in range(PUSH_UNROLL):
                t = it * PUSH_UNROLL + j
                src = x_ref.at[pl.ds(pl.multiple_of(t * ROW_TILE, ROW_TILE), ROW_TILE)]
                for k in range(TOP_K):
                    row = dest_ref[0, 0, t * TOP_K + k]
                    dst = xpad_ref.at[pl.ds(pl.multiple_of(row * ROW_TILE, ROW_TILE), ROW_TILE)]
                    pltpu.make_async_copy(src, dst, sem).start(priority=k % N_DMA_PRIORITIES)
            return carry

        lax.fori_loop(0, tile // PUSH_UNROLL, body, 0)
        for _ in range(TOP_K):
            _row_copy_wait(x_ref, xpad_ref, sem, tile)

    pl.when(i < last)(lambda: push(dest_p_ref, xp_ref, tile_p))
    pl.when(i == last)(lambda: push(dest_s_ref, xs_ref, tile_s))


def _dispatch(last_blk, nvalid, dest_p, dest_s, xn_p, xn_s, n_blocks, *, tile_p):
    n_p = xn_p.shape[0] // ROW_TILE
    tile_s = xn_s.shape[0] // ROW_TILE
    steps_p = n_p // tile_p
    kern = functools.partial(_dispatch_kernel, tile_p=tile_p, tile_s=tile_s, n_blocks=n_blocks)
    return pl.pallas_call(
        kern,
        grid_spec=pltpu.PrefetchScalarGridSpec(
            num_scalar_prefetch=2,
            grid=(steps_p + 1,),
            in_specs=[
                pl.BlockSpec((1, 1, tile_p * TOP_K), lambda i, lb, nv: (jnp.minimum(i, steps_p - 1), 0, 0),
                             memory_space=pltpu.SMEM),
                pl.BlockSpec((1, 1, tile_s * TOP_K), lambda i, lb, nv: (0, 0, 0), memory_space=pltpu.SMEM),
                pl.BlockSpec((tile_p * ROW_TILE, LANES), lambda i, lb, nv: (jnp.minimum(i, steps_p - 1), 0)),
                pl.BlockSpec((tile_s * ROW_TILE, LANES), lambda i, lb, nv: (0, 0)),
            ],
            out_specs=pl.BlockSpec(memory_space=pl.ANY),
            scratch_shapes=[pltpu.VMEM((BLOCK_ROWS, LANES), F32), pltpu.SemaphoreType.DMA, pltpu.SemaphoreType.DMA],
        ),
        out_shape=jax.ShapeDtypeStruct((n_blocks * BLOCK_ROWS, LANES), F32),
        compiler_params=pltpu.CompilerParams(dimension_semantics=("arbitrary",)),
    )(last_blk, nvalid, dest_p.reshape(steps_p, 1, tile_p * TOP_K), dest_s.reshape(1, 1, tile_s * TOP_K), xn_p, xn_s)


def _expert_kernel(blk_e_ref, nvalid_ref, next_e_ref, x_ref, bg_ref, bu_ref, bd_ref, wg_hbm, wu_hbm, wd_hbm, y_ref,
                   stage, w_bf, sems):
    b = pl.program_id(0)
    valid = b < nvalid_ref[0]
    e = blk_e_ref[b]
    weights = (wg_hbm, wu_hbm, wd_hbm)

    def fetch(expert):
        return [pltpu.make_async_copy(w.at[expert], stage.at[i], sems.at[i]) for i, w in enumerate(weights)]

    @pl.when(b == 0)
    def _():
        for cp in fetch(e):
            cp.start()

    @pl.when(jnp.logical_and(valid, jnp.logical_or(b == 0, blk_e_ref[jnp.maximum(b - 1, 0)] != e)))
    def _():
        for i, cp in enumerate(fetch(e)):
            cp.wait()
            w_bf[i] = stage[i].astype(BF16)

        @pl.when(next_e_ref[b] >= 0)
        def _():
            for cp in fetch(next_e_ref[b]):
                cp.start()

    @pl.when(valid)
    def _():
        x = jnp.concatenate([x_ref[pl.ds(s, MOE_BLOCK, stride=ROW_TILE), :] for s in range(ROW_TILE)], axis=1)
        x = x.astype(BF16)
        g = _dot(x, w_bf[0]) + bg_ref[0]
        u = _dot(x, w_bf[1]) + bu_ref[0]
        g = jnp.minimum(g, SWIGLU_LIMIT)
        u = jnp.clip(u, -SWIGLU_LIMIT, SWIGLU_LIMIT)
        hdn = g * (0.5 * (1.0 + jnp.tanh((0.5 * SWIGLU_ALPHA) * g))) * (u + 1.0)
        y = _dot(hdn.astype(BF16), w_bf[2]) + bd_ref[0]
        for s in range(ROW_TILE):
            y_ref[pl.ds(s, MOE_BLOCK, stride=ROW_TILE), :] = y[:, s * LANES:(s + 1) * LANES]

    @pl.when(b >= nvalid_ref[0])
    def _():
        y_ref[...] = jnp.zeros_like(y_ref)


def _experts(blk_e, nvalid, next_e, xpad, w_gate, b_gate, w_up, b_up, w_down, b_down, n_blocks):
    d = D_MODEL
    n_w = 3
    bspec = pl.BlockSpec((1, 1, d), lambda b, be, nv, ne: (be[b], 0, 0))
    xspec = pl.BlockSpec((BLOCK_ROWS, LANES), lambda b, be, nv, ne: (jnp.minimum(b, nv[0] - 1), 0))
    hbm = pl.BlockSpec(memory_space=pl.ANY)
    return pl.pallas_call(
        _expert_kernel,
        grid_spec=pltpu.PrefetchScalarGridSpec(
            num_scalar_prefetch=3,
            grid=(n_blocks,),
            in_specs=[xspec, bspec, bspec, bspec, hbm, hbm, hbm],
            out_specs=pl.BlockSpec((BLOCK_ROWS, LANES), lambda b, be, nv, ne: (b, 0)),
            scratch_shapes=[pltpu.VMEM((n_w, d, d), F32), pltpu.VMEM((n_w, d, d), BF16),
                            pltpu.SemaphoreType.DMA((n_w,))],
        ),
        out_shape=jax.ShapeDtypeStruct((n_blocks * BLOCK_ROWS, LANES), F32),
        compiler_params=pltpu.CompilerParams(dimension_semantics=("arbitrary",), vmem_limit_bytes=VMEM_LIMIT),
    )(blk_e, nvalid, next_e, xpad, b_gate.reshape(N_EXPERTS, 1, d), b_up.reshape(N_EXPERTS, 1, d),
      b_down.reshape(N_EXPERTS, 1, d), w_gate, w_up, w_down)


COMBINE_SLOTS = 3


def _combine_kernel(dest_ref, dest_1_ref, dest_2_ref, h_ref, gate_ref, gfin_ref, ypad_ref, y_ref, *bufs_sems,
                    tile, n_steps):
    bufs, sems = bufs_sems[:COMBINE_SLOTS], bufs_sems[COMBINE_SLOTS]
    i = pl.program_id(0)

    def gather(dref, buf, sem, t, k):
        row = dref[0, 0, t * TOP_K + k]
        slot = (k * tile + t) * ROW_TILE
        if not isinstance(slot, int):
            slot = pl.multiple_of(slot, ROW_TILE)
        src = ypad_ref.at[pl.ds(pl.multiple_of(row * ROW_TILE, ROW_TILE), ROW_TILE)]
        pltpu.make_async_copy(src, buf.at[pl.ds(slot, ROW_TILE)], sem).start(priority=k % N_DMA_PRIORITIES)

    def drain(buf, sem):
        for _ in range(TOP_K):
            _row_copy_wait(ypad_ref, buf, sem, tile)

    @pl.when(i == 0)
    def _():
        def body(t, carry):
            for k in range(TOP_K):
                gather(dest_ref, bufs[0], sems.at[0], t, k)
                gather(dest_1_ref, bufs[1], sems.at[1], t, k)
            return carry
        lax.fori_loop(0, tile, body, 0)

    def step(cur, cur_sem, nxt, nxt_sem):
        drain(cur, cur_sem)
        gates = gate_ref[...]
        h = h_ref[...]
        per_slab = tile // ROW_TILE
        cols = []
        for s in range(ROW_TILE):
            for t in range(s * per_slab, (s + 1) * per_slab):
                for k in range(TOP_K):
                    gather(dest_2_ref, nxt, nxt_sem, t, k)
            acc = h[:, s * LANES:(s + 1) * LANES]
            for k in range(TOP_K):
                acc = acc + gates[:, k:k + 1] * cur[pl.ds(k * tile * ROW_TILE + s, tile, stride=ROW_TILE), :]
            cols.append(acc)
        y_ref[...] = _rms(jnp.concatenate(cols, axis=1), gfin_ref[...])

    for r in range(COMBINE_SLOTS):
        ahead = (r + COMBINE_SLOTS - 1) % COMBINE_SLOTS
        pl.when(i % COMBINE_SLOTS == r)(functools.partial(step, bufs[r], sems.at[r], bufs[ahead], sems.at[ahead]))

    @pl.when(i == n_steps - 1)
    def _():
        for extra in (n_steps, n_steps + 1):
            drain(bufs[extra % COMBINE_SLOTS], sems.at[extra % COMBINE_SLOTS])


def _combine(dest, h2, gates, g_final, ypad, *, tile):
    n, d = h2.shape
    steps = n // tile
    kern = functools.partial(_combine_kernel, tile=tile, n_steps=steps)
    dest3 = dest.reshape(steps, 1, tile * TOP_K)
    buf = pltpu.VMEM((TOP_K * tile * ROW_TILE, LANES), F32)
    tile_ahead = lambda k: pl.BlockSpec((1, 1, tile * TOP_K), lambda i: (jnp.minimum(i + k, steps - 1), 0, 0),
                                        memory_space=pltpu.SMEM)
    return pl.pallas_call(
        kern,
        grid=(steps,),
        in_specs=[
            tile_ahead(0), tile_ahead(1), tile_ahead(2),
            pl.BlockSpec((tile, d), lambda i: (i, 0)),
            pl.BlockSpec((tile, LANES), lambda i: (i, 0)),
            _const_spec((1, d)),
            pl.BlockSpec(memory_space=pl.ANY),
        ],
        out_specs=pl.BlockSpec((tile, d), lambda i: (i, 0)),
        out_shape=jax.ShapeDtypeStruct((n, d), F32),
        scratch_shapes=[buf] * COMBINE_SLOTS + [pltpu.SemaphoreType.DMA((COMBINE_SLOTS,))],
        compiler_params=pltpu.CompilerParams(dimension_semantics=("arbitrary",), vmem_limit_bytes=VMEM_LIMIT),
    )(dest3, dest3, dest3, h2, gates, g_final, ypad)


def _ssm_params(a_re, a_im, log_dt, b_re, b_im, c_re, c_im):
    dt = jnp.exp(log_dt)[:, None]
    mag = jnp.exp(a_re * dt)
    lb_re = mag * jnp.cos(a_im * dt)
    lb_im = mag * jnp.sin(a_im * dt)
    den = a_re * a_re + a_im * a_im
    q_re = ((lb_re - 1.0) * a_re + lb_im * a_im) / den
    q_im = (lb_im * a_re - (lb_re - 1.0) * a_im) / den
    bb_re = q_re[:, :, None] * b_re - q_im[:, :, None] * b_im
    bb_im = q_re[:, :, None] * b_im + q_im[:, :, None] * b_re
    eye = jnp.eye(SSM_PACK, dtype=F32)

    def pack_blockdiag(m):
        m = m.reshape(N_PACKS, SSM_PACK, m.shape[1], m.shape[2])
        return jnp.einsum('ngab,gh->ngahb', m, eye).reshape(N_PACKS, SSM_PACK * m.shape[2], SSM_PACK * m.shape[3])

    pw_re, pw_im = [jnp.ones_like(lb_re), lb_re], [jnp.zeros_like(lb_im), lb_im]
    for _ in range(RUN - 1):
        r, i = pw_re[-1], pw_im[-1]
        pw_re.append(r * lb_re - i * lb_im)
        pw_im.append(r * lb_im + i * lb_re)
    b_rows = []
    for dd in range(RUN):
        s_re = bb_re * pw_re[dd][:, :, None] - bb_im * pw_im[dd][:, :, None]
        s_im = bb_re * pw_im[dd][:, :, None] + bb_im * pw_re[dd][:, :, None]
        b_rows.append(jnp.concatenate([pack_blockdiag(s_re.transpose(0, 2, 1)),
                                       pack_blockdiag(s_im.transpose(0, 2, 1))], axis=2))
    bmat = jnp.concatenate(b_rows, axis=1)
    cmat_re = pack_blockdiag(c_re.transpose(0, 2, 1))
    cmat_im = pack_blockdiag(-c_im.transpose(0, 2, 1))
    rows_of = lambda pw: [pw[(r % RUN) + 1].reshape(-1) for r in range(SUBLANES)]
    slabbed = lambda rows: jnp.stack(rows).reshape(SUBLANES, N_SLABS, SLAB).transpose(1, 0, 2)
    return (bmat.astype(BF16), cmat_re.astype(BF16), cmat_im.astype(BF16),
            slabbed(rows_of(pw_re)), slabbed(rows_of(pw_im)))


def kernel(x_prompt, x_sample, mem_prompt, cache_mem_k, cache_mem_v, cache_pool, state_ssm_re, state_ssm_im, norm_mix, w_in, w_pool, pool_scale, ssm_a_re, ssm_a_im, ssm_log_dt, ssm_b_re, ssm_b_im, ssm_c_re, ssm_c_im, ssm_d, w_glu, b_glu, w_out, norm_xattn, norm_mem, w_q, w_k, w_v, w_o, norm_ffn, w_router, b_router, w_gate, b_gate, w_up, b_up, w_down, b_down, norm_final):
    assert x_prompt.shape[2] == D_MODEL and norm_mix.shape[0] == 1
    bp, tp, d = x_prompt.shape
    bs, ts, _ = x_sample.shape
    n_p, n_s = bp * tp, bs * ts
    row = lambda v: v.reshape(1, -1)

    bmat, c_re, c_im, t_re, t_im = _ssm_params(ssm_a_re[0], ssm_a_im[0], ssm_log_dt[0], ssm_b_re[0], ssm_b_im[0],
                                               ssm_c_re[0], ssm_c_im[0])
    mix_prm = dict(g_mix=row(norm_mix[0]), w_in=w_in[0].astype(BF16), w_pool=w_pool[0].astype(BF16),
                   pool_scale=row(pool_scale[0]), bmat=bmat, c_re=c_re, c_im=c_im, t_re=t_re, t_im=t_im,
                   d_skip=row(ssm_d[0]), w_glu=w_glu[0].astype(BF16), b_glu=row(b_glu[0]),
                   w_out=w_out[0].astype(BF16))

    attn_prm = dict(g_xattn=row(norm_xattn[0]), w_q=w_q[0].astype(BF16), w_o=w_o[0].astype(BF16),
                    g_ffn=row(norm_ffn[0]),
                    w_router=jnp.pad(w_router[0], ((0, 0), (0, LANES - N_EXPERTS))).astype(BF16),
                    b_router=jnp.pad(row(b_router[0]), ((0, 0), (0, LANES - N_EXPERTS)), constant_values=-jnp.inf))
    slab_state = lambda s: s.reshape(s.shape[0], N_SLABS, 1, SLAB)

    zeros_state = jnp.zeros((bp, N_SLABS, 1, SLAB), F32)
    h1_p, pool_p, sre_p, sim_p = _mixer(x_prompt, jnp.zeros((bp, HIST_ROWS, D_POOL), F32), zeros_state, zeros_state,
                                        mix_prm, tt=MIXER_TILE, pos0=0)
    mk, mv = _memkv(mem_prompt, row(norm_mem[0]), w_k[0].astype(BF16), w_v[0].astype(BF16))
    h2_p, xn_p, e_p, g_p, cnt_p = _attn(h1_p, mk, mv, attn_prm, tt=ATTN_TILE)

    hist_s = jnp.pad(cache_pool[0], ((0, 0), (HIST_ROWS - POOL_BUF, 0), (0, 0)))
    h1_s, pool_s, sre_s, sim_s = _mixer(x_sample, hist_s, slab_state(state_ssm_re[0]), slab_state(state_ssm_im[0]),
                                        mix_prm, tt=ts, pos0=PAST_LEN)
    ck = cache_mem_k[0].reshape(bs, N_MEM, d)
    cv = cache_mem_v[0].reshape(bs, N_MEM, d)
    h2_s, xn_s, e_s, g_s, cnt_s = _attn(h1_s, ck, cv, attn_prm, tt=ts)

    n_all = n_p + n_s
    counts = (cnt_p + cnt_s)[0, :N_EXPERTS].astype(jnp.int32)
    padded = (counts + MOE_BLOCK - 1) // MOE_BLOCK * MOE_BLOCK
    pend = jnp.cumsum(padded)
    pstart = pend - padded
    n_blocks = -(-(n_all * TOP_K + N_EXPERTS * (MOE_BLOCK - 1)) // MOE_BLOCK)
    blk_e = jnp.minimum(jnp.sum(pend[None, :] <= (jnp.arange(n_blocks) * MOE_BLOCK)[:, None], axis=1),
                        N_EXPERTS - 1).astype(jnp.int32)
    nvalid = (pend[-1:] // MOE_BLOCK).astype(jnp.int32)
    last_blk = jnp.maximum(pend // MOE_BLOCK - 1, 0).astype(jnp.int32)
    after = (pend // MOE_BLOCK)[blk_e]
    next_e = jnp.where(after < nvalid[0], blk_e[jnp.minimum(after, n_blocks - 1)], -1).astype(jnp.int32)
    start = jnp.pad(pstart.astype(F32), (0, LANES - N_EXPERTS)).reshape(1, LANES)
    e_s_tile = jnp.pad(e_s, ((0, RANK_TILE - n_s), (0, 0)), constant_values=-1)
    dest = _rank(e_p, e_s_tile, start)[:n_all, :TOP_K].reshape(-1)
    dest_p, dest_s = dest[:n_p * TOP_K], dest[n_p * TOP_K:]

    xpad = _dispatch(last_blk, nvalid, dest_p, dest_s, xn_p, xn_s, n_blocks, tile_p=DISPATCH_TILE)
    ypad = _experts(blk_e, nvalid, next_e, xpad, w_gate[0], b_gate[0], w_up[0], b_up[0], w_down[0], b_down[0], n_blocks)
    g_fin = row(norm_final)
    y_p = _combine(dest_p, h2_p.reshape(n_p, d), g_p, g_fin, ypad, tile=ROUTE_TILE)
    y_s = _combine(dest_s, h2_s.reshape(n_s, d), g_s, g_fin, ypad, tile=n_s)

    unslab = lambda s: s.reshape(1, s.shape[0], N_SSM_GROUPS, SSM_STATE)
    kv5 = lambda a: a.reshape(1, bp, N_MEM, N_XHEADS, XHEAD_DIM)
    return (y_p.reshape(bp, tp, d), y_s.reshape(bs, ts, d), kv5(mk), kv5(mv),
            pool_p[None, :, HIST_ROWS - POOL_BUF:], pool_s[None, :, HIST_ROWS - POOL_BUF:],
            unslab(sre_p), unslab(sim_p), unslab(sre_s), unslab(sim_s))
```

```python
import functools
import math

import jax
import jax.numpy as jnp
from jax import lax
from jax.experimental import pallas as pl
from jax.experimental.pallas import tpu as pltpu

D_MODEL = 1024
D_POOL = 512
D_SSM = 512
POOL_WINDOWS = (2, 4, 8, 16)
POOL_GROUP = 128
HIST_ROWS = 16
POOL_BUF = 15
SSM_GROUP = 16
N_SSM_GROUPS = 32
SSM_STATE = 64
N_STATE = N_SSM_GROUPS * SSM_STATE
N_MEM = 256
N_XHEADS = 4
XHEAD_DIM = 256
N_EXPERTS = 32
TOP_K = 4
SWIGLU_LIMIT = 7.0
SWIGLU_ALPHA = 1.702
MOE_BLOCK = 512
EPS = 1e-6
PAST_LEN = 1024

LANES = 128
SUBLANES = 8
SLAB = 256
N_SLABS = N_STATE // SLAB
SSM_PACK = LANES // SSM_GROUP
N_PACKS = N_SSM_GROUPS // SSM_PACK
PACK_STATES = SSM_PACK * SSM_STATE
SLABS_PER_PACK = PACK_STATES // SLAB
RUN = 4
ROW_TILE = D_MODEL // LANES
N_DMA_PRIORITIES = 2
PUSH_UNROLL = 4
VMEM_LIMIT = 56 * 1024 * 1024

MIXER_TILE = 512
ATTN_TILE = 1024
ROUTE_TILE = 256
DISPATCH_TILE = 2048

BF16 = jnp.bfloat16
F32 = jnp.float32


def _rms(x, g):
    return x * lax.rsqrt(jnp.mean(x * x, axis=-1, keepdims=True) + EPS) * g


def _dot(a, b):
    return jnp.dot(a, b, preferred_element_type=F32)


def _chain_slab(s, tre_ref, tim_ref, hre_scr, him_scr, re_scr, im_scr, tt):
    t_re = tre_ref[s]
    t_im = tim_ref[s]
    upper = lax.broadcasted_iota(jnp.int32, (SUBLANES, SLAB), 0) < RUN
    ta_re, ta_im = jnp.where(upper, t_re, 0.0), jnp.where(upper, t_im, 0.0)
    tb_re, tb_im = jnp.where(upper, 0.0, t_re), jnp.where(upper, 0.0, t_im)
    c_re = hre_scr[s]
    c_im = him_scr[s]
    for v in range(tt // SUBLANES):
        rows = pl.ds(v * SUBLANES, SUBLANES)
        re = re_scr[rows, :]
        im = im_scr[rows, :]
        re, im = re + (ta_re * c_re - ta_im * c_im), im + (ta_re * c_im + ta_im * c_re)
        m_re = re[RUN - 1:RUN, :]
        m_im = im[RUN - 1:RUN, :]
        re, im = re + (tb_re * m_re - tb_im * m_im), im + (tb_re * m_im + tb_im * m_re)
        re_scr[rows, :] = re
        im_scr[rows, :] = im
        c_re = re[SUBLANES - 1:SUBLANES, :]
        c_im = im[SUBLANES - 1:SUBLANES, :]
    hre_scr[s] = c_re
    him_scr[s] = c_im


def _mixer_kernel(x_ref, hist_ref, h0re_ref, h0im_ref, gmix_ref, win_ref, wpool_ref, pscale_ref,
                  bmat_ref, cre_ref, cim_ref, tre_ref, tim_ref, dskip_ref, wglu_ref, bglu_ref, wout_ref,
                  h_ref, poolnew_ref, ssmre_ref, ssmim_ref,
                  hist_scr, hre_scr, him_scr, *slab_scr, tt, pos0):
    bure_scr, buim_scr = slab_scr[:N_SLABS], slab_scr[N_SLABS:]
    j = pl.program_id(1)

    @pl.when(j == 0)
    def _():
        hist_scr[...] = hist_ref[0]
        hre_scr[...] = h0re_ref[0]
        him_scr[...] = h0im_ref[0]

    x = x_ref[0]
    xn = _rms(x, gmix_ref[...])
    z = _dot(xn.astype(BF16), win_ref[...])
    zp = z[:, :D_POOL]
    u = z[:, D_POOL:]

    ext = jnp.concatenate([hist_scr[...], zp], axis=0)
    pos = pos0 + j * tt + lax.broadcasted_iota(jnp.int32, (tt, 1), 0)
    acc = ext
    outs = []
    for gi, w in enumerate(POOL_WINDOWS):
        lo = gi * POOL_GROUP
        acc = acc[:, POOL_GROUP * (1 if gi else 0):]
        acc = acc + pltpu.roll(acc, w // 2, 0)
        wsum = acc[HIST_ROWS:, :POOL_GROUP]
        cnt = jnp.minimum(pos + 1, w).astype(F32)
        d = wsum / cnt - zp[:, lo:lo + POOL_GROUP]
        outs.append(_dot(d.astype(BF16), wpool_ref[gi]))
    y_pool = jnp.concatenate(outs, axis=1) * pscale_ref[...]
    hist_scr[...] = ext[tt:tt + HIST_ROWS]
    poolnew_ref[0] = ext[tt:tt + HIST_ROWS]

    run_row = lax.broadcasted_iota(jnp.int32, (tt, 1), 0) % RUN
    shifted = [u.astype(BF16)]
    for dd in range(1, RUN):
        shifted.append(jnp.where(run_row >= dd, pltpu.roll(u, dd, 0), 0.0).astype(BF16))
    def project(c):
        lhs = jnp.concatenate([sh[:, c * LANES:(c + 1) * LANES] for sh in shifted], axis=1)
        bu = _dot(lhs, bmat_ref[c])
        for i in range(SLABS_PER_PACK):
            s = c * SLABS_PER_PACK + i
            bure_scr[s][...] = bu[:, i * SLAB:(i + 1) * SLAB]
            buim_scr[s][...] = bu[:, PACK_STATES + i * SLAB:PACK_STATES + (i + 1) * SLAB]

    ys = []
    project(0)
    for c in range(N_PACKS):
        slabs = range(c * SLABS_PER_PACK, (c + 1) * SLABS_PER_PACK)
        if c + 1 < N_PACKS:
            project(c + 1)
        for s in slabs:
            _chain_slab(s, tre_ref, tim_ref, hre_scr, him_scr, bure_scr[s], buim_scr[s], tt)
        hs_re = jnp.concatenate([bure_scr[s][...] for s in slabs], axis=1)
        hs_im = jnp.concatenate([buim_scr[s][...] for s in slabs], axis=1)
        ys.append(_dot(hs_re.astype(BF16), cre_ref[c]) + _dot(hs_im.astype(BF16), cim_ref[c]))
    ssmre_ref[0] = hre_scr[...]
    ssmim_ref[0] = him_scr[...]
    y = jnp.concatenate(ys, axis=1) + dskip_ref[...] * u
    g = 0.5 * y * (1.0 + jnp.tanh(math.sqrt(2.0 / math.pi) * (y + 0.044715 * (y * y * y))))
    y_ssm = g * jax.nn.sigmoid(_dot(g.astype(BF16), wglu_ref[...]) + bglu_ref[...])

    mix = jnp.concatenate([y_pool, y_ssm], axis=1)
    h_ref[0] = x + _dot(mix.astype(BF16), wout_ref[...])


def _const_spec(shape):
    return pl.BlockSpec(shape, lambda *_: (0,) * len(shape))


def _mixer(x, hist, h0re, h0im, prm, *, tt, pos0):
    b, t, d = x.shape
    kern = functools.partial(_mixer_kernel, tt=tt, pos0=pos0)
    per_b3 = lambda shp: pl.BlockSpec((1,) + shp, lambda i, j: (i, 0, 0))
    per_b4 = lambda shp: pl.BlockSpec((1,) + shp, lambda i, j: (i, 0, 0, 0))
    return pl.pallas_call(
        kern,
        grid=(b, t // tt),
        in_specs=[
            pl.BlockSpec((1, tt, d), lambda i, j: (i, j, 0)),
            per_b3((HIST_ROWS, D_POOL)),
            per_b4((N_SLABS, 1, SLAB)),
            per_b4((N_SLABS, 1, SLAB)),
            _const_spec((1, d)),
            _const_spec((d, d)),
            _const_spec((len(POOL_WINDOWS), POOL_GROUP, POOL_GROUP)),
            _const_spec((1, D_POOL)),
            _const_spec((N_PACKS, RUN * LANES, 2 * PACK_STATES)),
            _const_spec((N_PACKS, PACK_STATES, LANES)),
            _const_spec((N_PACKS, PACK_STATES, LANES)),
            _const_spec((N_SLABS, SUBLANES, SLAB)),
            _const_spec((N_SLABS, SUBLANES, SLAB)),
            _const_spec((1, D_SSM)),
            _const_spec((D_SSM, D_SSM)),
            _const_spec((1, D_SSM)),
            _const_spec((d, d)),
        ],
        out_specs=[
            pl.BlockSpec((1, tt, d), lambda i, j: (i, j, 0)),
            per_b3((HIST_ROWS, D_POOL)),
            per_b4((N_SLABS, 1, SLAB)),
            per_b4((N_SLABS, 1, SLAB)),
        ],
        out_shape=[
            jax.ShapeDtypeStruct((b, t, d), F32),
            jax.ShapeDtypeStruct((b, HIST_ROWS, D_POOL), F32),
            jax.ShapeDtypeStruct((b, N_SLABS, 1, SLAB), F32),
            jax.ShapeDtypeStruct((b, N_SLABS, 1, SLAB), F32),
        ],
        scratch_shapes=[
            pltpu.VMEM((HIST_ROWS, D_POOL), F32),
            pltpu.VMEM((N_SLABS, 1, SLAB), F32),
            pltpu.VMEM((N_SLABS, 1, SLAB), F32),
        ] + [pltpu.VMEM((tt, SLAB), F32)] * (2 * N_SLABS),
        compiler_params=pltpu.CompilerParams(
            dimension_semantics=("arbitrary", "arbitrary"), vmem_limit_bytes=VMEM_LIMIT),
    )(x, hist, h0re, h0im, prm['g_mix'], prm['w_in'], prm['w_pool'], prm['pool_scale'],
      prm['bmat'], prm['c_re'], prm['c_im'], prm['t_re'], prm['t_im'], prm['d_skip'],
      prm['w_glu'], prm['b_glu'], prm['w_out'])


def _memkv_kernel(m_ref, g_ref, wk_ref, wv_ref, k_ref, v_ref):
    m = _rms(m_ref[0], g_ref[...]).astype(BF16)
    k_ref[0] = _dot(m, wk_ref[...])
    v_ref[0] = _dot(m, wv_ref[...])


def _memkv(mem, g, wk, wv):
    b, n, d = mem.shape
    blk = pl.BlockSpec((1, n, d), lambda i: (i, 0, 0))
    return pl.pallas_call(
        _memkv_kernel,
        grid=(b,),
        in_specs=[blk, _const_spec((1, d)), _const_spec((d, d)), _const_spec((d, d))],
        out_specs=[blk, blk],
        out_shape=[jax.ShapeDtypeStruct((b, n, d), F32)] * 2,
        compiler_params=pltpu.CompilerParams(vmem_limit_bytes=VMEM_LIMIT),
    )(mem, g, wk, wv)


def _attn_kernel(h_ref, k_ref, v_ref, gx_ref, wq_ref, wo_ref, gf_ref, wr_ref, br_ref,
                 h2_ref, xn_ref, e_ref, gate_ref, cnt_ref, cnt_scr, kt_scr, v_scr, *, tt):
    @pl.when(pl.program_id(1) == 0)
    def _():
        kt_scr[...] = k_ref[0].T.astype(BF16)
        v_scr[...] = v_ref[0].astype(BF16)

    h = h_ref[0]
    hn = _rms(h, gx_ref[...])
    q = _dot(hn.astype(BF16), wq_ref[...])
    outs = []
    for hd in range(N_XHEADS):
        cols = slice(hd * XHEAD_DIM, (hd + 1) * XHEAD_DIM)
        s = _dot(q[:, cols].astype(BF16), kt_scr[cols, :]) * (XHEAD_DIM ** -0.5)
        p = jnp.exp(s - jnp.max(s, axis=-1, keepdims=True))
        p = p / jnp.sum(p, axis=-1, keepdims=True)
        outs.append(_dot(p.astype(BF16), v_scr[:, cols]))
    o = jnp.concatenate(outs, axis=1)
    h2 = h + _dot(o.astype(BF16), wo_ref[...])
    h2_ref[0] = h2

    xn = _rms(h2, gf_ref[...])
    for s in range(ROW_TILE):
        xn_ref[pl.ds(s, tt, stride=ROW_TILE), :] = xn[:, s * LANES:(s + 1) * LANES]
    logits = _dot(xn.astype(BF16), wr_ref[...]) + br_ref[...]
    lane = lax.broadcasted_iota(jnp.int32, logits.shape, 1)
    lane_f = lane.astype(F32)
    e_out = jnp.zeros(logits.shape, jnp.int32)
    hits = jnp.zeros(logits.shape, F32)
    top = []
    for k in range(TOP_K):
        m = jnp.max(logits, axis=-1, keepdims=True)
        idx = jnp.min(jnp.where(logits == m, lane_f, float(LANES)), axis=-1, keepdims=True)
        e_out = jnp.where(lane == k, idx.astype(jnp.int32), e_out)
        top.append(m)
        chosen = lane_f == idx
        hits = hits + chosen.astype(F32)
        logits = jnp.where(chosen, -jnp.inf, logits)

    @pl.when(jnp.logical_and(pl.program_id(0) == 0, pl.program_id(1) == 0))
    def _():
        cnt_scr[...] = jnp.zeros_like(cnt_scr)
    cnt_scr[...] = cnt_scr[...] + jnp.sum(hits, axis=0, keepdims=True)
    cnt_ref[...] = cnt_scr[...]
    ex = [jnp.exp(m - top[0]) for m in top]
    tot = ex[0] + ex[1] + ex[2] + ex[3]
    g_out = jnp.zeros(logits.shape, F32)
    for k in range(TOP_K):
        g_out = jnp.where(lane == k, ex[k] / tot, g_out)
    e_ref[...] = e_out
    gate_ref[...] = g_out


def _attn(h, k, v, prm, *, tt):
    b, t, d = h.shape
    n = b * t
    kern = functools.partial(_attn_kernel, tt=tt)
    nt = t // tt
    kv = pl.BlockSpec((1, N_MEM, d), lambda i, j: (i, 0, 0))
    tok = lambda width: pl.BlockSpec((tt, width), lambda i, j: (i * nt + j, 0))
    return pl.pallas_call(
        kern,
        grid=(b, nt),
        in_specs=[
            pl.BlockSpec((1, tt, d), lambda i, j: (i, j, 0)), kv, kv,
            _const_spec((1, d)), _const_spec((d, d)), _const_spec((d, d)),
            _const_spec((1, d)), _const_spec((d, LANES)), _const_spec((1, LANES)),
        ],
        out_specs=[
            pl.BlockSpec((1, tt, d), lambda i, j: (i, j, 0)),
            pl.BlockSpec((tt * ROW_TILE, LANES), lambda i, j: (i * nt + j, 0)),
            tok(LANES), tok(LANES), _const_spec((1, LANES)),
        ],
        out_shape=[
            jax.ShapeDtypeStruct((b, t, d), F32),
            jax.ShapeDtypeStruct((n * ROW_TILE, LANES), F32),
            jax.ShapeDtypeStruct((n, LANES), jnp.int32),
            jax.ShapeDtypeStruct((n, LANES), F32),
            jax.ShapeDtypeStruct((1, LANES), F32),
        ],
        scratch_shapes=[pltpu.VMEM((1, LANES), F32), pltpu.VMEM((d, N_MEM), BF16), pltpu.VMEM((N_MEM, d), BF16)],
        compiler_params=pltpu.CompilerParams(
            dimension_semantics=("arbitrary", "arbitrary"), vmem_limit_bytes=VMEM_LIMIT),
    )(h, k, v, prm['g_xattn'], prm['w_q'], prm['w_o'], prm['g_ffn'], prm['w_router'], prm['b_router'])


RANK_TILE = 512


def _rank_kernel(ep_ref, es_ref, start_ref, dest_ref, carry_scr, before_scr, *, steps_p):
    i = pl.program_id(0)

    @pl.when(i == 0)
    def _():
        carry_scr[...] = start_ref[...]
        r = lax.broadcasted_iota(jnp.int32, (RANK_TILE, RANK_TILE), 0)
        c = lax.broadcasted_iota(jnp.int32, (RANK_TILE, RANK_TILE), 1)
        before_scr[...] = (c < r).astype(BF16)

    e = jnp.where(i < steps_p, ep_ref[...], es_ref[...])
    lane = lax.broadcasted_iota(jnp.int32, e.shape, 1)
    onehot = [lane == e[:, k:k + 1] for k in range(TOP_K)]
    hits = jnp.zeros(e.shape, F32)
    for oh in onehot:
        hits = hits + oh.astype(F32)
    base = _dot(before_scr[...], hits.astype(BF16)) + carry_scr[...]
    rank = jnp.zeros(e.shape, jnp.int32)
    for k, oh in enumerate(onehot):
        rk = jnp.sum(jnp.where(oh, base, 0.0), axis=-1, keepdims=True).astype(jnp.int32)
        rank = jnp.where(lane == k, rk, rank)
    dest_ref[...] = rank
    carry_scr[...] = carry_scr[...] + jnp.sum(hits, axis=0, keepdims=True)


def _rank(e_p, e_s, start):
    steps_p = e_p.shape[0] // RANK_TILE
    blk = (RANK_TILE, LANES)
    return pl.pallas_call(
        functools.partial(_rank_kernel, steps_p=steps_p),
        grid=(steps_p + 1,),
        in_specs=[pl.BlockSpec(blk, lambda i: (jnp.minimum(i, steps_p - 1), 0)), _const_spec(blk),
                  _const_spec((1, LANES))],
        out_specs=pl.BlockSpec(blk, lambda i: (i, 0)),
        out_shape=jax.ShapeDtypeStruct(((steps_p + 1) * RANK_TILE, LANES), jnp.int32),
        scratch_shapes=[pltpu.VMEM((1, LANES), F32), pltpu.VMEM((RANK_TILE, RANK_TILE), BF16)],
        compiler_params=pltpu.CompilerParams(dimension_semantics=("arbitrary",)),
    )(e_p, e_s, start)


BLOCK_ROWS = MOE_BLOCK * ROW_TILE


def _row_copy_wait(src_ref, dst_ref, sem, rows):
    pltpu.make_async_copy(src_ref.at[pl.ds(0, rows * ROW_TILE)], dst_ref.at[pl.ds(0, rows * ROW_TILE)], sem).wait()


def _dispatch_kernel(last_blk_ref, nvalid_ref, dest_p_ref, dest_s_ref, xp_ref, xs_ref, xpad_ref,
                     zero_buf, sem, zero_sem, *, tile_p, tile_s, n_blocks):
    i = pl.program_id(0)
    last = pl.num_programs(0) - 1

    def zero_copy(blk):
        dst = xpad_ref.at[pl.ds(pl.multiple_of(blk * BLOCK_ROWS, BLOCK_ROWS), BLOCK_ROWS)]
        return pltpu.make_async_copy(zero_buf, dst, zero_sem)

    @pl.when(i == 0)
    def _():
        zero_buf[...] = jnp.zeros_like(zero_buf)

        def on_pad_blocks(fn):
            for e in range(N_EXPERTS):
                if e == 0:
                    fn(last_blk_ref[0])
                else:
                    pl.when(last_blk_ref[e] != last_blk_ref[e - 1])(functools.partial(fn, last_blk_ref[e]))
            lax.fori_loop(nvalid_ref[0], n_blocks, lambda blk, c: (fn(blk), c)[1], 0)

        on_pad_blocks(lambda blk: zero_copy(blk).start())
        on_pad_blocks(lambda blk: zero_copy(blk).wait())

    def push(dest_ref, x_ref, tile):
        def body(it, carry):
            for j in range(PUSH_UNROLL):
                t = it * PUSH_UNROLL + j
                src = x_ref.at[pl.ds(pl.multiple_of(t * ROW_TILE, ROW_TILE), ROW_TILE)]
                for k in range(TOP_K):
                    row = dest_ref[0, 0, t * TOP_K + k]
                    dst = xpad_ref.at[pl.ds(pl.multiple_of(row * ROW_TILE, ROW_TILE), ROW_TILE)]
                    pltpu.make_async_copy(src, dst, sem).start(priority=k % N_DMA_PRIORITIES)
            return carry

        lax.fori_loop(0, tile // PUSH_UNROLL, body, 0)
        for _ in range(TOP_K):
            _row_copy_wait(x_ref, xpad_ref, sem, tile)

    pl.when(i < last)(lambda: push(dest_p_ref, xp_ref, tile_p))
    pl.when(i == last)(lambda: push(dest_s_ref, xs_ref, tile_s))


def _dispatch(last_blk, nvalid, dest_p, dest_s, xn_p, xn_s, n_blocks, *, tile_p):
    n_p = xn_p.shape[0] // ROW_TILE
    tile_s = xn_s.shape[0] // ROW_TILE
    steps_p = n_p // tile_p
    kern = functools.partial(_dispatch_kernel, tile_p=tile_p, tile_s=tile_s, n_blocks=n_blocks)
    return pl.pallas_call(
        kern,
        grid_spec=pltpu.PrefetchScalarGridSpec(
            num_scalar_prefetch=2,
            grid=(steps_p + 1,),
            in_specs=[
                pl.BlockSpec((1, 1, tile_p * TOP_K), lambda i, lb, nv: (jnp.minimum(i, steps_p - 1), 0, 0),
                             memory_space=pltpu.SMEM),
                pl.BlockSpec((1, 1, tile_s * TOP_K), lambda i, lb, nv: (0, 0, 0), memory_space=pltpu.SMEM),
                pl.BlockSpec((tile_p * ROW_TILE, LANES), lambda i, lb, nv: (jnp.minimum(i, steps_p - 1), 0)),
                pl.BlockSpec((tile_s * ROW_TILE, LANES), lambda i, lb, nv: (0, 0)),
            ],
            out_specs=pl.BlockSpec(memory_space=pl.ANY),
            scratch_shapes=[pltpu.VMEM((BLOCK_ROWS, LANES), F32), pltpu.SemaphoreType.DMA, pltpu.SemaphoreType.DMA],
        ),
        out_shape=jax.ShapeDtypeStruct((n_blocks * BLOCK_ROWS, LANES), F32),
        compiler_params=pltpu.CompilerParams(dimension_semantics=("arbitrary",)),
    )(last_blk, nvalid, dest_p.reshape(steps_p, 1, tile_p * TOP_K), dest_s.reshape(1, 1, tile_s * TOP_K), xn_p, xn_s)


def _expert_kernel(blk_e_ref, nvalid_ref, next_e_ref, x_ref, bg_ref, bu_ref, bd_ref, wg_hbm, wu_hbm, wd_hbm, y_ref,
                   stage, w_bf, sems):
    b = pl.program_id(0)
    valid = b < nvalid_ref[0]
    e = blk_e_ref[b]
    weights = (wg_hbm, wu_hbm, wd_hbm)

    def fetch(expert):
        return [pltpu.make_async_copy(w.at[expert], stage.at[i], sems.at[i]) for i, w in enumerate(weights)]

    @pl.when(b == 0)
    def _():
        for cp in fetch(e):
            cp.start()

    @pl.when(jnp.logical_and(valid, jnp.logical_or(b == 0, blk_e_ref[jnp.maximum(b - 1, 0)] != e)))
    def _():
        for i, cp in enumerate(fetch(e)):
            cp.wait()
            w_bf[i] = stage[i].astype(BF16)

        @pl.when(next_e_ref[b] >= 0)
        def _():
            for cp in fetch(next_e_ref[b]):
                cp.start()

    @pl.when(valid)
    def _():
        x = jnp.concatenate([x_ref[pl.ds(s, MOE_BLOCK, stride=ROW_TILE), :] for s in range(ROW_TILE)], axis=1)
        x = x.astype(BF16)
        g = _dot(x, w_bf[0]) + bg_ref[0]
        u = _dot(x, w_bf[1]) + bu_ref[0]
        g = jnp.minimum(g, SWIGLU_LIMIT)
        u = jnp.clip(u, -SWIGLU_LIMIT, SWIGLU_LIMIT)
        hdn = g * (0.5 * (1.0 + jnp.tanh((0.5 * SWIGLU_ALPHA) * g))) * (u + 1.0)
        y = _dot(hdn.astype(BF16), w_bf[2]) + bd_ref[0]
        for s in range(ROW_TILE):
            y_ref[pl.ds(s, MOE_BLOCK, stride=ROW_TILE), :] = y[:, s * LANES:(s + 1) * LANES]

    @pl.when(b >= nvalid_ref[0])
    def _():
        y_ref[...] = jnp.zeros_like(y_ref)


def _experts(blk_e, nvalid, next_e, xpad, w_gate, b_gate, w_up, b_up, w_down, b_down, n_blocks):
    d = D_MODEL
    n_w = 3
    bspec = pl.BlockSpec((1, 1, d), lambda b, be, nv, ne: (be[b], 0, 0))
    xspec = pl.BlockSpec((BLOCK_ROWS, LANES), lambda b, be, nv, ne: (jnp.minimum(b, nv[0] - 1), 0))
    hbm = pl.BlockSpec(memory_space=pl.ANY)
    return pl.pallas_call(
        _expert_kernel,
        grid_spec=pltpu.PrefetchScalarGridSpec(
            num_scalar_prefetch=3,
            grid=(n_blocks,),
            in_specs=[xspec, bspec, bspec, bspec, hbm, hbm, hbm],
            out_specs=pl.BlockSpec((BLOCK_ROWS, LANES), lambda b, be, nv, ne: (b, 0)),
            scratch_shapes=[pltpu.VMEM((n_w, d, d), F32), pltpu.VMEM((n_w, d, d), BF16),
                            pltpu.SemaphoreType.DMA((n_w,))],
        ),
        out_shape=jax.ShapeDtypeStruct((n_blocks * BLOCK_ROWS, LANES), F32),
        compiler_params=pltpu.CompilerParams(dimension_semantics=("arbitrary",), vmem_limit_bytes=VMEM_LIMIT),
    )(blk_e, nvalid, next_e, xpad, b_gate.reshape(N_EXPERTS, 1, d), b_up.reshape(N_EXPERTS, 1, d),
      b_down.reshape(N_EXPERTS, 1, d), w_gate, w_up, w_down)


COMBINE_SLOTS = 3


def _combine_kernel(dest_ref, dest_1_ref, dest_2_ref, h_ref, gate_ref, gfin_ref, ypad_ref, y_ref, *bufs_sems,
                    tile, n_steps):
    bufs, sems = bufs_sems[:COMBINE_SLOTS], bufs_sems[COMBINE_SLOTS]
    i = pl.program_id(0)

    def gather(dref, buf, sem, t, k):
        row = dref[0, 0, t * TOP_K + k]
        slot = (k * tile + t) * ROW_TILE
        if not isinstance(slot, int):
            slot = pl.multiple_of(slot, ROW_TILE)
        src = ypad_ref.at[pl.ds(pl.multiple_of(row * ROW_TILE, ROW_TILE), ROW_TILE)]
        pltpu.make_async_copy(src, buf.at[pl.ds(slot, ROW_TILE)], sem).start(priority=k % N_DMA_PRIORITIES)

    def drain(buf, sem):
        for _ in range(TOP_K):
            _row_copy_wait(ypad_ref, buf, sem, tile)

    @pl.when(i == 0)
    def _():
        def body(t, carry):
            for k in range(TOP_K):
                gather(dest_ref, bufs[0], sems.at[0], t, k)
                gather(dest_1_ref, bufs[1], sems.at[1], t, k)
            return carry
        lax.fori_loop(0, tile, body, 0)

    def step(cur, cur_sem, nxt, nxt_sem):
        drain(cur, cur_sem)
        gates = gate_ref[...]
        h = h_ref[...]
        per_slab = tile // ROW_TILE
        cols = []
        for s in range(ROW_TILE):
            for t in range(s * per_slab, (s + 1) * per_slab):
                for k in range(TOP_K):
                    gather(dest_2_ref, nxt, nxt_sem, t, k)
            acc = h[:, s * LANES:(s + 1) * LANES]
            for k in range(TOP_K):
                acc = acc + gates[:, k:k + 1] * cur[pl.ds(k * tile * ROW_TILE + s, tile, stride=ROW_TILE), :]
            cols.append(acc)
        y_ref[...] = _rms(jnp.concatenate(cols, axis=1), gfin_ref[...])

    for r in range(COMBINE_SLOTS):
        ahead = (r + COMBINE_SLOTS - 1) % COMBINE_SLOTS
        pl.when(i % COMBINE_SLOTS == r)(functools.partial(step, bufs[r], sems.at[r], bufs[ahead], sems.at[ahead]))

    @pl.when(i == n_steps - 1)
    def _():
        for extra in (n_steps, n_steps + 1):
            drain(bufs[extra % COMBINE_SLOTS], sems.at[extra % COMBINE_SLOTS])


def _combine(dest, h2, gates, g_final, ypad, *, tile):
    n, d = h2.shape
    steps = n // tile
    kern = functools.partial(_combine_kernel, tile=tile, n_steps=steps)
    dest3 = dest.reshape(steps, 1, tile * TOP_K)
    buf = pltpu.VMEM((TOP_K * tile * ROW_TILE, LANES), F32)
    tile_ahead = lambda k: pl.BlockSpec((1, 1, tile * TOP_K), lambda i: (jnp.minimum(i + k, steps - 1), 0, 0),
                                        memory_space=pltpu.SMEM)
    return pl.pallas_call(
        kern,
        grid=(steps,),
        in_specs=[
            tile_ahead(0), tile_ahead(1), tile_ahead(2),
            pl.BlockSpec((tile, d), lambda i: (i, 0)),
            pl.BlockSpec((tile, LANES), lambda i: (i, 0)),
            _const_spec((1, d)),
            pl.BlockSpec(memory_space=pl.ANY),
        ],
        out_specs=pl.BlockSpec((tile, d), lambda i: (i, 0)),
        out_shape=jax.ShapeDtypeStruct((n, d), F32),
        scratch_shapes=[buf] * COMBINE_SLOTS + [pltpu.SemaphoreType.DMA((COMBINE_SLOTS,))],
        compiler_params=pltpu.CompilerParams(dimension_semantics=("arbitrary",), vmem_limit_bytes=VMEM_LIMIT),
    )(dest3, dest3, dest3, h2, gates, g_final, ypad)


def _ssm_params(a_re, a_im, log_dt, b_re, b_im, c_re, c_im):
    dt = jnp.exp(log_dt)[:, None]
    mag = jnp.exp(a_re * dt)
    lb_re = mag * jnp.cos(a_im * dt)
    lb_im = mag * jnp.sin(a_im * dt)
    den = a_re * a_re + a_im * a_im
    q_re = ((lb_re - 1.0) * a_re + lb_im * a_im) / den
    q_im = (lb_im * a_re - (lb_re - 1.0) * a_im) / den
    bb_re = q_re[:, :, None] * b_re - q_im[:, :, None] * b_im
    bb_im = q_re[:, :, None] * b_im + q_im[:, :, None] * b_re
    eye = jnp.eye(SSM_PACK, dtype=F32)

    def pack_blockdiag(m):
        m = m.reshape(N_PACKS, SSM_PACK, m.shape[1], m.shape[2])
        return jnp.einsum('ngab,gh->ngahb', m, eye).reshape(N_PACKS, SSM_PACK * m.shape[2], SSM_PACK * m.shape[3])

    pw_re, pw_im = [jnp.ones_like(lb_re), lb_re], [jnp.zeros_like(lb_im), lb_im]
    for _ in range(RUN - 1):
        r, i = pw_re[-1], pw_im[-1]
        pw_re.append(r * lb_re - i * lb_im)
        pw_im.append(r * lb_im + i * lb_re)
    b_rows = []
    for dd in range(RUN):
        s_re = bb_re * pw_re[dd][:, :, None] - bb_im * pw_im[dd][:, :, None]
        s_im = bb_re * pw_im[dd][:, :, None] + bb_im * pw_re[dd][:, :, None]
        b_rows.append(jnp.concatenate([pack_blockdiag(s_re.transpose(0, 2, 1)),
                                       pack_blockdiag(s_im.transpose(0, 2, 1))], axis=2))
    bmat = jnp.concatenate(b_rows, axis=1)
    cmat_re = pack_blockdiag(c_re.transpose(0, 2, 1))
    cmat_im = pack_blockdiag(-c_im.transpose(0, 2, 1))
    rows_of = lambda pw: [pw[(r % RUN) + 1].reshape(-1) for r in range(SUBLANES)]
    slabbed = lambda rows: jnp.stack(rows).reshape(SUBLANES, N_SLABS, SLAB).transpose(1, 0, 2)
    return (bmat.astype(BF16), cmat_re.astype(BF16), cmat_im.astype(BF16),
            slabbed(rows_of(pw_re)), slabbed(rows_of(pw_im)))


def kernel(x_prompt, x_sample, mem_prompt, cache_mem_k, cache_mem_v, cache_pool, state_ssm_re, state_ssm_im, norm_mix, w_in, w_pool, pool_scale, ssm_a_re, ssm_a_im, ssm_log_dt, ssm_b_re, ssm_b_im, ssm_c_re, ssm_c_im, ssm_d, w_glu, b_glu, w_out, norm_xattn, norm_mem, w_q, w_k, w_v, w_o, norm_ffn, w_router, b_router, w_gate, b_gate, w_up, b_up, w_down, b_down, norm_final):
    assert x_prompt.shape[2] == D_MODEL and norm_mix.shape[0] == 1
    bp, tp, d = x_prompt.shape
    bs, ts, _ = x_sample.shape
    n_p, n_s = bp * tp, bs * ts
    row = lambda v: v.reshape(1, -1)

    bmat, c_re, c_im, t_re, t_im = _ssm_params(ssm_a_re[0], ssm_a_im[0], ssm_log_dt[0], ssm_b_re[0], ssm_b_im[0],
                                               ssm_c_re[0], ssm_c_im[0])
    mix_prm = dict(g_mix=row(norm_mix[0]), w_in=w_in[0].astype(BF16), w_pool=w_pool[0].astype(BF16),
                   pool_scale=row(pool_scale[0]), bmat=bmat, c_re=c_re, c_im=c_im, t_re=t_re, t_im=t_im,
                   d_skip=row(ssm_d[0]), w_glu=w_glu[0].astype(BF16), b_glu=row(b_glu[0]),
                   w_out=w_out[0].astype(BF16))

    attn_prm = dict(g_xattn=row(norm_xattn[0]), w_q=w_q[0].astype(BF16), w_o=w_o[0].astype(BF16),
                    g_ffn=row(norm_ffn[0]),
                    w_router=jnp.pad(w_router[0], ((0, 0), (0, LANES - N_EXPERTS))).astype(BF16),
                    b_router=jnp.pad(row(b_router[0]), ((0, 0), (0, LANES - N_EXPERTS)), constant_values=-jnp.inf))
    slab_state = lambda s: s.reshape(s.shape[0], N_SLABS, 1, SLAB)

    zeros_state = jnp.zeros((bp, N_SLABS, 1, SLAB), F32)
    h1_p, pool_p, sre_p, sim_p = _mixer(x_prompt, jnp.zeros((bp, HIST_ROWS, D_POOL), F32), zeros_state, zeros_state,
                                        mix_prm, tt=MIXER_TILE, pos0=0)
    mk, mv = _memkv(mem_prompt, row(norm_mem[0]), w_k[0].astype(BF16), w_v[0].astype(BF16))
    h2_p, xn_p, e_p, g_p, cnt_p = _attn(h1_p, mk, mv, attn_prm, tt=ATTN_TILE)

    hist_s = jnp.pad(cache_pool[0], ((0, 0), (HIST_ROWS - POOL_BUF, 0), (0, 0)))
    h1_s, pool_s, sre_s, sim_s = _mixer(x_sample, hist_s, slab_state(state_ssm_re[0]), slab_state(state_ssm_im[0]),
                                        mix_prm, tt=ts, pos0=PAST_LEN)
    ck = cache_mem_k[0].reshape(bs, N_MEM, d)
    cv = cache_mem_v[0].reshape(bs, N_MEM, d)
    h2_s, xn_s, e_s, g_s, cnt_s = _attn(h1_s, ck, cv, attn_prm, tt=ts)

    n_all = n_p + n_s
    counts = (cnt_p + cnt_s)[0, :N_EXPERTS].astype(jnp.int32)
    padded = (counts + MOE_BLOCK - 1) // MOE_BLOCK * MOE_BLOCK
    pend = jnp.cumsum(padded)
    pstart = pend - padded
    n_blocks = -(-(n_all * TOP_K + N_EXPERTS * (MOE_BLOCK - 1)) // MOE_BLOCK)
    blk_e = jnp.minimum(jnp.sum(pend[None, :] <= (jnp.arange(n_blocks) * MOE_BLOCK)[:, None], axis=1),
                        N_EXPERTS - 1).astype(jnp.int32)
    nvalid = (pend[-1:] // MOE_BLOCK).astype(jnp.int32)
    last_blk = jnp.maximum(pend // MOE_BLOCK - 1, 0).astype(jnp.int32)
    after = (pend // MOE_BLOCK)[blk_e]
    next_e = jnp.where(after < nvalid[0], blk_e[jnp.minimum(after, n_blocks - 1)], -1).astype(jnp.int32)
    start = jnp.pad(pstart.astype(F32), (0, LANES - N_EXPERTS)).reshape(1, LANES)
    e_s_tile = jnp.pad(e_s, ((0, RANK_TILE - n_s), (0, 0)), constant_values=-1)
    dest = _rank(e_p, e_s_tile, start)[:n_all, :TOP_K].reshape(-1)
    dest_p, dest_s = dest[:n_p * TOP_K], dest[n_p * TOP_K:]

    xpad = _dispatch(last_blk, nvalid, dest_p, dest_s, xn_p, xn_s, n_blocks, tile_p=DISPATCH_TILE)
    ypad = _experts(blk_e, nvalid, next_e, xpad, w_gate[0], b_gate[0], w_up[0], b_up[0], w_down[0], b_down[0], n_blocks)
    g_fin = row(norm_final)
    y_p = _combine(dest_p, h2_p.reshape(n_p, d), g_p, g_fin, ypad, tile=ROUTE_TILE)
    y_s = _combine(dest_s, h2_s.reshape(n_s, d), g_s, g_fin, ypad, tile=n_s)

    unslab = lambda s: s.reshape(1, s.shape[0], N_SSM_GROUPS, SSM_STATE)
    kv5 = lambda a: a.reshape(1, bp, N_MEM, N_XHEADS, XHEAD_DIM)
    return (y_p.reshape(bp, tp, d), y_s.reshape(bs, ts, d), kv5(mk), kv5(mv),
            pool_p[None, :, HIST_ROWS - POOL_BUF:], pool_s[None, :, HIST_ROWS - POOL_BUF:],
            unslab(sre_p), unslab(sim_p), unslab(sre_s), unslab(sim_s))
```
